```python
import math
import jax, jax.numpy as jnp
from jax import lax
import numpy as np

D_MODEL = 1024
BATCH = 32
SEQ = 256
DEPTH = 2
DEC_BATCH = 4
DEC_SEQ = 4096
PAST_LEN = 256

GRID_W = 64
N_HEADS = 8
N_KV_HEADS = 2
HEAD_DIM = 64
GQA_GROUP = N_HEADS // N_KV_HEADS
ATTN_WIDTH = N_HEADS * HEAD_DIM
KV_WIDTH = N_KV_HEADS * HEAD_DIM
WINDOW = 128
BLOCK = 128
ROPE_BASE = 10000.0
ROPE_PAIRS_PER_AXIS = HEAD_DIM // 4
HYENA_WIDTH = D_MODEL // 4
FILTER_EMB = 33
FILTER_BANDS = (FILTER_EMB - 1) // 2
FILTER_HIDDEN = 64
DECAY_FAST = abs(math.log(1e-2) / 0.3)
DECAY_SLOW = abs(math.log(1e-2) / 1.5)
CONV_WIDTH = D_MODEL // 4
MIX_WIDTH = ATTN_WIDTH + HYENA_WIDTH + CONV_WIDTH
IN_WIDTH = ATTN_WIDTH + 2 * KV_WIDTH + 3 * HYENA_WIDTH + 3 * CONV_WIDTH
IN_SPLITS = [ATTN_WIDTH, ATTN_WIDTH + KV_WIDTH, ATTN_WIDTH + 2 * KV_WIDTH,
             ATTN_WIDTH + 2 * KV_WIDTH + 3 * HYENA_WIDTH]
N_EXPERTS = 32
TOP_K = 4
D_FF = D_MODEL
SWIGLU_LIMIT = 7.0
SWIGLU_ALPHA = 1.702
DEEPNORM_ALPHA = (2 * DEPTH) ** 0.25
DEEPNORM_BETA = (8 * DEPTH) ** -0.25
LN_EPS = 1e-5

kernel_name = "hybrid_hyena_swa_shortconv_moe_diffusion_step"


def layer_norm(x, g, b):
    xf = x.astype(jnp.float32)
    mu = jnp.mean(xf, axis=-1, keepdims=True)
    var = jnp.mean(jnp.square(xf - mu), axis=-1, keepdims=True)
    return ((xf - mu) * lax.rsqrt(var + LN_EPS)).astype(x.dtype) * g + b


def dwconv3(x, w):
    xp = jnp.pad(x, ((0, 0), (1, 1), (0, 0)))
    return xp[:, :-2] * w[0] + xp[:, 1:-1] * w[1] + xp[:, 2:] * w[2]


def axial_rope(x):
    L = x.shape[1]
    rows = L // GRID_W
    row = jnp.repeat(jnp.arange(rows, dtype=jnp.float32), GRID_W)
    col = jnp.tile(jnp.arange(GRID_W, dtype=jnp.float32), rows)
    inv = ROPE_BASE ** (-jnp.arange(ROPE_PAIRS_PER_AXIS, dtype=jnp.float32) / ROPE_PAIRS_PER_AXIS)
    ang = jnp.concatenate([row[:, None] * inv, col[:, None] * inv], axis=-1)
    cos = jnp.cos(ang)[None, :, None, :].astype(x.dtype)
    sin = jnp.sin(ang)[None, :, None, :].astype(x.dtype)
    x1, x2 = jnp.split(x, 2, axis=-1)
    return jnp.concatenate([x1 * cos - x2 * sin, x2 * cos + x1 * sin], axis=-1)


def context_attention(q, k, v, sink):
    B, L = q.shape[:2]
    nb = L // BLOCK
    qb = (q * HEAD_DIM ** -0.5).reshape(B, nb, BLOCK, N_KV_HEADS, GQA_GROUP, HEAD_DIM).swapaxes(0, 1)
    s_sink = sink.astype(jnp.float32).reshape(1, N_KV_HEADS, GQA_GROUP, 1, 1)

    def one_block(q_blk):
        s = jnp.einsum('bqgrd,bkgd->bgrqk', q_blk, k).astype(jnp.float32)
        s = jnp.concatenate([s, jnp.broadcast_to(s_sink, s.shape[:-1] + (1,))], axis=-1)
        p = jax.nn.softmax(s, axis=-1)[..., :-1].astype(v.dtype)
        return jnp.einsum('bgrqk,bkgd->bqgrd', p, v)

    o = lax.map(one_block, qb)
    return o.swapaxes(0, 1).reshape(B, L, ATTN_WIDTH)


def band_blocks(t, nb):
    B = t.shape[0]
    tp = jnp.pad(t, ((0, 0), (BLOCK, BLOCK), (0, 0), (0, 0))).reshape(B, nb + 2, BLOCK, *t.shape[2:])
    return jnp.concatenate([tp[:, :-2], tp[:, 1:-1], tp[:, 2:]], axis=2)


def latent_attention(q, k, v, k_ctx, v_ctx, sink):
    B, L = q.shape[:2]
    nb = L // BLOCK
    nk = 3 * BLOCK
    qb = (q * HEAD_DIM ** -0.5).reshape(B, nb, BLOCK, N_KV_HEADS, GQA_GROUP, HEAD_DIM)
    kb, vb = band_blocks(k, nb), band_blocks(v, nb)
    s_loc = jnp.einsum('bnqgrd,bnkgd->bngrqk', qb, kb).astype(jnp.float32)
    q_idx = jnp.arange(BLOCK)[:, None]
    k_off = jnp.arange(nk)[None, :] - BLOCK
    k_abs = jnp.arange(nb)[:, None, None] * BLOCK + k_off
    valid = (jnp.abs(k_off - q_idx) <= WINDOW)[None] & (k_abs >= 0) & (k_abs < L)
    s_loc = jnp.where(valid[None, :, None, None], s_loc, -jnp.inf)
    s_ctx = jnp.einsum('bnqgrd,bkgd->bngrqk', qb, k_ctx).astype(jnp.float32)
    s_sink = jnp.broadcast_to(sink.astype(jnp.float32).reshape(1, 1, N_KV_HEADS, GQA_GROUP, 1, 1),
                              s_loc.shape[:-1] + (1,))
    p = jax.nn.softmax(jnp.concatenate([s_loc, s_ctx, s_sink], axis=-1), axis=-1).astype(v.dtype)
    n_ctx = k_ctx.shape[1]
    o = (jnp.einsum('bngrqk,bnkgd->bnqgrd', p[..., :nk], vb)
         + jnp.einsum('bngrqk,bkgd->bnqgrd', p[..., nk:nk + n_ctx], v_ctx))
    return o.reshape(B, L, ATTN_WIDTH)


def hyena_filters(L, w1, b1, f1, w2, b2, f2, w3, deltas):
    f32 = jnp.float32
    t = jnp.linspace(0.0, 1.0, L, dtype=f32)[:, None]
    w = 2.0 * math.pi * jnp.arange(L, dtype=f32)[:, None] / L
    bands = jnp.linspace(1e-4, FILTER_BANDS - 1, FILTER_BANDS, dtype=f32)[None, :]
    z = jnp.concatenate([t, jnp.cos(bands * w), -jnp.sin(bands * w)], axis=-1)
    h = jnp.sin(f1.astype(f32) * (z @ w1.astype(f32) + b1.astype(f32)))
    h = jnp.sin(f2.astype(f32) * (h @ w2.astype(f32) + b2.astype(f32)))
    h = (h @ w3.astype(f32)) * jnp.exp(-t * jnp.abs(deltas.astype(f32)))
    return h.reshape(L, 2, HYENA_WIDTH)


def bidirectional_fftconv(u, h_fwd, h_bwd):
    L = u.shape[1]
    kern = jnp.concatenate([h_fwd, jnp.zeros_like(h_fwd[:1]), h_bwd[:0:-1]], axis=0)
    u_f = jnp.fft.rfft(u.astype(jnp.float32), n=2 * L, axis=1)
    k_f = jnp.fft.rfft(kern, axis=0)
    return jnp.fft.irfft(u_f * k_f[None], n=2 * L, axis=1)[:, :L].astype(u.dtype)


def hyena_mixer(u, p):
    L = u.shape[1]
    u = dwconv3(u, p['hy_conv_w']) + p['hy_conv_b']
    x0, x1, v = jnp.split(u, 3, axis=-1)
    filt = hyena_filters(L, p['hy_w1'], p['hy_b1'], p['hy_f1'], p['hy_w2'], p['hy_b2'], p['hy_f2'],
                         p['hy_w3'], p['hy_deltas'])
    g = x1 * v
    y = bidirectional_fftconv(g, filt[:, 0], filt[:, 1]) + g * p['hy_bias']
    return x0 * y


def short_conv_mixer(u, conv_w):
    b_gate, c_gate, hx = jnp.split(u, 3, axis=-1)
    return b_gate * dwconv3(c_gate * hx, conv_w)


def moe_ffn(h, p):
    shape = h.shape
    t = h.reshape(-1, shape[-1])
    logits = (t @ p['router_w'] + p['router_b']).astype(jnp.float32)
    top_val, top_idx = lax.top_k(logits, TOP_K)
    top_w = jax.nn.softmax(top_val, axis=-1)
    gates = jnp.einsum('tk,tke->et', top_w, jax.nn.one_hot(top_idx, N_EXPERTS, dtype=jnp.float32))

    def expert(acc, e):
        wg, bg, wu, bu, wd, bd, ge = e
        g = jnp.minimum(t @ wg + bg, SWIGLU_LIMIT)
        u = jnp.clip(t @ wu + bu, -SWIGLU_LIMIT, SWIGLU_LIMIT)
        y = ((u + 1) * (g * jax.nn.sigmoid(SWIGLU_ALPHA * g))) @ wd + bd
        return acc + ge[:, None] * y.astype(jnp.float32), None

    out, _ = lax.scan(expert, jnp.zeros(t.shape, jnp.float32),
                      (p['w_gate'], p['b_gate'], p['w_up'], p['b_up'], p['w_down'], p['b_down'], gates))
    return out.astype(h.dtype).reshape(shape)


def trunk_layer(x, mod, ctx_kv, p):
    B, L, _ = x.shape
    sh1, sc1, g1, sh2, sc2, g2 = jnp.split(mod[:, None, :], 6, axis=-1)
    h = x * (1 + sc1) + sh1
    q, k, v, u_hy, u_sc = jnp.split(h @ p['w_in'], IN_SPLITS, axis=-1)
    q = q.reshape(B, L, N_HEADS, HEAD_DIM)
    k = k.reshape(B, L, N_KV_HEADS, HEAD_DIM)
    v = v.reshape(B, L, N_KV_HEADS, HEAD_DIM)
    if ctx_kv is None:
        o_attn = context_attention(q, k, v, p['attn_sink'])
    else:
        o_attn = latent_attention(axial_rope(q), axial_rope(k), v, ctx_kv[0], ctx_kv[1], p['attn_sink'])
    o_hy = hyena_mixer(u_hy, p)
    o_sc = short_conv_mixer(u_sc, p['sc_conv_w'])
    o = jnp.concatenate([o_attn, o_hy, o_sc], axis=-1) @ p['w_out']
    x = layer_norm(DEEPNORM_ALPHA * x + g1 * o, p['ln1_g'], p['ln1_b'])
    h = x * (1 + sc2) + sh2
    x = layer_norm(DEEPNORM_ALPHA * x + g2 * moe_ffn(h, p), p['ln2_g'], p['ln2_b'])
    return x, k, v


def setup_inputs(seed: int = 0) -> dict:
    key = jax.random.key(seed)
    keys = iter(jax.random.split(key, 48))

    def nrm(shape, scale=1.0):
        return scale * jax.random.normal(next(keys), shape, jnp.float32)

    D, E, F = D_MODEL, N_EXPERTS, D_FF
    deltas = jnp.tile(jnp.linspace(DECAY_SLOW, DECAY_FAST, HYENA_WIDTH, dtype=jnp.float32), 2)
    return {
        'x_prompt': nrm((BATCH, SEQ, D)),
        'x_sample': nrm((DEC_BATCH, DEC_SEQ, D)),
        'cache_k': nrm((DEC_BATCH, DEPTH, PAST_LEN, N_KV_HEADS, HEAD_DIM)),
        'cache_v': nrm((DEC_BATCH, DEPTH, PAST_LEN, N_KV_HEADS, HEAD_DIM)),
        'c': nrm((DEC_BATCH, D)),
        'c_ctx': nrm((D,)),
        'w_mod': nrm((DEPTH, D, 6 * D), 0.5 * D ** -0.5),
        'b_mod': nrm((DEPTH, 6 * D), 0.02),
        'w_in': nrm((DEPTH, D, IN_WIDTH), D ** -0.5),
        'w_out': nrm((DEPTH, MIX_WIDTH, D), DEEPNORM_BETA * MIX_WIDTH ** -0.5),
        'attn_sink': nrm((DEPTH, N_HEADS), 0.5),
        'hy_conv_w': nrm((DEPTH, 3, 3 * HYENA_WIDTH), 3 ** -0.5),
        'hy_conv_b': nrm((DEPTH, 3 * HYENA_WIDTH), 0.01),
        'hy_w1': nrm((DEPTH, FILTER_EMB, FILTER_HIDDEN), FILTER_EMB ** -0.5),
        'hy_b1': nrm((DEPTH, FILTER_HIDDEN), 0.1),
        'hy_f1': 1.0 + nrm((DEPTH, FILTER_HIDDEN), 0.01),
        'hy_w2': nrm((DEPTH, FILTER_HIDDEN, FILTER_HIDDEN), FILTER_HIDDEN ** -0.5),
        'hy_b2': nrm((DEPTH, FILTER_HIDDEN), 0.1),
        'hy_f2': 1.0 + nrm((DEPTH, FILTER_HIDDEN), 0.01),
        'hy_w3': nrm((DEPTH, FILTER_HIDDEN, 2 * HYENA_WIDTH), 0.05 * FILTER_HIDDEN ** -0.5),
        'hy_deltas': deltas[None, :] + nrm((DEPTH, 2 * HYENA_WIDTH), 0.01),
        'hy_bias': nrm((DEPTH, HYENA_WIDTH), 0.1),
        'sc_conv_w': nrm((DEPTH, 3, CONV_WIDTH), 3 ** -0.5),
        'ln1_g': 1.0 + nrm((DEPTH, D), 0.01),
        'ln1_b': nrm((DEPTH, D), 0.01),
        'router_w': nrm((DEPTH, D, E), D ** -0.5),
        'router_b': nrm((DEPTH, E), 0.01),
        'w_gate': nrm((DEPTH, E, D, F), D ** -0.5),
        'b_gate': nrm((DEPTH, E, F), 0.01),
        'w_up': nrm((DEPTH, E, D, F), D ** -0.5),
        'b_up': nrm((DEPTH, E, F), 0.01),
        'w_down': nrm((DEPTH, E, F, D), DEEPNORM_BETA * F ** -0.5),
        'b_down': nrm((DEPTH, E, D), 0.01),
        'ln2_g': 1.0 + nrm((DEPTH, D), 0.01),
        'ln2_b': nrm((DEPTH, D), 0.01),
    }


def reference(x_prompt, x_sample, cache_k, cache_v, c, c_ctx, w_mod, b_mod, w_in, w_out, attn_sink,
              hy_conv_w, hy_conv_b, hy_w1, hy_b1, hy_f1, hy_w2, hy_b2, hy_f2, hy_w3, hy_deltas, hy_bias,
              sc_conv_w, ln1_g, ln1_b, router_w, router_b, w_gate, b_gate, w_up, b_up, w_down, b_down,
              ln2_g, ln2_b):
    cond_ctx = jax.nn.silu(c_ctx)[None, :]
    cond_lat = jax.nn.silu(c)
    xp, xs = x_prompt, x_sample
    new_k, new_v = [], []
    for l in range(DEPTH):
        p = {
            'w_in': w_in[l], 'w_out': w_out[l], 'attn_sink': attn_sink[l],
            'hy_conv_w': hy_conv_w[l], 'hy_conv_b': hy_conv_b[l],
            'hy_w1': hy_w1[l], 'hy_b1': hy_b1[l], 'hy_f1': hy_f1[l],
            'hy_w2': hy_w2[l], 'hy_b2': hy_b2[l], 'hy_f2': hy_f2[l],
            'hy_w3': hy_w3[l], 'hy_deltas': hy_deltas[l], 'hy_bias': hy_bias[l],
            'sc_conv_w': sc_conv_w[l], 'ln1_g': ln1_g[l], 'ln1_b': ln1_b[l],
            'router_w': router_w[l], 'router_b': router_b[l],
            'w_gate': w_gate[l], 'b_gate': b_gate[l], 'w_up': w_up[l], 'b_up': b_up[l],
            'w_down': w_down[l], 'b_down': b_down[l], 'ln2_g': ln2_g[l], 'ln2_b': ln2_b[l],
        }
        mod_ctx = cond_ctx @ w_mod[l] + b_mod[l]
        mod_lat = cond_lat @ w_mod[l] + b_mod[l]
        xp, k_l, v_l = trunk_layer(xp, mod_ctx, None, p)
        new_k.append(k_l)
        new_v.append(v_l)
        xs, _, _ = trunk_layer(xs, mod_lat, (cache_k[:, l], cache_v[:, l]), p)
    new_k = jnp.stack(new_k, axis=1)
    new_v = jnp.stack(new_v, axis=1)
    return (xp, xs, new_k, new_v)
```

```python
import functools
import math

import jax
import jax.numpy as jnp
from jax import lax
from jax.experimental import pallas as pl
from jax.experimental.pallas import tpu as pltpu

F32 = jnp.float32
BF16 = jnp.bfloat16
HI = lax.Precision.HIGHEST

D = 1024
DEPTH = 2
N_HEADS = 8
N_KV = 2
HD = 64
GQA = N_HEADS // N_KV
ATT_W = N_HEADS * HD
KV_W = N_KV * HD
WINDOW = 128
BLK = 128
GRID_W = 64
ROPE_BASE = 10000.0
HY_W = 256
SC_W = 256
FILTER_EMB = 33
FILTER_BANDS = 16
FILTER_HIDDEN = 64
IN_W = ATT_W + 2 * KV_W + 3 * HY_W + 3 * SC_W
N_EXP = 32
TOP_K = 4
SWIGLU_LIMIT = 7.0
SWIGLU_ALPHA = 1.702
DN_ALPHA = (2 * DEPTH) ** 0.25
LN_EPS = 1e-5
NEG_INF = float("-inf")

VMEM_LIMIT = 56 * 1024 * 1024


def _cp(sem):
    return pltpu.CompilerParams(dimension_semantics=sem, vmem_limit_bytes=VMEM_LIMIT)


def _dot(a, b, precision=None):
    return jnp.dot(a, b, preferred_element_type=F32, precision=precision)


def _dot_nt(a, b):
    return lax.dot_general(a, b, (((1,), (1,)), ((), ())), preferred_element_type=F32)


def _layer_norm(z, g, b):
    mu = jnp.mean(z, axis=-1, keepdims=True)
    zc = z - mu
    var = jnp.mean(zc * zc, axis=-1, keepdims=True)
    return zc * lax.rsqrt(var + LN_EPS) * g + b


def _mod_kernel(cond_ref, w_ref, b_ref, o_ref):
    c = cond_ref[...]
    s = c * jax.nn.sigmoid(c)
    o_ref[0] = _dot(s, w_ref[0], HI) + b_ref[0]


def _modulation(cond, w_mod, b_mod):
    tn = 1024
    out = pl.pallas_call(
        _mod_kernel,
        grid=(DEPTH, 6 * D // tn),
        in_specs=[
            pl.BlockSpec((8, D), lambda l, j: (0, 0)),
            pl.BlockSpec((1, D, tn), lambda l, j: (l, 0, j)),
            pl.BlockSpec((1, 1, tn), lambda l, j: (l, 0, j)),
        ],
        out_specs=pl.BlockSpec((1, 8, tn), lambda l, j: (l, 0, j)),
        out_shape=jax.ShapeDtypeStruct((DEPTH, 8, 6 * D), F32),
        compiler_params=_cp(("arbitrary", "arbitrary")),
        name="modulation",
    )(cond, w_mod, b_mod.reshape(DEPTH, 1, 6 * D))
    return out.reshape(DEPTH, 8, 6, D)


def _mod_row(i, tm, t_ctx, dec_seq):
    nct = t_ctx // tm
    return jnp.where(i < nct, 0, 1 + (i - nct) // (dec_seq // tm))


def _inproj_kernel(x_ref, mod_ref, w_ref, q_ref, k_ref, v_ref, uh_ref, us_ref):
    m = mod_ref[0, 0]
    h = x_ref[...] * (1.0 + m[1:2]) + m[0:1]
    y = _dot(h.astype(BF16), w_ref[...])
    o0 = ATT_W
    o1 = o0 + KV_W
    o2 = o1 + KV_W
    o3 = o2 + 3 * HY_W
    q_ref[...] = y[:, :o0]
    k_ref[...] = y[:, o0:o1]
    v_ref[...] = y[:, o1:o2]
    uh_ref[...] = y[:, o2:o3]
    us_ref[...] = y[:, o3:]


def _in_projection(x, mod, w_in_bf, layer, t_ctx, dec_seq):
    T = x.shape[0]
    tm = 512
    row = lambda i: (layer, _mod_row(i, tm, t_ctx, dec_seq), 0, 0)
    tok = lambda w: pl.BlockSpec((tm, w), lambda i: (i, 0))
    widths = (ATT_W, KV_W, KV_W, 3 * HY_W, 3 * SC_W)
    return pl.pallas_call(
        _inproj_kernel,
        grid=(T // tm,),
        in_specs=[
            tok(D),
            pl.BlockSpec((1, 1, 6, D), row),
            pl.BlockSpec((D, IN_W), lambda i: (0, 0)),
        ],
        out_specs=[tok(w) for w in widths],
        out_shape=[jax.ShapeDtypeStruct((T, w), F32) for w in widths],
        compiler_params=_cp(("arbitrary",)),
        name="in_projection",
    )(x, mod, w_in_bf)


def _swap_halves(x):
    w = x.shape[-1]
    lane = lax.broadcasted_iota(jnp.int32, x.shape, 1)
    return jnp.where((lane % HD) < HD // 2, pltpu.roll(x, w - HD // 2, 1), pltpu.roll(x, HD // 2, 1))


def _rope(x, cos, sin_signed):
    return x * cos + _swap_halves(x) * sin_signed


def _group_rows(q, g):
    return jnp.concatenate([q[:, (GQA * g + r) * HD:(GQA * g + r + 1) * HD] for r in range(GQA)], axis=0)


def _sink_column(sink_ref, g, rows):
    ridx = lax.broadcasted_iota(jnp.int32, (GQA * rows, 1), 0)
    col = jnp.full((GQA * rows, 1), sink_ref[GQA * g + GQA - 1], F32)
    for r in range(GQA - 2, -1, -1):
        col = jnp.where(ridx < (r + 1) * rows, sink_ref[GQA * g + r], col)
    return col


def _ungroup(outs, rows):
    return jnp.concatenate([o[r * rows:(r + 1) * rows] for o in outs for r in range(GQA)], axis=1)


def _ctx_attn_kernel(sink_ref, q_ref, k_ref, v_ref, o_ref):
    rows = q_ref.shape[1]
    q = q_ref[0] * (HD ** -0.5)
    k = k_ref[0]
    v = v_ref[0]
    outs = []
    for g in range(N_KV):
        qg = _group_rows(q, g).astype(BF16)
        kg = k[:, g * HD:(g + 1) * HD].astype(BF16)
        vg = v[:, g * HD:(g + 1) * HD].astype(BF16)
        s = _dot_nt(qg, kg)
        sink = _sink_column(sink_ref, g, rows)
        m = jnp.maximum(jnp.max(s, axis=-1, keepdims=True), sink)
        e = jnp.exp(s - m)
        den = jnp.sum(e, axis=-1, keepdims=True) + jnp.exp(sink - m)
        outs.append(_dot(e.astype(BF16), vg) / den)
    o_ref[0] = _ungroup(outs, rows)


def _context_attention(q, k, v, sink):
    B, L, _ = q.shape
    return pl.pallas_call(
        _ctx_attn_kernel,
        grid=(B,),
        in_specs=[
            pl.BlockSpec(memory_space=pltpu.SMEM),
            pl.BlockSpec((1, L, ATT_W), lambda b: (b, 0, 0)),
            pl.BlockSpec((1, L, KV_W), lambda b: (b, 0, 0)),
            pl.BlockSpec((1, L, KV_W), lambda b: (b, 0, 0)),
        ],
        out_specs=pl.BlockSpec((1, L, ATT_W), lambda b: (b, 0, 0)),
        out_shape=jax.ShapeDtypeStruct((B, L, ATT_W), F32),
        compiler_params=_cp(("arbitrary",)),
        name="context_attention",
    )(sink, q, k, v)


def _lat_attn_kernel(sink_ref, q_ref, k_ref, v_ref, kc_ref, vc_ref, cq_ref, sq_ref, ck_ref, sk_ref, o_ref):
    n = pl.program_id(1)
    L = k_ref.shape[1]
    nk = 3 * BLK
    ws = pl.multiple_of(jnp.clip((n - 1) * BLK, 0, L - nk), BLK)
    q = _rope(q_ref[0], cq_ref[...], sq_ref[...]) * (HD ** -0.5)
    kl = _rope(k_ref[0, pl.ds(ws, nk), :], ck_ref[pl.ds(ws, nk), :], sk_ref[pl.ds(ws, nk), :])
    vl = v_ref[0, pl.ds(ws, nk), :]
    kc = kc_ref[0]
    vc = vc_ref[0]
    qpos = n * BLK + lax.broadcasted_iota(jnp.int32, (GQA * BLK, 1), 0) % BLK
    kpos = ws + lax.broadcasted_iota(jnp.int32, (1, nk), 1)
    valid = jnp.abs(kpos - qpos) <= WINDOW
    outs = []
    for g in range(N_KV):
        sl = slice(g * HD, (g + 1) * HD)
        qg = _group_rows(q, g).astype(BF16)
        s_loc = jnp.where(valid, _dot_nt(qg, kl[:, sl].astype(BF16)), NEG_INF)
        s_ctx = _dot_nt(qg, kc[:, sl].astype(BF16))
        sink = _sink_column(sink_ref, g, BLK)
        m = jnp.maximum(jnp.maximum(jnp.max(s_loc, axis=-1, keepdims=True),
                                    jnp.max(s_ctx, axis=-1, keepdims=True)), sink)
        e_loc = jnp.exp(s_loc - m)
        e_ctx = jnp.exp(s_ctx - m)
        den = (jnp.sum(e_loc, axis=-1, keepdims=True) + jnp.sum(e_ctx, axis=-1, keepdims=True)
               + jnp.exp(sink - m))
        o = _dot(e_loc.astype(BF16), vl[:, sl].astype(BF16)) + _dot(e_ctx.astype(BF16), vc[:, sl].astype(BF16))
        outs.append(o / den)
    o_ref[0] = _ungroup(outs, BLK)


def _rope_tables(L):
    rows = L // GRID_W
    row = jnp.repeat(jnp.arange(rows, dtype=F32), GRID_W)
    col = jnp.tile(jnp.arange(GRID_W, dtype=F32), rows)
    pairs = HD // 4
    inv = ROPE_BASE ** (-jnp.arange(pairs, dtype=F32) / pairs)
    ang = jnp.concatenate([row[:, None] * inv, col[:, None] * inv], axis=-1)
    cos = jnp.cos(ang)
    sin = jnp.sin(ang)
    return jnp.concatenate([cos, cos], axis=-1), jnp.concatenate([-sin, sin], axis=-1)


def _latent_attention(q, k, v, kc, vc, sink, tables):
    B, L, _ = q.shape
    P = kc.shape[1]
    cq, sq, ck, sk = tables
    return pl.pallas_call(
        _lat_attn_kernel,
        grid=(B, L // BLK),
        in_specs=[
            pl.BlockSpec(memory_space=pltpu.SMEM),
            pl.BlockSpec((1, BLK, ATT_W), lambda b, n: (b, n, 0)),
            pl.BlockSpec((1, L, KV_W), lambda b, n: (b, 0, 0)),
            pl.BlockSpec((1, L, KV_W), lambda b, n: (b, 0, 0)),
            pl.BlockSpec((1, P, KV_W), lambda b, n: (b, 0, 0)),
            pl.BlockSpec((1, P, KV_W), lambda b, n: (b, 0, 0)),
            pl.BlockSpec((BLK, ATT_W), lambda b, n: (n, 0)),
            pl.BlockSpec((BLK, ATT_W), lambda b, n: (n, 0)),
            pl.BlockSpec((L, KV_W), lambda b, n: (0, 0)),
            pl.BlockSpec((L, KV_W), lambda b, n: (0, 0)),
        ],
        out_specs=pl.BlockSpec((1, BLK, ATT_W), lambda b, n: (b, n, 0)),
        out_shape=jax.ShapeDtypeStruct((B, L, ATT_W), F32),
        compiler_params=_cp(("arbitrary", "arbitrary")),
        name="latent_attention",
    )(sink, q, k, v, kc, vc, cq, sq, ck, sk)


def _conv3(u, prev_row, next_row, w):
    tl = u.shape[0]
    ridx = lax.broadcasted_iota(jnp.int32, (tl, 1), 0)
    dn = jnp.where(ridx == 0, prev_row, pltpu.roll(u, 1, 0))
    up = jnp.where(ridx == tl - 1, next_row, pltpu.roll(u, tl - 1, 0))
    return dn * w[0:1] + u * w[1:2] + up * w[2:3]


def _mixprep_kernel(uh_ref, uhp_ref, uhn_ref, us_ref, usp_ref, usn_ref, hw_ref, hb_ref, sw_ref,
                    x0_ref, g_ref, osc_ref):
    t = pl.program_id(1)
    first = t == 0
    last = t == pl.num_programs(1) - 1
    c = HY_W
    hw = hw_ref[...]
    uh = _conv3(uh_ref[0], jnp.where(first, 0.0, uhp_ref[0, 7:8]), jnp.where(last, 0.0, uhn_ref[0, 0:1]), hw)
    uh = uh + hb_ref[...]
    x0_ref[0] = uh[:, :c]
    g_ref[0] = uh[:, c:2 * c] * uh[:, 2 * c:]
    us = us_ref[0]
    usp = usp_ref[0, 7:8]
    usn = usn_ref[0, 0:1]
    prod = us[:, c:2 * c] * us[:, 2 * c:]
    pprev = jnp.where(first, 0.0, usp[:, c:2 * c] * usp[:, 2 * c:])
    pnext = jnp.where(last, 0.0, usn[:, c:2 * c] * usn[:, 2 * c:])
    osc_ref[0] = us[:, :c] * _conv3(prod, pprev, pnext, sw_ref[...])


def _mixer_prep(u_hy, u_sc, hy_conv_w, hy_conv_b, sc_conv_w):
    B, L, W = u_hy.shape
    tl = min(L, 512)
    r8 = tl // 8
    main = pl.BlockSpec((1, tl, W), lambda b, t: (b, t, 0))
    prev = pl.BlockSpec((1, 8, W), lambda b, t: (b, jnp.maximum(t * r8 - 1, 0), 0))
    nxt = pl.BlockSpec((1, 8, W), lambda b, t: (b, jnp.minimum((t + 1) * r8, L // 8 - 1), 0))
    const = lambda shape: pl.BlockSpec(shape, lambda b, t: (0, 0))
    out = pl.BlockSpec((1, tl, HY_W), lambda b, t: (b, t, 0))
    return pl.pallas_call(
        _mixprep_kernel,
        grid=(B, L // tl),
        in_specs=[main, prev, nxt, main, prev, nxt, const((3, W)), const((1, W)), const((3, SC_W))],
        out_specs=[out, out, out],
        out_shape=[jax.ShapeDtypeStruct((B, L, HY_W), F32)] * 3,
        compiler_params=_cp(("arbitrary", "arbitrary")),
        name="mixer_prep",
    )(u_hy, u_hy, u_hy, u_sc, u_sc, u_sc, hy_conv_w, hy_conv_b.reshape(1, W), sc_conv_w)


def _filter_kernel(z_ref, w1_ref, b1_ref, f1_ref, w2_ref, b2_ref, f2_ref, w3_ref, dl_ref, hf_ref, hb_ref):
    tl = z_ref.shape[0]
    z = z_ref[...]
    h = jnp.sin(f1_ref[...] * (_dot(z, w1_ref[...], HI) + b1_ref[...]))
    h = jnp.sin(f2_ref[...] * (_dot(h, w2_ref[...], HI) + b2_ref[...]))
    h = _dot(h, w3_ref[...], HI) * jnp.exp(-z[:, 0:1] * jnp.abs(dl_ref[...]))
    row = pl.program_id(0) * tl + lax.broadcasted_iota(jnp.int32, (tl, 1), 0)
    hf_ref[0] = h[:, :HY_W]
    hb_ref[0] = jnp.where(row == 0, 0.0, h[:, HY_W:])


def _filter_features(L):
    t = jnp.linspace(0.0, 1.0, L, dtype=F32)[:, None]
    w = 2.0 * math.pi * jnp.arange(L, dtype=F32)[:, None] / L
    bands = jnp.linspace(1e-4, FILTER_BANDS - 1, FILTER_BANDS, dtype=F32)[None, :]
    z = jnp.concatenate([t, jnp.cos(bands * w), -jnp.sin(bands * w)], axis=-1)
    return jnp.pad(z, ((0, 0), (0, 128 - FILTER_EMB)))


def _hyena_filters(L, w1, b1, f1, w2, b2, f2, w3, deltas):
    tl = min(L, 512)
    z = _filter_features(L)
    w1p = jnp.pad(w1, ((0, 128 - FILTER_EMB), (0, 0)))
    const = lambda shape: pl.BlockSpec(shape, lambda i: (0, 0))
    H = FILTER_HIDDEN
    hf, hb = pl.pallas_call(
        _filter_kernel,
        grid=(L // tl,),
        in_specs=[pl.BlockSpec((tl, 128), lambda i: (i, 0)), const((128, H)), const((1, H)), const((1, H)),
                  const((H, H)), const((1, H)), const((1, H)), const((H, 2 * HY_W)), const((1, 2 * HY_W))],
        out_specs=[pl.BlockSpec((1, tl, HY_W), lambda i: (0, i, 0)), pl.BlockSpec((1, tl, HY_W), lambda i: (0, i, 0))],
        out_shape=[jax.ShapeDtypeStruct((1, L, HY_W), F32)] * 2,
        compiler_params=_cp(("arbitrary",)),
        name="hyena_filters",
    )(z, w1p, b1.reshape(1, H), f1.reshape(1, H), w2, b2.reshape(1, H), f2.reshape(1, H), w3,
      deltas.reshape(1, 2 * HY_W))
    return jnp.concatenate([hf, hb], axis=0)


def _fft_split(L):
    n = 2 * L
    n1 = 128 if n >= 8192 else 16
    return n1, n // n1


def _fft_constants(L):
    n1, n2 = _fft_split(L)
    n = n1 * n2
    k1 = jnp.arange(n1, dtype=jnp.int32)
    a = (k1[:, None] * jnp.arange(n1 // 2, dtype=jnp.int32)[None, :]) % n1
    th = (2.0 * math.pi / n1) * a.astype(F32)
    f1 = jnp.concatenate([jnp.cos(th), -jnp.sin(th)], axis=0)
    f3 = jnp.concatenate([jnp.cos(th).T, -jnp.sin(th).T], axis=1) * (1.0 / n)
    k = k1[:, None, None] + n1 * jnp.arange(n2, dtype=jnp.int32)[None, :, None]
    ph = (k * jnp.arange(n2, dtype=jnp.int32)[None, None, :]) % n
    ph = (2.0 * math.pi / n) * ph.astype(F32)
    mr, mi = jnp.cos(ph), -jnp.sin(ph)
    mf = jnp.concatenate([jnp.concatenate([mr, -mi], axis=2), jnp.concatenate([mi, mr], axis=2)], axis=1)
    mrt, mit = jnp.swapaxes(mr, 1, 2), jnp.swapaxes(mi, 1, 2)
    mb = jnp.concatenate([jnp.concatenate([mrt, mit], axis=2), jnp.concatenate([-mit, mrt], axis=2)], axis=1)
    return f1, f3, mf, mb


def _dft_rows_kernel(f_ref, x_ref, o_ref):
    o_ref[0] = _dot(f_ref[...], x_ref[0], HI)


def _dft_rows(f1, x):
    B, kh, W = x.shape
    m = f1.shape[0]
    tn = min(W, 2048)
    return pl.pallas_call(
        _dft_rows_kernel,
        grid=(B, W // tn),
        in_specs=[pl.BlockSpec((m, kh), lambda b, j: (0, 0)), pl.BlockSpec((1, kh, tn), lambda b, j: (b, 0, j))],
        out_specs=pl.BlockSpec((1, m, tn), lambda b, j: (b, 0, j)),
        out_shape=jax.ShapeDtypeStruct((B, m, W), F32),
        compiler_params=_cp(("arbitrary", "arbitrary")),
        name="fft_rows",
    )(f1, x)


def _fft_mid_kernel(a_ref, hf_ref, hb_ref, mf_ref, mb_ref, o_ref, kf_ref):
    n2 = a_ref.shape[3]
    m = mf_ref[0]

    @pl.when(pl.program_id(1) == 0)
    def _():
        xf = _dot(m, jnp.concatenate([hf_ref[0, 0, 0], hf_ref[0, 1, 0]], axis=0), HI)
        xb = _dot(m, jnp.concatenate([hb_ref[0, 0, 0], hb_ref[0, 1, 0]], axis=0), HI)
        kf_ref[0] = xf[:n2] + xb[:n2]
        kf_ref[1] = xf[n2:] - xb[n2:]

    x = _dot(m, jnp.concatenate([a_ref[0, 0, 0], a_ref[0, 1, 0]], axis=0), HI)
    xr, xi = x[:n2], x[n2:]
    kr, ki = kf_ref[0], kf_ref[1]
    y = jnp.concatenate([xr * kr - xi * ki, xr * ki + xi * kr], axis=0)
    o = _dot(mb_ref[0], y, HI)
    o_ref[0, 0, 0] = o[:n2]
    o_ref[0, 1, 0] = o[n2:]


def _fft_mid(a, ah, mf, mb):
    B, _, n1, n2, C = a.shape
    blk = (1, 2, 1, n2, C)
    return pl.pallas_call(
        _fft_mid_kernel,
        grid=(n1, B),
        in_specs=[
            pl.BlockSpec(blk, lambda k, b: (b, 0, k, 0, 0)),
            pl.BlockSpec(blk, lambda k, b: (0, 0, k, 0, 0)),
            pl.BlockSpec(blk, lambda k, b: (1, 0, k, 0, 0)),
            pl.BlockSpec((1, 2 * n2, 2 * n2), lambda k, b: (k, 0, 0)),
            pl.BlockSpec((1, 2 * n2, 2 * n2), lambda k, b: (k, 0, 0)),
        ],
        out_specs=pl.BlockSpec(blk, lambda k, b: (b, 0, k, 0, 0)),
        out_shape=jax.ShapeDtypeStruct(a.shape, F32),
        scratch_shapes=[pltpu.VMEM((2, n2, C), F32)],
        compiler_params=_cp(("arbitrary", "arbitrary")),
        name="fft_mid",
    )(a, ah, ah, mf, mb)


def _idft_rows_kernel(f_ref, b_ref, x0_ref, g_ref, bias_ref, o_ref):
    y = _dot(f_ref[...], b_ref[0], HI)
    o_ref[0] = x0_ref[0] * (y + g_ref[0] * bias_ref[...])


def _idft_rows_gate(f3, bo, x0, g, bias_row):
    B, m2, W = bo.shape
    kh = f3.shape[0]
    tn = min(W, 2048)
    tile = pl.BlockSpec((1, kh, tn), lambda b, j: (b, 0, j))
    return pl.pallas_call(
        _idft_rows_kernel,
        grid=(B, W // tn),
        in_specs=[pl.BlockSpec((kh, m2), lambda b, j: (0, 0)), pl.BlockSpec((1, m2, tn), lambda b, j: (b, 0, j)),
                  tile, tile, pl.BlockSpec((1, tn), lambda b, j: (0, j))],
        out_specs=tile,
        out_shape=jax.ShapeDtypeStruct((B, kh, W), F32),
        compiler_params=_cp(("arbitrary", "arbitrary")),
        name="ifft_rows_gate",
    )(f3, bo, x0, g, bias_row)


def _hyena_long_conv(x0, g, filt, hy_bias, consts):
    B, L, C = g.shape
    n1, n2 = _fft_split(L)
    f1, f3, mf, mb = consts
    wide = lambda a: a.reshape(a.shape[0], n1 // 2, n2 * C)
    a = _dft_rows(f1, wide(g)).reshape(B, 2, n1, n2, C)
    ah = _dft_rows(f1, wide(filt)).reshape(2, 2, n1, n2, C)
    bo = _fft_mid(a, ah, mf, mb).reshape(B, 2 * n1, n2 * C)
    out = _idft_rows_gate(f3, bo, wide(x0), wide(g), jnp.tile(hy_bias, n2).reshape(1, n2 * C))
    return out.reshape(B, L, C)


def _outproj_kernel(oa_ref, oh_ref, os_ref, x_ref, mod_ref, w_ref, lg_ref, lb_ref, rw_ref, rb_ref,
                    x1_ref, h2_ref, gates_ref):
    m = mod_ref[0, 0]
    w = w_ref[...]
    o = (_dot(oa_ref[...].astype(BF16), w[:ATT_W]) + _dot(oh_ref[...].astype(BF16), w[ATT_W:ATT_W + HY_W])
         + _dot(os_ref[...].astype(BF16), w[ATT_W + HY_W:]))
    x1 = _layer_norm(DN_ALPHA * x_ref[...] + m[2:3] * o, lg_ref[...], lb_ref[...])
    x1_ref[...] = x1
    h2 = x1 * (1.0 + m[4:5]) + m[3:4]
    h2_ref[...] = h2.astype(BF16)
    lane = lax.broadcasted_iota(jnp.int32, (h2.shape[0], 128), 1)
    logits = jnp.where(lane < N_EXP, _dot(h2, rw_ref[...], HI) + rb_ref[...], NEG_INF)
    picks = []
    vals = []
    for _ in range(TOP_K):
        v = jnp.max(logits, axis=-1, keepdims=True)
        idx = jnp.min(jnp.where(logits == v, lane, 128), axis=-1, keepdims=True)
        hit = lane == idx
        picks.append(hit)
        vals.append(v)
        logits = jnp.where(hit, NEG_INF, logits)
    es = [jnp.exp(v - vals[0]) for v in vals]
    den = es[0] + es[1] + es[2] + es[3]
    gates = jnp.zeros(logits.shape, F32)
    for hit, e in zip(picks, es):
        gates = jnp.where(hit, e / den, gates)
    gates_ref[...] = gates


def _out_projection(o_attn, o_hy, o_sc, x, mod, w_out_bf, ln_g, ln_b, router_w, router_b, layer, t_ctx, dec_seq):
    T = x.shape[0]
    tm = 256
    row = lambda i: (layer, _mod_row(i, tm, t_ctx, dec_seq), 0, 0)
    tok = lambda w: pl.BlockSpec((tm, w), lambda i: (i, 0))
    const = lambda shape: pl.BlockSpec(shape, lambda i: (0, 0))
    rw = jnp.pad(router_w, ((0, 0), (0, 128 - N_EXP)))
    rb = jnp.pad(router_b, (0, 128 - N_EXP)).reshape(1, 128)
    return pl.pallas_call(
        _outproj_kernel,
        grid=(T // tm,),
        in_specs=[tok(ATT_W), tok(HY_W), tok(SC_W), tok(D), pl.BlockSpec((1, 1, 6, D), row),
                  const((D, D)), const((1, D)), const((1, D)), const((D, 128)), const((1, 128))],
        out_specs=[tok(D), tok(D), tok(128)],
        out_shape=[jax.ShapeDtypeStruct((T, D), F32), jax.ShapeDtypeStruct((T, D), BF16),
                   jax.ShapeDtypeStruct((T, 128), F32)],
        compiler_params=_cp(("arbitrary",)),
        name="out_projection_router",
    )(o_attn, o_hy, o_sc, x, mod, w_out_bf, ln_g.reshape(1, D), ln_b.reshape(1, D), rw, rb)


def _moe_kernel(h_ref, gates_ref, x1_ref, mod_ref, wg_ref, bg_ref, wu_ref, bu_ref, wd_ref, bd_ref,
                lg_ref, lb_ref, o_ref, acc_ref):
    e = pl.program_id(1)

    @pl.when(e == 0)
    def _():
        acc_ref[...] = jnp.zeros_like(acc_ref)

    h = h_ref[...]
    g = jnp.minimum(_dot(h, wg_ref[0, 0].astype(BF16)) + bg_ref[0], SWIGLU_LIMIT)
    u = jnp.clip(_dot(h, wu_ref[0, 0].astype(BF16)) + bu_ref[0], -SWIGLU_LIMIT, SWIGLU_LIMIT)
    a = (u + 1.0) * (g * jax.nn.sigmoid(SWIGLU_ALPHA * g))
    y = _dot(a.astype(BF16), wd_ref[0, 0].astype(BF16)) + bd_ref[0]
    lane = lax.broadcasted_iota(jnp.int32, (1, 128), 1)
    ge = jnp.sum(jnp.where(lane == e, gates_ref[...], 0.0), axis=-1, keepdims=True)
    acc_ref[...] += ge * y

    @pl.when(e == N_EXP - 1)
    def _():
        m = mod_ref[0, 0]
        o_ref[...] = _layer_norm(DN_ALPHA * x1_ref[...] + m[5:6] * acc_ref[...], lg_ref[...], lb_ref[...])


def _moe_ln2(h2, gates, x1, mod, w_gate, b_gate, w_up, b_up, w_down, b_down, ln_g, ln_b, layer, t_ctx, dec_seq):
    T = x1.shape[0]
    tm = 512
    row = lambda i, e: (layer, _mod_row(i, tm, t_ctx, dec_seq), 0, 0)
    tok = lambda w: pl.BlockSpec((tm, w), lambda i, e: (i, 0))
    wspec = pl.BlockSpec((1, 1, D, D), lambda i, e: (layer, e, 0, 0))
    bspec = pl.BlockSpec((1, 1, D), lambda i, e: (e, 0, 0))
    const = lambda shape: pl.BlockSpec(shape, lambda i, e: (0, 0))
    return pl.pallas_call(
        _moe_kernel,
        grid=(T // tm, N_EXP),
        in_specs=[tok(D), tok(128), tok(D), pl.BlockSpec((1, 1, 6, D), row),
                  wspec, bspec, wspec, bspec, wspec, bspec, const((1, D)), const((1, D))],
        out_specs=tok(D),
        out_shape=jax.ShapeDtypeStruct((T, D), F32),
        scratch_shapes=[pltpu.VMEM((tm, D), F32)],
        compiler_params=_cp(("arbitrary", "arbitrary")),
        name="moe_ln2",
    )(h2, gates, x1, mod, w_gate, b_gate[layer].reshape(N_EXP, 1, D), w_up, b_up[layer].reshape(N_EXP, 1, D),
      w_down, b_down[layer].reshape(N_EXP, 1, D), ln_g.reshape(1, D), ln_b.reshape(1, D))


def kernel(x_prompt, x_sample, cache_k, cache_v, c, c_ctx, w_mod, b_mod, w_in, w_out, attn_sink, hy_conv_w, hy_conv_b, hy_w1, hy_b1, hy_f1, hy_w2, hy_b2, hy_f2, hy_w3, hy_deltas, hy_bias, sc_conv_w, ln1_g, ln1_b, router_w, router_b, w_gate, b_gate, w_up, b_up, w_down, b_down, ln2_g, ln2_b):
    nb, seq, _ = x_prompt.shape
    db, dseq, _ = x_sample.shape
    past = cache_k.shape[2]
    t_ctx = nb * seq
    x = jnp.concatenate([x_prompt.reshape(t_ctx, D), x_sample.reshape(db * dseq, D)], axis=0)

    cond = jnp.zeros((8, D), F32).at[0].set(c_ctx).at[1:1 + db].set(c)
    mod = _modulation(cond, w_mod, b_mod)

    cos64, sin64 = _rope_tables(dseq)
    tables = (jnp.tile(cos64, (1, N_HEADS)), jnp.tile(sin64, (1, N_HEADS)),
              jnp.tile(cos64, (1, N_KV)), jnp.tile(sin64, (1, N_KV)))
    fft_c = {L: _fft_constants(L) for L in (seq, dseq)}
    w_in_bf = w_in.astype(BF16)
    w_out_bf = w_out.astype(BF16)

    new_k, new_v = [], []
    for l in range(DEPTH):
        q, k, v, u_hy, u_sc = _in_projection(x, mod, w_in_bf[l], l, t_ctx, dseq)
        k_ctx = k[:t_ctx].reshape(nb, seq, KV_W)
        v_ctx = v[:t_ctx].reshape(nb, seq, KV_W)
        new_k.append(k_ctx.reshape(nb, seq, N_KV, HD))
        new_v.append(v_ctx.reshape(nb, seq, N_KV, HD))
        oa_ctx = _context_attention(q[:t_ctx].reshape(nb, seq, ATT_W), k_ctx, v_ctx, attn_sink[l])
        oa_lat = _latent_attention(q[t_ctx:].reshape(db, dseq, ATT_W), k[t_ctx:].reshape(db, dseq, KV_W),
                                   v[t_ctx:].reshape(db, dseq, KV_W), cache_k[:, l].reshape(db, past, KV_W),
                                   cache_v[:, l].reshape(db, past, KV_W), attn_sink[l], tables)
        o_hy, o_sc = [], []
        for (B, L, lo) in ((nb, seq, 0), (db, dseq, t_ctx)):
            uh = u_hy[lo:lo + B * L].reshape(B, L, 3 * HY_W)
            us = u_sc[lo:lo + B * L].reshape(B, L, 3 * SC_W)
            x0, g, osc = _mixer_prep(uh, us, hy_conv_w[l], hy_conv_b[l], sc_conv_w[l])
            filt = _hyena_filters(L, hy_w1[l], hy_b1[l], hy_f1[l], hy_w2[l], hy_b2[l], hy_f2[l], hy_w3[l],
                                  hy_deltas[l])
            ohy = _hyena_long_conv(x0, g, filt, hy_bias[l], fft_c[L])
            o_hy.append(ohy.reshape(B * L, HY_W))
            o_sc.append(osc.reshape(B * L, SC_W))
        o_attn = jnp.concatenate([oa_ctx.reshape(t_ctx, ATT_W), oa_lat.reshape(db * dseq, ATT_W)], axis=0)
        o_hy = jnp.concatenate(o_hy, axis=0)
        o_sc = jnp.concatenate(o_sc, axis=0)
        x1, h2, gates = _out_projection(o_attn, o_hy, o_sc, x, mod, w_out_bf[l], ln1_g[l], ln1_b[l],
                                        router_w[l], router_b[l], l, t_ctx, dseq)
        x = _moe_ln2(h2, gates, x1, mod, w_gate, b_gate, w_up, b_up, w_down, b_down, ln2_g[l], ln2_b[l],
                     l, t_ctx, dseq)

    y_prompt = x[:t_ctx].reshape(nb, seq, D)
    y_sample = x[t_ctx:].reshape(db, dseq, D)
    return (y_prompt, y_sample, jnp.stack(new_k, axis=1), jnp.stack(new_v, axis=1))
```

```python
import functools
import math

import jax
import jax.numpy as jnp
from jax import lax
from jax.experimental import pallas as pl
from jax.experimental.pallas import tpu as pltpu

F32 = jnp.float32
BF16 = jnp.bfloat16
HI = lax.Precision.HIGHEST

D = 1024
DEPTH = 2
N_HEADS = 8
N_KV = 2
HD = 64
GQA = N_HEADS // N_KV
ATT_W = N_HEADS * HD
KV_W = N_KV * HD
WINDOW = 128
BLK = 128
GRID_W = 64
ROPE_BASE = 10000.0
HY_W = 256
SC_W = 256
FILTER_EMB = 33
FILTER_BANDS = 16
FILTER_HIDDEN = 64
IN_W = ATT_W + 2 * KV_W + 3 * HY_W + 3 * SC_W
N_EXP = 32
TOP_K = 4
SWIGLU_LIMIT = 7.0
SWIGLU_ALPHA = 1.702
DN_ALPHA = (2 * DEPTH) ** 0.25
LN_EPS = 1e-5
NEG_INF = float("-inf")

VMEM_LIMIT = 56 * 1024 * 1024

TOKEN_BLOCK = 256
ROW_PAD = 8
RUN_PIECES = (256, 128, 64, 32, 16, 8)
EXPERT_TILE = 512
LOCAL_SLOTS = TOP_K * TOKEN_BLOCK + N_EXP * ROW_PAD


def _cp(sem):
    return pltpu.CompilerParams(dimension_semantics=sem, vmem_limit_bytes=VMEM_LIMIT)


def _dot(a, b, precision=None):
    return jnp.dot(a, b, preferred_element_type=F32, precision=precision)


def _dot_nt(a, b):
    return lax.dot_general(a, b, (((1,), (1,)), ((), ())), preferred_element_type=F32)


def _layer_norm(z, g, b):
    mu = jnp.mean(z, axis=-1, keepdims=True)
    zc = z - mu
    var = jnp.mean(zc * zc, axis=-1, keepdims=True)
    return zc * lax.rsqrt(var + LN_EPS) * g + b


def _mod_kernel(cond_ref, w_ref, b_ref, o_ref):
    c = cond_ref[...]
    s = c * jax.nn.sigmoid(c)
    o_ref[0] = _dot(s, w_ref[0], HI) + b_ref[0]


def _modulation(cond, w_mod, b_mod):
    tn = 1024
    out = pl.pallas_call(
        _mod_kernel,
        grid=(DEPTH, 6 * D // tn),
        in_specs=[
            pl.BlockSpec((8, D), lambda l, j: (0, 0)),
            pl.BlockSpec((1, D, tn), lambda l, j: (l, 0, j)),
            pl.BlockSpec((1, 1, tn), lambda l, j: (l, 0, j)),
        ],
        out_specs=pl.BlockSpec((1, 8, tn), lambda l, j: (l, 0, j)),
        out_shape=jax.ShapeDtypeStruct((DEPTH, 8, 6 * D), F32),
        compiler_params=_cp(("arbitrary", "arbitrary")),
        name="modulation",
    )(cond, w_mod, b_mod.reshape(DEPTH, 1, 6 * D))
    return out.reshape(DEPTH, 8, 6, D)


def _mod_row(i, tm, t_ctx, dec_seq):
    nct = t_ctx // tm
    return jnp.where(i < nct, 0, 1 + (i - nct) // (dec_seq // tm))


def _inproj_kernel(x_ref, mod_ref, w_ref, q_ref, k_ref, v_ref, uh_ref, us_ref):
    m = mod_ref[0, 0]
    h = x_ref[...] * (1.0 + m[1:2]) + m[0:1]
    y = _dot(h.astype(BF16), w_ref[...])
    o0 = ATT_W
    o1 = o0 + KV_W
    o2 = o1 + KV_W
    o3 = o2 + 3 * HY_W
    q_ref[...] = y[:, :o0]
    k_ref[...] = y[:, o0:o1]
    v_ref[...] = y[:, o1:o2]
    uh_ref[...] = y[:, o2:o3]
    us_ref[...] = y[:, o3:]


def _in_projection(x, mod, w_in_bf, layer, t_ctx, dec_seq):
    T = x.shape[0]
    tm = 512
    row = lambda i: (layer, _mod_row(i, tm, t_ctx, dec_seq), 0, 0)
    tok = lambda w: pl.BlockSpec((tm, w), lambda i: (i, 0))
    widths = (ATT_W, KV_W, KV_W, 3 * HY_W, 3 * SC_W)
    return pl.pallas_call(
        _inproj_kernel,
        grid=(T // tm,),
        in_specs=[
            tok(D),
            pl.BlockSpec((1, 1, 6, D), row),
            pl.BlockSpec((D, IN_W), lambda i: (0, 0)),
        ],
        out_specs=[tok(w) for w in widths],
        out_shape=[jax.ShapeDtypeStruct((T, w), F32) for w in widths],
        compiler_params=_cp(("arbitrary",)),
        name="in_projection",
    )(x, mod, w_in_bf)


def _swap_halves(x):
    w = x.shape[-1]
    lane = lax.broadcasted_iota(jnp.int32, x.shape, 1)
    return jnp.where((lane % HD) < HD // 2, pltpu.roll(x, w - HD // 2, 1), pltpu.roll(x, HD // 2, 1))


def _rope(x, cos, sin_signed):
    return x * cos + _swap_halves(x) * sin_signed


def _group_rows(q, g):
    return jnp.concatenate([q[:, (GQA * g + r) * HD:(GQA * g + r + 1) * HD] for r in range(GQA)], axis=0)


def _sink_column(sink_ref, g, rows):
    ridx = lax.broadcasted_iota(jnp.int32, (GQA * rows, 1), 0)
    col = jnp.full((GQA * rows, 1), sink_ref[GQA * g + GQA - 1], F32)
    for r in range(GQA - 2, -1, -1):
        col = jnp.where(ridx < (r + 1) * rows, sink_ref[GQA * g + r], col)
    return col


def _ungroup(outs, rows):
    return jnp.concatenate([o[r * rows:(r + 1) * rows] for o in outs for r in range(GQA)], axis=1)


def _ctx_attn_kernel(sink_ref, q_ref, k_ref, v_ref, o_ref):
    rows = q_ref.shape[1]
    q = q_ref[0] * (HD ** -0.5)
    k = k_ref[0]
    v = v_ref[0]
    outs = []
    for g in range(N_KV):
        qg = _group_rows(q, g).astype(BF16)
        kg = k[:, g * HD:(g + 1) * HD].astype(BF16)
        vg = v[:, g * HD:(g + 1) * HD].astype(BF16)
        s = _dot_nt(qg, kg)
        sink = _sink_column(sink_ref, g, rows)
        m = jnp.maximum(jnp.max(s, axis=-1, keepdims=True), sink)
        e = jnp.exp(s - m)
        den = jnp.sum(e, axis=-1, keepdims=True) + jnp.exp(sink - m)
        outs.append(_dot(e.astype(BF16), vg) / den)
    o_ref[0] = _ungroup(outs, rows)


def _context_attention(q, k, v, sink):
    B, L, _ = q.shape
    return pl.pallas_call(
        _ctx_attn_kernel,
        grid=(B,),
        in_specs=[
            pl.BlockSpec(memory_space=pltpu.SMEM),
            pl.BlockSpec((1, L, ATT_W), lambda b: (b, 0, 0)),
            pl.BlockSpec((1, L, KV_W), lambda b: (b, 0, 0)),
            pl.BlockSpec((1, L, KV_W), lambda b: (b, 0, 0)),
        ],
        out_specs=pl.BlockSpec((1, L, ATT_W), lambda b: (b, 0, 0)),
        out_shape=jax.ShapeDtypeStruct((B, L, ATT_W), F32),
        compiler_params=_cp(("arbitrary",)),
        name="context_attention",
    )(sink, q, k, v)


def _lat_attn_kernel(sink_ref, q_ref, k_ref, v_ref, kc_ref, vc_ref, cq_ref, sq_ref, ck_ref, sk_ref, o_ref):
    n = pl.program_id(1)
    L = k_ref.shape[1]
    nk = 3 * BLK
    ws = pl.multiple_of(jnp.clip((n - 1) * BLK, 0, L - nk), BLK)
    q = _rope(q_ref[0], cq_ref[...], sq_ref[...]) * (HD ** -0.5)
    kl = _rope(k_ref[0, pl.ds(ws, nk), :], ck_ref[pl.ds(ws, nk), :], sk_ref[pl.ds(ws, nk), :])
    vl = v_ref[0, pl.ds(ws, nk), :]
    kc = kc_ref[0]
    vc = vc_ref[0]
    qpos = n * BLK + lax.broadcasted_iota(jnp.int32, (GQA * BLK, 1), 0) % BLK
    kpos = ws + lax.broadcasted_iota(jnp.int32, (1, nk), 1)
    valid = jnp.abs(kpos - qpos) <= WINDOW
    outs = []
    for g in range(N_KV):
        sl = slice(g * HD, (g + 1) * HD)
        qg = _group_rows(q, g).astype(BF16)
        s_loc = jnp.where(valid, _dot_nt(qg, kl[:, sl].astype(BF16)), NEG_INF)
        s_ctx = _dot_nt(qg, kc[:, sl].astype(BF16))
        sink = _sink_column(sink_ref, g, BLK)
        m = jnp.maximum(jnp.maximum(jnp.max(s_loc, axis=-1, keepdims=True),
                                    jnp.max(s_ctx, axis=-1, keepdims=True)), sink)
        e_loc = jnp.exp(s_loc - m)
        e_ctx = jnp.exp(s_ctx - m)
        den = (jnp.sum(e_loc, axis=-1, keepdims=True) + jnp.sum(e_ctx, axis=-1, keepdims=True)
               + jnp.exp(sink - m))
        o = _dot(e_loc.astype(BF16), vl[:, sl].astype(BF16)) + _dot(e_ctx.astype(BF16), vc[:, sl].astype(BF16))
        outs.append(o / den)
    o_ref[0] = _ungroup(outs, BLK)


def _rope_tables(L):
    rows = L // GRID_W
    row = jnp.repeat(jnp.arange(rows, dtype=F32), GRID_W)
    col = jnp.tile(jnp.arange(GRID_W, dtype=F32), rows)
    pairs = HD // 4
    inv = ROPE_BASE ** (-jnp.arange(pairs, dtype=F32) / pairs)
    ang = jnp.concatenate([row[:, None] * inv, col[:, None] * inv], axis=-1)
    cos = jnp.cos(ang)
    sin = jnp.sin(ang)
    return jnp.concatenate([cos, cos], axis=-1), jnp.concatenate([-sin, sin], axis=-1)


def _latent_attention(q, k, v, kc, vc, sink, tables):
    B, L, _ = q.shape
    P = kc.shape[1]
    cq, sq, ck, sk = tables
    return pl.pallas_call(
        _lat_attn_kernel,
        grid=(B, L // BLK),
        in_specs=[
            pl.BlockSpec(memory_space=pltpu.SMEM),
            pl.BlockSpec((1, BLK, ATT_W), lambda b, n: (b, n, 0)),
            pl.BlockSpec((1, L, KV_W), lambda b, n: (b, 0, 0)),
            pl.BlockSpec((1, L, KV_W), lambda b, n: (b, 0, 0)),
            pl.BlockSpec((1, P, KV_W), lambda b, n: (b, 0, 0)),
            pl.BlockSpec((1, P, KV_W), lambda b, n: (b, 0, 0)),
            pl.BlockSpec((BLK, ATT_W), lambda b, n: (n, 0)),
            pl.BlockSpec((BLK, ATT_W), lambda b, n: (n, 0)),
            pl.BlockSpec((L, KV_W), lambda b, n: (0, 0)),
            pl.BlockSpec((L, KV_W), lambda b, n: (0, 0)),
        ],
        out_specs=pl.BlockSpec((1, BLK, ATT_W), lambda b, n: (b, n, 0)),
        out_shape=jax.ShapeDtypeStruct((B, L, ATT_W), F32),
        compiler_params=_cp(("arbitrary", "arbitrary")),
        name="latent_attention",
    )(sink, q, k, v, kc, vc, cq, sq, ck, sk)


def _conv3(u, prev_row, next_row, w):
    tl = u.shape[0]
    ridx = lax.broadcasted_iota(jnp.int32, (tl, 1), 0)
    dn = jnp.where(ridx == 0, prev_row, pltpu.roll(u, 1, 0))
    up = jnp.where(ridx == tl - 1, next_row, pltpu.roll(u, tl - 1, 0))
    return dn * w[0:1] + u * w[1:2] + up * w[2:3]


def _mixprep_kernel(uh_ref, uhp_ref, uhn_ref, us_ref, usp_ref, usn_ref, hw_ref, hb_ref, sw_ref,
                    x0_ref, g_ref, osc_ref):
    t = pl.program_id(1)
    first = t == 0
    last = t == pl.num_programs(1) - 1
    c = HY_W
    hw = hw_ref[...]
    uh = _conv3(uh_ref[0], jnp.where(first, 0.0, uhp_ref[0, 7:8]), jnp.where(last, 0.0, uhn_ref[0, 0:1]), hw)
    uh = uh + hb_ref[...]
    x0_ref[0] = uh[:, :c]
    g_ref[0] = uh[:, c:2 * c] * uh[:, 2 * c:]
    us = us_ref[0]
    usp = usp_ref[0, 7:8]
    usn = usn_ref[0, 0:1]
    prod = us[:, c:2 * c] * us[:, 2 * c:]
    pprev = jnp.where(first, 0.0, usp[:, c:2 * c] * usp[:, 2 * c:])
    pnext = jnp.where(last, 0.0, usn[:, c:2 * c] * usn[:, 2 * c:])
    osc_ref[0] = us[:, :c] * _conv3(prod, pprev, pnext, sw_ref[...])


def _mixer_prep(u_hy, u_sc, hy_conv_w, hy_conv_b, sc_conv_w):
    B, L, W = u_hy.shape
    tl = min(L, 512)
    r8 = tl // 8
    main = pl.BlockSpec((1, tl, W), lambda b, t: (b, t, 0))
    prev = pl.BlockSpec((1, 8, W), lambda b, t: (b, jnp.maximum(t * r8 - 1, 0), 0))
    nxt = pl.BlockSpec((1, 8, W), lambda b, t: (b, jnp.minimum((t + 1) * r8, L // 8 - 1), 0))
    const = lambda shape: pl.BlockSpec(shape, lambda b, t: (0, 0))
    out = pl.BlockSpec((1, tl, HY_W), lambda b, t: (b, t, 0))
    return pl.pallas_call(
        _mixprep_kernel,
        grid=(B, L // tl),
        in_specs=[main, prev, nxt, main, prev, nxt, const((3, W)), const((1, W)), const((3, SC_W))],
        out_specs=[out, out, out],
        out_shape=[jax.ShapeDtypeStruct((B, L, HY_W), F32)] * 3,
        compiler_params=_cp(("arbitrary", "arbitrary")),
        name="mixer_prep",
    )(u_hy, u_hy, u_hy, u_sc, u_sc, u_sc, hy_conv_w, hy_conv_b.reshape(1, W), sc_conv_w)


def _filter_kernel(z_ref, w1_ref, b1_ref, f1_ref, w2_ref, b2_ref, f2_ref, w3_ref, dl_ref, hf_ref, hb_ref):
    tl = z_ref.shape[0]
    z = z_ref[...]
    h = jnp.sin(f1_ref[...] * (_dot(z, w1_ref[...], HI) + b1_ref[...]))
    h = jnp.sin(f2_ref[...] * (_dot(h, w2_ref[...], HI) + b2_ref[...]))
    h = _dot(h, w3_ref[...], HI) * jnp.exp(-z[:, 0:1] * jnp.abs(dl_ref[...]))
    row = pl.program_id(0) * tl + lax.broadcasted_iota(jnp.int32, (tl, 1), 0)
    hf_ref[0] = h[:, :HY_W]
    hb_ref[0] = jnp.where(row == 0, 0.0, h[:, HY_W:])


def _filter_features(L):
    t = jnp.linspace(0.0, 1.0, L, dtype=F32)[:, None]
    w = 2.0 * math.pi * jnp.arange(L, dtype=F32)[:, None] / L
    bands = jnp.linspace(1e-4, FILTER_BANDS - 1, FILTER_BANDS, dtype=F32)[None, :]
    z = jnp.concatenate([t, jnp.cos(bands * w), -jnp.sin(bands * w)], axis=-1)
    return jnp.pad(z, ((0, 0), (0, 128 - FILTER_EMB)))


def _hyena_filters(L, w1, b1, f1, w2, b2, f2, w3, deltas):
    tl = min(L, 512)
    z = _filter_features(L)
    w1p = jnp.pad(w1, ((0, 128 - FILTER_EMB), (0, 0)))
    const = lambda shape: pl.BlockSpec(shape, lambda i: (0, 0))
    H = FILTER_HIDDEN
    hf, hb = pl.pallas_call(
        _filter_kernel,
        grid=(L // tl,),
        in_specs=[pl.BlockSpec((tl, 128), lambda i: (i, 0)), const((128, H)), const((1, H)), const((1, H)),
                  const((H, H)), const((1, H)), const((1, H)), const((H, 2 * HY_W)), const((1, 2 * HY_W))],
        out_specs=[pl.BlockSpec((1, tl, HY_W), lambda i: (0, i, 0)), pl.BlockSpec((1, tl, HY_W), lambda i: (0, i, 0))],
        out_shape=[jax.ShapeDtypeStruct((1, L, HY_W), F32)] * 2,
        compiler_params=_cp(("arbitrary",)),
        name="hyena_filters",
    )(z, w1p, b1.reshape(1, H), f1.reshape(1, H), w2, b2.reshape(1, H), f2.reshape(1, H), w3,
      deltas.reshape(1, 2 * HY_W))
    return jnp.concatenate([hf, hb], axis=0)


def _fft_split(L):
    n = 2 * L
    n1 = 128 if n >= 8192 else 16
    return n1, n // n1


def _fft_constants(L):
    n1, n2 = _fft_split(L)
    n = n1 * n2
    k1 = jnp.arange(n1, dtype=jnp.int32)
    a = (k1[:, None] * jnp.arange(n1 // 2, dtype=jnp.int32)[None, :]) % n1
    th = (2.0 * math.pi / n1) * a.astype(F32)
    f1 = jnp.concatenate([jnp.cos(th), -jnp.sin(th)], axis=0)
    f3 = jnp.concatenate([jnp.cos(th).T, -jnp.sin(th).T], axis=1) * (1.0 / n)
    k = k1[:, None, None] + n1 * jnp.arange(n2, dtype=jnp.int32)[None, :, None]
    ph = (k * jnp.arange(n2, dtype=jnp.int32)[None, None, :]) % n
    ph = (2.0 * math.pi / n) * ph.astype(F32)
    mr, mi = jnp.cos(ph), -jnp.sin(ph)
    mf = jnp.concatenate([jnp.concatenate([mr, -mi], axis=2), jnp.concatenate([mi, mr], axis=2)], axis=1)
    mrt, mit = jnp.swapaxes(mr, 1, 2), jnp.swapaxes(mi, 1, 2)
    mb = jnp.concatenate([jnp.concatenate([mrt, mit], axis=2), jnp.concatenate([-mit, mrt], axis=2)], axis=1)
    return f1, f3, mf, mb


def _dft_rows_kernel(f_ref, x_ref, o_ref):
    o_ref[0] = _dot(f_ref[...], x_ref[0], HI)


def _dft_rows(f1, x):
    B, kh, W = x.shape
    m = f1.shape[0]
    tn = min(W, 2048)
    return pl.pallas_call(
        _dft_rows_kernel,
        grid=(B, W // tn),
        in_specs=[pl.BlockSpec((m, kh), lambda b, j: (0, 0)), pl.BlockSpec((1, kh, tn), lambda b, j: (b, 0, j))],
        out_specs=pl.BlockSpec((1, m, tn), lambda b, j: (b, 0, j)),
        out_shape=jax.ShapeDtypeStruct((B, m, W), F32),
        compiler_params=_cp(("arbitrary", "arbitrary")),
        name="fft_rows",
    )(f1, x)


def _fft_mid_kernel(a_ref, hf_ref, hb_ref, mf_ref, mb_ref, o_ref, kf_ref):
    n2 = a_ref.shape[3]
    m = mf_ref[0]

    @pl.when(pl.program_id(1) == 0)
    def _():
        xf = _dot(m, jnp.concatenate([hf_ref[0, 0, 0], hf_ref[0, 1, 0]], axis=0), HI)
        xb = _dot(m, jnp.concatenate([hb_ref[0, 0, 0], hb_ref[0, 1, 0]], axis=0), HI)
        kf_ref[0] = xf[:n2] + xb[:n2]
        kf_ref[1] = xf[n2:] - xb[n2:]

    x = _dot(m, jnp.concatenate([a_ref[0, 0, 0], a_ref[0, 1, 0]], axis=0), HI)
    xr, xi = x[:n2], x[n2:]
    kr, ki = kf_ref[0], kf_ref[1]
    y = jnp.concatenate([xr * kr - xi * ki, xr * ki + xi * kr], axis=0)
    o = _dot(mb_ref[0], y, HI)
    o_ref[0, 0, 0] = o[:n2]
    o_ref[0, 1, 0] = o[n2:]


def _fft_mid(a, ah, mf, mb):
    B, _, n1, n2, C = a.shape
    blk = (1, 2, 1, n2, C)
    return pl.pallas_call(
        _fft_mid_kernel,
        grid=(n1, B),
        in_specs=[
            pl.BlockSpec(blk, lambda k, b: (b, 0, k, 0, 0)),
            pl.BlockSpec(blk, lambda k, b: (0, 0, k, 0, 0)),
            pl.BlockSpec(blk, lambda k, b: (1, 0, k, 0, 0)),
            pl.BlockSpec((1, 2 * n2, 2 * n2), lambda k, b: (k, 0, 0)),
            pl.BlockSpec((1, 2 * n2, 2 * n2), lambda k, b: (k, 0, 0)),
        ],
        out_specs=pl.BlockSpec(blk, lambda k, b: (b, 0, k, 0, 0)),
        out_shape=jax.ShapeDtypeStruct(a.shape, F32),
        scratch_shapes=[pltpu.VMEM((2, n2, C), F32)],
        compiler_params=_cp(("arbitrary", "arbitrary")),
        name="fft_mid",
    )(a, ah, ah, mf, mb)


def _idft_rows_kernel(f_ref, b_ref, x0_ref, g_ref, bias_ref, o_ref):
    y = _dot(f_ref[...], b_ref[0], HI)
    o_ref[0] = x0_ref[0] * (y + g_ref[0] * bias_ref[...])


def _idft_rows_gate(f3, bo, x0, g, bias_row):
    B, m2, W = bo.shape
    kh = f3.shape[0]
    tn = min(W, 2048)
    tile = pl.BlockSpec((1, kh, tn), lambda b, j: (b, 0, j))
    return pl.pallas_call(
        _idft_rows_kernel,
        grid=(B, W // tn),
        in_specs=[pl.BlockSpec((kh, m2), lambda b, j: (0, 0)), pl.BlockSpec((1, m2, tn), lambda b, j: (b, 0, j)),
                  tile, tile, pl.BlockSpec((1, tn), lambda b, j: (0, j))],
        out_specs=tile,
        out_shape=jax.ShapeDtypeStruct((B, kh, W), F32),
        compiler_params=_cp(("arbitrary", "arbitrary")),
        name="ifft_rows_gate",
    )(f3, bo, x0, g, bias_row)


def _hyena_long_conv(x0, g, filt, hy_bias, consts):
    B, L, C = g.shape
    n1, n2 = _fft_split(L)
    f1, f3, mf, mb = consts
    wide = lambda a: a.reshape(a.shape[0], n1 // 2, n2 * C)
    a = _dft_rows(f1, wide(g)).reshape(B, 2, n1, n2, C)
    ah = _dft_rows(f1, wide(filt)).reshape(2, 2, n1, n2, C)
    bo = _fft_mid(a, ah, mf, mb).reshape(B, 2 * n1, n2 * C)
    out = _idft_rows_gate(f3, bo, wide(x0), wide(g), jnp.tile(hy_bias, n2).reshape(1, n2 * C))
    return out.reshape(B, L, C)


def _outproj_kernel(oa_ref, oh_ref, os_ref, x_ref, mod_ref, w_ref, lg_ref, lb_ref, rw_ref, rb_ref,
                    x1_ref, h2_ref, meta_ref, cnt_ref):
    m = mod_ref[0, 0]
    w = w_ref[...]
    o = (_dot(oa_ref[...].astype(BF16), w[:ATT_W]) + _dot(oh_ref[...].astype(BF16), w[ATT_W:ATT_W + HY_W])
         + _dot(os_ref[...].astype(BF16), w[ATT_W + HY_W:]))
    x1 = _layer_norm(DN_ALPHA * x_ref[...] + m[2:3] * o, lg_ref[...], lb_ref[...])
    x1_ref[...] = x1
    h2 = x1 * (1.0 + m[4:5]) + m[3:4]
    h2_ref[...] = h2.astype(BF16)
    lane = lax.broadcasted_iota(jnp.int32, (h2.shape[0], 128), 1)
    logits = jnp.where(lane < N_EXP, _dot(h2, rw_ref[...], HI) + rb_ref[...], NEG_INF)
    picks = []
    vals = []
    for _ in range(TOP_K):
        v = jnp.max(logits, axis=-1, keepdims=True)
        idx = jnp.min(jnp.where(logits == v, lane, 128), axis=-1, keepdims=True)
        hit = lane == idx
        picks.append(hit)
        vals.append(v)
        logits = jnp.where(hit, NEG_INF, logits)
    es = [jnp.exp(v - vals[0]) for v in vals]
    den = es[0] + es[1] + es[2] + es[3]
    tb = h2.shape[0]
    msel = jnp.zeros(logits.shape, F32)
    for hit in picks:
        msel = msel + jnp.where(hit, 1.0, 0.0)
    cnt = jnp.sum(msel, axis=0, keepdims=True)
    pc = jnp.floor((cnt + (ROW_PAD - 1.0)) * (1.0 / ROW_PAD)) * ROW_PAD
    upper = jnp.where(lax.broadcasted_iota(jnp.int32, (128, 128), 0)
                      < lax.broadcasted_iota(jnp.int32, (128, 128), 1), 1.0, 0.0)
    lo = _dot(jnp.broadcast_to(pc, (8, 128)), upper, HI)[0:1]
    lower = jnp.where(lax.broadcasted_iota(jnp.int32, (tb, tb), 1)
                      < lax.broadcasted_iota(jnp.int32, (tb, tb), 0), 1.0, 0.0).astype(BF16)
    pos = lo + _dot(lower, msel.astype(BF16))
    meta = jnp.zeros(logits.shape, F32)
    for kk, (hit, e) in enumerate(zip(picks, es)):
        slot = jnp.sum(jnp.where(hit, pos, 0.0), axis=-1, keepdims=True)
        meta = jnp.where(lane == kk, slot, meta)
        meta = jnp.where(lane == TOP_K + kk, e / den, meta)
    meta_ref[...] = meta
    cnt_ref[0] = jnp.broadcast_to(cnt, (8, 128))


def _out_projection(o_attn, o_hy, o_sc, x, mod, w_out_bf, ln_g, ln_b, router_w, router_b, layer, t_ctx, dec_seq):
    T = x.shape[0]
    tm = TOKEN_BLOCK
    row = lambda i: (layer, _mod_row(i, tm, t_ctx, dec_seq), 0, 0)
    tok = lambda w: pl.BlockSpec((tm, w), lambda i: (i, 0))
    const = lambda shape: pl.BlockSpec(shape, lambda i: (0, 0))
    rw = jnp.pad(router_w, ((0, 0), (0, 128 - N_EXP)))
    rb = jnp.pad(router_b, (0, 128 - N_EXP)).reshape(1, 128)
    return pl.pallas_call(
        _outproj_kernel,
        grid=(T // tm,),
        in_specs=[tok(ATT_W), tok(HY_W), tok(SC_W), tok(D), pl.BlockSpec((1, 1, 6, D), row),
                  const((D, D)), const((1, D)), const((1, D)), const((D, 128)), const((1, 128))],
        out_specs=[tok(D), tok(D), tok(128), pl.BlockSpec((1, 8, 128), lambda i: (i, 0, 0))],
        out_shape=[jax.ShapeDtypeStruct((T, D), F32), jax.ShapeDtypeStruct((T, D), BF16),
                   jax.ShapeDtypeStruct((T, 128), F32), jax.ShapeDtypeStruct((T // tm, 8, 128), F32)],
        compiler_params=_cp(("arbitrary",)),
        name="out_projection_router",
    )(o_attn, o_hy, o_sc, x, mod, w_out_bf, ln_g.reshape(1, D), ln_b.reshape(1, D), rw, rb)


def _routing_tables(cnt):
    nb = cnt.shape[0]
    pc = (cnt + ROW_PAD - 1) // ROW_PAD * ROW_PAD
    lo = jnp.cumsum(pc, axis=1) - pc
    tot = jnp.sum(pc, axis=0)
    tot_pad = (tot + EXPERT_TILE - 1) // EXPERT_TILE * EXPERT_TILE
    off = jnp.cumsum(tot_pad) - tot_pad
    gstart = off[None, :] + jnp.cumsum(pc, axis=0) - pc
    ntile = tot_pad // EXPERT_TILE
    cum = jnp.cumsum(ntile)
    n_tiles = _max_expert_tiles(nb)
    i = jnp.arange(n_tiles, dtype=jnp.int32)
    ic = jnp.minimum(i, cum[-1] - 1)
    te = jnp.sum((cum[None, :] <= ic[:, None]).astype(jnp.int32), axis=1)
    first = ic == (cum - ntile)[te]
    flags = jnp.where(i < cum[-1], 1 + 2 * first.astype(jnp.int32), 0)
    i32 = lambda a: a.astype(jnp.int32).reshape(-1)
    return dict(pc=i32(pc), lo=i32(lo), gstart=i32(gstart), tail_start=i32(off + tot), tail_len=i32(tot_pad - tot),
                tile_expert=te, tile_row=ic, tile_flags=flags)


def _max_expert_tiles(nb):
    rows = TOP_K * nb * TOKEN_BLOCK + nb * N_EXP * (ROW_PAD - 1) + N_EXP * (EXPERT_TILE - 1)
    return rows // EXPERT_TILE + 1


def _run_copies(n, src_ref, src0, dst_ref, dst0, sem, wait):
    done = jnp.int32(0)
    for p in RUN_PIECES:
        take = (n & p) != 0

        @pl.when(take)
        def _():
            cp = pltpu.make_async_copy(src_ref.at[pl.ds(pl.multiple_of(src0 + done, ROW_PAD), p), :],
                                       dst_ref.at[pl.ds(pl.multiple_of(dst0 + done, ROW_PAD), p), :], sem)
            if wait:
                cp.wait()
            else:
                cp.start()

        done = done + jnp.where(take, p, 0)


def _dispatch_kernel(pc_ref, lo_ref, gs_ref, ts_ref, tl_ref, h_ref, meta_ref, xs_ref, buf_ref, zero_ref, sem):
    b = pl.program_id(0)
    s = buf_ref.shape[0]

    @pl.when(b == 0)
    def _():
        zero_ref[...] = jnp.zeros_like(zero_ref)
        for wait in (False, True):
            def tail(e, carry):
                _run_copies(tl_ref[e], zero_ref, 0, xs_ref, ts_ref[e], sem, wait)
                return carry
            lax.fori_loop(0, N_EXP, tail, 0)

    mt = meta_ref[...].T
    srow = lax.broadcasted_iota(jnp.int32, (s, 1), 0).astype(F32)
    perm = jnp.zeros((s, TOKEN_BLOCK), F32)
    gperm = jnp.zeros((s, TOKEN_BLOCK), F32)
    for k in range(TOP_K):
        hit = srow == mt[k:k + 1]
        perm = perm + jnp.where(hit, 1.0, 0.0)
        gperm = gperm + jnp.where(hit, mt[TOP_K + k:TOP_K + k + 1], 0.0)
    buf_ref[:, :D] = _dot(perm.astype(BF16), h_ref[...])
    buf_ref[:, D:] = jnp.broadcast_to(jnp.sum(gperm, axis=-1, keepdims=True), (s, 128))
    for wait in (False, True):
        def run(e, carry):
            j = b * N_EXP + e
            _run_copies(pc_ref[j], buf_ref, lo_ref[j], xs_ref, gs_ref[j], sem, wait)
            return carry
        lax.fori_loop(0, N_EXP, run, 0)


def _dispatch(tabs, h2, meta):
    T = h2.shape[0]
    nb = T // TOKEN_BLOCK
    rows = _max_expert_tiles(nb) * EXPERT_TILE
    tok = lambda w: pl.BlockSpec((TOKEN_BLOCK, w), lambda b, *_: (b, 0))
    return pl.pallas_call(
        _dispatch_kernel,
        grid_spec=pltpu.PrefetchScalarGridSpec(
            num_scalar_prefetch=5,
            grid=(nb,),
            in_specs=[tok(D), tok(128)],
            out_specs=pl.BlockSpec(memory_space=pl.ANY),
            scratch_shapes=[pltpu.VMEM((LOCAL_SLOTS, D + 128), F32), pltpu.VMEM((EXPERT_TILE, D + 128), F32),
                            pltpu.SemaphoreType.DMA(())],
        ),
        out_shape=jax.ShapeDtypeStruct((rows, D + 128), F32),
        compiler_params=_cp(("arbitrary",)),
        name="moe_dispatch",
    )(tabs["pc"], tabs["lo"], tabs["gstart"], tabs["tail_start"], tabs["tail_len"], h2, meta)


def _expert_ffn_kernel(te_ref, tr_ref, fl_ref, x_ref, wg_ref, bg_ref, wu_ref, bu_ref, wd_ref, bd_ref, y_ref, wbf_ref):
    fl = fl_ref[pl.program_id(0)]

    @pl.when((fl & 1) != 0)
    def _():
        @pl.when((fl & 2) != 0)
        def _():
            wbf_ref[0] = wg_ref[0, 0].astype(BF16)
            wbf_ref[1] = wu_ref[0, 0].astype(BF16)
            wbf_ref[2] = wd_ref[0, 0].astype(BF16)

        x = x_ref[:, :D].astype(BF16)
        gate = x_ref[:, D:D + 1]
        g = jnp.minimum(_dot(x, wbf_ref[0]) + bg_ref[0], SWIGLU_LIMIT)
        u = jnp.clip(_dot(x, wbf_ref[1]) + bu_ref[0], -SWIGLU_LIMIT, SWIGLU_LIMIT)
        a = (u + 1.0) * (g * jax.nn.sigmoid(SWIGLU_ALPHA * g))
        y_ref[...] = gate * (_dot(a.astype(BF16), wbf_ref[2]) + bd_ref[0])


def _expert_ffn(tabs, xs, w_gate, b_gate, w_up, b_up, w_down, b_down, layer):
    rows = xs.shape[0]
    wspec = pl.BlockSpec((1, 1, D, D), lambda i, te, tr, fl: (layer, te[i], 0, 0))
    bspec = pl.BlockSpec((1, 1, D), lambda i, te, tr, fl: (te[i], 0, 0))
    return pl.pallas_call(
        _expert_ffn_kernel,
        grid_spec=pltpu.PrefetchScalarGridSpec(
            num_scalar_prefetch=3,
            grid=(rows // EXPERT_TILE,),
            in_specs=[pl.BlockSpec((EXPERT_TILE, D + 128), lambda i, te, tr, fl: (tr[i], 0)),
                      wspec, bspec, wspec, bspec, wspec, bspec],
            out_specs=pl.BlockSpec((EXPERT_TILE, D), lambda i, te, tr, fl: (tr[i], 0)),
            scratch_shapes=[pltpu.VMEM((3, D, D), BF16)],
        ),
        out_shape=jax.ShapeDtypeStruct((rows, D), F32),
        compiler_params=_cp(("arbitrary",)),
        name="expert_ffn",
    )(tabs["tile_expert"], tabs["tile_row"], tabs["tile_flags"], xs,
      w_gate, b_gate[layer].reshape(N_EXP, 1, D), w_up, b_up[layer].reshape(N_EXP, 1, D),
      w_down, b_down[layer].reshape(N_EXP, 1, D))


def _combine_kernel(pc_ref, lo_ref, gs_ref, ys_ref, meta_ref, x1_ref, mod_ref, lg_ref, lb_ref, o_ref, buf_ref, sem):
    b = pl.program_id(0)
    s = buf_ref.shape[0]

    @pl.when(b == 0)
    def _():
        buf_ref[...] = jnp.zeros_like(buf_ref)

    for wait in (False, True):
        def run(e, carry):
            j = b * N_EXP + e
            _run_copies(pc_ref[j], ys_ref, gs_ref[j], buf_ref, lo_ref[j], sem, wait)
            return carry
        lax.fori_loop(0, N_EXP, run, 0)

    meta = meta_ref[...]
    scol = lax.broadcasted_iota(jnp.int32, (1, s), 1).astype(F32)
    w = jnp.zeros((TOKEN_BLOCK, s), F32)
    for k in range(TOP_K):
        w = w + jnp.where(meta[:, k:k + 1] == scol, 1.0, 0.0)
    w = w.astype(BF16)
    y = buf_ref[...]
    y_hi = y.astype(BF16)
    y_lo = (y - y_hi.astype(F32)).astype(BF16)
    moe = _dot(w, y_hi) + _dot(w, y_lo)
    m = mod_ref[0, 0]
    o_ref[...] = _layer_norm(DN_ALPHA * x1_ref[...] + m[5:6] * moe, lg_ref[...], lb_ref[...])


def _combine_ln2(tabs, ys, meta, x1, mod, ln_g, ln_b, layer, t_ctx, dec_seq):
    T = x1.shape[0]
    row = lambda b, *_: (layer, _mod_row(b, TOKEN_BLOCK, t_ctx, dec_seq), 0, 0)
    tok = lambda w: pl.BlockSpec((TOKEN_BLOCK, w), lambda b, *_: (b, 0))
    const = lambda shape: pl.BlockSpec(shape, lambda b, *_: (0, 0))
    return pl.pallas_call(
        _combine_kernel,
        grid_spec=pltpu.PrefetchScalarGridSpec(
            num_scalar_prefetch=3,
            grid=(T // TOKEN_BLOCK,),
            in_specs=[pl.BlockSpec(memory_space=pl.ANY), tok(128), tok(D), pl.BlockSpec((1, 1, 6, D), row),
                      const((1, D)), const((1, D))],
            out_specs=tok(D),
            scratch_shapes=[pltpu.VMEM((LOCAL_SLOTS, D), F32), pltpu.SemaphoreType.DMA(())],
        ),
        out_shape=jax.ShapeDtypeStruct((T, D), F32),
        compiler_params=_cp(("arbitrary",)),
        name="moe_combine_ln2",
    )(tabs["pc"], tabs["lo"], tabs["gstart"], ys, meta, x1, mod, ln_g.reshape(1, D), ln_b.reshape(1, D))


def _moe_ln2(h2, meta, cnt, x1, mod, w_gate, b_gate, w_up, b_up, w_down, b_down, ln_g, ln_b, layer, t_ctx, dec_seq):
    tabs = _routing_tables(cnt[:, 0, :N_EXP].astype(jnp.int32))
    xs = _dispatch(tabs, h2, meta)
    ys = _expert_ffn(tabs, xs, w_gate, b_gate, w_up, b_up, w_down, b_down, layer)
    return _combine_ln2(tabs, ys, meta, x1, mod, ln_g, ln_b, layer, t_ctx, dec_seq)


def kernel(x_prompt, x_sample, cache_k, cache_v, c, c_ctx, w_mod, b_mod, w_in, w_out, attn_sink, hy_conv_w, hy_conv_b, hy_w1, hy_b1, hy_f1, hy_w2, hy_b2, hy_f2, hy_w3, hy_deltas, hy_bias, sc_conv_w, ln1_g, ln1_b, router_w, router_b, w_gate, b_gate, w_up, b_up, w_down, b_down, ln2_g, ln2_b):
    nb, seq, _ = x_prompt.shape
    db, dseq, _ = x_sample.shape
    past = cache_k.shape[2]
    t_ctx = nb * seq
    x = jnp.concatenate([x_prompt.reshape(t_ctx, D), x_sample.reshape(db * dseq, D)], axis=0)

    cond = jnp.zeros((8, D), F32).at[0].set(c_ctx).at[1:1 + db].set(c)
    mod = _modulation(cond, w_mod, b_mod)

    cos64, sin64 = _rope_tables(dseq)
    tables = (jnp.tile(cos64, (1, N_HEADS)), jnp.tile(sin64, (1, N_HEADS)),
              jnp.tile(cos64, (1, N_KV)), jnp.tile(sin64, (1, N_KV)))
    fft_c = {L: _fft_constants(L) for L in (seq, dseq)}
    w_in_bf = w_in.astype(BF16)
    w_out_bf = w_out.astype(BF16)

    new_k, new_v = [], []
    for l in range(DEPTH):
        q, k, v, u_hy, u_sc = _in_projection(x, mod, w_in_bf[l], l, t_ctx, dseq)
        k_ctx = k[:t_ctx].reshape(nb, seq, KV_W)
        v_ctx = v[:t_ctx].reshape(nb, seq, KV_W)
        new_k.append(k_ctx.reshape(nb, seq, N_KV, HD))
        new_v.append(v_ctx.reshape(nb, seq, N_KV, HD))
        oa_ctx = _context_attention(q[:t_ctx].reshape(nb, seq, ATT_W), k_ctx, v_ctx, attn_sink[l])
        oa_lat = _latent_attention(q[t_ctx:].reshape(db, dseq, ATT_W), k[t_ctx:].reshape(db, dseq, KV_W),
                                   v[t_ctx:].reshape(db, dseq, KV_W), cache_k[:, l].reshape(db, past, KV_W),
                                   cache_v[:, l].reshape(db, past, KV_W), attn_sink[l], tables)
        o_hy, o_sc = [], []
        for (B, L, lo) in ((nb, seq, 0), (db, dseq, t_ctx)):
            uh = u_hy[lo:lo + B * L].reshape(B, L, 3 * HY_W)
            us = u_sc[lo:lo + B * L].reshape(B, L, 3 * SC_W)
            x0, g, osc = _mixer_prep(uh, us, hy_conv_w[l], hy_conv_b[l], sc_conv_w[l])
            filt = _hyena_filters(L, hy_w1[l], hy_b1[l], hy_f1[l], hy_w2[l], hy_b2[l], hy_f2[l], hy_w3[l],
                                  hy_deltas[l])
            ohy = _hyena_long_conv(x0, g, filt, hy_bias[l], fft_c[L])
            o_hy.append(ohy.reshape(B * L, HY_W))
            o_sc.append(osc.reshape(B * L, SC_W))
        o_attn = jnp.concatenate([oa_ctx.reshape(t_ctx, ATT_W), oa_lat.reshape(db * dseq, ATT_W)], axis=0)
        o_hy = jnp.concatenate(o_hy, axis=0)
        o_sc = jnp.concatenate(o_sc, axis=0)
        x1, h2, meta, cnt = _out_projection(o_attn, o_hy, o_sc, x, mod, w_out_bf[l], ln1_g[l], ln1_b[l],
                                            router_w[l], router_b[l], l, t_ctx, dseq)
        x = _moe_ln2(h2, meta, cnt, x1, mod, w_gate, b_gate, w_up, b_up, w_down, b_down, ln2_g[l], ln2_b[l],
                     l, t_ctx, dseq)

    y_prompt = x[:t_ctx].reshape(nb, seq, D)
    y_sample = x[t_ctx:].reshape(db, dseq, D)
    return (y_prompt, y_sample, jnp.stack(new_k, axis=1), jnp.stack(new_v, axis=1))
```

```python
import functools
import math

import jax
import jax.numpy as jnp
from jax import lax
from jax.experimental import pallas as pl
from jax.experimental.pallas import tpu as pltpu

F32 = jnp.float32
BF16 = jnp.bfloat16
HI = lax.Precision.HIGHEST

D = 1024
DEPTH = 2
N_HEADS = 8
N_KV = 2
HD = 64
GQA = N_HEADS // N_KV
ATT_W = N_HEADS * HD
KV_W = N_KV * HD
WINDOW = 128
BLK = 128
GRID_W = 64
ROPE_BASE = 10000.0
HY_W = 256
SC_W = 256
FILTER_EMB = 33
FILTER_BANDS = 16
FILTER_HIDDEN = 64
IN_W = ATT_W + 2 * KV_W + 3 * HY_W + 3 * SC_W
N_EXP = 32
TOP_K = 4
SWIGLU_LIMIT = 7.0
SWIGLU_ALPHA = 1.702
DN_ALPHA = (2 * DEPTH) ** 0.25
LN_EPS = 1e-5
NEG_INF = float("-inf")

VMEM_LIMIT = 56 * 1024 * 1024

TOKEN_BLOCK = 256
ROW_PAD = 8
RUN_PIECES = (256, 128, 64, 32, 16, 8)
EXPERT_TILE = 512
LOCAL_SLOTS = TOP_K * TOKEN_BLOCK + N_EXP * ROW_PAD


def _cp(sem):
    return pltpu.CompilerParams(dimension_semantics=sem, vmem_limit_bytes=VMEM_LIMIT)


def _dot(a, b, precision=None):
    return jnp.dot(a, b, preferred_element_type=F32, precision=precision)


def _dot_nt(a, b):
    return lax.dot_general(a, b, (((1,), (1,)), ((), ())), preferred_element_type=F32)


def _layer_norm(z, g, b):
    mu = jnp.mean(z, axis=-1, keepdims=True)
    zc = z - mu
    var = jnp.mean(zc * zc, axis=-1, keepdims=True)
    return zc * lax.rsqrt(var + LN_EPS) * g + b


def _mod_kernel(cond_ref, w_ref, b_ref, o_ref):
    c = cond_ref[...]
    s = c * jax.nn.sigmoid(c)
    o_ref[0] = _dot(s, w_ref[0], HI) + b_ref[0]


def _modulation(cond, w_mod, b_mod):
    tn = 1024
    out = pl.pallas_call(
        _mod_kernel,
        grid=(DEPTH, 6 * D // tn),
        in_specs=[
            pl.BlockSpec((8, D), lambda l, j: (0, 0)),
            pl.BlockSpec((1, D, tn), lambda l, j: (l, 0, j)),
            pl.BlockSpec((1, 1, tn), lambda l, j: (l, 0, j)),
        ],
        out_specs=pl.BlockSpec((1, 8, tn), lambda l, j: (l, 0, j)),
        out_shape=jax.ShapeDtypeStruct((DEPTH, 8, 6 * D), F32),
        compiler_params=_cp(("arbitrary", "arbitrary")),
        name="modulation",
    )(cond, w_mod, b_mod.reshape(DEPTH, 1, 6 * D))
    return out.reshape(DEPTH, 8, 6, D)


def _mod_row(i, tm, t_ctx, dec_seq):
    nct = t_ctx // tm
    return jnp.where(i < nct, 0, 1 + (i - nct) // (dec_seq // tm))


def _ctx_lat_specs(shape, nct, ctx_offset=0, lat_offset=0):
    ctx = pl.BlockSpec(shape, lambda i, *_: (jnp.minimum(i, nct - 1) + ctx_offset, 0))
    lat = pl.BlockSpec(shape, lambda i, *_: (jnp.maximum(i - nct, 0) + lat_offset, 0))
    return ctx, lat


def _inproj_kernel(xc_ref, xl_ref, mod_ref, w_ref, q_ref, k_ref, v_ref, uh_ref, us_ref, *, nct):
    m = mod_ref[0, 0]
    x = jnp.where(pl.program_id(0) < nct, xc_ref[...], xl_ref[...])
    h = x * (1.0 + m[1:2]) + m[0:1]
    y = _dot(h.astype(BF16), w_ref[...])
    o0 = ATT_W
    o1 = o0 + KV_W
    o2 = o1 + KV_W
    o3 = o2 + 3 * HY_W
    q_ref[...] = y[:, :o0]
    k_ref[...] = y[:, o0:o1]
    v_ref[...] = y[:, o1:o2]
    uh_ref[...] = y[:, o2:o3]
    us_ref[...] = y[:, o3:]


def _in_projection(xc, xl, mod, w_in_bf, layer, dec_seq):
    t_ctx = xc.shape[0]
    T = t_ctx + xl.shape[0]
    tm = 512
    row = lambda i: (layer, _mod_row(i, tm, t_ctx, dec_seq), 0, 0)
    tok = lambda w: pl.BlockSpec((tm, w), lambda i: (i, 0))
    widths = (ATT_W, KV_W, KV_W, 3 * HY_W, 3 * SC_W)
    return pl.pallas_call(
        functools.partial(_inproj_kernel, nct=t_ctx // tm),
        grid=(T // tm,),
        in_specs=[
            *_ctx_lat_specs((tm, D), t_ctx // tm),
            pl.BlockSpec((1, 1, 6, D), row),
            pl.BlockSpec((D, IN_W), lambda i: (0, 0)),
        ],
        out_specs=[tok(w) for w in widths],
        out_shape=[jax.ShapeDtypeStruct((T, w), F32) for w in widths],
        compiler_params=_cp(("arbitrary",)),
        name="in_projection",
    )(xc, xl, mod, w_in_bf)


def _swap_halves(x):
    w = x.shape[-1]
    lane = lax.broadcasted_iota(jnp.int32, x.shape, 1)
    return jnp.where((lane % HD) < HD // 2, pltpu.roll(x, w - HD // 2, 1), pltpu.roll(x, HD // 2, 1))


def _rope(x, cos, sin_signed):
    return x * cos + _swap_halves(x) * sin_signed


def _group_rows(q, g):
    return jnp.concatenate([q[:, (GQA * g + r) * HD:(GQA * g + r + 1) * HD] for r in range(GQA)], axis=0)


def _sink_column(sink_ref, g, rows):
    ridx = lax.broadcasted_iota(jnp.int32, (GQA * rows, 1), 0)
    col = jnp.full((GQA * rows, 1), sink_ref[GQA * g + GQA - 1], F32)
    for r in range(GQA - 2, -1, -1):
        col = jnp.where(ridx < (r + 1) * rows, sink_ref[GQA * g + r], col)
    return col


def _ungroup(outs, rows):
    return jnp.concatenate([o[r * rows:(r + 1) * rows] for o in outs for r in range(GQA)], axis=1)


def _ctx_attn_kernel(sink_ref, q_ref, k_ref, v_ref, o_ref):
    rows = q_ref.shape[0]
    q = q_ref[...] * (HD ** -0.5)
    k = k_ref[...]
    v = v_ref[...]
    outs = []
    for g in range(N_KV):
        qg = _group_rows(q, g).astype(BF16)
        kg = k[:, g * HD:(g + 1) * HD].astype(BF16)
        vg = v[:, g * HD:(g + 1) * HD].astype(BF16)
        s = _dot_nt(qg, kg)
        sink = _sink_column(sink_ref, g, rows)
        m = jnp.maximum(jnp.max(s, axis=-1, keepdims=True), sink)
        e = jnp.exp(s - m)
        den = jnp.sum(e, axis=-1, keepdims=True) + jnp.exp(sink - m)
        outs.append(_dot(e.astype(BF16), vg) / den)
    o_ref[...] = _ungroup(outs, rows)


def _context_attention(q, k, v, sink, B, L):
    tok = lambda w: pl.BlockSpec((L, w), lambda b: (b, 0))
    return pl.pallas_call(
        _ctx_attn_kernel,
        grid=(B,),
        in_specs=[pl.BlockSpec(memory_space=pltpu.SMEM), tok(ATT_W), tok(KV_W), tok(KV_W)],
        out_specs=tok(ATT_W),
        out_shape=jax.ShapeDtypeStruct((B * L, ATT_W), F32),
        compiler_params=_cp(("arbitrary",)),
        name="context_attention",
    )(sink, q, k, v)


def _lat_attn_kernel(sink_ref, q_ref, k_ref, v_ref, kc_ref, vc_ref, cq_ref, sq_ref, ck_ref, sk_ref, o_ref):
    n = pl.program_id(1)
    L = k_ref.shape[0]
    nk = 3 * BLK
    ws = pl.multiple_of(jnp.clip((n - 1) * BLK, 0, L - nk), BLK)
    q = _rope(q_ref[...], cq_ref[...], sq_ref[...]) * (HD ** -0.5)
    kl = _rope(k_ref[pl.ds(ws, nk), :], ck_ref[pl.ds(ws, nk), :], sk_ref[pl.ds(ws, nk), :])
    vl = v_ref[pl.ds(ws, nk), :]
    kc = kc_ref[...]
    vc = vc_ref[...]
    qpos = n * BLK + lax.broadcasted_iota(jnp.int32, (GQA * BLK, 1), 0) % BLK
    kpos = ws + lax.broadcasted_iota(jnp.int32, (1, nk), 1)
    valid = jnp.abs(kpos - qpos) <= WINDOW
    outs = []
    for g in range(N_KV):
        sl = slice(g * HD, (g + 1) * HD)
        qg = _group_rows(q, g).astype(BF16)
        s_loc = jnp.where(valid, _dot_nt(qg, kl[:, sl].astype(BF16)), NEG_INF)
        s_ctx = _dot_nt(qg, kc[:, sl].astype(BF16))
        sink = _sink_column(sink_ref, g, BLK)
        m = jnp.maximum(jnp.maximum(jnp.max(s_loc, axis=-1, keepdims=True),
                                    jnp.max(s_ctx, axis=-1, keepdims=True)), sink)
        e_loc = jnp.exp(s_loc - m)
        e_ctx = jnp.exp(s_ctx - m)
        den = (jnp.sum(e_loc, axis=-1, keepdims=True) + jnp.sum(e_ctx, axis=-1, keepdims=True)
               + jnp.exp(sink - m))
        o = _dot(e_loc.astype(BF16), vl[:, sl].astype(BF16)) + _dot(e_ctx.astype(BF16), vc[:, sl].astype(BF16))
        outs.append(o / den)
    o_ref[...] = _ungroup(outs, BLK)


def _rope_tables(L):
    rows = L // GRID_W
    row = jnp.repeat(jnp.arange(rows, dtype=F32), GRID_W)
    col = jnp.tile(jnp.arange(GRID_W, dtype=F32), rows)
    pairs = HD // 4
    inv = ROPE_BASE ** (-jnp.arange(pairs, dtype=F32) / pairs)
    ang = jnp.concatenate([row[:, None] * inv, col[:, None] * inv], axis=-1)
    cos = jnp.cos(ang)
    sin = jnp.sin(ang)
    return jnp.concatenate([cos, cos], axis=-1), jnp.concatenate([-sin, sin], axis=-1)


def _latent_attention(q, k, v, kc, vc, sink, tables, t_ctx, B, L):
    assert t_ctx % L == 0
    P = kc.shape[1]
    cq, sq, ck, sk = tables
    nbk = L // BLK
    seq = pl.BlockSpec((L, KV_W), lambda b, n: (t_ctx // L + b, 0))
    ctx = pl.BlockSpec((None, P, KV_W), lambda b, n: (b, 0, 0))
    return pl.pallas_call(
        _lat_attn_kernel,
        grid=(B, nbk),
        in_specs=[
            pl.BlockSpec(memory_space=pltpu.SMEM),
            pl.BlockSpec((BLK, ATT_W), lambda b, n: (t_ctx // BLK + b * nbk + n, 0)),
            seq, seq, ctx, ctx,
            pl.BlockSpec((BLK, ATT_W), lambda b, n: (n, 0)),
            pl.BlockSpec((BLK, ATT_W), lambda b, n: (n, 0)),
            pl.BlockSpec((L, KV_W), lambda b, n: (0, 0)),
            pl.BlockSpec((L, KV_W), lambda b, n: (0, 0)),
        ],
        out_specs=pl.BlockSpec((BLK, ATT_W), lambda b, n: (b * nbk + n, 0)),
        out_shape=jax.ShapeDtypeStruct((B * L, ATT_W), F32),
        compiler_params=_cp(("arbitrary", "arbitrary")),
        name="latent_attention",
    )(sink, q, k, v, kc, vc, cq, sq, ck, sk)


def _conv3(u, prev_row, next_row, w):
    tl = u.shape[0]
    ridx = lax.broadcasted_iota(jnp.int32, (tl, 1), 0)
    dn = jnp.where(ridx == 0, prev_row, pltpu.roll(u, 1, 0))
    up = jnp.where(ridx == tl - 1, next_row, pltpu.roll(u, tl - 1, 0))
    return dn * w[0:1] + u * w[1:2] + up * w[2:3]


def _store_time_major_inner(dst_ref, src, scr_ref):
    n2 = dst_ref.shape[0]
    rows = src.shape[0] // n2
    for h in range(scr_ref.shape[0]):
        scr_ref[h] = src[:, h * 128:(h + 1) * 128]
    for j in range(n2):
        for h in range(scr_ref.shape[0]):
            dst_ref[j, :, h * 128:(h + 1) * 128] = scr_ref[h, pl.ds(j, rows, stride=n2), :]


def _mixprep_kernel(uh_ref, uhp_ref, uhn_ref, us_ref, usp_ref, usn_ref, hw_ref, hb_ref, sw_ref,
                    x0_ref, g_ref, gt_ref, osc_ref, scr_ref):
    t = pl.program_id(1)
    first = t == 0
    last = t == pl.num_programs(1) - 1
    c = HY_W
    hw = hw_ref[...]
    uh = _conv3(uh_ref[...], jnp.where(first, 0.0, uhp_ref[7:8]), jnp.where(last, 0.0, uhn_ref[0:1]), hw)
    uh = uh + hb_ref[...]
    x0_ref[...] = uh[:, :c]
    g = uh[:, c:2 * c] * uh[:, 2 * c:]
    g_ref[...] = g
    _store_time_major_inner(gt_ref, g, scr_ref)
    us = us_ref[...]
    usp = usp_ref[7:8]
    usn = usn_ref[0:1]
    prod = us[:, c:2 * c] * us[:, 2 * c:]
    pprev = jnp.where(first, 0.0, usp[:, c:2 * c] * usp[:, 2 * c:])
    pnext = jnp.where(last, 0.0, usn[:, c:2 * c] * usn[:, 2 * c:])
    osc_ref[...] = us[:, :c] * _conv3(prod, pprev, pnext, sw_ref[...])


def _mixer_prep(u_hy, u_sc, hy_conv_w, hy_conv_b, sc_conv_w, row0, B, L):
    W = u_hy.shape[1]
    n1, n2 = _fft_split(L)
    tl = 8 * n2
    assert row0 % tl == 0 and L % tl == 0
    nt = L // tl
    r8 = tl // 8
    base = lambda b: row0 // 8 + b * (L // 8)
    main = pl.BlockSpec((tl, W), lambda b, t: (row0 // tl + b * nt + t, 0))
    prev = pl.BlockSpec((8, W), lambda b, t: (base(b) + jnp.maximum(t * r8 - 1, 0), 0))
    nxt = pl.BlockSpec((8, W), lambda b, t: (base(b) + jnp.minimum((t + 1) * r8, L // 8 - 1), 0))
    const = lambda shape: pl.BlockSpec(shape, lambda b, t: (0, 0))
    out = pl.BlockSpec((tl, HY_W), lambda b, t: (b * nt + t, 0))
    flat = jax.ShapeDtypeStruct((B * L, HY_W), F32)
    return pl.pallas_call(
        _mixprep_kernel,
        grid=(B, nt),
        in_specs=[main, prev, nxt, main, prev, nxt, const((3, W)), const((1, W)), const((3, SC_W))],
        out_specs=[out, out, pl.BlockSpec((None, n2, 8, HY_W), lambda b, t: (b, 0, t, 0)), out],
        out_shape=[flat, flat, jax.ShapeDtypeStruct((B, n2, n1 // 2, HY_W), F32), flat],
        scratch_shapes=[pltpu.VMEM((HY_W // 128, tl, 128), F32)],
        compiler_params=_cp(("arbitrary", "arbitrary")),
        name="mixer_prep",
    )(u_hy, u_hy, u_hy, u_sc, u_sc, u_sc, hy_conv_w, hy_conv_b.reshape(1, W), sc_conv_w)


def _filter_kernel(z_ref, w1_ref, b1_ref, f1_ref, w2_ref, b2_ref, f2_ref, w3_ref, dl_ref, o_ref, scr_ref):
    tl = z_ref.shape[0]
    z = z_ref[...]
    h = jnp.sin(f1_ref[...] * (_dot(z, w1_ref[...], HI) + b1_ref[...]))
    h = jnp.sin(f2_ref[...] * (_dot(h, w2_ref[...], HI) + b2_ref[...]))
    h = _dot(h, w3_ref[...], HI) * jnp.exp(-z[:, 0:1] * jnp.abs(dl_ref[...]))
    row = pl.program_id(0) * tl + lax.broadcasted_iota(jnp.int32, (tl, 1), 0)
    _store_time_major_inner(o_ref.at[0], h[:, :HY_W], scr_ref)
    _store_time_major_inner(o_ref.at[1], jnp.where(row == 0, 0.0, h[:, HY_W:]), scr_ref)


def _filter_features(L):
    t = jnp.linspace(0.0, 1.0, L, dtype=F32)[:, None]
    w = 2.0 * math.pi * jnp.arange(L, dtype=F32)[:, None] / L
    bands = jnp.linspace(1e-4, FILTER_BANDS - 1, FILTER_BANDS, dtype=F32)[None, :]
    z = jnp.concatenate([t, jnp.cos(bands * w), -jnp.sin(bands * w)], axis=-1)
    return jnp.pad(z, ((0, 0), (0, 128 - FILTER_EMB)))


def _hyena_filters(L, w1, b1, f1, w2, b2, f2, w3, deltas):
    n1, n2 = _fft_split(L)
    tl = 8 * n2
    z = _filter_features(L)
    w1p = jnp.pad(w1, ((0, 128 - FILTER_EMB), (0, 0)))
    const = lambda shape: pl.BlockSpec(shape, lambda i: (0, 0))
    H = FILTER_HIDDEN
    return pl.pallas_call(
        _filter_kernel,
        grid=(L // tl,),
        in_specs=[pl.BlockSpec((tl, 128), lambda i: (i, 0)), const((128, H)), const((1, H)), const((1, H)),
                  const((H, H)), const((1, H)), const((1, H)), const((H, 2 * HY_W)), const((1, 2 * HY_W))],
        out_specs=pl.BlockSpec((2, n2, 8, HY_W), lambda i: (0, 0, i, 0)),
        out_shape=jax.ShapeDtypeStruct((2, n2, n1 // 2, HY_W), F32),
        scratch_shapes=[pltpu.VMEM((HY_W // 128, tl, 128), F32)],
        compiler_params=_cp(("arbitrary",)),
        name="hyena_filters",
    )(z, w1p, b1.reshape(1, H), f1.reshape(1, H), w2, b2.reshape(1, H), f2.reshape(1, H), w3,
      deltas.reshape(1, 2 * HY_W))


def _fft_split(L):
    n = 2 * L
    n1 = 128 if n >= 8192 else 16
    return n1, n // n1


def _fft_constants(L):
    n1, n2 = _fft_split(L)
    n = n1 * n2
    k1 = jnp.arange(n1, dtype=jnp.int32)
    a = (k1[:, None] * jnp.arange(n1 // 2, dtype=jnp.int32)[None, :]) % n1
    th = (2.0 * math.pi / n1) * a.astype(F32)
    f1 = jnp.concatenate([jnp.cos(th), -jnp.sin(th)], axis=0)
    f3 = jnp.concatenate([jnp.cos(th).T, -jnp.sin(th).T], axis=1) * (1.0 / n)
    k = k1[:, None, None] + n1 * jnp.arange(n2, dtype=jnp.int32)[None, :, None]
    ph = (k * jnp.arange(n2, dtype=jnp.int32)[None, None, :]) % n
    ph = (2.0 * math.pi / n) * ph.astype(F32)
    mr, mi = jnp.cos(ph), -jnp.sin(ph)
    mf = jnp.concatenate([jnp.concatenate([mr, -mi], axis=2), jnp.concatenate([mi, mr], axis=2)], axis=1)
    mrt, mit = jnp.swapaxes(mr, 1, 2), jnp.swapaxes(mi, 1, 2)
    mb = jnp.concatenate([jnp.concatenate([mrt, mit], axis=2), jnp.concatenate([-mit, mrt], axis=2)], axis=1)
    return f1, f3, mf, mb


FFT_GROUP = 8


def _dft_rows_kernel(f_ref, x_ref, o_ref):
    for i in range(FFT_GROUP):
        o_ref[:, i, :] = _dot(f_ref[...], x_ref[i], HI)


def _dft_rows(f1, xt):
    B, n2, kh, C = xt.shape
    m = f1.shape[0]
    return pl.pallas_call(
        _dft_rows_kernel,
        grid=(B, n2 // FFT_GROUP),
        in_specs=[pl.BlockSpec((m, kh), lambda b, j: (0, 0)),
                  pl.BlockSpec((None, FFT_GROUP, kh, C), lambda b, j: (b, j, 0, 0))],
        out_specs=pl.BlockSpec((None, m, FFT_GROUP, C), lambda b, j: (b, 0, j, 0)),
        out_shape=jax.ShapeDtypeStruct((B, m, n2, C), F32),
        compiler_params=_cp(("arbitrary", "arbitrary")),
        name="fft_rows",
    )(f1, xt)


def _fft_mid_kernel(a_ref, hf_ref, hb_ref, mf_ref, mb_ref, o_ref, kf_ref):
    n2 = a_ref.shape[2]

    @pl.when(pl.program_id(1) == 0)
    def _():
        for i in range(FFT_GROUP):
            xf = _dot(mf_ref[i], jnp.concatenate([hf_ref[0, i], hf_ref[1, i]], axis=0), HI)
            xb = _dot(mf_ref[i], jnp.concatenate([hb_ref[0, i], hb_ref[1, i]], axis=0), HI)
            kf_ref[i, 0] = xf[:n2] + xb[:n2]
            kf_ref[i, 1] = xf[n2:] - xb[n2:]

    for i in range(FFT_GROUP):
        x = _dot(mf_ref[i], jnp.concatenate([a_ref[0, i], a_ref[1, i]], axis=0), HI)
        xr, xi = x[:n2], x[n2:]
        kr, ki = kf_ref[i, 0], kf_ref[i, 1]
        y = jnp.concatenate([xr * kr - xi * ki, xr * ki + xi * kr], axis=0)
        o = _dot(mb_ref[i], y, HI)
        o_ref[:, 0, i, :] = o[:n2]
        o_ref[:, 1, i, :] = o[n2:]


def _fft_mid(a, ah, mf, mb):
    B, _, n1, n2, C = a.shape
    blk = (None, 2, FFT_GROUP, n2, C)
    return pl.pallas_call(
        _fft_mid_kernel,
        grid=(n1 // FFT_GROUP, B),
        in_specs=[
            pl.BlockSpec(blk, lambda k, b: (b, 0, k, 0, 0)),
            pl.BlockSpec(blk, lambda k, b: (0, 0, k, 0, 0)),
            pl.BlockSpec(blk, lambda k, b: (1, 0, k, 0, 0)),
            pl.BlockSpec((FFT_GROUP, 2 * n2, 2 * n2), lambda k, b: (k, 0, 0)),
            pl.BlockSpec((FFT_GROUP, 2 * n2, 2 * n2), lambda k, b: (k, 0, 0)),
        ],
        out_specs=pl.BlockSpec((None, n2, 2, FFT_GROUP, C), lambda k, b: (b, 0, 0, k, 0)),
        out_shape=jax.ShapeDtypeStruct((B, n2, 2, n1, C), F32),
        scratch_shapes=[pltpu.VMEM((FFT_GROUP, 2, n2, C), F32)],
        compiler_params=_cp(("arbitrary", "arbitrary")),
        name="fft_mid",
    )(a, ah, ah, mf, mb)


def _idft_rows_kernel(f_ref, b_ref, x0_ref, g_ref, bias_ref, o_ref):
    for i in range(FFT_GROUP):
        y = _dot(f_ref[...], b_ref[i], HI)
        o_ref[:, i, :] = x0_ref[:, i, :] * (y + g_ref[:, i, :] * bias_ref[...])


def _idft_rows_gate(f3, bo, x0, g, bias):
    B, n2, m2, C = bo.shape
    kh = f3.shape[0]
    tile = pl.BlockSpec((None, kh, FFT_GROUP, C), lambda b, j: (b, 0, j, 0))
    return pl.pallas_call(
        _idft_rows_kernel,
        grid=(B, n2 // FFT_GROUP),
        in_specs=[pl.BlockSpec((kh, m2), lambda b, j: (0, 0)),
                  pl.BlockSpec((None, FFT_GROUP, m2, C), lambda b, j: (b, j, 0, 0)),
                  tile, tile, pl.BlockSpec((1, C), lambda b, j: (0, 0))],
        out_specs=tile,
        out_shape=jax.ShapeDtypeStruct((B, kh, n2, C), F32),
        compiler_params=_cp(("arbitrary", "arbitrary")),
        name="ifft_rows_gate",
    )(f3, bo, x0, g, bias)


def _hyena_long_conv(x0, g, gt, filt_t, hy_bias, consts, B, L):
    C = HY_W
    n1, n2 = _fft_split(L)
    f1, f3, mf, mb = consts
    a = _dft_rows(f1, gt).reshape(B, 2, n1, n2, C)
    ah = _dft_rows(f1, filt_t).reshape(2, 2, n1, n2, C)
    bo = _fft_mid(a, ah, mf, mb).reshape(B, n2, 2 * n1, C)
    nat = lambda z: z.reshape(B, n1 // 2, n2, C)
    out = _idft_rows_gate(f3, bo, nat(x0), nat(g), hy_bias.reshape(1, C))
    return out.reshape(B * L, C)


def _outproj_kernel(oac_ref, oal_ref, ohc_ref, ohl_ref, osc_ref, osl_ref, xc_ref, xl_ref, mod_ref, w_ref,
                    lg_ref, lb_ref, rw_ref, rb_ref, x1_ref, h2_ref, meta_ref, cnt_ref, *, nct):
    m = mod_ref[0, 0]
    w = w_ref[...]
    is_ctx = pl.program_id(0) < nct
    pick = lambda c_ref, l_ref: jnp.where(is_ctx, c_ref[...], l_ref[...])
    o = (_dot(pick(oac_ref, oal_ref).astype(BF16), w[:ATT_W])
         + _dot(pick(ohc_ref, ohl_ref).astype(BF16), w[ATT_W:ATT_W + HY_W])
         + _dot(pick(osc_ref, osl_ref).astype(BF16), w[ATT_W + HY_W:]))
    x1 = _layer_norm(DN_ALPHA * pick(xc_ref, xl_ref) + m[2:3] * o, lg_ref[...], lb_ref[...])
    x1_ref[...] = x1
    h2 = x1 * (1.0 + m[4:5]) + m[3:4]
    h2_ref[...] = h2.astype(BF16)
    lane = lax.broadcasted_iota(jnp.int32, (h2.shape[0], 128), 1)
    logits = jnp.where(lane < N_EXP, _dot(h2, rw_ref[...], HI) + rb_ref[...], NEG_INF)
    picks = []
    vals = []
    for _ in range(TOP_K):
        v = jnp.max(logits, axis=-1, keepdims=True)
        idx = jnp.min(jnp.where(logits == v, lane, 128), axis=-1, keepdims=True)
        hit = lane == idx
        picks.append(hit)
        vals.append(v)
        logits = jnp.where(hit, NEG_INF, logits)
    es = [jnp.exp(v - vals[0]) for v in vals]
    den = es[0] + es[1] + es[2] + es[3]
    tb = h2.shape[0]
    msel = jnp.zeros(logits.shape, F32)
    for hit in picks:
        msel = msel + jnp.where(hit, 1.0, 0.0)
    cnt = jnp.sum(msel, axis=0, keepdims=True)
    pc = jnp.floor((cnt + (ROW_PAD - 1.0)) * (1.0 / ROW_PAD)) * ROW_PAD
    upper = jnp.where(lax.broadcasted_iota(jnp.int32, (128, 128), 0)
                      < lax.broadcasted_iota(jnp.int32, (128, 128), 1), 1.0, 0.0)
    lo = _dot(jnp.broadcast_to(pc, (8, 128)), upper, HI)[0:1]
    lower = jnp.where(lax.broadcasted_iota(jnp.int32, (tb, tb), 1)
                      < lax.broadcasted_iota(jnp.int32, (tb, tb), 0), 1.0, 0.0).astype(BF16)
    pos = lo + _dot(lower, msel.astype(BF16))
    meta = jnp.zeros(logits.shape, F32)
    for kk, (hit, e) in enumerate(zip(picks, es)):
        slot = jnp.sum(jnp.where(hit, pos, 0.0), axis=-1, keepdims=True)
        meta = jnp.where(lane == kk, slot, meta)
        meta = jnp.where(lane == TOP_K + kk, e / den, meta)
    meta_ref[...] = meta
    cnt_ref[0] = jnp.broadcast_to(cnt, (8, 128))


def _out_projection(o_attn, o_hy, o_sc, x, mod, w_out_bf, ln_g, ln_b, router_w, router_b, layer, dec_seq):
    t_ctx = x[0].shape[0]
    T = t_ctx + x[1].shape[0]
    tm = TOKEN_BLOCK
    nct = t_ctx // tm
    row = lambda i: (layer, _mod_row(i, tm, t_ctx, dec_seq), 0, 0)
    tok = lambda w: pl.BlockSpec((tm, w), lambda i: (i, 0))
    const = lambda shape: pl.BlockSpec(shape, lambda i: (0, 0))
    rw = jnp.pad(router_w, ((0, 0), (0, 128 - N_EXP)))
    rb = jnp.pad(router_b, (0, 128 - N_EXP)).reshape(1, 128)
    return pl.pallas_call(
        functools.partial(_outproj_kernel, nct=nct),
        grid=(T // tm,),
        in_specs=[*_ctx_lat_specs((tm, ATT_W), nct), *_ctx_lat_specs((tm, HY_W), nct),
                  *_ctx_lat_specs((tm, SC_W), nct), *_ctx_lat_specs((tm, D), nct),
                  pl.BlockSpec((1, 1, 6, D), row),
                  const((D, D)), const((1, D)), const((1, D)), const((D, 128)), const((1, 128))],
        out_specs=[tok(D), tok(D), tok(128), pl.BlockSpec((1, 8, 128), lambda i: (i, 0, 0))],
        out_shape=[jax.ShapeDtypeStruct((T, D), F32), jax.ShapeDtypeStruct((T, D), BF16),
                   jax.ShapeDtypeStruct((T, 128), F32), jax.ShapeDtypeStruct((T // tm, 8, 128), F32)],
        compiler_params=_cp(("arbitrary",)),
        name="out_projection_router",
    )(*o_attn, *o_hy, *o_sc, *x, mod, w_out_bf, ln_g.reshape(1, D), ln_b.reshape(1, D), rw, rb)


def _routing_tables(cnt):
    nb = cnt.shape[0]
    pc = (cnt + ROW_PAD - 1) // ROW_PAD * ROW_PAD
    lo = jnp.cumsum(pc, axis=1) - pc
    tot = jnp.sum(pc, axis=0)
    tot_pad = (tot + EXPERT_TILE - 1) // EXPERT_TILE * EXPERT_TILE
    off = jnp.cumsum(tot_pad) - tot_pad
    gstart = off[None, :] + jnp.cumsum(pc, axis=0) - pc
    ntile = tot_pad // EXPERT_TILE
    cum = jnp.cumsum(ntile)
    n_tiles = _max_expert_tiles(nb)
    i = jnp.arange(n_tiles, dtype=jnp.int32)
    ic = jnp.maximum(jnp.minimum(i, cum[-1] - 1), 0)
    te = jnp.sum((cum[None, :] <= ic[:, None]).astype(jnp.int32), axis=1)
    first = ic == (cum - ntile)[te]
    flags = jnp.where(i < cum[-1], 1 + 2 * first.astype(jnp.int32), 0)
    i32 = lambda a: a.astype(jnp.int32).reshape(-1)
    return dict(pc=i32(pc), lo=i32(lo), gstart=i32(gstart), tail_start=i32(off + tot), tail_len=i32(tot_pad - tot),
                tile_expert=te, tile_row=ic, tile_flags=flags)


def _max_expert_tiles(nb):
    rows = TOP_K * nb * TOKEN_BLOCK + nb * N_EXP * (ROW_PAD - 1) + N_EXP * (EXPERT_TILE - 1)
    return rows // EXPERT_TILE + 1


def _run_copies(n, src_ref, src0, dst_ref, dst0, sem, wait):
    done = jnp.int32(0)
    for p in RUN_PIECES:
        take = (n & p) != 0

        @pl.when(take)
        def _():
            cp = pltpu.make_async_copy(src_ref.at[pl.ds(pl.multiple_of(src0 + done, ROW_PAD), p), :],
                                       dst_ref.at[pl.ds(pl.multiple_of(dst0 + done, ROW_PAD), p), :], sem)
            if wait:
                cp.wait()
            else:
                cp.start()

        done = done + jnp.where(take, p, 0)


def _dispatch_kernel(pc_ref, lo_ref, gs_ref, ts_ref, tl_ref, h_ref, meta_ref, xs_ref, buf_ref, zero_ref, sem):
    b = pl.program_id(0)
    s = buf_ref.shape[0]

    @pl.when(b == 0)
    def _():
        zero_ref[...] = jnp.zeros_like(zero_ref)
        for wait in (False, True):
            def tail(e, carry):
                _run_copies(tl_ref[e], zero_ref, 0, xs_ref, ts_ref[e], sem, wait)
                return carry
            lax.fori_loop(0, N_EXP, tail, 0)

    mt = meta_ref[...].T
    srow = lax.broadcasted_iota(jnp.int32, (s, 1), 0).astype(F32)
    perm = jnp.zeros((s, TOKEN_BLOCK), F32)
    gperm = jnp.zeros((s, TOKEN_BLOCK), F32)
    for k in range(TOP_K):
        hit = srow == mt[k:k + 1]
        perm = perm + jnp.where(hit, 1.0, 0.0)
        gperm = gperm + jnp.where(hit, mt[TOP_K + k:TOP_K + k + 1], 0.0)
    buf_ref[:, :D] = _dot(perm.astype(BF16), h_ref[...])
    buf_ref[:, D:] = jnp.broadcast_to(jnp.sum(gperm, axis=-1, keepdims=True), (s, 128))
    for wait in (False, True):
        def run(e, carry):
            j = b * N_EXP + e
            _run_copies(pc_ref[j], buf_ref, lo_ref[j], xs_ref, gs_ref[j], sem, wait)
            return carry
        lax.fori_loop(0, N_EXP, run, 0)


def _dispatch(tabs, h2, meta):
    T = h2.shape[0]
    nb = T // TOKEN_BLOCK
    rows = _max_expert_tiles(nb) * EXPERT_TILE
    tok = lambda w: pl.BlockSpec((TOKEN_BLOCK, w), lambda b, *_: (b, 0))
    return pl.pallas_call(
        _dispatch_kernel,
        grid_spec=pltpu.PrefetchScalarGridSpec(
            num_scalar_prefetch=5,
            grid=(nb,),
            in_specs=[tok(D), tok(128)],
            out_specs=pl.BlockSpec(memory_space=pl.ANY),
            scratch_shapes=[pltpu.VMEM((LOCAL_SLOTS, D + 128), F32), pltpu.VMEM((EXPERT_TILE, D + 128), F32),
                            pltpu.SemaphoreType.DMA(())],
        ),
        out_shape=jax.ShapeDtypeStruct((rows, D + 128), F32),
        compiler_params=_cp(("arbitrary",)),
        name="moe_dispatch",
    )(tabs["pc"], tabs["lo"], tabs["gstart"], tabs["tail_start"], tabs["tail_len"], h2, meta)


def _expert_ffn_kernel(te_ref, tr_ref, fl_ref, x_ref, wg_ref, bg_ref, wu_ref, bu_ref, wd_ref, bd_ref, y_ref, wbf_ref):
    fl = fl_ref[pl.program_id(0)]

    @pl.when((fl & 1) != 0)
    def _():
        @pl.when((fl & 2) != 0)
        def _():
            wbf_ref[0] = wg_ref[0, 0].astype(BF16)
            wbf_ref[1] = wu_ref[0, 0].astype(BF16)
            wbf_ref[2] = wd_ref[0, 0].astype(BF16)

        x = x_ref[:, :D].astype(BF16)
        gate = x_ref[:, D:D + 1]
        g = jnp.minimum(_dot(x, wbf_ref[0]) + bg_ref[0], SWIGLU_LIMIT)
        u = jnp.clip(_dot(x, wbf_ref[1]) + bu_ref[0], -SWIGLU_LIMIT, SWIGLU_LIMIT)
        a = (u + 1.0) * (g * jax.nn.sigmoid(SWIGLU_ALPHA * g))
        y_ref[...] = gate * (_dot(a.astype(BF16), wbf_ref[2]) + bd_ref[0])


def _expert_ffn(tabs, xs, w_gate, b_gate, w_up, b_up, w_down, b_down, layer):
    rows = xs.shape[0]
    wspec = pl.BlockSpec((1, 1, D, D), lambda i, te, tr, fl: (layer, te[i], 0, 0))
    bspec = pl.BlockSpec((1, 1, D), lambda i, te, tr, fl: (te[i], 0, 0))
    return pl.pallas_call(
        _expert_ffn_kernel,
        grid_spec=pltpu.PrefetchScalarGridSpec(
            num_scalar_prefetch=3,
            grid=(rows // EXPERT_TILE,),
            in_specs=[pl.BlockSpec((EXPERT_TILE, D + 128), lambda i, te, tr, fl: (tr[i], 0)),
                      wspec, bspec, wspec, bspec, wspec, bspec],
            out_specs=pl.BlockSpec((EXPERT_TILE, D), lambda i, te, tr, fl: (tr[i], 0)),
            scratch_shapes=[pltpu.VMEM((3, D, D), BF16)],
        ),
        out_shape=jax.ShapeDtypeStruct((rows, D), F32),
        compiler_params=_cp(("arbitrary",)),
        name="expert_ffn",
    )(tabs["tile_expert"], tabs["tile_row"], tabs["tile_flags"], xs,
      w_gate, b_gate[layer].reshape(N_EXP, 1, D), w_up, b_up[layer].reshape(N_EXP, 1, D),
      w_down, b_down[layer].reshape(N_EXP, 1, D))


def _combine_kernel(pc_ref, lo_ref, gs_ref, ys_ref, meta_ref, x1_ref, mod_ref, lg_ref, lb_ref, oc_ref, ol_ref,
                    buf_ref, sem, *, nct):
    b = pl.program_id(0)
    s = buf_ref.shape[0]

    @pl.when(b == 0)
    def _():
        buf_ref[...] = jnp.zeros_like(buf_ref)

    for wait in (False, True):
        def run(e, carry):
            j = b * N_EXP + e
            _run_copies(pc_ref[j], ys_ref, gs_ref[j], buf_ref, lo_ref[j], sem, wait)
            return carry
        lax.fori_loop(0, N_EXP, run, 0)

    meta = meta_ref[...]
    scol = lax.broadcasted_iota(jnp.int32, (1, s), 1).astype(F32)
    w = jnp.zeros((TOKEN_BLOCK, s), F32)
    for k in range(TOP_K):
        w = w + jnp.where(meta[:, k:k + 1] == scol, 1.0, 0.0)
    w = w.astype(BF16)
    y = buf_ref[...]
    y_hi = y.astype(BF16)
    y_lo = (y - y_hi.astype(F32)).astype(BF16)
    moe = _dot(w, y_hi) + _dot(w, y_lo)
    m = mod_ref[0, 0]
    out = _layer_norm(DN_ALPHA * x1_ref[...] + m[5:6] * moe, lg_ref[...], lb_ref[...])

    @pl.when(b < nct)
    def _():
        oc_ref[...] = out

    @pl.when(b >= nct)
    def _():
        ol_ref[...] = out


def _combine_ln2(tabs, ys, meta, x1, mod, ln_g, ln_b, layer, t_ctx, dec_seq):
    T = x1.shape[0]
    nct = t_ctx // TOKEN_BLOCK
    row = lambda b, *_: (layer, _mod_row(b, TOKEN_BLOCK, t_ctx, dec_seq), 0, 0)
    tok = lambda w: pl.BlockSpec((TOKEN_BLOCK, w), lambda b, *_: (b, 0))
    const = lambda shape: pl.BlockSpec(shape, lambda b, *_: (0, 0))
    return pl.pallas_call(
        functools.partial(_combine_kernel, nct=nct),
        grid_spec=pltpu.PrefetchScalarGridSpec(
            num_scalar_prefetch=3,
            grid=(T // TOKEN_BLOCK,),
            in_specs=[pl.BlockSpec(memory_space=pl.ANY), tok(128), tok(D), pl.BlockSpec((1, 1, 6, D), row),
                      const((1, D)), const((1, D))],
            out_specs=list(_ctx_lat_specs((TOKEN_BLOCK, D), nct)),
            scratch_shapes=[pltpu.VMEM((LOCAL_SLOTS, D), F32), pltpu.SemaphoreType.DMA(())],
        ),
        out_shape=[jax.ShapeDtypeStruct((t_ctx, D), F32), jax.ShapeDtypeStruct((T - t_ctx, D), F32)],
        compiler_params=_cp(("arbitrary",)),
        name="moe_combine_ln2",
    )(tabs["pc"], tabs["lo"], tabs["gstart"], ys, meta, x1, mod, ln_g.reshape(1, D), ln_b.reshape(1, D))


def _moe_ln2(h2, meta, cnt, x1, mod, w_gate, b_gate, w_up, b_up, w_down, b_down, ln_g, ln_b, layer, t_ctx, dec_seq):
    tabs = _routing_tables(cnt[:, 0, :N_EXP].astype(jnp.int32))
    xs = _dispatch(tabs, h2, meta)
    ys = _expert_ffn(tabs, xs, w_gate, b_gate, w_up, b_up, w_down, b_down, layer)
    return _combine_ln2(tabs, ys, meta, x1, mod, ln_g, ln_b, layer, t_ctx, dec_seq)


def kernel(x_prompt, x_sample, cache_k, cache_v, c, c_ctx, w_mod, b_mod, w_in, w_out, attn_sink, hy_conv_w, hy_conv_b, hy_w1, hy_b1, hy_f1, hy_w2, hy_b2, hy_f2, hy_w3, hy_deltas, hy_bias, sc_conv_w, ln1_g, ln1_b, router_w, router_b, w_gate, b_gate, w_up, b_up, w_down, b_down, ln2_g, ln2_b):
    nb, seq, _ = x_prompt.shape
    db, dseq, _ = x_sample.shape
    past = cache_k.shape[2]
    t_ctx = nb * seq
    x = (x_prompt.reshape(t_ctx, D), x_sample.reshape(db * dseq, D))

    cond = jnp.zeros((8, D), F32).at[0].set(c_ctx).at[1:1 + db].set(c)
    mod = _modulation(cond, w_mod, b_mod)

    cos64, sin64 = _rope_tables(dseq)
    tables = (jnp.tile(cos64, (1, N_HEADS)), jnp.tile(sin64, (1, N_HEADS)),
              jnp.tile(cos64, (1, N_KV)), jnp.tile(sin64, (1, N_KV)))
    fft_c = {L: _fft_constants(L) for L in (seq, dseq)}
    w_in_bf = w_in.astype(BF16)
    w_out_bf = w_out.astype(BF16)

    new_k, new_v = [], []
    for l in range(DEPTH):
        q, k, v, u_hy, u_sc = _in_projection(x[0], x[1], mod, w_in_bf[l], l, dseq)
        new_k.append(k[:t_ctx].reshape(nb, seq, N_KV, HD))
        new_v.append(v[:t_ctx].reshape(nb, seq, N_KV, HD))
        oa_ctx = _context_attention(q, k, v, attn_sink[l], nb, seq)
        oa_lat = _latent_attention(q, k, v, cache_k[:, l].reshape(db, past, KV_W),
                                   cache_v[:, l].reshape(db, past, KV_W), attn_sink[l], tables, t_ctx, db, dseq)
        o_hy, o_sc = [], []
        for (B, L, row0) in ((nb, seq, 0), (db, dseq, t_ctx)):
            x0, g, gt, osc = _mixer_prep(u_hy, u_sc, hy_conv_w[l], hy_conv_b[l], sc_conv_w[l], row0, B, L)
            filt_t = _hyena_filters(L, hy_w1[l], hy_b1[l], hy_f1[l], hy_w2[l], hy_b2[l], hy_f2[l], hy_w3[l],
                                    hy_deltas[l])
            o_hy.append(_hyena_long_conv(x0, g, gt, filt_t, hy_bias[l], fft_c[L], B, L))
            o_sc.append(osc)
        x1, h2, meta, cnt = _out_projection((oa_ctx, oa_lat), o_hy, o_sc, x, mod, w_out_bf[l], ln1_g[l], ln1_b[l],
                                            router_w[l], router_b[l], l, dseq)
        x = _moe_ln2(h2, meta, cnt, x1, mod, w_gate, b_gate, w_up, b_up, w_down, b_down, ln2_g[l], ln2_b[l],
                     l, t_ctx, dseq)

    y_prompt = x[0].reshape(nb, seq, D)
    y_sample = x[1].reshape(db, dseq, D)
    return (y_prompt, y_sample, jnp.stack(new_k, axis=1), jnp.stack(new_v, axis=1))
```

```python
import functools
import math

import jax
import jax.numpy as jnp
from jax import lax
from jax.experimental import pallas as pl
from jax.experimental.pallas import tpu as pltpu

F32 = jnp.float32
BF16 = jnp.bfloat16
HI = lax.Precision.HIGHEST

D = 1024
DEPTH = 2
N_HEADS = 8
N_KV = 2
HD = 64
GQA = N_HEADS // N_KV
ATT_W = N_HEADS * HD
KV_W = N_KV * HD
WINDOW = 128
BLK = 128
GRID_W = 64
ROPE_BASE = 10000.0
HY_W = 256
SC_W = 256
FILTER_EMB = 33
FILTER_BANDS = 16
FILTER_HIDDEN = 64
IN_W = ATT_W + 2 * KV_W + 3 * HY_W + 3 * SC_W
N_EXP = 32
TOP_K = 4
SWIGLU_LIMIT = 7.0
SWIGLU_ALPHA = 1.702
DN_ALPHA = (2 * DEPTH) ** 0.25
LN_EPS = 1e-5
NEG_INF = float("-inf")

VMEM_LIMIT = 56 * 1024 * 1024

TOKEN_BLOCK = 256
ROW_PAD = 8
RUN_PIECES = (256, 128, 64, 32, 16, 8)
EXPERT_TILE = 512
LOCAL_SLOTS = TOP_K * TOKEN_BLOCK + N_EXP * ROW_PAD


def _cp(sem):
    return pltpu.CompilerParams(dimension_semantics=sem, vmem_limit_bytes=VMEM_LIMIT)


def _dot(a, b, precision=None):
    return jnp.dot(a, b, preferred_element_type=F32, precision=precision)


def _dot_nt(a, b):
    return lax.dot_general(a, b, (((1,), (1,)), ((), ())), preferred_element_type=F32)


def _layer_norm(z, g, b):
    mu = jnp.mean(z, axis=-1, keepdims=True)
    zc = z - mu
    var = jnp.mean(zc * zc, axis=-1, keepdims=True)
    return zc * lax.rsqrt(var + LN_EPS) * g + b


def _mod_kernel(cond_ref, w_ref, b_ref, o_ref):
    c = cond_ref[...]
    s = c * jax.nn.sigmoid(c)
    o_ref[0] = _dot(s, w_ref[0], HI) + b_ref[0]


def _modulation(cond, w_mod, b_mod):
    tn = 1024
    out = pl.pallas_call(
        _mod_kernel,
        grid=(DEPTH, 6 * D // tn),
        in_specs=[
            pl.BlockSpec((8, D), lambda l, j: (0, 0)),
            pl.BlockSpec((1, D, tn), lambda l, j: (l, 0, j)),
            pl.BlockSpec((1, 1, tn), lambda l, j: (l, 0, j)),
        ],
        out_specs=pl.BlockSpec((1, 8, tn), lambda l, j: (l, 0, j)),
        out_shape=jax.ShapeDtypeStruct((DEPTH, 8, 6 * D), F32),
        compiler_params=_cp(("arbitrary", "arbitrary")),
        name="modulation",
    )(cond, w_mod, b_mod.reshape(DEPTH, 1, 6 * D))
    return out.reshape(DEPTH, 8, 6, D)


def _mod_row(i, tm, t_ctx, dec_seq):
    nct = t_ctx // tm
    return jnp.where(i < nct, 0, 1 + (i - nct) // (dec_seq // tm))


def _ctx_lat_specs(shape, nct, ctx_offset=0, lat_offset=0):
    ctx = pl.BlockSpec(shape, lambda i, *_: (jnp.minimum(i, nct - 1) + ctx_offset, 0))
    lat = pl.BlockSpec(shape, lambda i, *_: (jnp.maximum(i - nct, 0) + lat_offset, 0))
    return ctx, lat


def _inproj_kernel(xc_ref, xl_ref, mod_ref, w_ref, q_ref, k_ref, v_ref, uh_ref, us_ref, *, nct):
    m = mod_ref[0, 0]
    x = jnp.where(pl.program_id(0) < nct, xc_ref[...], xl_ref[...])
    h = x * (1.0 + m[1:2]) + m[0:1]
    y = _dot(h.astype(BF16), w_ref[...])
    o0 = ATT_W
    o1 = o0 + KV_W
    o2 = o1 + KV_W
    o3 = o2 + 3 * HY_W
    q_ref[...] = y[:, :o0]
    k_ref[...] = y[:, o0:o1]
    v_ref[...] = y[:, o1:o2]
    uh_ref[...] = y[:, o2:o3]
    us_ref[...] = y[:, o3:]


def _in_projection(xc, xl, mod, w_in_bf, layer, dec_seq):
    t_ctx = xc.shape[0]
    T = t_ctx + xl.shape[0]
    tm = 512
    row = lambda i: (layer, _mod_row(i, tm, t_ctx, dec_seq), 0, 0)
    tok = lambda w: pl.BlockSpec((tm, w), lambda i: (i, 0))
    widths = (ATT_W, KV_W, KV_W, 3 * HY_W, 3 * SC_W)
    return pl.pallas_call(
        functools.partial(_inproj_kernel, nct=t_ctx // tm),
        grid=(T // tm,),
        in_specs=[
            *_ctx_lat_specs((tm, D), t_ctx // tm),
            pl.BlockSpec((1, 1, 6, D), row),
            pl.BlockSpec((D, IN_W), lambda i: (0, 0)),
        ],
        out_specs=[tok(w) for w in widths],
        out_shape=[jax.ShapeDtypeStruct((T, w), F32) for w in widths],
        compiler_params=_cp(("arbitrary",)),
        name="in_projection",
    )(xc, xl, mod, w_in_bf)


def _swap_halves(x):
    w = x.shape[-1]
    lane = lax.broadcasted_iota(jnp.int32, x.shape, 1)
    return jnp.where((lane % HD) < HD // 2, pltpu.roll(x, w - HD // 2, 1), pltpu.roll(x, HD // 2, 1))


def _rope(x, cos, sin_signed):
    return x * cos + _swap_halves(x) * sin_signed


def _group_rows(q, g):
    return jnp.concatenate([q[:, (GQA * g + r) * HD:(GQA * g + r + 1) * HD] for r in range(GQA)], axis=0)


def _sink_column(sink_ref, g, rows):
    ridx = lax.broadcasted_iota(jnp.int32, (GQA * rows, 1), 0)
    col = jnp.full((GQA * rows, 1), sink_ref[GQA * g + GQA - 1], F32)
    for r in range(GQA - 2, -1, -1):
        col = jnp.where(ridx < (r + 1) * rows, sink_ref[GQA * g + r], col)
    return col


def _ungroup(outs, rows):
    return jnp.concatenate([o[r * rows:(r + 1) * rows] for o in outs for r in range(GQA)], axis=1)


def _ctx_attn_kernel(sink_ref, q_ref, k_ref, v_ref, o_ref):
    rows = q_ref.shape[0]
    q = q_ref[...] * (HD ** -0.5)
    k = k_ref[...]
    v = v_ref[...]
    outs = []
    for g in range(N_KV):
        qg = _group_rows(q, g).astype(BF16)
        kg = k[:, g * HD:(g + 1) * HD].astype(BF16)
        vg = v[:, g * HD:(g + 1) * HD].astype(BF16)
        s = _dot_nt(qg, kg)
        sink = _sink_column(sink_ref, g, rows)
        m = jnp.maximum(jnp.max(s, axis=-1, keepdims=True), sink)
        e = jnp.exp(s - m)
        den = jnp.sum(e, axis=-1, keepdims=True) + jnp.exp(sink - m)
        outs.append(_dot(e.astype(BF16), vg) / den)
    o_ref[...] = _ungroup(outs, rows)


def _context_attention(q, k, v, sink, B, L):
    tok = lambda w: pl.BlockSpec((L, w), lambda b: (b, 0))
    return pl.pallas_call(
        _ctx_attn_kernel,
        grid=(B,),
        in_specs=[pl.BlockSpec(memory_space=pltpu.SMEM), tok(ATT_W), tok(KV_W), tok(KV_W)],
        out_specs=tok(ATT_W),
        out_shape=jax.ShapeDtypeStruct((B * L, ATT_W), F32),
        compiler_params=_cp(("arbitrary",)),
        name="context_attention",
    )(sink, q, k, v)


def _lat_attn_kernel(sink_ref, q_ref, k_ref, v_ref, kc_ref, vc_ref, cq_ref, sq_ref, ck_ref, sk_ref, o_ref):
    n = pl.program_id(1)
    L = k_ref.shape[0]
    nk = 3 * BLK
    ws = pl.multiple_of(jnp.clip((n - 1) * BLK, 0, L - nk), BLK)
    q = _rope(q_ref[...], cq_ref[...], sq_ref[...]) * (HD ** -0.5)
    kl = _rope(k_ref[pl.ds(ws, nk), :], ck_ref[pl.ds(ws, nk), :], sk_ref[pl.ds(ws, nk), :])
    vl = v_ref[pl.ds(ws, nk), :]
    kc = kc_ref[...]
    vc = vc_ref[...]
    qpos = n * BLK + lax.broadcasted_iota(jnp.int32, (GQA * BLK, 1), 0) % BLK
    kpos = ws + lax.broadcasted_iota(jnp.int32, (1, nk), 1)
    valid = jnp.abs(kpos - qpos) <= WINDOW
    outs = []
    for g in range(N_KV):
        sl = slice(g * HD, (g + 1) * HD)
        qg = _group_rows(q, g).astype(BF16)
        s_loc = jnp.where(valid, _dot_nt(qg, kl[:, sl].astype(BF16)), NEG_INF)
        s_ctx = _dot_nt(qg, kc[:, sl].astype(BF16))
        sink = _sink_column(sink_ref, g, BLK)
        m = jnp.maximum(jnp.maximum(jnp.max(s_loc, axis=-1, keepdims=True),
                                    jnp.max(s_ctx, axis=-1, keepdims=True)), sink)
        e_loc = jnp.exp(s_loc - m)
        e_ctx = jnp.exp(s_ctx - m)
        den = (jnp.sum(e_loc, axis=-1, keepdims=True) + jnp.sum(e_ctx, axis=-1, keepdims=True)
               + jnp.exp(sink - m))
        o = _dot(e_loc.astype(BF16), vl[:, sl].astype(BF16)) + _dot(e_ctx.astype(BF16), vc[:, sl].astype(BF16))
        outs.append(o / den)
    o_ref[...] = _ungroup(outs, BLK)


def _rope_tables(L):
    rows = L // GRID_W
    row = jnp.repeat(jnp.arange(rows, dtype=F32), GRID_W)
    col = jnp.tile(jnp.arange(GRID_W, dtype=F32), rows)
    pairs = HD // 4
    inv = ROPE_BASE ** (-jnp.arange(pairs, dtype=F32) / pairs)
    ang = jnp.concatenate([row[:, None] * inv, col[:, None] * inv], axis=-1)
    cos = jnp.cos(ang)
    sin = jnp.sin(ang)
    return jnp.concatenate([cos, cos], axis=-1), jnp.concatenate([-sin, sin], axis=-1)


def _latent_attention(q, k, v, kc, vc, sink, tables, t_ctx, B, L):
    assert t_ctx % L == 0
    P = kc.shape[1]
    cq, sq, ck, sk = tables
    nbk = L // BLK
    seq = pl.BlockSpec((L, KV_W), lambda b, n: (t_ctx // L + b, 0))
    ctx = pl.BlockSpec((None, P, KV_W), lambda b, n: (b, 0, 0))
    return pl.pallas_call(
        _lat_attn_kernel,
        grid=(B, nbk),
        in_specs=[
            pl.BlockSpec(memory_space=pltpu.SMEM),
            pl.BlockSpec((BLK, ATT_W), lambda b, n: (t_ctx // BLK + b * nbk + n, 0)),
            seq, seq, ctx, ctx,
            pl.BlockSpec((BLK, ATT_W), lambda b, n: (n, 0)),
            pl.BlockSpec((BLK, ATT_W), lambda b, n: (n, 0)),
            pl.BlockSpec((L, KV_W), lambda b, n: (0, 0)),
            pl.BlockSpec((L, KV_W), lambda b, n: (0, 0)),
        ],
        out_specs=pl.BlockSpec((BLK, ATT_W), lambda b, n: (b * nbk + n, 0)),
        out_shape=jax.ShapeDtypeStruct((B * L, ATT_W), F32),
        compiler_params=_cp(("arbitrary", "arbitrary")),
        name="latent_attention",
    )(sink, q, k, v, kc, vc, cq, sq, ck, sk)


def _conv3(u, prev_row, next_row, w):
    tl = u.shape[0]
    ridx = lax.broadcasted_iota(jnp.int32, (tl, 1), 0)
    dn = jnp.where(ridx == 0, prev_row, pltpu.roll(u, 1, 0))
    up = jnp.where(ridx == tl - 1, next_row, pltpu.roll(u, tl - 1, 0))
    return dn * w[0:1] + u * w[1:2] + up * w[2:3]


def _store_time_major_inner(dst_ref, src, scr_ref):
    n2 = dst_ref.shape[0]
    rows = src.shape[0] // n2
    for h in range(scr_ref.shape[0]):
        scr_ref[h] = src[:, h * 128:(h + 1) * 128]
    for j in range(n2):
        for h in range(scr_ref.shape[0]):
            dst_ref[j, :, h * 128:(h + 1) * 128] = scr_ref[h, pl.ds(j, rows, stride=n2), :]


def _mixprep_kernel(uh_ref, uhp_ref, uhn_ref, us_ref, usp_ref, usn_ref, hw_ref, hb_ref, sw_ref,
                    x0_ref, g_ref, gt_ref, osc_ref, scr_ref):
    t = pl.program_id(1)
    first = t == 0
    last = t == pl.num_programs(1) - 1
    c = HY_W
    hw = hw_ref[...]
    uh = _conv3(uh_ref[...], jnp.where(first, 0.0, uhp_ref[7:8]), jnp.where(last, 0.0, uhn_ref[0:1]), hw)
    uh = uh + hb_ref[...]
    x0_ref[...] = uh[:, :c]
    g = uh[:, c:2 * c] * uh[:, 2 * c:]
    g_ref[...] = g
    _store_time_major_inner(gt_ref, g, scr_ref)
    us = us_ref[...]
    usp = usp_ref[7:8]
    usn = usn_ref[0:1]
    prod = us[:, c:2 * c] * us[:, 2 * c:]
    pprev = jnp.where(first, 0.0, usp[:, c:2 * c] * usp[:, 2 * c:])
    pnext = jnp.where(last, 0.0, usn[:, c:2 * c] * usn[:, 2 * c:])
    osc_ref[...] = us[:, :c] * _conv3(prod, pprev, pnext, sw_ref[...])


def _mixer_prep(u_hy, u_sc, hy_conv_w, hy_conv_b, sc_conv_w, row0, B, L):
    W = u_hy.shape[1]
    n1, n2 = _fft_split(L)
    tl = 8 * n2
    assert row0 % tl == 0 and L % tl == 0
    nt = L // tl
    r8 = tl // 8
    base = lambda b: row0 // 8 + b * (L // 8)
    main = pl.BlockSpec((tl, W), lambda b, t: (row0 // tl + b * nt + t, 0))
    prev = pl.BlockSpec((8, W), lambda b, t: (base(b) + jnp.maximum(t * r8 - 1, 0), 0))
    nxt = pl.BlockSpec((8, W), lambda b, t: (base(b) + jnp.minimum((t + 1) * r8, L // 8 - 1), 0))
    const = lambda shape: pl.BlockSpec(shape, lambda b, t: (0, 0))
    out = pl.BlockSpec((tl, HY_W), lambda b, t: (b * nt + t, 0))
    flat = jax.ShapeDtypeStruct((B * L, HY_W), F32)
    return pl.pallas_call(
        _mixprep_kernel,
        grid=(B, nt),
        in_specs=[main, prev, nxt, main, prev, nxt, const((3, W)), const((1, W)), const((3, SC_W))],
        out_specs=[out, out, pl.BlockSpec((None, n2, 8, HY_W), lambda b, t: (b, 0, t, 0)), out],
        out_shape=[flat, flat, jax.ShapeDtypeStruct((B, n2, n1 // 2, HY_W), F32), flat],
        scratch_shapes=[pltpu.VMEM((HY_W // 128, tl, 128), F32)],
        compiler_params=_cp(("arbitrary", "arbitrary")),
        name="mixer_prep",
    )(u_hy, u_hy, u_hy, u_sc, u_sc, u_sc, hy_conv_w, hy_conv_b.reshape(1, W), sc_conv_w)


def _filter_kernel(z_ref, w1_ref, b1_ref, f1_ref, w2_ref, b2_ref, f2_ref, w3_ref, dl_ref, o_ref, scr_ref):
    tl = z_ref.shape[0]
    z = z_ref[...]
    h = jnp.sin(f1_ref[...] * (_dot(z, w1_ref[...], HI) + b1_ref[...]))
    h = jnp.sin(f2_ref[...] * (_dot(h, w2_ref[...], HI) + b2_ref[...]))
    h = _dot(h, w3_ref[...], HI) * jnp.exp(-z[:, 0:1] * jnp.abs(dl_ref[...]))
    row = pl.program_id(0) * tl + lax.broadcasted_iota(jnp.int32, (tl, 1), 0)
    _store_time_major_inner(o_ref.at[0], h[:, :HY_W], scr_ref)
    _store_time_major_inner(o_ref.at[1], jnp.where(row == 0, 0.0, h[:, HY_W:]), scr_ref)


def _filter_features(L):
    t = jnp.linspace(0.0, 1.0, L, dtype=F32)[:, None]
    w = 2.0 * math.pi * jnp.arange(L, dtype=F32)[:, None] / L
    bands = jnp.linspace(1e-4, FILTER_BANDS - 1, FILTER_BANDS, dtype=F32)[None, :]
    z = jnp.concatenate([t, jnp.cos(bands * w), -jnp.sin(bands * w)], axis=-1)
    return jnp.pad(z, ((0, 0), (0, 128 - FILTER_EMB)))


def _hyena_filters(L, w1, b1, f1, w2, b2, f2, w3, deltas):
    n1, n2 = _fft_split(L)
    tl = 8 * n2
    z = _filter_features(L)
    w1p = jnp.pad(w1, ((0, 128 - FILTER_EMB), (0, 0)))
    const = lambda shape: pl.BlockSpec(shape, lambda i: (0, 0))
    H = FILTER_HIDDEN
    return pl.pallas_call(
        _filter_kernel,
        grid=(L // tl,),
        in_specs=[pl.BlockSpec((tl, 128), lambda i: (i, 0)), const((128, H)), const((1, H)), const((1, H)),
                  const((H, H)), const((1, H)), const((1, H)), const((H, 2 * HY_W)), const((1, 2 * HY_W))],
        out_specs=pl.BlockSpec((2, n2, 8, HY_W), lambda i: (0, 0, i, 0)),
        out_shape=jax.ShapeDtypeStruct((2, n2, n1 // 2, HY_W), F32),
        scratch_shapes=[pltpu.VMEM((HY_W // 128, tl, 128), F32)],
        compiler_params=_cp(("arbitrary",)),
        name="hyena_filters",
    )(z, w1p, b1.reshape(1, H), f1.reshape(1, H), w2, b2.reshape(1, H), f2.reshape(1, H), w3,
      deltas.reshape(1, 2 * HY_W))


def _fft_split(L):
    n = 2 * L
    n1 = 128 if n >= 8192 else 16
    return n1, n // n1


def _fft_constants(L):
    n1, n2 = _fft_split(L)
    n = n1 * n2
    k1 = jnp.arange(n1, dtype=jnp.int32)
    a = (k1[:, None] * jnp.arange(n1 // 2, dtype=jnp.int32)[None, :]) % n1
    th = (2.0 * math.pi / n1) * a.astype(F32)
    f1 = jnp.concatenate([jnp.cos(th), -jnp.sin(th)], axis=0)
    f3 = jnp.concatenate([jnp.cos(th).T, -jnp.sin(th).T], axis=1) * (1.0 / n)
    k = k1[:, None, None] + n1 * jnp.arange(n2, dtype=jnp.int32)[None, :, None]
    ph = (k * jnp.arange(n2, dtype=jnp.int32)[None, None, :]) % n
    ph = (2.0 * math.pi / n) * ph.astype(F32)
    mr, mi = jnp.cos(ph), -jnp.sin(ph)
    mf = jnp.concatenate([jnp.concatenate([mr, -mi], axis=2), jnp.concatenate([mi, mr], axis=2)], axis=1)
    mrt, mit = jnp.swapaxes(mr, 1, 2), jnp.swapaxes(mi, 1, 2)
    mb = jnp.concatenate([jnp.concatenate([mrt, mit], axis=2), jnp.concatenate([-mit, mrt], axis=2)], axis=1)
    return f1, f3, mf, mb


FFT_GROUP = 8


def _dft_rows_kernel(f_ref, x_ref, o_ref):
    for i in range(FFT_GROUP):
        o_ref[:, i, :] = _dot(f_ref[...], x_ref[i], HI)


def _dft_rows(f1, xt):
    B, n2, kh, C = xt.shape
    m = f1.shape[0]
    return pl.pallas_call(
        _dft_rows_kernel,
        grid=(B, n2 // FFT_GROUP),
        in_specs=[pl.BlockSpec((m, kh), lambda b, j: (0, 0)),
                  pl.BlockSpec((None, FFT_GROUP, kh, C), lambda b, j: (b, j, 0, 0))],
        out_specs=pl.BlockSpec((None, m, FFT_GROUP, C), lambda b, j: (b, 0, j, 0)),
        out_shape=jax.ShapeDtypeStruct((B, m, n2, C), F32),
        compiler_params=_cp(("arbitrary", "arbitrary")),
        name="fft_rows",
    )(f1, xt)


def _fft_mid_kernel(a_ref, hf_ref, hb_ref, mf_ref, mb_ref, o_ref, kf_ref):
    n2 = a_ref.shape[2]

    @pl.when(pl.program_id(1) == 0)
    def _():
        for i in range(FFT_GROUP):
            xf = _dot(mf_ref[i], jnp.concatenate([hf_ref[0, i], hf_ref[1, i]], axis=0), HI)
            xb = _dot(mf_ref[i], jnp.concatenate([hb_ref[0, i], hb_ref[1, i]], axis=0), HI)
            kf_ref[i, 0] = xf[:n2] + xb[:n2]
            kf_ref[i, 1] = xf[n2:] - xb[n2:]

    for i in range(FFT_GROUP):
        x = _dot(mf_ref[i], jnp.concatenate([a_ref[0, i], a_ref[1, i]], axis=0), HI)
        xr, xi = x[:n2], x[n2:]
        kr, ki = kf_ref[i, 0], kf_ref[i, 1]
        y = jnp.concatenate([xr * kr - xi * ki, xr * ki + xi * kr], axis=0)
        o = _dot(mb_ref[i], y, HI)
        o_ref[:, 0, i, :] = o[:n2]
        o_ref[:, 1, i, :] = o[n2:]


def _fft_mid(a, ah, mf, mb):
    B, _, n1, n2, C = a.shape
    blk = (None, 2, FFT_GROUP, n2, C)
    return pl.pallas_call(
        _fft_mid_kernel,
        grid=(n1 // FFT_GROUP, B),
        in_specs=[
            pl.BlockSpec(blk, lambda k, b: (b, 0, k, 0, 0)),
            pl.BlockSpec(blk, lambda k, b: (0, 0, k, 0, 0)),
            pl.BlockSpec(blk, lambda k, b: (1, 0, k, 0, 0)),
            pl.BlockSpec((FFT_GROUP, 2 * n2, 2 * n2), lambda k, b: (k, 0, 0)),
            pl.BlockSpec((FFT_GROUP, 2 * n2, 2 * n2), lambda k, b: (k, 0, 0)),
        ],
        out_specs=pl.BlockSpec((None, n2, 2, FFT_GROUP, C), lambda k, b: (b, 0, 0, k, 0)),
        out_shape=jax.ShapeDtypeStruct((B, n2, 2, n1, C), F32),
        scratch_shapes=[pltpu.VMEM((FFT_GROUP, 2, n2, C), F32)],
        compiler_params=_cp(("arbitrary", "arbitrary")),
        name="fft_mid",
    )(a, ah, ah, mf, mb)


def _idft_rows_kernel(f_ref, b_ref, x0_ref, g_ref, bias_ref, o_ref):
    for i in range(FFT_GROUP):
        y = _dot(f_ref[...], b_ref[i], HI)
        o_ref[:, i, :] = x0_ref[:, i, :] * (y + g_ref[:, i, :] * bias_ref[...])


def _idft_rows_gate(f3, bo, x0, g, bias):
    B, n2, m2, C = bo.shape
    kh = f3.shape[0]
    tile = pl.BlockSpec((None, kh, FFT_GROUP, C), lambda b, j: (b, 0, j, 0))
    return pl.pallas_call(
        _idft_rows_kernel,
        grid=(B, n2 // FFT_GROUP),
        in_specs=[pl.BlockSpec((kh, m2), lambda b, j: (0, 0)),
                  pl.BlockSpec((None, FFT_GROUP, m2, C), lambda b, j: (b, j, 0, 0)),
                  tile, tile, pl.BlockSpec((1, C), lambda b, j: (0, 0))],
        out_specs=tile,
        out_shape=jax.ShapeDtypeStruct((B, kh, n2, C), F32),
        compiler_params=_cp(("arbitrary", "arbitrary")),
        name="ifft_rows_gate",
    )(f3, bo, x0, g, bias)


def _hyena_long_conv(x0, g, gt, filt_t, hy_bias, consts, B, L):
    C = HY_W
    n1, n2 = _fft_split(L)
    f1, f3, mf, mb = consts
    a = _dft_rows(f1, gt).reshape(B, 2, n1, n2, C)
    ah = _dft_rows(f1, filt_t).reshape(2, 2, n1, n2, C)
    bo = _fft_mid(a, ah, mf, mb).reshape(B, n2, 2 * n1, C)
    nat = lambda z: z.reshape(B, n1 // 2, n2, C)
    out = _idft_rows_gate(f3, bo, nat(x0), nat(g), hy_bias.reshape(1, C))
    return out.reshape(B * L, C)


def _outproj_kernel(oac_ref, oal_ref, ohc_ref, ohl_ref, osc_ref, osl_ref, xc_ref, xl_ref, mod_ref, w_ref,
                    lg_ref, lb_ref, rw_ref, rb_ref, x1_ref, h2_ref, meta_ref, cnt_ref, *, nct):
    m = mod_ref[0, 0]
    w = w_ref[...]
    is_ctx = pl.program_id(0) < nct
    pick = lambda c_ref, l_ref: jnp.where(is_ctx, c_ref[...], l_ref[...])
    o = (_dot(pick(oac_ref, oal_ref).astype(BF16), w[:ATT_W])
         + _dot(pick(ohc_ref, ohl_ref).astype(BF16), w[ATT_W:ATT_W + HY_W])
         + _dot(pick(osc_ref, osl_ref).astype(BF16), w[ATT_W + HY_W:]))
    x1 = _layer_norm(DN_ALPHA * pick(xc_ref, xl_ref) + m[2:3] * o, lg_ref[...], lb_ref[...])
    x1_ref[...] = x1
    h2 = x1 * (1.0 + m[4:5]) + m[3:4]
    h2_hi = h2.astype(BF16)
    h2_ref[...] = h2_hi
    lane = lax.broadcasted_iota(jnp.int32, (h2.shape[0], 128), 1)
    h2_lo = (h2 - h2_hi.astype(F32)).astype(BF16)
    logits = _dot(h2_hi, rw_ref[0]) + _dot(h2_lo, rw_ref[0]) + _dot(h2_hi, rw_ref[1]) + rb_ref[...]
    logits = jnp.where(lane < N_EXP, logits, NEG_INF)
    picks = []
    vals = []
    for _ in range(TOP_K):
        v = jnp.max(logits, axis=-1, keepdims=True)
        idx = jnp.min(jnp.where(logits == v, lane, 128), axis=-1, keepdims=True)
        hit = lane == idx
        picks.append(hit)
        vals.append(v)
        logits = jnp.where(hit, NEG_INF, logits)
    es = [jnp.exp(v - vals[0]) for v in vals]
    den = es[0] + es[1] + es[2] + es[3]
    tb = h2.shape[0]
    msel = jnp.zeros(logits.shape, F32)
    for hit in picks:
        msel = msel + jnp.where(hit, 1.0, 0.0)
    cnt = jnp.sum(msel, axis=0, keepdims=True)
    pc = jnp.floor((cnt + (ROW_PAD - 1.0)) * (1.0 / ROW_PAD)) * ROW_PAD
    upper = jnp.where(lax.broadcasted_iota(jnp.int32, (128, 128), 0)
                      < lax.broadcasted_iota(jnp.int32, (128, 128), 1), 1.0, 0.0)
    lo = _dot(jnp.broadcast_to(pc, (8, 128)), upper, HI)[0:1]
    lower = jnp.where(lax.broadcasted_iota(jnp.int32, (tb, tb), 1)
                      < lax.broadcasted_iota(jnp.int32, (tb, tb), 0), 1.0, 0.0).astype(BF16)
    pos = lo + _dot(lower, msel.astype(BF16))
    meta = jnp.zeros(logits.shape, F32)
    for kk, (hit, e) in enumerate(zip(picks, es)):
        slot = jnp.sum(jnp.where(hit, pos, 0.0), axis=-1, keepdims=True)
        meta = jnp.where(lane == kk, slot, meta)
        meta = jnp.where(lane == TOP_K + kk, e / den, meta)
    meta_ref[...] = meta
    cnt_ref[0] = jnp.broadcast_to(cnt, (8, 128))


def _out_projection(o_attn, o_hy, o_sc, x, mod, w_out_bf, ln_g, ln_b, router_w, router_b, layer, dec_seq):
    t_ctx = x[0].shape[0]
    T = t_ctx + x[1].shape[0]
    tm = TOKEN_BLOCK
    nct = t_ctx // tm
    row = lambda i: (layer, _mod_row(i, tm, t_ctx, dec_seq), 0, 0)
    tok = lambda w: pl.BlockSpec((tm, w), lambda i: (i, 0))
    const = lambda shape: pl.BlockSpec(shape, lambda i: (0, 0))
    rw = jnp.pad(router_w, ((0, 0), (0, 128 - N_EXP)))
    rw_hi = rw.astype(BF16)
    rw = jnp.stack([rw_hi, (rw - rw_hi.astype(F32)).astype(BF16)])
    rb = jnp.pad(router_b, (0, 128 - N_EXP)).reshape(1, 128)
    return pl.pallas_call(
        functools.partial(_outproj_kernel, nct=nct),
        grid=(T // tm,),
        in_specs=[*_ctx_lat_specs((tm, ATT_W), nct), *_ctx_lat_specs((tm, HY_W), nct),
                  *_ctx_lat_specs((tm, SC_W), nct), *_ctx_lat_specs((tm, D), nct),
                  pl.BlockSpec((1, 1, 6, D), row),
                  const((D, D)), const((1, D)), const((1, D)),
                  pl.BlockSpec((2, D, 128), lambda i: (0, 0, 0)), const((1, 128))],
        out_specs=[tok(D), tok(D), tok(128), pl.BlockSpec((1, 8, 128), lambda i: (i, 0, 0))],
        out_shape=[jax.ShapeDtypeStruct((T, D), F32), jax.ShapeDtypeStruct((T, D), BF16),
                   jax.ShapeDtypeStruct((T, 128), F32), jax.ShapeDtypeStruct((T // tm, 8, 128), F32)],
        compiler_params=_cp(("arbitrary",)),
        name="out_projection_router",
    )(*o_attn, *o_hy, *o_sc, *x, mod, w_out_bf, ln_g.reshape(1, D), ln_b.reshape(1, D), rw, rb)


def _routing_tables(cnt):
    nb = cnt.shape[0]
    pc = (cnt + ROW_PAD - 1) // ROW_PAD * ROW_PAD
    lo = jnp.cumsum(pc, axis=1) - pc
    tot = jnp.sum(pc, axis=0)
    tot_pad = (tot + EXPERT_TILE - 1) // EXPERT_TILE * EXPERT_TILE
    off = jnp.cumsum(tot_pad) - tot_pad
    gstart = off[None, :] + jnp.cumsum(pc, axis=0) - pc
    ntile = tot_pad // EXPERT_TILE
    cum = jnp.cumsum(ntile)
    n_tiles = _max_expert_tiles(nb)
    i = jnp.arange(n_tiles, dtype=jnp.int32)
    ic = jnp.maximum(jnp.minimum(i, cum[-1] - 1), 0)
    te = jnp.sum((cum[None, :] <= ic[:, None]).astype(jnp.int32), axis=1)
    first = ic == (cum - ntile)[te]
    flags = jnp.where(i < cum[-1], 1 + 2 * first.astype(jnp.int32), 0)
    i32 = lambda a: a.astype(jnp.int32).reshape(-1)
    return dict(pc=i32(pc), lo=i32(lo), gstart=i32(gstart), tail_start=i32(off + tot), tail_len=i32(tot_pad - tot),
                rows=i32(jnp.sum(pc, axis=1)), tile_expert=te, tile_row=ic, tile_flags=flags)


def _max_expert_tiles(nb):
    rows = TOP_K * nb * TOKEN_BLOCK + nb * N_EXP * (ROW_PAD - 1) + N_EXP * (EXPERT_TILE - 1)
    return rows // EXPERT_TILE + 1


def _run_copies(n, src_ref, src0, dst_ref, dst0, sem, wait):
    done = jnp.int32(0)
    for p in RUN_PIECES:
        take = (n & p) != 0

        @pl.when(take)
        def _():
            cp = pltpu.make_async_copy(src_ref.at[pl.ds(pl.multiple_of(src0 + done, ROW_PAD), p), :],
                                       dst_ref.at[pl.ds(pl.multiple_of(dst0 + done, ROW_PAD), p), :], sem)
            if wait:
                cp.wait()
            else:
                cp.start()

        done = done + jnp.where(take, p, 0)


def _wait_rows(n, vmem_ref, hbm_ref, sem, to_vmem):
    base = TOP_K * TOKEN_BLOCK

    def wait(p):
        src, dst = (hbm_ref, vmem_ref) if to_vmem else (vmem_ref, hbm_ref)
        pltpu.make_async_copy(src.at[pl.ds(0, p), :], dst.at[pl.ds(0, p), :], sem).wait()

    wait(base)
    for p in RUN_PIECES[1:]:
        pl.when(((n - base) & p) != 0)(functools.partial(wait, p))


def _dispatch_kernel(pc_ref, lo_ref, gs_ref, ts_ref, tl_ref, rows_ref, h_ref, meta_ref, xs_ref,
                     buf_ref, zero_ref, sem, zsem):
    b = pl.program_id(0)
    nb = pl.num_programs(0)
    s = buf_ref.shape[1]
    slot = b % 2

    @pl.when(b == 0)
    def _():
        zero_ref[...] = jnp.zeros_like(zero_ref)
        for wait in (False, True):
            def tail(e, carry):
                _run_copies(tl_ref[e], zero_ref, 0, xs_ref, ts_ref[e], zsem, wait)
                return carry
            lax.fori_loop(0, N_EXP, tail, 0)

    mt = meta_ref[...].T
    srow = lax.broadcasted_iota(jnp.int32, (s, 1), 0).astype(F32)
    perm = jnp.zeros((s, TOKEN_BLOCK), F32)
    gperm = jnp.zeros((s, TOKEN_BLOCK), F32)
    for k in range(TOP_K):
        hit = srow == mt[k:k + 1]
        perm = perm + jnp.where(hit, 1.0, 0.0)
        gperm = gperm + jnp.where(hit, mt[TOP_K + k:TOP_K + k + 1], 0.0)
    sorted_rows = _dot(perm.astype(BF16), h_ref[...])
    gate_col = jnp.sum(gperm, axis=-1, keepdims=True)

    @pl.when(b >= 2)
    def _():
        _wait_rows(rows_ref[b - 2], buf_ref.at[slot], xs_ref, sem.at[slot], to_vmem=False)

    buf = buf_ref.at[slot]
    buf[:, :D] = sorted_rows
    buf[:, D:] = jnp.broadcast_to(gate_col, (s, 128))

    def run(e, carry):
        j = b * N_EXP + e
        _run_copies(pc_ref[j], buf, lo_ref[j], xs_ref, gs_ref[j], sem.at[slot], wait=False)
        return carry
    lax.fori_loop(0, N_EXP, run, 0)

    @pl.when(b == nb - 1)
    def _():
        @pl.when(b >= 1)
        def _():
            _wait_rows(rows_ref[b - 1], buf_ref.at[1 - slot], xs_ref, sem.at[1 - slot], to_vmem=False)
        _wait_rows(rows_ref[b], buf, xs_ref, sem.at[slot], to_vmem=False)


def _dispatch(tabs, h2, meta):
    T = h2.shape[0]
    nb = T // TOKEN_BLOCK
    rows = _max_expert_tiles(nb) * EXPERT_TILE
    tok = lambda w: pl.BlockSpec((TOKEN_BLOCK, w), lambda b, *_: (b, 0))
    return pl.pallas_call(
        _dispatch_kernel,
        grid_spec=pltpu.PrefetchScalarGridSpec(
            num_scalar_prefetch=6,
            grid=(nb,),
            in_specs=[tok(D), tok(128)],
            out_specs=pl.BlockSpec(memory_space=pl.ANY),
            scratch_shapes=[pltpu.VMEM((2, LOCAL_SLOTS, D + 128), F32), pltpu.VMEM((EXPERT_TILE, D + 128), F32),
                            pltpu.SemaphoreType.DMA((2,)), pltpu.SemaphoreType.DMA(())],
        ),
        out_shape=jax.ShapeDtypeStruct((rows, D + 128), F32),
        compiler_params=_cp(("arbitrary",)),
        name="moe_dispatch",
    )(tabs["pc"], tabs["lo"], tabs["gstart"], tabs["tail_start"], tabs["tail_len"], tabs["rows"], h2, meta)


def _expert_ffn_kernel(te_ref, tr_ref, fl_ref, x_ref, wg_ref, bg_ref, wu_ref, bu_ref, wd_ref, bd_ref, y_ref, wbf_ref):
    fl = fl_ref[pl.program_id(0)]

    @pl.when((fl & 1) != 0)
    def _():
        @pl.when((fl & 2) != 0)
        def _():
            wbf_ref[0] = wg_ref[0, 0].astype(BF16)
            wbf_ref[1] = wu_ref[0, 0].astype(BF16)
            wbf_ref[2] = wd_ref[0, 0].astype(BF16)

        x = x_ref[:, :D].astype(BF16)
        gate = x_ref[:, D:D + 1]
        g = jnp.minimum(_dot(x, wbf_ref[0]) + bg_ref[0], SWIGLU_LIMIT)
        u = jnp.clip(_dot(x, wbf_ref[1]) + bu_ref[0], -SWIGLU_LIMIT, SWIGLU_LIMIT)
        a = (u + 1.0) * (g * jax.nn.sigmoid(SWIGLU_ALPHA * g))
        y_ref[...] = gate * (_dot(a.astype(BF16), wbf_ref[2]) + bd_ref[0])


def _expert_ffn(tabs, xs, w_gate, b_gate, w_up, b_up, w_down, b_down, layer):
    rows = xs.shape[0]
    wspec = pl.BlockSpec((1, 1, D, D), lambda i, te, tr, fl: (layer, te[i], 0, 0))
    bspec = pl.BlockSpec((1, 1, D), lambda i, te, tr, fl: (te[i], 0, 0))
    return pl.pallas_call(
        _expert_ffn_kernel,
        grid_spec=pltpu.PrefetchScalarGridSpec(
            num_scalar_prefetch=3,
            grid=(rows // EXPERT_TILE,),
            in_specs=[pl.BlockSpec((EXPERT_TILE, D + 128), lambda i, te, tr, fl: (tr[i], 0)),
                      wspec, bspec, wspec, bspec, wspec, bspec],
            out_specs=pl.BlockSpec((EXPERT_TILE, D), lambda i, te, tr, fl: (tr[i], 0)),
            scratch_shapes=[pltpu.VMEM((3, D, D), BF16)],
        ),
        out_shape=jax.ShapeDtypeStruct((rows, D), F32),
        compiler_params=_cp(("arbitrary",)),
        name="expert_ffn",
    )(tabs["tile_expert"], tabs["tile_row"], tabs["tile_flags"], xs,
      w_gate, b_gate[layer].reshape(N_EXP, 1, D), w_up, b_up[layer].reshape(N_EXP, 1, D),
      w_down, b_down[layer].reshape(N_EXP, 1, D))


def _combine_kernel(pc_ref, lo_ref, gs_ref, rows_ref, ys_ref, meta_ref, x1_ref, mod_ref, lg_ref, lb_ref,
                    oc_ref, ol_ref, buf_ref, sem, *, nct):
    b = pl.program_id(0)
    nb = pl.num_programs(0)
    s = buf_ref.shape[1]
    slot = b % 2

    def fetch(blk, to_slot):
        def run(e, carry):
            j = blk * N_EXP + e
            _run_copies(pc_ref[j], ys_ref, gs_ref[j], buf_ref.at[to_slot], lo_ref[j], sem.at[to_slot], wait=False)
            return carry
        lax.fori_loop(0, N_EXP, run, 0)

    @pl.when(b == 0)
    def _():
        buf_ref[...] = jnp.zeros_like(buf_ref)
        fetch(b, slot)

    @pl.when(b + 1 < nb)
    def _():
        fetch(b + 1, 1 - slot)

    _wait_rows(rows_ref[b], buf_ref.at[slot], ys_ref, sem.at[slot], to_vmem=True)

    meta = meta_ref[...]
    scol = lax.broadcasted_iota(jnp.int32, (1, s), 1).astype(F32)
    w = jnp.zeros((TOKEN_BLOCK, s), F32)
    for k in range(TOP_K):
        w = w + jnp.where(meta[:, k:k + 1] == scol, 1.0, 0.0)
    w = w.astype(BF16)
    y = buf_ref[slot]
    y_hi = y.astype(BF16)
    y_lo = (y - y_hi.astype(F32)).astype(BF16)
    moe = _dot(w, y_hi) + _dot(w, y_lo)
    m = mod_ref[0, 0]
    out = _layer_norm(DN_ALPHA * x1_ref[...] + m[5:6] * moe, lg_ref[...], lb_ref[...])

    @pl.when(b < nct)
    def _():
        oc_ref[...] = out

    @pl.when(b >= nct)
    def _():
        ol_ref[...] = out


def _combine_ln2(tabs, ys, meta, x1, mod, ln_g, ln_b, layer, t_ctx, dec_seq):
    T = x1.shape[0]
    nct = t_ctx // TOKEN_BLOCK
    row = lambda b, *_: (layer, _mod_row(b, TOKEN_BLOCK, t_ctx, dec_seq), 0, 0)
    tok = lambda w: pl.BlockSpec((TOKEN_BLOCK, w), lambda b, *_: (b, 0))
    const = lambda shape: pl.BlockSpec(shape, lambda b, *_: (0, 0))
    return pl.pallas_call(
        functools.partial(_combine_kernel, nct=nct),
        grid_spec=pltpu.PrefetchScalarGridSpec(
            num_scalar_prefetch=4,
            grid=(T // TOKEN_BLOCK,),
            in_specs=[pl.BlockSpec(memory_space=pl.ANY), tok(128), tok(D), pl.BlockSpec((1, 1, 6, D), row),
                      const((1, D)), const((1, D))],
            out_specs=list(_ctx_lat_specs((TOKEN_BLOCK, D), nct)),
            scratch_shapes=[pltpu.VMEM((2, LOCAL_SLOTS, D), F32), pltpu.SemaphoreType.DMA((2,))],
        ),
        out_shape=[jax.ShapeDtypeStruct((t_ctx, D), F32), jax.ShapeDtypeStruct((T - t_ctx, D), F32)],
        compiler_params=_cp(("arbitrary",)),
        name="moe_combine_ln2",
    )(tabs["pc"], tabs["lo"], tabs["gstart"], tabs["rows"], ys, meta, x1, mod, ln_g.reshape(1, D),
      ln_b.reshape(1, D))


def _moe_ln2(h2, meta, cnt, x1, mod, w_gate, b_gate, w_up, b_up, w_down, b_down, ln_g, ln_b, layer, t_ctx, dec_seq):
    tabs = _routing_tables(cnt[:, 0, :N_EXP].astype(jnp.int32))
    xs = _dispatch(tabs, h2, meta)
    ys = _expert_ffn(tabs, xs, w_gate, b_gate, w_up, b_up, w_down, b_down, layer)
    return _combine_ln2(tabs, ys, meta, x1, mod, ln_g, ln_b, layer, t_ctx, dec_seq)


def kernel(x_prompt, x_sample, cache_k, cache_v, c, c_ctx, w_mod, b_mod, w_in, w_out, attn_sink, hy_conv_w, hy_conv_b, hy_w1, hy_b1, hy_f1, hy_w2, hy_b2, hy_f2, hy_w3, hy_deltas, hy_bias, sc_conv_w, ln1_g, ln1_b, router_w, router_b, w_gate, b_gate, w_up, b_up, w_down, b_down, ln2_g, ln2_b):
    nb, seq, _ = x_prompt.shape
    db, dseq, _ = x_sample.shape
    past = cache_k.shape[2]
    t_ctx = nb * seq
    x = (x_prompt.reshape(t_ctx, D), x_sample.reshape(db * dseq, D))

    cond = jnp.zeros((8, D), F32).at[0].set(c_ctx).at[1:1 + db].set(c)
    mod = _modulation(cond, w_mod, b_mod)

    cos64, sin64 = _rope_tables(dseq)
    tables = (jnp.tile(cos64, (1, N_HEADS)), jnp.tile(sin64, (1, N_HEADS)),
              jnp.tile(cos64, (1, N_KV)), jnp.tile(sin64, (1, N_KV)))
    fft_c = {L: _fft_constants(L) for L in (seq, dseq)}
    w_in_bf = w_in.astype(BF16)
    w_out_bf = w_out.astype(BF16)

    new_k, new_v = [], []
    for l in range(DEPTH):
        q, k, v, u_hy, u_sc = _in_projection(x[0], x[1], mod, w_in_bf[l], l, dseq)
        new_k.append(k[:t_ctx].reshape(nb, seq, N_KV, HD))
        new_v.append(v[:t_ctx].reshape(nb, seq, N_KV, HD))
        oa_ctx = _context_attention(q, k, v, attn_sink[l], nb, seq)
        oa_lat = _latent_attention(q, k, v, cache_k[:, l].reshape(db, past, KV_W),
                                   cache_v[:, l].reshape(db, past, KV_W), attn_sink[l], tables, t_ctx, db, dseq)
        o_hy, o_sc = [], []
        for (B, L, row0) in ((nb, seq, 0), (db, dseq, t_ctx)):
            x0, g, gt, osc = _mixer_prep(u_hy, u_sc, hy_conv_w[l], hy_conv_b[l], sc_conv_w[l], row0, B, L)
            filt_t = _hyena_filters(L, hy_w1[l], hy_b1[l], hy_f1[l], hy_w2[l], hy_b2[l], hy_f2[l], hy_w3[l],
                                    hy_deltas[l])
            o_hy.append(_hyena_long_conv(x0, g, gt, filt_t, hy_bias[l], fft_c[L], B, L))
            o_sc.append(osc)
        x1, h2, meta, cnt = _out_projection((oa_ctx, oa_lat), o_hy, o_sc, x, mod, w_out_bf[l], ln1_g[l], ln1_b[l],
                                            router_w[l], router_b[l], l, dseq)
        x = _moe_ln2(h2, meta, cnt, x1, mod, w_gate, b_gate, w_up, b_up, w_down, b_down, ln2_g[l], ln2_b[l],
                     l, t_ctx, dseq)

    y_prompt = x[0].reshape(nb, seq, D)
    y_sample = x[1].reshape(db, dseq, D)
    return (y_prompt, y_sample, jnp.stack(new_k, axis=1), jnp.stack(new_v, axis=1))
```

```python
import functools
import math

import jax
import jax.numpy as jnp
from jax import lax
from jax.experimental import pallas as pl
from jax.experimental.pallas import tpu as pltpu

F32 = jnp.float32
BF16 = jnp.bfloat16
HI = lax.Precision.HIGHEST

D = 1024
DEPTH = 2
N_HEADS = 8
N_KV = 2
HD = 64
GQA = N_HEADS // N_KV
ATT_W = N_HEADS * HD
KV_W = N_KV * HD
WINDOW = 128
BLK = 128
GRID_W = 64
ROPE_BASE = 10000.0
HY_W = 256
SC_W = 256
FILTER_EMB = 33
FILTER_BANDS = 16
FILTER_HIDDEN = 64
IN_W = ATT_W + 2 * KV_W + 3 * HY_W + 3 * SC_W
N_EXP = 32
TOP_K = 4
SWIGLU_LIMIT = 7.0
SWIGLU_ALPHA = 1.702
DN_ALPHA = (2 * DEPTH) ** 0.25
LN_EPS = 1e-5
NEG_INF = float("-inf")

VMEM_LIMIT = 56 * 1024 * 1024

TOKEN_BLOCK = 256
ROW_PAD = 8
RUN_PIECES = (256, 128, 64, 32, 16, 8)
EXPERT_TILE = 512
LOCAL_SLOTS = TOP_K * TOKEN_BLOCK + N_EXP * ROW_PAD


def _cp(sem):
    return pltpu.CompilerParams(dimension_semantics=sem, vmem_limit_bytes=VMEM_LIMIT)


def _dot(a, b, precision=None):
    return jnp.dot(a, b, preferred_element_type=F32, precision=precision)


def _dot_nt(a, b):
    return lax.dot_general(a, b, (((1,), (1,)), ((), ())), preferred_element_type=F32)


def _layer_norm(z, g, b):
    mu = jnp.mean(z, axis=-1, keepdims=True)
    zc = z - mu
    var = jnp.mean(zc * zc, axis=-1, keepdims=True)
    return zc * lax.rsqrt(var + LN_EPS) * g + b


def _mod_kernel(cond_ref, w_ref, b_ref, o_ref):
    c = cond_ref[...]
    s = c * jax.nn.sigmoid(c)
    o_ref[0] = _dot(s, w_ref[0], HI) + b_ref[0]


def _modulation(cond, w_mod, b_mod):
    tn = 1024
    out = pl.pallas_call(
        _mod_kernel,
        grid=(DEPTH, 6 * D // tn),
        in_specs=[
            pl.BlockSpec((8, D), lambda l, j: (0, 0)),
            pl.BlockSpec((1, D, tn), lambda l, j: (l, 0, j)),
            pl.BlockSpec((1, 1, tn), lambda l, j: (l, 0, j)),
        ],
        out_specs=pl.BlockSpec((1, 8, tn), lambda l, j: (l, 0, j)),
        out_shape=jax.ShapeDtypeStruct((DEPTH, 8, 6 * D), F32),
        compiler_params=_cp(("arbitrary", "arbitrary")),
        name="modulation",
    )(cond, w_mod, b_mod.reshape(DEPTH, 1, 6 * D))
    return out.reshape(DEPTH, 8, 6, D)


def _mod_row(i, tm, t_ctx, dec_seq):
    nct = t_ctx // tm
    return jnp.where(i < nct, 0, 1 + (i - nct) // (dec_seq // tm))


def _ctx_lat_specs(shape, nct, ctx_offset=0, lat_offset=0):
    ctx = pl.BlockSpec(shape, lambda i, *_: (jnp.minimum(i, nct - 1) + ctx_offset, 0))
    lat = pl.BlockSpec(shape, lambda i, *_: (jnp.maximum(i - nct, 0) + lat_offset, 0))
    return ctx, lat


def _inproj_kernel(xc_ref, xl_ref, mod_ref, w_ref, q_ref, k_ref, v_ref, uh_ref, us_ref, *, nct):
    m = mod_ref[0, 0]
    x = jnp.where(pl.program_id(0) < nct, xc_ref[...], xl_ref[...])
    h = x * (1.0 + m[1:2]) + m[0:1]
    y = _dot(h.astype(BF16), w_ref[...])
    o0 = ATT_W
    o1 = o0 + KV_W
    o2 = o1 + KV_W
    o3 = o2 + 3 * HY_W
    q_ref[...] = y[:, :o0]
    k_ref[...] = y[:, o0:o1]
    v_ref[...] = y[:, o1:o2]
    uh_ref[...] = y[:, o2:o3]
    us_ref[...] = y[:, o3:]


def _in_projection(xc, xl, mod, w_in_bf, layer, dec_seq):
    t_ctx = xc.shape[0]
    T = t_ctx + xl.shape[0]
    tm = 512
    row = lambda i: (layer, _mod_row(i, tm, t_ctx, dec_seq), 0, 0)
    tok = lambda w: pl.BlockSpec((tm, w), lambda i: (i, 0))
    widths = (ATT_W, KV_W, KV_W, 3 * HY_W, 3 * SC_W)
    return pl.pallas_call(
        functools.partial(_inproj_kernel, nct=t_ctx // tm),
        grid=(T // tm,),
        in_specs=[
            *_ctx_lat_specs((tm, D), t_ctx // tm),
            pl.BlockSpec((1, 1, 6, D), row),
            pl.BlockSpec((D, IN_W), lambda i: (0, 0)),
        ],
        out_specs=[tok(w) for w in widths],
        out_shape=[jax.ShapeDtypeStruct((T, w), F32) for w in widths],
        compiler_params=_cp(("arbitrary",)),
        name="in_projection",
    )(xc, xl, mod, w_in_bf)


def _swap_halves(x):
    w = x.shape[-1]
    lane = lax.broadcasted_iota(jnp.int32, x.shape, 1)
    return jnp.where((lane % HD) < HD // 2, pltpu.roll(x, w - HD // 2, 1), pltpu.roll(x, HD // 2, 1))


def _rope(x, cos, sin_signed):
    return x * cos + _swap_halves(x) * sin_signed


def _group_rows(q, g):
    return jnp.concatenate([q[:, (GQA * g + r) * HD:(GQA * g + r + 1) * HD] for r in range(GQA)], axis=0)


def _sink_column(sink_ref, g, rows):
    ridx = lax.broadcasted_iota(jnp.int32, (GQA * rows, 1), 0)
    col = jnp.full((GQA * rows, 1), sink_ref[GQA * g + GQA - 1], F32)
    for r in range(GQA - 2, -1, -1):
        col = jnp.where(ridx < (r + 1) * rows, sink_ref[GQA * g + r], col)
    return col


def _ungroup(outs, rows):
    return jnp.concatenate([o[r * rows:(r + 1) * rows] for o in outs for r in range(GQA)], axis=1)


def _ctx_attn_kernel(sink_ref, q_ref, k_ref, v_ref, o_ref):
    rows = q_ref.shape[0]
    q = q_ref[...] * (HD ** -0.5)
    k = k_ref[...]
    v = v_ref[...]
    outs = []
    for g in range(N_KV):
        qg = _group_rows(q, g).astype(BF16)
        kg = k[:, g * HD:(g + 1) * HD].astype(BF16)
        vg = v[:, g * HD:(g + 1) * HD].astype(BF16)
        s = _dot_nt(qg, kg)
        sink = _sink_column(sink_ref, g, rows)
        m = jnp.maximum(jnp.max(s, axis=-1, keepdims=True), sink)
        e = jnp.exp(s - m)
        den = jnp.sum(e, axis=-1, keepdims=True) + jnp.exp(sink - m)
        outs.append(_dot(e.astype(BF16), vg) / den)
    o_ref[...] = _ungroup(outs, rows)


def _context_attention(q, k, v, sink, B, L):
    tok = lambda w: pl.BlockSpec((L, w), lambda b: (b, 0))
    return pl.pallas_call(
        _ctx_attn_kernel,
        grid=(B,),
        in_specs=[pl.BlockSpec(memory_space=pltpu.SMEM), tok(ATT_W), tok(KV_W), tok(KV_W)],
        out_specs=tok(ATT_W),
        out_shape=jax.ShapeDtypeStruct((B * L, ATT_W), F32),
        compiler_params=_cp(("arbitrary",)),
        name="context_attention",
    )(sink, q, k, v)


def _lat_attn_kernel(sink_ref, q_ref, k_ref, v_ref, kc_ref, vc_ref, cq_ref, sq_ref, ck_ref, sk_ref, o_ref):
    n = pl.program_id(1)
    L = k_ref.shape[0]
    nk = 3 * BLK
    ws = pl.multiple_of(jnp.clip((n - 1) * BLK, 0, L - nk), BLK)
    q = _rope(q_ref[...], cq_ref[...], sq_ref[...]) * (HD ** -0.5)
    kl = _rope(k_ref[pl.ds(ws, nk), :], ck_ref[pl.ds(ws, nk), :], sk_ref[pl.ds(ws, nk), :])
    vl = v_ref[pl.ds(ws, nk), :]
    kc = kc_ref[...]
    vc = vc_ref[...]
    qpos = n * BLK + lax.broadcasted_iota(jnp.int32, (GQA * BLK, 1), 0) % BLK
    kpos = ws + lax.broadcasted_iota(jnp.int32, (1, nk), 1)
    valid = jnp.abs(kpos - qpos) <= WINDOW
    outs = []
    for g in range(N_KV):
        sl = slice(g * HD, (g + 1) * HD)
        qg = _group_rows(q, g).astype(BF16)
        s_loc = jnp.where(valid, _dot_nt(qg, kl[:, sl].astype(BF16)), NEG_INF)
        s_ctx = _dot_nt(qg, kc[:, sl].astype(BF16))
        sink = _sink_column(sink_ref, g, BLK)
        m = jnp.maximum(jnp.maximum(jnp.max(s_loc, axis=-1, keepdims=True),
                                    jnp.max(s_ctx, axis=-1, keepdims=True)), sink)
        e_loc = jnp.exp(s_loc - m)
        e_ctx = jnp.exp(s_ctx - m)
        den = (jnp.sum(e_loc, axis=-1, keepdims=True) + jnp.sum(e_ctx, axis=-1, keepdims=True)
               + jnp.exp(sink - m))
        o = _dot(e_loc.astype(BF16), vl[:, sl].astype(BF16)) + _dot(e_ctx.astype(BF16), vc[:, sl].astype(BF16))
        outs.append(o / den)
    o_ref[...] = _ungroup(outs, BLK)


def _rope_tables(L):
    rows = L // GRID_W
    row = jnp.repeat(jnp.arange(rows, dtype=F32), GRID_W)
    col = jnp.tile(jnp.arange(GRID_W, dtype=F32), rows)
    pairs = HD // 4
    inv = ROPE_BASE ** (-jnp.arange(pairs, dtype=F32) / pairs)
    ang = jnp.concatenate([row[:, None] * inv, col[:, None] * inv], axis=-1)
    cos = jnp.cos(ang)
    sin = jnp.sin(ang)
    return jnp.concatenate([cos, cos], axis=-1), jnp.concatenate([-sin, sin], axis=-1)


def _latent_attention(q, k, v, kc, vc, sink, tables, t_ctx, B, L):
    assert t_ctx % L == 0
    P = kc.shape[1]
    cq, sq, ck, sk = tables
    nbk = L // BLK
    seq = pl.BlockSpec((L, KV_W), lambda b, n: (t_ctx // L + b, 0))
    ctx = pl.BlockSpec((None, P, KV_W), lambda b, n: (b, 0, 0))
    return pl.pallas_call(
        _lat_attn_kernel,
        grid=(B, nbk),
        in_specs=[
            pl.BlockSpec(memory_space=pltpu.SMEM),
            pl.BlockSpec((BLK, ATT_W), lambda b, n: (t_ctx // BLK + b * nbk + n, 0)),
            seq, seq, ctx, ctx,
            pl.BlockSpec((BLK, ATT_W), lambda b, n: (n, 0)),
            pl.BlockSpec((BLK, ATT_W), lambda b, n: (n, 0)),
            pl.BlockSpec((L, KV_W), lambda b, n: (0, 0)),
            pl.BlockSpec((L, KV_W), lambda b, n: (0, 0)),
        ],
        out_specs=pl.BlockSpec((BLK, ATT_W), lambda b, n: (b * nbk + n, 0)),
        out_shape=jax.ShapeDtypeStruct((B * L, ATT_W), F32),
        compiler_params=_cp(("arbitrary", "arbitrary")),
        name="latent_attention",
    )(sink, q, k, v, kc, vc, cq, sq, ck, sk)


def _conv3(u, prev_row, next_row, w):
    tl = u.shape[0]
    ridx = lax.broadcasted_iota(jnp.int32, (tl, 1), 0)
    dn = jnp.where(ridx == 0, prev_row, pltpu.roll(u, 1, 0))
    up = jnp.where(ridx == tl - 1, next_row, pltpu.roll(u, tl - 1, 0))
    return dn * w[0:1] + u * w[1:2] + up * w[2:3]


def _store_time_major_inner(dst_ref, src, scr_ref):
    n2 = dst_ref.shape[0]
    rows = src.shape[0] // n2
    for h in range(scr_ref.shape[0]):
        scr_ref[h] = src[:, h * 128:(h + 1) * 128]
    for j in range(n2):
        for h in range(scr_ref.shape[0]):
            dst_ref[j, :, h * 128:(h + 1) * 128] = scr_ref[h, pl.ds(j, rows, stride=n2), :]


def _mixprep_kernel(uh_ref, uhp_ref, uhn_ref, us_ref, usp_ref, usn_ref, hw_ref, hb_ref, sw_ref,
                    x0_ref, g_ref, gt_ref, osc_ref, scr_ref):
    t = pl.program_id(1)
    first = t == 0
    last = t == pl.num_programs(1) - 1
    c = HY_W
    hw = hw_ref[...]
    uh = _conv3(uh_ref[...], jnp.where(first, 0.0, uhp_ref[7:8]), jnp.where(last, 0.0, uhn_ref[0:1]), hw)
    uh = uh + hb_ref[...]
    x0_ref[...] = uh[:, :c]
    g = uh[:, c:2 * c] * uh[:, 2 * c:]
    g_ref[...] = g
    _store_time_major_inner(gt_ref, g, scr_ref)
    us = us_ref[...]
    usp = usp_ref[7:8]
    usn = usn_ref[0:1]
    prod = us[:, c:2 * c] * us[:, 2 * c:]
    pprev = jnp.where(first, 0.0, usp[:, c:2 * c] * usp[:, 2 * c:])
    pnext = jnp.where(last, 0.0, usn[:, c:2 * c] * usn[:, 2 * c:])
    osc_ref[...] = us[:, :c] * _conv3(prod, pprev, pnext, sw_ref[...])


def _mixer_prep(u_hy, u_sc, hy_conv_w, hy_conv_b, sc_conv_w, row0, B, L):
    W = u_hy.shape[1]
    n1, n2 = _fft_split(L)
    tl = 8 * n2
    assert row0 % tl == 0 and L % tl == 0
    nt = L // tl
    r8 = tl // 8
    base = lambda b: row0 // 8 + b * (L // 8)
    main = pl.BlockSpec((tl, W), lambda b, t: (row0 // tl + b * nt + t, 0))
    prev = pl.BlockSpec((8, W), lambda b, t: (base(b) + jnp.maximum(t * r8 - 1, 0), 0))
    nxt = pl.BlockSpec((8, W), lambda b, t: (base(b) + jnp.minimum((t + 1) * r8, L // 8 - 1), 0))
    const = lambda shape: pl.BlockSpec(shape, lambda b, t: (0, 0))
    out = pl.BlockSpec((tl, HY_W), lambda b, t: (b * nt + t, 0))
    flat = jax.ShapeDtypeStruct((B * L, HY_W), F32)
    return pl.pallas_call(
        _mixprep_kernel,
        grid=(B, nt),
        in_specs=[main, prev, nxt, main, prev, nxt, const((3, W)), const((1, W)), const((3, SC_W))],
        out_specs=[out, out, pl.BlockSpec((None, n2, 8, HY_W), lambda b, t: (b, 0, t, 0)), out],
        out_shape=[flat, flat, jax.ShapeDtypeStruct((B, n2, n1 // 2, HY_W), F32), flat],
        scratch_shapes=[pltpu.VMEM((HY_W // 128, tl, 128), F32)],
        compiler_params=_cp(("arbitrary", "arbitrary")),
        name="mixer_prep",
    )(u_hy, u_hy, u_hy, u_sc, u_sc, u_sc, hy_conv_w, hy_conv_b.reshape(1, W), sc_conv_w)


def _filter_kernel(z_ref, w1_ref, b1_ref, f1_ref, w2_ref, b2_ref, f2_ref, w3_ref, dl_ref, o_ref, scr_ref):
    tl = z_ref.shape[0]
    z = z_ref[...]
    h = jnp.sin(f1_ref[...] * (_dot(z, w1_ref[...], HI) + b1_ref[...]))
    h = jnp.sin(f2_ref[...] * (_dot(h, w2_ref[...], HI) + b2_ref[...]))
    h = _dot(h, w3_ref[...], HI) * jnp.exp(-z[:, 0:1] * jnp.abs(dl_ref[...]))
    row = pl.program_id(0) * tl + lax.broadcasted_iota(jnp.int32, (tl, 1), 0)
    _store_time_major_inner(o_ref.at[0], h[:, :HY_W], scr_ref)
    _store_time_major_inner(o_ref.at[1], jnp.where(row == 0, 0.0, h[:, HY_W:]), scr_ref)


def _filter_features(L):
    t = jnp.linspace(0.0, 1.0, L, dtype=F32)[:, None]
    w = 2.0 * math.pi * jnp.arange(L, dtype=F32)[:, None] / L
    bands = jnp.linspace(1e-4, FILTER_BANDS - 1, FILTER_BANDS, dtype=F32)[None, :]
    z = jnp.concatenate([t, jnp.cos(bands * w), -jnp.sin(bands * w)], axis=-1)
    return jnp.pad(z, ((0, 0), (0, 128 - FILTER_EMB)))


def _hyena_filters(L, w1, b1, f1, w2, b2, f2, w3, deltas):
    n1, n2 = _fft_split(L)
    tl = 8 * n2
    z = _filter_features(L)
    w1p = jnp.pad(w1, ((0, 128 - FILTER_EMB), (0, 0)))
    const = lambda shape: pl.BlockSpec(shape, lambda i: (0, 0))
    H = FILTER_HIDDEN
    return pl.pallas_call(
        _filter_kernel,
        grid=(L // tl,),
        in_specs=[pl.BlockSpec((tl, 128), lambda i: (i, 0)), const((128, H)), const((1, H)), const((1, H)),
                  const((H, H)), const((1, H)), const((1, H)), const((H, 2 * HY_W)), const((1, 2 * HY_W))],
        out_specs=pl.BlockSpec((2, n2, 8, HY_W), lambda i: (0, 0, i, 0)),
        out_shape=jax.ShapeDtypeStruct((2, n2, n1 // 2, HY_W), F32),
        scratch_shapes=[pltpu.VMEM((HY_W // 128, tl, 128), F32)],
        compiler_params=_cp(("arbitrary",)),
        name="hyena_filters",
    )(z, w1p, b1.reshape(1, H), f1.reshape(1, H), w2, b2.reshape(1, H), f2.reshape(1, H), w3,
      deltas.reshape(1, 2 * HY_W))


def _fft_split(L):
    n = 2 * L
    n1 = 128 if n >= 8192 else 32
    return n1, n // n1


def _fft_constants(L):
    n1, n2 = _fft_split(L)
    n = n1 * n2
    k1 = jnp.arange(n1, dtype=jnp.int32)
    a = (k1[:, None] * jnp.arange(n1 // 2, dtype=jnp.int32)[None, :]) % n1
    th = (2.0 * math.pi / n1) * a.astype(F32)
    f1 = jnp.concatenate([jnp.cos(th), -jnp.sin(th)], axis=0)
    f3 = jnp.concatenate([jnp.cos(th).T, -jnp.sin(th).T], axis=1) * (1.0 / n)
    k = k1[:, None, None] + n1 * jnp.arange(n2, dtype=jnp.int32)[None, :, None]
    ph = (k * jnp.arange(n2, dtype=jnp.int32)[None, None, :]) % n
    ph = (2.0 * math.pi / n) * ph.astype(F32)
    mr, mi = jnp.cos(ph), -jnp.sin(ph)
    mf = jnp.concatenate([jnp.concatenate([mr, -mi], axis=2), jnp.concatenate([mi, mr], axis=2)], axis=1)
    mrt, mit = jnp.swapaxes(mr, 1, 2), jnp.swapaxes(mi, 1, 2)
    mb = jnp.concatenate([jnp.concatenate([mrt, mit], axis=2), jnp.concatenate([-mit, mrt], axis=2)], axis=1)
    return tuple(c.astype(BF16) for c in (f1, f3, mf, mb))


FFT_GROUP = 8


def _dft_rows_kernel(f_ref, x_ref, o_ref):
    for i in range(FFT_GROUP):
        o_ref[:, i, :] = _dot(f_ref[...], x_ref[i].astype(BF16))


def _dft_rows(f1, xt):
    B, n2, kh, C = xt.shape
    m = f1.shape[0]
    return pl.pallas_call(
        _dft_rows_kernel,
        grid=(B, n2 // FFT_GROUP),
        in_specs=[pl.BlockSpec((m, kh), lambda b, j: (0, 0)),
                  pl.BlockSpec((None, FFT_GROUP, kh, C), lambda b, j: (b, j, 0, 0))],
        out_specs=pl.BlockSpec((None, m, FFT_GROUP, C), lambda b, j: (b, 0, j, 0)),
        out_shape=jax.ShapeDtypeStruct((B, m, n2, C), F32),
        compiler_params=_cp(("arbitrary", "arbitrary")),
        name="fft_rows",
    )(f1, xt)


def _fft_mid_kernel(a_ref, hf_ref, hb_ref, mf_ref, mb_ref, o_ref, kf_ref):
    n2 = a_ref.shape[2]

    @pl.when(pl.program_id(1) == 0)
    def _():
        for i in range(FFT_GROUP):
            xf = _dot(mf_ref[i], jnp.concatenate([hf_ref[0, i], hf_ref[1, i]], axis=0).astype(BF16))
            xb = _dot(mf_ref[i], jnp.concatenate([hb_ref[0, i], hb_ref[1, i]], axis=0).astype(BF16))
            kf_ref[i, 0] = xf[:n2] + xb[:n2]
            kf_ref[i, 1] = xf[n2:] - xb[n2:]

    for i in range(FFT_GROUP):
        x = _dot(mf_ref[i], jnp.concatenate([a_ref[0, i], a_ref[1, i]], axis=0).astype(BF16))
        xr, xi = x[:n2], x[n2:]
        kr, ki = kf_ref[i, 0], kf_ref[i, 1]
        y = jnp.concatenate([xr * kr - xi * ki, xr * ki + xi * kr], axis=0)
        o = _dot(mb_ref[i], y.astype(BF16))
        o_ref[:, 0, i, :] = o[:n2]
        o_ref[:, 1, i, :] = o[n2:]


def _fft_mid(a, ah, mf, mb):
    B, _, n1, n2, C = a.shape
    blk = (None, 2, FFT_GROUP, n2, C)
    return pl.pallas_call(
        _fft_mid_kernel,
        grid=(n1 // FFT_GROUP, B),
        in_specs=[
            pl.BlockSpec(blk, lambda k, b: (b, 0, k, 0, 0)),
            pl.BlockSpec(blk, lambda k, b: (0, 0, k, 0, 0)),
            pl.BlockSpec(blk, lambda k, b: (1, 0, k, 0, 0)),
            pl.BlockSpec((FFT_GROUP, 2 * n2, 2 * n2), lambda k, b: (k, 0, 0)),
            pl.BlockSpec((FFT_GROUP, 2 * n2, 2 * n2), lambda k, b: (k, 0, 0)),
        ],
        out_specs=pl.BlockSpec((None, n2, 2, FFT_GROUP, C), lambda k, b: (b, 0, 0, k, 0)),
        out_shape=jax.ShapeDtypeStruct((B, n2, 2, n1, C), F32),
        scratch_shapes=[pltpu.VMEM((FFT_GROUP, 2, n2, C), F32)],
        compiler_params=_cp(("arbitrary", "arbitrary")),
        name="fft_mid",
    )(a, ah, ah, mf, mb)


def _idft_rows_kernel(f_ref, b_ref, x0_ref, g_ref, bias_ref, o_ref):
    for i in range(FFT_GROUP):
        y = _dot(f_ref[...], b_ref[i].astype(BF16))
        o_ref[:, i, :] = x0_ref[:, i, :] * (y + g_ref[:, i, :] * bias_ref[...])


def _idft_rows_gate(f3, bo, x0, g, bias):
    B, n2, m2, C = bo.shape
    kh = f3.shape[0]
    tile = pl.BlockSpec((None, kh, FFT_GROUP, C), lambda b, j: (b, 0, j, 0))
    return pl.pallas_call(
        _idft_rows_kernel,
        grid=(B, n2 // FFT_GROUP),
        in_specs=[pl.BlockSpec((kh, m2), lambda b, j: (0, 0)),
                  pl.BlockSpec((None, FFT_GROUP, m2, C), lambda b, j: (b, j, 0, 0)),
                  tile, tile, pl.BlockSpec((1, C), lambda b, j: (0, 0))],
        out_specs=tile,
        out_shape=jax.ShapeDtypeStruct((B, kh, n2, C), F32),
        compiler_params=_cp(("arbitrary", "arbitrary")),
        name="ifft_rows_gate",
    )(f3, bo, x0, g, bias)


def _hyena_long_conv(x0, g, gt, filt_t, hy_bias, consts, B, L):
    C = HY_W
    n1, n2 = _fft_split(L)
    f1, f3, mf, mb = consts
    a = _dft_rows(f1, gt).reshape(B, 2, n1, n2, C)
    ah = _dft_rows(f1, filt_t).reshape(2, 2, n1, n2, C)
    bo = _fft_mid(a, ah, mf, mb).reshape(B, n2, 2 * n1, C)
    nat = lambda z: z.reshape(B, n1 // 2, n2, C)
    out = _idft_rows_gate(f3, bo, nat(x0), nat(g), hy_bias.reshape(1, C))
    return out.reshape(B * L, C)


def _outproj_kernel(oac_ref, oal_ref, ohc_ref, ohl_ref, osc_ref, osl_ref, xc_ref, xl_ref, mod_ref, w_ref,
                    lg_ref, lb_ref, rw_ref, rb_ref, x1_ref, h2_ref, meta_ref, cnt_ref, *, nct):
    m = mod_ref[0, 0]
    w = w_ref[...]
    is_ctx = pl.program_id(0) < nct
    pick = lambda c_ref, l_ref: jnp.where(is_ctx, c_ref[...], l_ref[...])
    o = (_dot(pick(oac_ref, oal_ref).astype(BF16), w[:ATT_W])
         + _dot(pick(ohc_ref, ohl_ref).astype(BF16), w[ATT_W:ATT_W + HY_W])
         + _dot(pick(osc_ref, osl_ref).astype(BF16), w[ATT_W + HY_W:]))
    x1 = _layer_norm(DN_ALPHA * pick(xc_ref, xl_ref) + m[2:3] * o, lg_ref[...], lb_ref[...])
    x1_ref[...] = x1
    h2 = x1 * (1.0 + m[4:5]) + m[3:4]
    h2_hi = h2.astype(BF16)
    h2_ref[...] = h2_hi
    lane = lax.broadcasted_iota(jnp.int32, (h2.shape[0], 128), 1)
    h2_lo = (h2 - h2_hi.astype(F32)).astype(BF16)
    logits = _dot(h2_hi, rw_ref[0]) + _dot(h2_lo, rw_ref[0]) + _dot(h2_hi, rw_ref[1]) + rb_ref[...]
    logits = jnp.where(lane < N_EXP, logits, NEG_INF)
    picks = []
    vals = []
    for _ in range(TOP_K):
        v = jnp.max(logits, axis=-1, keepdims=True)
        idx = jnp.min(jnp.where(logits == v, lane, 128), axis=-1, keepdims=True)
        hit = lane == idx
        picks.append(hit)
        vals.append(v)
        logits = jnp.where(hit, NEG_INF, logits)
    es = [jnp.exp(v - vals[0]) for v in vals]
    den = es[0] + es[1] + es[2] + es[3]
    tb = h2.shape[0]
    msel = jnp.zeros(logits.shape, F32)
    for hit in picks:
        msel = msel + jnp.where(hit, 1.0, 0.0)
    cnt = jnp.sum(msel, axis=0, keepdims=True)
    pc = jnp.floor((cnt + (ROW_PAD - 1.0)) * (1.0 / ROW_PAD)) * ROW_PAD
    upper = jnp.where(lax.broadcasted_iota(jnp.int32, (128, 128), 0)
                      < lax.broadcasted_iota(jnp.int32, (128, 128), 1), 1.0, 0.0)
    lo = _dot(jnp.broadcast_to(pc, (8, 128)), upper, HI)[0:1]
    lower = jnp.where(lax.broadcasted_iota(jnp.int32, (tb, tb), 1)
                      < lax.broadcasted_iota(jnp.int32, (tb, tb), 0), 1.0, 0.0).astype(BF16)
    pos = lo + _dot(lower, msel.astype(BF16))
    meta = jnp.zeros(logits.shape, F32)
    for kk, (hit, e) in enumerate(zip(picks, es)):
        slot = jnp.sum(jnp.where(hit, pos, 0.0), axis=-1, keepdims=True)
        meta = jnp.where(lane == kk, slot, meta)
        meta = jnp.where(lane == TOP_K + kk, e / den, meta)
    meta_ref[...] = meta
    cnt_ref[0] = jnp.broadcast_to(cnt, (8, 128))


def _out_projection(o_attn, o_hy, o_sc, x, mod, w_out_bf, ln_g, ln_b, router_w, router_b, layer, dec_seq):
    t_ctx = x[0].shape[0]
    T = t_ctx + x[1].shape[0]
    tm = TOKEN_BLOCK
    nct = t_ctx // tm
    row = lambda i: (layer, _mod_row(i, tm, t_ctx, dec_seq), 0, 0)
    tok = lambda w: pl.BlockSpec((tm, w), lambda i: (i, 0))
    const = lambda shape: pl.BlockSpec(shape, lambda i: (0, 0))
    rw = jnp.pad(router_w, ((0, 0), (0, 128 - N_EXP)))
    rw_hi = rw.astype(BF16)
    rw = jnp.stack([rw_hi, (rw - rw_hi.astype(F32)).astype(BF16)])
    rb = jnp.pad(router_b, (0, 128 - N_EXP)).reshape(1, 128)
    return pl.pallas_call(
        functools.partial(_outproj_kernel, nct=nct),
        grid=(T // tm,),
        in_specs=[*_ctx_lat_specs((tm, ATT_W), nct), *_ctx_lat_specs((tm, HY_W), nct),
                  *_ctx_lat_specs((tm, SC_W), nct), *_ctx_lat_specs((tm, D), nct),
                  pl.BlockSpec((1, 1, 6, D), row),
                  const((D, D)), const((1, D)), const((1, D)),
                  pl.BlockSpec((2, D, 128), lambda i: (0, 0, 0)), const((1, 128))],
        out_specs=[tok(D), tok(D), tok(128), pl.BlockSpec((1, 8, 128), lambda i: (i, 0, 0))],
        out_shape=[jax.ShapeDtypeStruct((T, D), F32), jax.ShapeDtypeStruct((T, D), BF16),
                   jax.ShapeDtypeStruct((T, 128), F32), jax.ShapeDtypeStruct((T // tm, 8, 128), F32)],
        compiler_params=_cp(("arbitrary",)),
        name="out_projection_router",
    )(*o_attn, *o_hy, *o_sc, *x, mod, w_out_bf, ln_g.reshape(1, D), ln_b.reshape(1, D), rw, rb)


def _routing_tables(cnt):
    nb = cnt.shape[0]
    pc = (cnt + ROW_PAD - 1) // ROW_PAD * ROW_PAD
    lo = jnp.cumsum(pc, axis=1) - pc
    tot = jnp.sum(pc, axis=0)
    tot_pad = (tot + EXPERT_TILE - 1) // EXPERT_TILE * EXPERT_TILE
    off = jnp.cumsum(tot_pad) - tot_pad
    gstart = off[None, :] + jnp.cumsum(pc, axis=0) - pc
    ntile = tot_pad // EXPERT_TILE
    cum = jnp.cumsum(ntile)
    n_tiles = _max_expert_tiles(nb)
    i = jnp.arange(n_tiles, dtype=jnp.int32)
    ic = jnp.maximum(jnp.minimum(i, cum[-1] - 1), 0)
    te = jnp.sum((cum[None, :] <= ic[:, None]).astype(jnp.int32), axis=1)
    first = ic == (cum - ntile)[te]
    flags = jnp.where(i < cum[-1], 1 + 2 * first.astype(jnp.int32), 0)
    i32 = lambda a: a.astype(jnp.int32).reshape(-1)
    return dict(pc=i32(pc), lo=i32(lo), gstart=i32(gstart), tail_start=i32(off + tot), tail_len=i32(tot_pad - tot),
                rows=i32(jnp.sum(pc, axis=1)), tile_expert=te, tile_row=ic, tile_flags=flags)


def _max_expert_tiles(nb):
    rows = TOP_K * nb * TOKEN_BLOCK + nb * N_EXP * (ROW_PAD - 1) + N_EXP * (EXPERT_TILE - 1)
    return rows // EXPERT_TILE + 1


def _run_copies(n, src_ref, src0, dst_ref, dst0, sem, wait):
    done = jnp.int32(0)
    for p in RUN_PIECES:
        take = (n & p) != 0

        @pl.when(take)
        def _():
            cp = pltpu.make_async_copy(src_ref.at[pl.ds(pl.multiple_of(src0 + done, ROW_PAD), p), :],
                                       dst_ref.at[pl.ds(pl.multiple_of(dst0 + done, ROW_PAD), p), :], sem)
            if wait:
                cp.wait()
            else:
                cp.start()

        done = done + jnp.where(take, p, 0)


def _wait_rows(n, vmem_ref, hbm_ref, sem, to_vmem):
    base = TOP_K * TOKEN_BLOCK

    def wait(p):
        src, dst = (hbm_ref, vmem_ref) if to_vmem else (vmem_ref, hbm_ref)
        pltpu.make_async_copy(src.at[pl.ds(0, p), :], dst.at[pl.ds(0, p), :], sem).wait()

    wait(base)
    for p in RUN_PIECES[1:]:
        pl.when(((n - base) & p) != 0)(functools.partial(wait, p))


def _dispatch_kernel(pc_ref, lo_ref, gs_ref, ts_ref, tl_ref, rows_ref, h_ref, meta_ref, xs_ref,
                     buf_ref, zero_ref, sem, zsem):
    b = pl.program_id(0)
    nb = pl.num_programs(0)
    s = buf_ref.shape[1]
    slot = b % 2

    @pl.when(b == 0)
    def _():
        zero_ref[...] = jnp.zeros_like(zero_ref)
        for wait in (False, True):
            def tail(e, carry):
                _run_copies(tl_ref[e], zero_ref, 0, xs_ref, ts_ref[e], zsem, wait)
                return carry
            lax.fori_loop(0, N_EXP, tail, 0)

    mt = meta_ref[...].T
    srow = lax.broadcasted_iota(jnp.int32, (s, 1), 0).astype(F32)
    perm = jnp.zeros((s, TOKEN_BLOCK), F32)
    gperm = jnp.zeros((s, TOKEN_BLOCK), F32)
    for k in range(TOP_K):
        hit = srow == mt[k:k + 1]
        perm = perm + jnp.where(hit, 1.0, 0.0)
        gperm = gperm + jnp.where(hit, mt[TOP_K + k:TOP_K + k + 1], 0.0)
    sorted_rows = _dot(perm.astype(BF16), h_ref[...])
    gate_col = jnp.sum(gperm, axis=-1, keepdims=True)

    @pl.when(b >= 2)
    def _():
        _wait_rows(rows_ref[b - 2], buf_ref.at[slot], xs_ref, sem.at[slot], to_vmem=False)

    buf = buf_ref.at[slot]
    buf[:, :D] = sorted_rows
    buf[:, D:] = jnp.broadcast_to(gate_col, (s, 128))

    def run(e, carry):
        j = b * N_EXP + e
        _run_copies(pc_ref[j], buf, lo_ref[j], xs_ref, gs_ref[j], sem.at[slot], wait=False)
        return carry
    lax.fori_loop(0, N_EXP, run, 0)

    @pl.when(b == nb - 1)
    def _():
        @pl.when(b >= 1)
        def _():
            _wait_rows(rows_ref[b - 1], buf_ref.at[1 - slot], xs_ref, sem.at[1 - slot], to_vmem=False)
        _wait_rows(rows_ref[b], buf, xs_ref, sem.at[slot], to_vmem=False)


def _dispatch(tabs, h2, meta):
    T = h2.shape[0]
    nb = T // TOKEN_BLOCK
    rows = _max_expert_tiles(nb) * EXPERT_TILE
    tok = lambda w: pl.BlockSpec((TOKEN_BLOCK, w), lambda b, *_: (b, 0))
    return pl.pallas_call(
        _dispatch_kernel,
        grid_spec=pltpu.PrefetchScalarGridSpec(
            num_scalar_prefetch=6,
            grid=(nb,),
            in_specs=[tok(D), tok(128)],
            out_specs=pl.BlockSpec(memory_space=pl.ANY),
            scratch_shapes=[pltpu.VMEM((2, LOCAL_SLOTS, D + 128), F32), pltpu.VMEM((EXPERT_TILE, D + 128), F32),
                            pltpu.SemaphoreType.DMA((2,)), pltpu.SemaphoreType.DMA(())],
        ),
        out_shape=jax.ShapeDtypeStruct((rows, D + 128), F32),
        compiler_params=_cp(("arbitrary",)),
        name="moe_dispatch",
    )(tabs["pc"], tabs["lo"], tabs["gstart"], tabs["tail_start"], tabs["tail_len"], tabs["rows"], h2, meta)


def _expert_ffn_kernel(te_ref, tr_ref, fl_ref, x_ref, wg_ref, bg_ref, wu_ref, bu_ref, wd_ref, bd_ref, y_ref, wbf_ref):
    fl = fl_ref[pl.program_id(0)]

    @pl.when((fl & 1) != 0)
    def _():
        @pl.when((fl & 2) != 0)
        def _():
            wbf_ref[0] = wg_ref[0, 0].astype(BF16)
            wbf_ref[1] = wu_ref[0, 0].astype(BF16)
            wbf_ref[2] = wd_ref[0, 0].astype(BF16)

        x = x_ref[:, :D].astype(BF16)
        gate = x_ref[:, D:D + 1]
        g = jnp.minimum(_dot(x, wbf_ref[0]) + bg_ref[0], SWIGLU_LIMIT)
        u = jnp.clip(_dot(x, wbf_ref[1]) + bu_ref[0], -SWIGLU_LIMIT, SWIGLU_LIMIT)
        a = (u + 1.0) * (g * jax.nn.sigmoid(SWIGLU_ALPHA * g))
        y_ref[...] = gate * (_dot(a.astype(BF16), wbf_ref[2]) + bd_ref[0])


def _expert_ffn(tabs, xs, w_gate, b_gate, w_up, b_up, w_down, b_down, layer):
    rows = xs.shape[0]
    wspec = pl.BlockSpec((1, 1, D, D), lambda i, te, tr, fl: (layer, te[i], 0, 0))
    bspec = pl.BlockSpec((1, 1, D), lambda i, te, tr, fl: (te[i], 0, 0))
    return pl.pallas_call(
        _expert_ffn_kernel,
        grid_spec=pltpu.PrefetchScalarGridSpec(
            num_scalar_prefetch=3,
            grid=(rows // EXPERT_TILE,),
            in_specs=[pl.BlockSpec((EXPERT_TILE, D + 128), lambda i, te, tr, fl: (tr[i], 0)),
                      wspec, bspec, wspec, bspec, wspec, bspec],
            out_specs=pl.BlockSpec((EXPERT_TILE, D), lambda i, te, tr, fl: (tr[i], 0)),
            scratch_shapes=[pltpu.VMEM((3, D, D), BF16)],
        ),
        out_shape=jax.ShapeDtypeStruct((rows, D), F32),
        compiler_params=_cp(("arbitrary",)),
        name="expert_ffn",
    )(tabs["tile_expert"], tabs["tile_row"], tabs["tile_flags"], xs,
      w_gate, b_gate[layer].reshape(N_EXP, 1, D), w_up, b_up[layer].reshape(N_EXP, 1, D),
      w_down, b_down[layer].reshape(N_EXP, 1, D))


def _combine_kernel(pc_ref, lo_ref, gs_ref, rows_ref, ys_ref, meta_ref, x1_ref, mod_ref, lg_ref, lb_ref,
                    oc_ref, ol_ref, buf_ref, sem, *, nct):
    b = pl.program_id(0)
    nb = pl.num_programs(0)
    s = buf_ref.shape[1]
    slot = b % 2

    def fetch(blk, to_slot):
        def run(e, carry):
            j = blk * N_EXP + e
            _run_copies(pc_ref[j], ys_ref, gs_ref[j], buf_ref.at[to_slot], lo_ref[j], sem.at[to_slot], wait=False)
            return carry
        lax.fori_loop(0, N_EXP, run, 0)

    @pl.when(b == 0)
    def _():
        buf_ref[...] = jnp.zeros_like(buf_ref)
        fetch(b, slot)

    @pl.when(b + 1 < nb)
    def _():
        fetch(b + 1, 1 - slot)

    _wait_rows(rows_ref[b], buf_ref.at[slot], ys_ref, sem.at[slot], to_vmem=True)

    meta = meta_ref[...]
    scol = lax.broadcasted_iota(jnp.int32, (1, s), 1).astype(F32)
    w = jnp.zeros((TOKEN_BLOCK, s), F32)
    for k in range(TOP_K):
        w = w + jnp.where(meta[:, k:k + 1] == scol, 1.0, 0.0)
    w = w.astype(BF16)
    y = buf_ref[slot]
    y_hi = y.astype(BF16)
    y_lo = (y - y_hi.astype(F32)).astype(BF16)
    moe = _dot(w, y_hi) + _dot(w, y_lo)
    m = mod_ref[0, 0]
    out = _layer_norm(DN_ALPHA * x1_ref[...] + m[5:6] * moe, lg_ref[...], lb_ref[...])

    @pl.when(b < nct)
    def _():
        oc_ref[...] = out

    @pl.when(b >= nct)
    def _():
        ol_ref[...] = out


def _combine_ln2(tabs, ys, meta, x1, mod, ln_g, ln_b, layer, t_ctx, dec_seq):
    T = x1.shape[0]
    nct = t_ctx // TOKEN_BLOCK
    row = lambda b, *_: (layer, _mod_row(b, TOKEN_BLOCK, t_ctx, dec_seq), 0, 0)
    tok = lambda w: pl.BlockSpec((TOKEN_BLOCK, w), lambda b, *_: (b, 0))
    const = lambda shape: pl.BlockSpec(shape, lambda b, *_: (0, 0))
    return pl.pallas_call(
        functools.partial(_combine_kernel, nct=nct),
        grid_spec=pltpu.PrefetchScalarGridSpec(
            num_scalar_prefetch=4,
            grid=(T // TOKEN_BLOCK,),
            in_specs=[pl.BlockSpec(memory_space=pl.ANY), tok(128), tok(D), pl.BlockSpec((1, 1, 6, D), row),
                      const((1, D)), const((1, D))],
            out_specs=list(_ctx_lat_specs((TOKEN_BLOCK, D), nct)),
            scratch_shapes=[pltpu.VMEM((2, LOCAL_SLOTS, D), F32), pltpu.SemaphoreType.DMA((2,))],
        ),
        out_shape=[jax.ShapeDtypeStruct((t_ctx, D), F32), jax.ShapeDtypeStruct((T - t_ctx, D), F32)],
        compiler_params=_cp(("arbitrary",)),
        name="moe_combine_ln2",
    )(tabs["pc"], tabs["lo"], tabs["gstart"], tabs["rows"], ys, meta, x1, mod, ln_g.reshape(1, D),
      ln_b.reshape(1, D))


def _moe_ln2(h2, meta, cnt, x1, mod, w_gate, b_gate, w_up, b_up, w_down, b_down, ln_g, ln_b, layer, t_ctx, dec_seq):
    tabs = _routing_tables(cnt[:, 0, :N_EXP].astype(jnp.int32))
    xs = _dispatch(tabs, h2, meta)
    ys = _expert_ffn(tabs, xs, w_gate, b_gate, w_up, b_up, w_down, b_down, layer)
    return _combine_ln2(tabs, ys, meta, x1, mod, ln_g, ln_b, layer, t_ctx, dec_seq)


def kernel(x_prompt, x_sample, cache_k, cache_v, c, c_ctx, w_mod, b_mod, w_in, w_out, attn_sink, hy_conv_w, hy_conv_b, hy_w1, hy_b1, hy_f1, hy_w2, hy_b2, hy_f2, hy_w3, hy_deltas, hy_bias, sc_conv_w, ln1_g, ln1_b, router_w, router_b, w_gate, b_gate, w_up, b_up, w_down, b_down, ln2_g, ln2_b):
    nb, seq, _ = x_prompt.shape
    db, dseq, _ = x_sample.shape
    past = cache_k.shape[2]
    t_ctx = nb * seq
    x = (x_prompt.reshape(t_ctx, D), x_sample.reshape(db * dseq, D))

    cond = jnp.zeros((8, D), F32).at[0].set(c_ctx).at[1:1 + db].set(c)
    mod = _modulation(cond, w_mod, b_mod)

    cos64, sin64 = _rope_tables(dseq)
    tables = (jnp.tile(cos64, (1, N_HEADS)), jnp.tile(sin64, (1, N_HEADS)),
              jnp.tile(cos64, (1, N_KV)), jnp.tile(sin64, (1, N_KV)))
    fft_c = {L: _fft_constants(L) for L in (seq, dseq)}
    w_in_bf = w_in.astype(BF16)
    w_out_bf = w_out.astype(BF16)

    new_k, new_v = [], []
    for l in range(DEPTH):
        q, k, v, u_hy, u_sc = _in_projection(x[0], x[1], mod, w_in_bf[l], l, dseq)
        new_k.append(k[:t_ctx].reshape(nb, seq, N_KV, HD))
        new_v.append(v[:t_ctx].reshape(nb, seq, N_KV, HD))
        oa_ctx = _context_attention(q, k, v, attn_sink[l], nb, seq)
        oa_lat = _latent_attention(q, k, v, cache_k[:, l].reshape(db, past, KV_W),
                                   cache_v[:, l].reshape(db, past, KV_W), attn_sink[l], tables, t_ctx, db, dseq)
        o_hy, o_sc = [], []
        for (B, L, row0) in ((nb, seq, 0), (db, dseq, t_ctx)):
            x0, g, gt, osc = _mixer_prep(u_hy, u_sc, hy_conv_w[l], hy_conv_b[l], sc_conv_w[l], row0, B, L)
            filt_t = _hyena_filters(L, hy_w1[l], hy_b1[l], hy_f1[l], hy_w2[l], hy_b2[l], hy_f2[l], hy_w3[l],
                                    hy_deltas[l])
            o_hy.append(_hyena_long_conv(x0, g, gt, filt_t, hy_bias[l], fft_c[L], B, L))
            o_sc.append(osc)
        x1, h2, meta, cnt = _out_projection((oa_ctx, oa_lat), o_hy, o_sc, x, mod, w_out_bf[l], ln1_g[l], ln1_b[l],
                                            router_w[l], router_b[l], l, dseq)
        x = _moe_ln2(h2, meta, cnt, x1, mod, w_gate, b_gate, w_up, b_up, w_down, b_down, ln2_g[l], ln2_b[l],
                     l, t_ctx, dseq)

    y_prompt = x[0].reshape(nb, seq, D)
    y_sample = x[1].reshape(db, dseq, D)
    return (y_prompt, y_sample, jnp.stack(new_k, axis=1), jnp.stack(new_v, axis=1))
```

```python
import functools
import math

import jax
import jax.numpy as jnp
from jax import lax
from jax.experimental import pallas as pl
from jax.experimental.pallas import tpu as pltpu

F32 = jnp.float32
BF16 = jnp.bfloat16
HI = lax.Precision.HIGHEST

D = 1024
DEPTH = 2
N_HEADS = 8
N_KV = 2
HD = 64
GQA = N_HEADS // N_KV
ATT_W = N_HEADS * HD
KV_W = N_KV * HD
WINDOW = 128
BLK = 128
GRID_W = 64
ROPE_BASE = 10000.0
HY_W = 256
SC_W = 256
FILTER_EMB = 33
FILTER_BANDS = 16
FILTER_HIDDEN = 64
IN_W = ATT_W + 2 * KV_W + 3 * HY_W + 3 * SC_W
N_EXP = 32
TOP_K = 4
SWIGLU_LIMIT = 7.0
SWIGLU_ALPHA = 1.702
DN_ALPHA = (2 * DEPTH) ** 0.25
LN_EPS = 1e-5
NEG_INF = float("-inf")

VMEM_LIMIT = 56 * 1024 * 1024

TOKEN_BLOCK = 256
ROW_PAD = 8
RUN_PIECES = (256, 128, 64, 32, 16, 8)
EXPERT_TILE = 512
LOCAL_SLOTS = TOP_K * TOKEN_BLOCK + N_EXP * ROW_PAD


def _cp(sem):
    return pltpu.CompilerParams(dimension_semantics=sem, vmem_limit_bytes=VMEM_LIMIT)


def _dot(a, b, precision=None):
    return jnp.dot(a, b, preferred_element_type=F32, precision=precision)


def _dot_nt(a, b):
    return lax.dot_general(a, b, (((1,), (1,)), ((), ())), preferred_element_type=F32)


def _layer_norm(z, g, b):
    mu = jnp.mean(z, axis=-1, keepdims=True)
    zc = z - mu
    var = jnp.mean(zc * zc, axis=-1, keepdims=True)
    return zc * lax.rsqrt(var + LN_EPS) * g + b


def _mod_kernel(cond_ref, w_ref, b_ref, o_ref):
    c = cond_ref[...]
    s = c * jax.nn.sigmoid(c)
    o_ref[0] = _dot(s, w_ref[0], HI) + b_ref[0]


def _modulation(cond, w_mod, b_mod):
    tn = 1024
    out = pl.pallas_call(
        _mod_kernel,
        grid=(DEPTH, 6 * D // tn),
        in_specs=[
            pl.BlockSpec((8, D), lambda l, j: (0, 0)),
            pl.BlockSpec((1, D, tn), lambda l, j: (l, 0, j)),
            pl.BlockSpec((1, 1, tn), lambda l, j: (l, 0, j)),
        ],
        out_specs=pl.BlockSpec((1, 8, tn), lambda l, j: (l, 0, j)),
        out_shape=jax.ShapeDtypeStruct((DEPTH, 8, 6 * D), F32),
        compiler_params=_cp(("arbitrary", "arbitrary")),
        name="modulation",
    )(cond, w_mod, b_mod.reshape(DEPTH, 1, 6 * D))
    return out.reshape(DEPTH, 8, 6, D)


def _mod_row(i, tm, t_ctx, dec_seq):
    nct = t_ctx // tm
    return jnp.where(i < nct, 0, 1 + (i - nct) // (dec_seq // tm))


def _ctx_lat_specs(shape, nct, ctx_offset=0, lat_offset=0):
    ctx = pl.BlockSpec(shape, lambda i, *_: (jnp.minimum(i, nct - 1) + ctx_offset, 0))
    lat = pl.BlockSpec(shape, lambda i, *_: (jnp.maximum(i - nct, 0) + lat_offset, 0))
    return ctx, lat


def _inproj_kernel(xc_ref, xl_ref, mod_ref, w_ref, q_ref, k_ref, v_ref, uh_ref, us_ref, *, nct):
    m = mod_ref[0, 0]
    x = jnp.where(pl.program_id(0) < nct, xc_ref[...], xl_ref[...])
    h = x * (1.0 + m[1:2]) + m[0:1]
    y = _dot(h.astype(BF16), w_ref[...])
    o0 = ATT_W
    o1 = o0 + KV_W
    o2 = o1 + KV_W
    o3 = o2 + 3 * HY_W
    q_ref[...] = y[:, :o0]
    k_ref[...] = y[:, o0:o1]
    v_ref[...] = y[:, o1:o2]
    uh_ref[...] = y[:, o2:o3]
    us_ref[...] = y[:, o3:]


def _in_projection(xc, xl, mod, w_in_bf, layer, dec_seq):
    t_ctx = xc.shape[0]
    T = t_ctx + xl.shape[0]
    tm = 512
    row = lambda i: (layer, _mod_row(i, tm, t_ctx, dec_seq), 0, 0)
    tok = lambda w: pl.BlockSpec((tm, w), lambda i: (i, 0))
    widths = (ATT_W, KV_W, KV_W, 3 * HY_W, 3 * SC_W)
    return pl.pallas_call(
        functools.partial(_inproj_kernel, nct=t_ctx // tm),
        grid=(T // tm,),
        in_specs=[
            *_ctx_lat_specs((tm, D), t_ctx // tm),
            pl.BlockSpec((1, 1, 6, D), row),
            pl.BlockSpec((D, IN_W), lambda i: (0, 0)),
        ],
        out_specs=[tok(w) for w in widths],
        out_shape=[jax.ShapeDtypeStruct((T, w), F32) for w in widths],
        compiler_params=_cp(("arbitrary",)),
        name="in_projection",
    )(xc, xl, mod, w_in_bf)


def _swap_halves(x):
    w = x.shape[-1]
    lane = lax.broadcasted_iota(jnp.int32, x.shape, 1)
    return jnp.where((lane % HD) < HD // 2, pltpu.roll(x, w - HD // 2, 1), pltpu.roll(x, HD // 2, 1))


def _rope(x, cos, sin_signed):
    return x * cos + _swap_halves(x) * sin_signed


def _group_rows(q, g):
    return jnp.concatenate([q[:, (GQA * g + r) * HD:(GQA * g + r + 1) * HD] for r in range(GQA)], axis=0)


def _sink_column(sink_ref, g, rows):
    ridx = lax.broadcasted_iota(jnp.int32, (GQA * rows, 1), 0)
    col = jnp.full((GQA * rows, 1), sink_ref[GQA * g + GQA - 1], F32)
    for r in range(GQA - 2, -1, -1):
        col = jnp.where(ridx < (r + 1) * rows, sink_ref[GQA * g + r], col)
    return col


def _ungroup(outs, rows):
    return jnp.concatenate([o[r * rows:(r + 1) * rows] for o in outs for r in range(GQA)], axis=1)


def _ctx_attn_kernel(sink_ref, q_ref, k_ref, v_ref, o_ref):
    rows = q_ref.shape[0]
    q = q_ref[...] * (HD ** -0.5)
    k = k_ref[...]
    v = v_ref[...]
    outs = []
    for g in range(N_KV):
        qg = _group_rows(q, g).astype(BF16)
        kg = k[:, g * HD:(g + 1) * HD].astype(BF16)
        vg = v[:, g * HD:(g + 1) * HD].astype(BF16)
        s = _dot_nt(qg, kg)
        sink = _sink_column(sink_ref, g, rows)
        m = jnp.maximum(jnp.max(s, axis=-1, keepdims=True), sink)
        e = jnp.exp(s - m)
        den = jnp.sum(e, axis=-1, keepdims=True) + jnp.exp(sink - m)
        outs.append(_dot(e.astype(BF16), vg) / den)
    o_ref[...] = _ungroup(outs, rows)


def _context_attention(q, k, v, sink, B, L):
    tok = lambda w: pl.BlockSpec((L, w), lambda b: (b, 0))
    return pl.pallas_call(
        _ctx_attn_kernel,
        grid=(B,),
        in_specs=[pl.BlockSpec(memory_space=pltpu.SMEM), tok(ATT_W), tok(KV_W), tok(KV_W)],
        out_specs=tok(ATT_W),
        out_shape=jax.ShapeDtypeStruct((B * L, ATT_W), F32),
        compiler_params=_cp(("arbitrary",)),
        name="context_attention",
    )(sink, q, k, v)


def _lat_attn_kernel(sink_ref, q_ref, k_ref, v_ref, kc_ref, vc_ref, cq_ref, sq_ref, ck_ref, sk_ref, o_ref):
    n = pl.program_id(1)
    L = k_ref.shape[0]
    nk = 3 * BLK
    ws = pl.multiple_of(jnp.clip((n - 1) * BLK, 0, L - nk), BLK)
    q = _rope(q_ref[...], cq_ref[...], sq_ref[...]) * (HD ** -0.5)
    kl = _rope(k_ref[pl.ds(ws, nk), :], ck_ref[pl.ds(ws, nk), :], sk_ref[pl.ds(ws, nk), :])
    vl = v_ref[pl.ds(ws, nk), :]
    kc = kc_ref[...]
    vc = vc_ref[...]
    qpos = n * BLK + lax.broadcasted_iota(jnp.int32, (GQA * BLK, 1), 0) % BLK
    kpos = ws + lax.broadcasted_iota(jnp.int32, (1, nk), 1)
    valid = jnp.abs(kpos - qpos) <= WINDOW
    outs = []
    for g in range(N_KV):
        sl = slice(g * HD, (g + 1) * HD)
        qg = _group_rows(q, g).astype(BF16)
        s_loc = jnp.where(valid, _dot_nt(qg, kl[:, sl].astype(BF16)), NEG_INF)
        s_ctx = _dot_nt(qg, kc[:, sl].astype(BF16))
        sink = _sink_column(sink_ref, g, BLK)
        m = jnp.maximum(jnp.maximum(jnp.max(s_loc, axis=-1, keepdims=True),
                                    jnp.max(s_ctx, axis=-1, keepdims=True)), sink)
        e_loc = jnp.exp(s_loc - m)
        e_ctx = jnp.exp(s_ctx - m)
        den = (jnp.sum(e_loc, axis=-1, keepdims=True) + jnp.sum(e_ctx, axis=-1, keepdims=True)
               + jnp.exp(sink - m))
        o = _dot(e_loc.astype(BF16), vl[:, sl].astype(BF16)) + _dot(e_ctx.astype(BF16), vc[:, sl].astype(BF16))
        outs.append(o / den)
    o_ref[...] = _ungroup(outs, BLK)


def _rope_tables(L):
    rows = L // GRID_W
    row = jnp.repeat(jnp.arange(rows, dtype=F32), GRID_W)
    col = jnp.tile(jnp.arange(GRID_W, dtype=F32), rows)
    pairs = HD // 4
    inv = ROPE_BASE ** (-jnp.arange(pairs, dtype=F32) / pairs)
    ang = jnp.concatenate([row[:, None] * inv, col[:, None] * inv], axis=-1)
    cos = jnp.cos(ang)
    sin = jnp.sin(ang)
    return jnp.concatenate([cos, cos], axis=-1), jnp.concatenate([-sin, sin], axis=-1)


def _latent_attention(q, k, v, kc, vc, sink, tables, t_ctx, B, L):
    assert t_ctx % L == 0
    P = kc.shape[1]
    cq, sq, ck, sk = tables
    nbk = L // BLK
    seq = pl.BlockSpec((L, KV_W), lambda b, n: (t_ctx // L + b, 0))
    ctx = pl.BlockSpec((None, P, KV_W), lambda b, n: (b, 0, 0))
    return pl.pallas_call(
        _lat_attn_kernel,
        grid=(B, nbk),
        in_specs=[
            pl.BlockSpec(memory_space=pltpu.SMEM),
            pl.BlockSpec((BLK, ATT_W), lambda b, n: (t_ctx // BLK + b * nbk + n, 0)),
            seq, seq, ctx, ctx,
            pl.BlockSpec((BLK, ATT_W), lambda b, n: (n, 0)),
            pl.BlockSpec((BLK, ATT_W), lambda b, n: (n, 0)),
            pl.BlockSpec((L, KV_W), lambda b, n: (0, 0)),
            pl.BlockSpec((L, KV_W), lambda b, n: (0, 0)),
        ],
        out_specs=pl.BlockSpec((BLK, ATT_W), lambda b, n: (b * nbk + n, 0)),
        out_shape=jax.ShapeDtypeStruct((B * L, ATT_W), F32),
        compiler_params=_cp(("arbitrary", "arbitrary")),
        name="latent_attention",
    )(sink, q, k, v, kc, vc, cq, sq, ck, sk)


def _conv3(u, prev_row, next_row, w):
    tl = u.shape[0]
    ridx = lax.broadcasted_iota(jnp.int32, (tl, 1), 0)
    dn = jnp.where(ridx == 0, prev_row, pltpu.roll(u, 1, 0))
    up = jnp.where(ridx == tl - 1, next_row, pltpu.roll(u, tl - 1, 0))
    return dn * w[0:1] + u * w[1:2] + up * w[2:3]


def _store_time_major_inner(dst_ref, src, scr_ref):
    n2 = dst_ref.shape[0]
    rows = src.shape[0] // n2
    for h in range(scr_ref.shape[0]):
        scr_ref[h] = src[:, h * 128:(h + 1) * 128]
    for j in range(n2):
        for h in range(scr_ref.shape[0]):
            dst_ref[j, :, h * 128:(h + 1) * 128] = scr_ref[h, pl.ds(j, rows, stride=n2), :]


def _mixprep_kernel(uh_ref, uhp_ref, uhn_ref, us_ref, usp_ref, usn_ref, hw_ref, hb_ref, sw_ref,
                    x0_ref, g_ref, gt_ref, osc_ref, scr_ref):
    t = pl.program_id(1)
    first = t == 0
    last = t == pl.num_programs(1) - 1
    c = HY_W
    hw = hw_ref[...]
    uh = _conv3(uh_ref[...], jnp.where(first, 0.0, uhp_ref[7:8]), jnp.where(last, 0.0, uhn_ref[0:1]), hw)
    uh = uh + hb_ref[...]
    x0_ref[...] = uh[:, :c]
    g = uh[:, c:2 * c] * uh[:, 2 * c:]
    g_ref[...] = g
    _store_time_major_inner(gt_ref, g, scr_ref)
    us = us_ref[...]
    usp = usp_ref[7:8]
    usn = usn_ref[0:1]
    prod = us[:, c:2 * c] * us[:, 2 * c:]
    pprev = jnp.where(first, 0.0, usp[:, c:2 * c] * usp[:, 2 * c:])
    pnext = jnp.where(last, 0.0, usn[:, c:2 * c] * usn[:, 2 * c:])
    osc_ref[...] = us[:, :c] * _conv3(prod, pprev, pnext, sw_ref[...])


def _mixer_prep(u_hy, u_sc, hy_conv_w, hy_conv_b, sc_conv_w, row0, B, L):
    W = u_hy.shape[1]
    n1, n2 = _fft_split(L)
    tl = 8 * n2
    assert row0 % tl == 0 and L % tl == 0
    nt = L // tl
    r8 = tl // 8
    base = lambda b: row0 // 8 + b * (L // 8)
    main = pl.BlockSpec((tl, W), lambda b, t: (row0 // tl + b * nt + t, 0))
    prev = pl.BlockSpec((8, W), lambda b, t: (base(b) + jnp.maximum(t * r8 - 1, 0), 0))
    nxt = pl.BlockSpec((8, W), lambda b, t: (base(b) + jnp.minimum((t + 1) * r8, L // 8 - 1), 0))
    const = lambda shape: pl.BlockSpec(shape, lambda b, t: (0, 0))
    out = pl.BlockSpec((tl, HY_W), lambda b, t: (b * nt + t, 0))
    flat = jax.ShapeDtypeStruct((B * L, HY_W), F32)
    return pl.pallas_call(
        _mixprep_kernel,
        grid=(B, nt),
        in_specs=[main, prev, nxt, main, prev, nxt, const((3, W)), const((1, W)), const((3, SC_W))],
        out_specs=[out, out, pl.BlockSpec((None, n2, 8, HY_W), lambda b, t: (b, 0, t, 0)), out],
        out_shape=[flat, flat, jax.ShapeDtypeStruct((B, n2, n1 // 2, HY_W), F32), flat],
        scratch_shapes=[pltpu.VMEM((HY_W // 128, tl, 128), F32)],
        compiler_params=_cp(("arbitrary", "arbitrary")),
        name="mixer_prep",
    )(u_hy, u_hy, u_hy, u_sc, u_sc, u_sc, hy_conv_w, hy_conv_b.reshape(1, W), sc_conv_w)


def _filter_mlp(z, w1_ref, b1_ref, f1_ref, w2_ref, b2_ref, f2_ref, w3_ref, dl_ref, row0):
    tl = z.shape[0]
    h = jnp.sin(f1_ref[...] * (_dot(z, w1_ref[...], HI) + b1_ref[...]))
    h = jnp.sin(f2_ref[...] * (_dot(h, w2_ref[...], HI) + b2_ref[...]))
    h = _dot(h, w3_ref[...], HI) * jnp.exp(-z[:, 0:1] * jnp.abs(dl_ref[...]))
    row = row0 + lax.broadcasted_iota(jnp.int32, (tl, 1), 0)
    return h[:, :HY_W], jnp.where(row == 0, 0.0, h[:, HY_W:])


def _filter_kernel(z_ref, w1_ref, b1_ref, f1_ref, w2_ref, b2_ref, f2_ref, w3_ref, dl_ref, o_ref, scr_ref):
    hf, hb = _filter_mlp(z_ref[...], w1_ref, b1_ref, f1_ref, w2_ref, b2_ref, f2_ref, w3_ref, dl_ref,
                         pl.program_id(0) * z_ref.shape[0])
    _store_time_major_inner(o_ref.at[0], hf, scr_ref)
    _store_time_major_inner(o_ref.at[1], hb, scr_ref)


def _filter_features(L):
    t = jnp.linspace(0.0, 1.0, L, dtype=F32)[:, None]
    w = 2.0 * math.pi * jnp.arange(L, dtype=F32)[:, None] / L
    bands = jnp.linspace(1e-4, FILTER_BANDS - 1, FILTER_BANDS, dtype=F32)[None, :]
    z = jnp.concatenate([t, jnp.cos(bands * w), -jnp.sin(bands * w)], axis=-1)
    return jnp.pad(z, ((0, 0), (0, 128 - FILTER_EMB)))


def _filter_mlp_operands(w1, b1, f1, w2, b2, f2, w3, deltas):
    H = FILTER_HIDDEN
    ops = (jnp.pad(w1, ((0, 128 - FILTER_EMB), (0, 0))), b1.reshape(1, H), f1.reshape(1, H), w2, b2.reshape(1, H),
           f2.reshape(1, H), w3, deltas.reshape(1, 2 * HY_W))
    specs = [pl.BlockSpec(o.shape, lambda *_: (0, 0)) for o in ops]
    return ops, specs


def _hyena_filters(L, mlp):
    n1, n2 = _fft_split(L)
    tl = 8 * n2
    ops, specs = _filter_mlp_operands(*mlp)
    return pl.pallas_call(
        _filter_kernel,
        grid=(L // tl,),
        in_specs=[pl.BlockSpec((tl, 128), lambda i: (i, 0)), *specs],
        out_specs=pl.BlockSpec((2, n2, 8, HY_W), lambda i: (0, 0, i, 0)),
        out_shape=jax.ShapeDtypeStruct((2, n2, n1 // 2, HY_W), F32),
        scratch_shapes=[pltpu.VMEM((HY_W // 128, tl, 128), F32)],
        compiler_params=_cp(("arbitrary",)),
        name="hyena_filters",
    )(_filter_features(L), *ops)


def _dense_dft_constants(L):
    n = 2 * L
    a = (jnp.arange(n, dtype=jnp.int32)[:, None] * jnp.arange(L, dtype=jnp.int32)[None, :]) % n
    th = (2.0 * math.pi / n) * a.astype(F32)
    fwd = jnp.concatenate([jnp.cos(th), -jnp.sin(th)], axis=0)
    inv = jnp.concatenate([jnp.cos(th).T, -jnp.sin(th).T], axis=1) * (1.0 / n)
    return fwd.astype(BF16), inv.astype(BF16)


def _ctx_mixers_kernel(uh_ref, us_ref, hw_ref, hb_ref, sw_ref, bias_ref, fwd_ref, inv_ref, z_ref,
                       w1_ref, b1_ref, f1_ref, w2_ref, b2_ref, f2_ref, w3_ref, dl_ref, ohy_ref, osc_ref, kf_ref):
    c = HY_W
    nf = kf_ref.shape[1]

    @pl.when(pl.program_id(0) == 0)
    def _():
        hf, hb = _filter_mlp(z_ref[...], w1_ref, b1_ref, f1_ref, w2_ref, b2_ref, f2_ref, w3_ref, dl_ref, 0)
        xf = _dot(fwd_ref[...], hf.astype(BF16))
        xb = _dot(fwd_ref[...], hb.astype(BF16))
        kf_ref[0] = xf[:nf] + xb[:nf]
        kf_ref[1] = xf[nf:] - xb[nf:]

    zero = jnp.zeros((1, 1), F32)
    uh = _conv3(uh_ref[...], zero, zero, hw_ref[...]) + hb_ref[...]
    x0, g = uh[:, :c], uh[:, c:2 * c] * uh[:, 2 * c:]
    s = _dot(fwd_ref[...], g.astype(BF16))
    sr, si = s[:nf], s[nf:]
    kr, ki = kf_ref[0], kf_ref[1]
    y = _dot(inv_ref[...], jnp.concatenate([sr * kr - si * ki, sr * ki + si * kr], axis=0).astype(BF16))
    ohy_ref[...] = x0 * (y + g * bias_ref[...])
    us = us_ref[...]
    osc_ref[...] = us[:, :c] * _conv3(us[:, c:2 * c] * us[:, 2 * c:], zero, zero, sw_ref[...])


def _ctx_mixers(u_hy, u_sc, hy_conv_w, hy_conv_b, sc_conv_w, hy_bias, mlp, B, L):
    W = u_hy.shape[1]
    fwd, inv = _dense_dft_constants(L)
    ops, specs = _filter_mlp_operands(*mlp)
    const = lambda shape: pl.BlockSpec(shape, lambda b: (0, 0))
    seq = lambda w: pl.BlockSpec((L, w), lambda b: (b, 0))
    return pl.pallas_call(
        _ctx_mixers_kernel,
        grid=(B,),
        in_specs=[seq(W), seq(W), const((3, W)), const((1, W)), const((3, SC_W)), const((1, HY_W)),
                  const(fwd.shape), const(inv.shape), const((L, 128)), *specs],
        out_specs=[seq(HY_W), seq(SC_W)],
        out_shape=[jax.ShapeDtypeStruct((B * L, HY_W), F32), jax.ShapeDtypeStruct((B * L, SC_W), F32)],
        scratch_shapes=[pltpu.VMEM((2, 2 * L, HY_W), F32)],
        compiler_params=_cp(("arbitrary",)),
        name="context_mixers",
    )(u_hy, u_sc, hy_conv_w, hy_conv_b.reshape(1, W), sc_conv_w, hy_bias.reshape(1, HY_W), fwd, inv,
      _filter_features(L), *ops)


def _fft_split(L):
    n = 2 * L
    n1 = 128 if n >= 8192 else 32
    return n1, n // n1


def _fft_constants(L):
    n1, n2 = _fft_split(L)
    n = n1 * n2
    k1 = jnp.arange(n1, dtype=jnp.int32)
    a = (k1[:, None] * jnp.arange(n1 // 2, dtype=jnp.int32)[None, :]) % n1
    th = (2.0 * math.pi / n1) * a.astype(F32)
    f1 = jnp.concatenate([jnp.cos(th), -jnp.sin(th)], axis=0)
    f3 = jnp.concatenate([jnp.cos(th).T, -jnp.sin(th).T], axis=1) * (1.0 / n)
    k = k1[:, None, None] + n1 * jnp.arange(n2, dtype=jnp.int32)[None, :, None]
    ph = (k * jnp.arange(n2, dtype=jnp.int32)[None, None, :]) % n
    ph = (2.0 * math.pi / n) * ph.astype(F32)
    mr, mi = jnp.cos(ph), -jnp.sin(ph)
    mf = jnp.concatenate([jnp.concatenate([mr, -mi], axis=2), jnp.concatenate([mi, mr], axis=2)], axis=1)
    mrt, mit = jnp.swapaxes(mr, 1, 2), jnp.swapaxes(mi, 1, 2)
    mb = jnp.concatenate([jnp.concatenate([mrt, mit], axis=2), jnp.concatenate([-mit, mrt], axis=2)], axis=1)
    return tuple(c.astype(BF16) for c in (f1, f3, mf, mb))


FFT_GROUP = 8


def _dft_rows_kernel(f_ref, x_ref, o_ref):
    for i in range(FFT_GROUP):
        o_ref[:, i, :] = _dot(f_ref[...], x_ref[i].astype(BF16))


def _dft_rows(f1, xt):
    B, n2, kh, C = xt.shape
    m = f1.shape[0]
    return pl.pallas_call(
        _dft_rows_kernel,
        grid=(B, n2 // FFT_GROUP),
        in_specs=[pl.BlockSpec((m, kh), lambda b, j: (0, 0)),
                  pl.BlockSpec((None, FFT_GROUP, kh, C), lambda b, j: (b, j, 0, 0))],
        out_specs=pl.BlockSpec((None, m, FFT_GROUP, C), lambda b, j: (b, 0, j, 0)),
        out_shape=jax.ShapeDtypeStruct((B, m, n2, C), F32),
        compiler_params=_cp(("arbitrary", "arbitrary")),
        name="fft_rows",
    )(f1, xt)


def _fft_mid_kernel(a_ref, hf_ref, hb_ref, mf_ref, mb_ref, o_ref, kf_ref):
    n2 = a_ref.shape[2]

    @pl.when(pl.program_id(1) == 0)
    def _():
        for i in range(FFT_GROUP):
            xf = _dot(mf_ref[i], jnp.concatenate([hf_ref[0, i], hf_ref[1, i]], axis=0).astype(BF16))
            xb = _dot(mf_ref[i], jnp.concatenate([hb_ref[0, i], hb_ref[1, i]], axis=0).astype(BF16))
            kf_ref[i, 0] = xf[:n2] + xb[:n2]
            kf_ref[i, 1] = xf[n2:] - xb[n2:]

    for i in range(FFT_GROUP):
        x = _dot(mf_ref[i], jnp.concatenate([a_ref[0, i], a_ref[1, i]], axis=0).astype(BF16))
        xr, xi = x[:n2], x[n2:]
        kr, ki = kf_ref[i, 0], kf_ref[i, 1]
        y = jnp.concatenate([xr * kr - xi * ki, xr * ki + xi * kr], axis=0)
        o = _dot(mb_ref[i], y.astype(BF16))
        o_ref[:, 0, i, :] = o[:n2]
        o_ref[:, 1, i, :] = o[n2:]


def _fft_mid(a, ah, mf, mb):
    B, _, n1, n2, C = a.shape
    blk = (None, 2, FFT_GROUP, n2, C)
    return pl.pallas_call(
        _fft_mid_kernel,
        grid=(n1 // FFT_GROUP, B),
        in_specs=[
            pl.BlockSpec(blk, lambda k, b: (b, 0, k, 0, 0)),
            pl.BlockSpec(blk, lambda k, b: (0, 0, k, 0, 0)),
            pl.BlockSpec(blk, lambda k, b: (1, 0, k, 0, 0)),
            pl.BlockSpec((FFT_GROUP, 2 * n2, 2 * n2), lambda k, b: (k, 0, 0)),
            pl.BlockSpec((FFT_GROUP, 2 * n2, 2 * n2), lambda k, b: (k, 0, 0)),
        ],
        out_specs=pl.BlockSpec((None, n2, 2, FFT_GROUP, C), lambda k, b: (b, 0, 0, k, 0)),
        out_shape=jax.ShapeDtypeStruct((B, n2, 2, n1, C), F32),
        scratch_shapes=[pltpu.VMEM((FFT_GROUP, 2, n2, C), F32)],
        compiler_params=_cp(("arbitrary", "arbitrary")),
        name="fft_mid",
    )(a, ah, ah, mf, mb)


def _idft_rows_kernel(f_ref, b_ref, x0_ref, g_ref, bias_ref, o_ref):
    for i in range(FFT_GROUP):
        y = _dot(f_ref[...], b_ref[i].astype(BF16))
        o_ref[:, i, :] = x0_ref[:, i, :] * (y + g_ref[:, i, :] * bias_ref[...])


def _idft_rows_gate(f3, bo, x0, g, bias):
    B, n2, m2, C = bo.shape
    kh = f3.shape[0]
    tile = pl.BlockSpec((None, kh, FFT_GROUP, C), lambda b, j: (b, 0, j, 0))
    return pl.pallas_call(
        _idft_rows_kernel,
        grid=(B, n2 // FFT_GROUP),
        in_specs=[pl.BlockSpec((kh, m2), lambda b, j: (0, 0)),
                  pl.BlockSpec((None, FFT_GROUP, m2, C), lambda b, j: (b, j, 0, 0)),
                  tile, tile, pl.BlockSpec((1, C), lambda b, j: (0, 0))],
        out_specs=tile,
        out_shape=jax.ShapeDtypeStruct((B, kh, n2, C), F32),
        compiler_params=_cp(("arbitrary", "arbitrary")),
        name="ifft_rows_gate",
    )(f3, bo, x0, g, bias)


def _hyena_long_conv(x0, g, gt, filt_t, hy_bias, consts, B, L):
    C = HY_W
    n1, n2 = _fft_split(L)
    f1, f3, mf, mb = consts
    a = _dft_rows(f1, gt).reshape(B, 2, n1, n2, C)
    ah = _dft_rows(f1, filt_t).reshape(2, 2, n1, n2, C)
    bo = _fft_mid(a, ah, mf, mb).reshape(B, n2, 2 * n1, C)
    nat = lambda z: z.reshape(B, n1 // 2, n2, C)
    out = _idft_rows_gate(f3, bo, nat(x0), nat(g), hy_bias.reshape(1, C))
    return out.reshape(B * L, C)


def _outproj_kernel(oac_ref, oal_ref, ohc_ref, ohl_ref, osc_ref, osl_ref, xc_ref, xl_ref, mod_ref, w_ref,
                    lg_ref, lb_ref, rw_ref, rb_ref, x1_ref, h2_ref, meta_ref, cnt_ref, *, nct):
    m = mod_ref[0, 0]
    w = w_ref[...]
    is_ctx = pl.program_id(0) < nct
    pick = lambda c_ref, l_ref: jnp.where(is_ctx, c_ref[...], l_ref[...])
    o = (_dot(pick(oac_ref, oal_ref).astype(BF16), w[:ATT_W])
         + _dot(pick(ohc_ref, ohl_ref).astype(BF16), w[ATT_W:ATT_W + HY_W])
         + _dot(pick(osc_ref, osl_ref).astype(BF16), w[ATT_W + HY_W:]))
    x1 = _layer_norm(DN_ALPHA * pick(xc_ref, xl_ref) + m[2:3] * o, lg_ref[...], lb_ref[...])
    x1_ref[...] = x1
    h2 = x1 * (1.0 + m[4:5]) + m[3:4]
    h2_hi = h2.astype(BF16)
    h2_ref[...] = h2_hi
    lane = lax.broadcasted_iota(jnp.int32, (h2.shape[0], 128), 1)
    h2_lo = (h2 - h2_hi.astype(F32)).astype(BF16)
    logits = _dot(h2_hi, rw_ref[0]) + _dot(h2_lo, rw_ref[0]) + _dot(h2_hi, rw_ref[1]) + rb_ref[...]
    logits = jnp.where(lane < N_EXP, logits, NEG_INF)
    picks = []
    vals = []
    for _ in range(TOP_K):
        v = jnp.max(logits, axis=-1, keepdims=True)
        idx = jnp.min(jnp.where(logits == v, lane, 128), axis=-1, keepdims=True)
        hit = lane == idx
        picks.append(hit)
        vals.append(v)
        logits = jnp.where(hit, NEG_INF, logits)
    es = [jnp.exp(v - vals[0]) for v in vals]
    den = es[0] + es[1] + es[2] + es[3]
    tb = h2.shape[0]
    msel = jnp.zeros(logits.shape, F32)
    for hit in picks:
        msel = msel + jnp.where(hit, 1.0, 0.0)
    cnt = jnp.sum(msel, axis=0, keepdims=True)
    pc = jnp.floor((cnt + (ROW_PAD - 1.0)) * (1.0 / ROW_PAD)) * ROW_PAD
    upper = jnp.where(lax.broadcasted_iota(jnp.int32, (128, 128), 0)
                      < lax.broadcasted_iota(jnp.int32, (128, 128), 1), 1.0, 0.0)
    lo = _dot(jnp.broadcast_to(pc, (8, 128)), upper, HI)[0:1]
    lower = jnp.where(lax.broadcasted_iota(jnp.int32, (tb, tb), 1)
                      < lax.broadcasted_iota(jnp.int32, (tb, tb), 0), 1.0, 0.0).astype(BF16)
    pos = lo + _dot(lower, msel.astype(BF16))
    meta = jnp.zeros(logits.shape, F32)
    for kk, (hit, e) in enumerate(zip(picks, es)):
        slot = jnp.sum(jnp.where(hit, pos, 0.0), axis=-1, keepdims=True)
        meta = jnp.where(lane == kk, slot, meta)
        meta = jnp.where(lane == TOP_K + kk, e / den, meta)
    meta_ref[...] = meta
    cnt_ref[0] = jnp.broadcast_to(cnt, (8, 128))


def _out_projection(o_attn, o_hy, o_sc, x, mod, w_out_bf, ln_g, ln_b, router_w, router_b, layer, dec_seq):
    t_ctx = x[0].shape[0]
    T = t_ctx + x[1].shape[0]
    tm = TOKEN_BLOCK
    nct = t_ctx // tm
    row = lambda i: (layer, _mod_row(i, tm, t_ctx, dec_seq), 0, 0)
    tok = lambda w: pl.BlockSpec((tm, w), lambda i: (i, 0))
    const = lambda shape: pl.BlockSpec(shape, lambda i: (0, 0))
    rw = jnp.pad(router_w, ((0, 0), (0, 128 - N_EXP)))
    rw_hi = rw.astype(BF16)
    rw = jnp.stack([rw_hi, (rw - rw_hi.astype(F32)).astype(BF16)])
    rb = jnp.pad(router_b, (0, 128 - N_EXP)).reshape(1, 128)
    return pl.pallas_call(
        functools.partial(_outproj_kernel, nct=nct),
        grid=(T // tm,),
        in_specs=[*_ctx_lat_specs((tm, ATT_W), nct), *_ctx_lat_specs((tm, HY_W), nct),
                  *_ctx_lat_specs((tm, SC_W), nct), *_ctx_lat_specs((tm, D), nct),
                  pl.BlockSpec((1, 1, 6, D), row),
                  const((D, D)), const((1, D)), const((1, D)),
                  pl.BlockSpec((2, D, 128), lambda i: (0, 0, 0)), const((1, 128))],
        out_specs=[tok(D), tok(D), tok(128), pl.BlockSpec((1, 8, 128), lambda i: (i, 0, 0))],
        out_shape=[jax.ShapeDtypeStruct((T, D), F32), jax.ShapeDtypeStruct((T, D), BF16),
                   jax.ShapeDtypeStruct((T, 128), F32), jax.ShapeDtypeStruct((T // tm, 8, 128), F32)],
        compiler_params=_cp(("arbitrary",)),
        name="out_projection_router",
    )(*o_attn, *o_hy, *o_sc, *x, mod, w_out_bf, ln_g.reshape(1, D), ln_b.reshape(1, D), rw, rb)


def _routing_tables(cnt):
    nb = cnt.shape[0]
    pc = (cnt + ROW_PAD - 1) // ROW_PAD * ROW_PAD
    lo = jnp.cumsum(pc, axis=1) - pc
    tot = jnp.sum(pc, axis=0)
    tot_pad = (tot + EXPERT_TILE - 1) // EXPERT_TILE * EXPERT_TILE
    off = jnp.cumsum(tot_pad) - tot_pad
    gstart = off[None, :] + jnp.cumsum(pc, axis=0) - pc
    ntile = tot_pad // EXPERT_TILE
    cum = jnp.cumsum(ntile)
    n_tiles = _max_expert_tiles(nb)
    i = jnp.arange(n_tiles, dtype=jnp.int32)
    ic = jnp.maximum(jnp.minimum(i, cum[-1] - 1), 0)
    te = jnp.sum((cum[None, :] <= ic[:, None]).astype(jnp.int32), axis=1)
    first = ic == (cum - ntile)[te]
    flags = jnp.where(i < cum[-1], 1 + 2 * first.astype(jnp.int32), 0)
    i32 = lambda a: a.astype(jnp.int32).reshape(-1)
    return dict(pc=i32(pc), lo=i32(lo), gstart=i32(gstart), tail_start=i32(off + tot), tail_len=i32(tot_pad - tot),
                rows=i32(jnp.sum(pc, axis=1)), tile_expert=te, tile_row=ic, tile_flags=flags)


def _max_expert_tiles(nb):
    rows = TOP_K * nb * TOKEN_BLOCK + nb * N_EXP * (ROW_PAD - 1) + N_EXP * (EXPERT_TILE - 1)
    return rows // EXPERT_TILE + 1


def _run_copies(n, src_ref, src0, dst_ref, dst0, sem, wait):
    done = jnp.int32(0)
    for p in RUN_PIECES:
        take = (n & p) != 0

        @pl.when(take)
        def _():
            cp = pltpu.make_async_copy(src_ref.at[pl.ds(pl.multiple_of(src0 + done, ROW_PAD), p), :],
                                       dst_ref.at[pl.ds(pl.multiple_of(dst0 + done, ROW_PAD), p), :], sem)
            if wait:
                cp.wait()
            else:
                cp.start()

        done = done + jnp.where(take, p, 0)


def _wait_rows(n, vmem_ref, hbm_ref, sem, to_vmem):
    base = TOP_K * TOKEN_BLOCK

    def wait(p):
        src, dst = (hbm_ref, vmem_ref) if to_vmem else (vmem_ref, hbm_ref)
        pltpu.make_async_copy(src.at[pl.ds(0, p), :], dst.at[pl.ds(0, p), :], sem).wait()

    wait(base)
    for p in RUN_PIECES[1:]:
        pl.when(((n - base) & p) != 0)(functools.partial(wait, p))


def _dispatch_kernel(pc_ref, lo_ref, gs_ref, ts_ref, tl_ref, rows_ref, h_ref, meta_ref, xs_ref,
                     buf_ref, zero_ref, sem, zsem):
    b = pl.program_id(0)
    nb = pl.num_programs(0)
    s = buf_ref.shape[1]
    slot = b % 2

    @pl.when(b == 0)
    def _():
        zero_ref[...] = jnp.zeros_like(zero_ref)
        for wait in (False, True):
            def tail(e, carry):
                _run_copies(tl_ref[e], zero_ref, 0, xs_ref, ts_ref[e], zsem, wait)
                return carry
            lax.fori_loop(0, N_EXP, tail, 0)

    mt = meta_ref[...].T
    srow = lax.broadcasted_iota(jnp.int32, (s, 1), 0).astype(F32)
    perm = jnp.zeros((s, TOKEN_BLOCK), F32)
    gperm = jnp.zeros((s, TOKEN_BLOCK), F32)
    for k in range(TOP_K):
        hit = srow == mt[k:k + 1]
        perm = perm + jnp.where(hit, 1.0, 0.0)
        gperm = gperm + jnp.where(hit, mt[TOP_K + k:TOP_K + k + 1], 0.0)
    sorted_rows = _dot(perm.astype(BF16), h_ref[...])
    gate_col = jnp.sum(gperm, axis=-1, keepdims=True)

    @pl.when(b >= 2)
    def _():
        _wait_rows(rows_ref[b - 2], buf_ref.at[slot], xs_ref, sem.at[slot], to_vmem=False)

    buf = buf_ref.at[slot]
    buf[:, :D] = sorted_rows
    buf[:, D:] = jnp.broadcast_to(gate_col, (s, 128))

    def run(e, carry):
        j = b * N_EXP + e
        _run_copies(pc_ref[j], buf, lo_ref[j], xs_ref, gs_ref[j], sem.at[slot], wait=False)
        return carry
    lax.fori_loop(0, N_EXP, run, 0)

    @pl.when(b == nb - 1)
    def _():
        @pl.when(b >= 1)
        def _():
            _wait_rows(rows_ref[b - 1], buf_ref.at[1 - slot], xs_ref, sem.at[1 - slot], to_vmem=False)
        _wait_rows(rows_ref[b], buf, xs_ref, sem.at[slot], to_vmem=False)


def _dispatch(tabs, h2, meta):
    T = h2.shape[0]
    nb = T // TOKEN_BLOCK
    rows = _max_expert_tiles(nb) * EXPERT_TILE
    tok = lambda w: pl.BlockSpec((TOKEN_BLOCK, w), lambda b, *_: (b, 0))
    return pl.pallas_call(
        _dispatch_kernel,
        grid_spec=pltpu.PrefetchScalarGridSpec(
            num_scalar_prefetch=6,
            grid=(nb,),
            in_specs=[tok(D), tok(128)],
            out_specs=pl.BlockSpec(memory_space=pl.ANY),
            scratch_shapes=[pltpu.VMEM((2, LOCAL_SLOTS, D + 128), F32), pltpu.VMEM((EXPERT_TILE, D + 128), F32),
                            pltpu.SemaphoreType.DMA((2,)), pltpu.SemaphoreType.DMA(())],
        ),
        out_shape=jax.ShapeDtypeStruct((rows, D + 128), F32),
        compiler_params=_cp(("arbitrary",)),
        name="moe_dispatch",
    )(tabs["pc"], tabs["lo"], tabs["gstart"], tabs["tail_start"], tabs["tail_len"], tabs["rows"], h2, meta)


def _expert_ffn_kernel(te_ref, tr_ref, fl_ref, x_ref, wg_ref, bg_ref, wu_ref, bu_ref, wd_ref, bd_ref, y_ref, wbf_ref):
    fl = fl_ref[pl.program_id(0)]

    @pl.when((fl & 1) != 0)
    def _():
        @pl.when((fl & 2) != 0)
        def _():
            wbf_ref[0] = wg_ref[0, 0].astype(BF16)
            wbf_ref[1] = wu_ref[0, 0].astype(BF16)
            wbf_ref[2] = wd_ref[0, 0].astype(BF16)

        x = x_ref[:, :D].astype(BF16)
        gate = x_ref[:, D:D + 1]
        g = jnp.minimum(_dot(x, wbf_ref[0]) + bg_ref[0], SWIGLU_LIMIT)
        u = jnp.clip(_dot(x, wbf_ref[1]) + bu_ref[0], -SWIGLU_LIMIT, SWIGLU_LIMIT)
        a = (u + 1.0) * (g * jax.nn.sigmoid(SWIGLU_ALPHA * g))
        y_ref[...] = gate * (_dot(a.astype(BF16), wbf_ref[2]) + bd_ref[0])


def _expert_ffn(tabs, xs, w_gate, b_gate, w_up, b_up, w_down, b_down, layer):
    rows = xs.shape[0]
    wspec = pl.BlockSpec((1, 1, D, D), lambda i, te, tr, fl: (layer, te[i], 0, 0))
    bspec = pl.BlockSpec((1, 1, D), lambda i, te, tr, fl: (te[i], 0, 0))
    return pl.pallas_call(
        _expert_ffn_kernel,
        grid_spec=pltpu.PrefetchScalarGridSpec(
            num_scalar_prefetch=3,
            grid=(rows // EXPERT_TILE,),
            in_specs=[pl.BlockSpec((EXPERT_TILE, D + 128), lambda i, te, tr, fl: (tr[i], 0)),
                      wspec, bspec, wspec, bspec, wspec, bspec],
            out_specs=pl.BlockSpec((EXPERT_TILE, D), lambda i, te, tr, fl: (tr[i], 0)),
            scratch_shapes=[pltpu.VMEM((3, D, D), BF16)],
        ),
        out_shape=jax.ShapeDtypeStruct((rows, D), F32),
        compiler_params=_cp(("arbitrary",)),
        name="expert_ffn",
    )(tabs["tile_expert"], tabs["tile_row"], tabs["tile_flags"], xs,
      w_gate, b_gate[layer].reshape(N_EXP, 1, D), w_up, b_up[layer].reshape(N_EXP, 1, D),
      w_down, b_down[layer].reshape(N_EXP, 1, D))


def _combine_kernel(pc_ref, lo_ref, gs_ref, rows_ref, ys_ref, meta_ref, x1_ref, mod_ref, lg_ref, lb_ref,
                    oc_ref, ol_ref, buf_ref, sem, *, nct):
    b = pl.program_id(0)
    nb = pl.num_programs(0)
    s = buf_ref.shape[1]
    slot = b % 2

    def fetch(blk, to_slot):
        def run(e, carry):
            j = blk * N_EXP + e
            _run_copies(pc_ref[j], ys_ref, gs_ref[j], buf_ref.at[to_slot], lo_ref[j], sem.at[to_slot], wait=False)
            return carry
        lax.fori_loop(0, N_EXP, run, 0)

    @pl.when(b == 0)
    def _():
        buf_ref[...] = jnp.zeros_like(buf_ref)
        fetch(b, slot)

    @pl.when(b + 1 < nb)
    def _():
        fetch(b + 1, 1 - slot)

    _wait_rows(rows_ref[b], buf_ref.at[slot], ys_ref, sem.at[slot], to_vmem=True)

    meta = meta_ref[...]
    scol = lax.broadcasted_iota(jnp.int32, (1, s), 1).astype(F32)
    w = jnp.zeros((TOKEN_BLOCK, s), F32)
    for k in range(TOP_K):
        w = w + jnp.where(meta[:, k:k + 1] == scol, 1.0, 0.0)
    w = w.astype(BF16)
    moe = _dot(w, buf_ref[slot].astype(BF16))
    m = mod_ref[0, 0]
    out = _layer_norm(DN_ALPHA * x1_ref[...] + m[5:6] * moe, lg_ref[...], lb_ref[...])

    @pl.when(b < nct)
    def _():
        oc_ref[...] = out

    @pl.when(b >= nct)
    def _():
        ol_ref[...] = out


def _combine_ln2(tabs, ys, meta, x1, mod, ln_g, ln_b, layer, t_ctx, dec_seq):
    T = x1.shape[0]
    nct = t_ctx // TOKEN_BLOCK
    row = lambda b, *_: (layer, _mod_row(b, TOKEN_BLOCK, t_ctx, dec_seq), 0, 0)
    tok = lambda w: pl.BlockSpec((TOKEN_BLOCK, w), lambda b, *_: (b, 0))
    const = lambda shape: pl.BlockSpec(shape, lambda b, *_: (0, 0))
    return pl.pallas_call(
        functools.partial(_combine_kernel, nct=nct),
        grid_spec=pltpu.PrefetchScalarGridSpec(
            num_scalar_prefetch=4,
            grid=(T // TOKEN_BLOCK,),
            in_specs=[pl.BlockSpec(memory_space=pl.ANY), tok(128), tok(D), pl.BlockSpec((1, 1, 6, D), row),
                      const((1, D)), const((1, D))],
            out_specs=list(_ctx_lat_specs((TOKEN_BLOCK, D), nct)),
            scratch_shapes=[pltpu.VMEM((2, LOCAL_SLOTS, D), F32), pltpu.SemaphoreType.DMA((2,))],
        ),
        out_shape=[jax.ShapeDtypeStruct((t_ctx, D), F32), jax.ShapeDtypeStruct((T - t_ctx, D), F32)],
        compiler_params=_cp(("arbitrary",)),
        name="moe_combine_ln2",
    )(tabs["pc"], tabs["lo"], tabs["gstart"], tabs["rows"], ys, meta, x1, mod, ln_g.reshape(1, D),
      ln_b.reshape(1, D))


def _moe_ln2(h2, meta, cnt, x1, mod, w_gate, b_gate, w_up, b_up, w_down, b_down, ln_g, ln_b, layer, t_ctx, dec_seq):
    tabs = _routing_tables(cnt[:, 0, :N_EXP].astype(jnp.int32))
    xs = _dispatch(tabs, h2, meta)
    ys = _expert_ffn(tabs, xs, w_gate, b_gate, w_up, b_up, w_down, b_down, layer)
    return _combine_ln2(tabs, ys, meta, x1, mod, ln_g, ln_b, layer, t_ctx, dec_seq)


def kernel(x_prompt, x_sample, cache_k, cache_v, c, c_ctx, w_mod, b_mod, w_in, w_out, attn_sink, hy_conv_w, hy_conv_b, hy_w1, hy_b1, hy_f1, hy_w2, hy_b2, hy_f2, hy_w3, hy_deltas, hy_bias, sc_conv_w, ln1_g, ln1_b, router_w, router_b, w_gate, b_gate, w_up, b_up, w_down, b_down, ln2_g, ln2_b):
    nb, seq, _ = x_prompt.shape
    db, dseq, _ = x_sample.shape
    past = cache_k.shape[2]
    t_ctx = nb * seq
    x = (x_prompt.reshape(t_ctx, D), x_sample.reshape(db * dseq, D))

    cond = jnp.zeros((8, D), F32).at[0].set(c_ctx).at[1:1 + db].set(c)
    mod = _modulation(cond, w_mod, b_mod)

    cos64, sin64 = _rope_tables(dseq)
    tables = (jnp.tile(cos64, (1, N_HEADS)), jnp.tile(sin64, (1, N_HEADS)),
              jnp.tile(cos64, (1, N_KV)), jnp.tile(sin64, (1, N_KV)))
    fft_c = _fft_constants(dseq)
    w_in_bf = w_in.astype(BF16)
    w_out_bf = w_out.astype(BF16)

    new_k, new_v = [], []
    for l in range(DEPTH):
        q, k, v, u_hy, u_sc = _in_projection(x[0], x[1], mod, w_in_bf[l], l, dseq)
        new_k.append(k[:t_ctx].reshape(nb, seq, N_KV, HD))
        new_v.append(v[:t_ctx].reshape(nb, seq, N_KV, HD))
        oa_ctx = _context_attention(q, k, v, attn_sink[l], nb, seq)
        oa_lat = _latent_attention(q, k, v, cache_k[:, l].reshape(db, past, KV_W),
                                   cache_v[:, l].reshape(db, past, KV_W), attn_sink[l], tables, t_ctx, db, dseq)
        mlp = (hy_w1[l], hy_b1[l], hy_f1[l], hy_w2[l], hy_b2[l], hy_f2[l], hy_w3[l], hy_deltas[l])
        ohy_ctx, osc_ctx = _ctx_mixers(u_hy, u_sc, hy_conv_w[l], hy_conv_b[l], sc_conv_w[l], hy_bias[l], mlp,
                                       nb, seq)
        x0, g, gt, osc_lat = _mixer_prep(u_hy, u_sc, hy_conv_w[l], hy_conv_b[l], sc_conv_w[l], t_ctx, db, dseq)
        ohy_lat = _hyena_long_conv(x0, g, gt, _hyena_filters(dseq, mlp), hy_bias[l], fft_c, db, dseq)
        x1, h2, meta, cnt = _out_projection((oa_ctx, oa_lat), (ohy_ctx, ohy_lat), (osc_ctx, osc_lat), x, mod,
                                            w_out_bf[l], ln1_g[l], ln1_b[l], router_w[l], router_b[l], l, dseq)
        x = _moe_ln2(h2, meta, cnt, x1, mod, w_gate, b_gate, w_up, b_up, w_down, b_down, ln2_g[l], ln2_b[l],
                     l, t_ctx, dseq)

    y_prompt = x[0].reshape(nb, seq, D)
    y_sample = x[1].reshape(db, dseq, D)
    return (y_prompt, y_sample, jnp.stack(new_k, axis=1), jnp.stack(new_v, axis=1))
```

```python
import functools
import math

import jax
import jax.numpy as jnp
from jax import lax
from jax.experimental import pallas as pl
from jax.experimental.pallas import tpu as pltpu

F32 = jnp.float32
BF16 = jnp.bfloat16
HI = lax.Precision.HIGHEST

D = 1024
DEPTH = 2
N_HEADS = 8
N_KV = 2
HD = 64
GQA = N_HEADS // N_KV
ATT_W = N_HEADS * HD
KV_W = N_KV * HD
WINDOW = 128
BLK = 128
GRID_W = 64
ROPE_BASE = 10000.0
HY_W = 256
SC_W = 256
FILTER_EMB = 33
FILTER_BANDS = 16
FILTER_HIDDEN = 64
IN_W = ATT_W + 2 * KV_W + 3 * HY_W + 3 * SC_W
N_EXP = 32
TOP_K = 4
SWIGLU_LIMIT = 7.0
SWIGLU_ALPHA = 1.702
DN_ALPHA = (2 * DEPTH) ** 0.25
LN_EPS = 1e-5
NEG_INF = float("-inf")

VMEM_LIMIT = 56 * 1024 * 1024

TOKEN_BLOCK = 256
ROW_PAD = 8
RUN_PIECES = (256, 128, 64, 32, 16, 8)
EXPERT_TILE = 512
LOCAL_SLOTS = TOP_K * TOKEN_BLOCK + N_EXP * ROW_PAD


def _cp(sem):
    return pltpu.CompilerParams(dimension_semantics=sem, vmem_limit_bytes=VMEM_LIMIT)


def _dot(a, b, precision=None):
    return jnp.dot(a, b, preferred_element_type=F32, precision=precision)


def _dot_nt(a, b):
    return lax.dot_general(a, b, (((1,), (1,)), ((), ())), preferred_element_type=F32)


def _layer_norm(z, g, b):
    mu = jnp.mean(z, axis=-1, keepdims=True)
    zc = z - mu
    var = jnp.mean(zc * zc, axis=-1, keepdims=True)
    return zc * lax.rsqrt(var + LN_EPS) * g + b


def _mod_kernel(cond_ref, w_ref, b_ref, o_ref):
    c = cond_ref[...]
    s = c * jax.nn.sigmoid(c)
    o_ref[0] = _dot(s, w_ref[0], HI) + b_ref[0]


def _modulation(cond, w_mod, b_mod):
    tn = 1024
    out = pl.pallas_call(
        _mod_kernel,
        grid=(DEPTH, 6 * D // tn),
        in_specs=[
            pl.BlockSpec((8, D), lambda l, j: (0, 0)),
            pl.BlockSpec((1, D, tn), lambda l, j: (l, 0, j)),
            pl.BlockSpec((1, 1, tn), lambda l, j: (l, 0, j)),
        ],
        out_specs=pl.BlockSpec((1, 8, tn), lambda l, j: (l, 0, j)),
        out_shape=jax.ShapeDtypeStruct((DEPTH, 8, 6 * D), F32),
        compiler_params=_cp(("arbitrary", "arbitrary")),
        name="modulation",
    )(cond, w_mod, b_mod.reshape(DEPTH, 1, 6 * D))
    return out.reshape(DEPTH, 8, 6, D)


def _mod_row(i, tm, t_ctx, dec_seq):
    nct = t_ctx // tm
    return jnp.where(i < nct, 0, 1 + (i - nct) // (dec_seq // tm))


def _ctx_lat_specs(shape, nct, ctx_offset=0, lat_offset=0):
    ctx = pl.BlockSpec(shape, lambda i, *_: (jnp.minimum(i, nct - 1) + ctx_offset, 0))
    lat = pl.BlockSpec(shape, lambda i, *_: (jnp.maximum(i - nct, 0) + lat_offset, 0))
    return ctx, lat


def _inproj_kernel(xc_ref, xl_ref, mod_ref, w_ref, q_ref, k_ref, v_ref, uh_ref, us_ref, *, nct):
    m = mod_ref[0, 0]
    x = jnp.where(pl.program_id(0) < nct, xc_ref[...], xl_ref[...])
    h = x * (1.0 + m[1:2]) + m[0:1]
    y = _dot(h.astype(BF16), w_ref[...])
    o0 = ATT_W
    o1 = o0 + KV_W
    o2 = o1 + KV_W
    o3 = o2 + 3 * HY_W
    q_ref[...] = y[:, :o0].astype(q_ref.dtype)
    k_ref[...] = y[:, o0:o1]
    v_ref[...] = y[:, o1:o2]
    uh_ref[...] = y[:, o2:o3].astype(uh_ref.dtype)
    us_ref[...] = y[:, o3:].astype(us_ref.dtype)


def _in_projection(xc, xl, mod, w_in_bf, layer, dec_seq):
    t_ctx = xc.shape[0]
    T = t_ctx + xl.shape[0]
    tm = 512
    row = lambda i: (layer, _mod_row(i, tm, t_ctx, dec_seq), 0, 0)
    tok = lambda w: pl.BlockSpec((tm, w), lambda i: (i, 0))
    widths = (ATT_W, KV_W, KV_W, 3 * HY_W, 3 * SC_W)
    return pl.pallas_call(
        functools.partial(_inproj_kernel, nct=t_ctx // tm),
        grid=(T // tm,),
        in_specs=[
            *_ctx_lat_specs((tm, D), t_ctx // tm),
            pl.BlockSpec((1, 1, 6, D), row),
            pl.BlockSpec((D, IN_W), lambda i: (0, 0)),
        ],
        out_specs=[tok(w) for w in widths],
        out_shape=[jax.ShapeDtypeStruct((T, w), dt) for w, dt in zip(widths, (BF16, F32, F32, BF16, BF16))],
        compiler_params=_cp(("arbitrary",)),
        name="in_projection",
    )(xc, xl, mod, w_in_bf)


def _swap_halves(x):
    w = x.shape[-1]
    lane = lax.broadcasted_iota(jnp.int32, x.shape, 1)
    return jnp.where((lane % HD) < HD // 2, pltpu.roll(x, w - HD // 2, 1), pltpu.roll(x, HD // 2, 1))


def _rope(x, cos, sin_signed):
    return x * cos + _swap_halves(x) * sin_signed


def _group_rows(q, g):
    return jnp.concatenate([q[:, (GQA * g + r) * HD:(GQA * g + r + 1) * HD] for r in range(GQA)], axis=0)


def _sink_column(sink_ref, g, rows):
    ridx = lax.broadcasted_iota(jnp.int32, (GQA * rows, 1), 0)
    col = jnp.full((GQA * rows, 1), sink_ref[GQA * g + GQA - 1], F32)
    for r in range(GQA - 2, -1, -1):
        col = jnp.where(ridx < (r + 1) * rows, sink_ref[GQA * g + r], col)
    return col


def _ungroup(outs, rows):
    return jnp.concatenate([o[r * rows:(r + 1) * rows] for o in outs for r in range(GQA)], axis=1)


def _ctx_attn_kernel(sink_ref, q_ref, k_ref, v_ref, o_ref):
    rows = q_ref.shape[0]
    q = q_ref[...].astype(F32) * (HD ** -0.5)
    k = k_ref[...]
    v = v_ref[...]
    outs = []
    for g in range(N_KV):
        qg = _group_rows(q, g).astype(BF16)
        kg = k[:, g * HD:(g + 1) * HD].astype(BF16)
        vg = v[:, g * HD:(g + 1) * HD].astype(BF16)
        s = _dot_nt(qg, kg)
        sink = _sink_column(sink_ref, g, rows)
        m = jnp.maximum(jnp.max(s, axis=-1, keepdims=True), sink)
        e = jnp.exp(s - m)
        den = jnp.sum(e, axis=-1, keepdims=True) + jnp.exp(sink - m)
        outs.append(_dot(e.astype(BF16), vg) / den)
    o_ref[...] = _ungroup(outs, rows).astype(o_ref.dtype)


def _context_attention(q, k, v, sink, B, L):
    tok = lambda w: pl.BlockSpec((L, w), lambda b: (b, 0))
    return pl.pallas_call(
        _ctx_attn_kernel,
        grid=(B,),
        in_specs=[pl.BlockSpec(memory_space=pltpu.SMEM), tok(ATT_W), tok(KV_W), tok(KV_W)],
        out_specs=tok(ATT_W),
        out_shape=jax.ShapeDtypeStruct((B * L, ATT_W), BF16),
        compiler_params=_cp(("arbitrary",)),
        name="context_attention",
    )(sink, q, k, v)


def _lat_attn_kernel(sink_ref, q_ref, k_ref, v_ref, kc_ref, vc_ref, cq_ref, sq_ref, ck_ref, sk_ref, o_ref):
    n = pl.program_id(1)
    L = k_ref.shape[0]
    nk = 3 * BLK
    ws = pl.multiple_of(jnp.clip((n - 1) * BLK, 0, L - nk), BLK)
    q = _rope(q_ref[...].astype(F32), cq_ref[...], sq_ref[...]) * (HD ** -0.5)
    kl = _rope(k_ref[pl.ds(ws, nk), :], ck_ref[pl.ds(ws, nk), :], sk_ref[pl.ds(ws, nk), :])
    vl = v_ref[pl.ds(ws, nk), :]
    kc = kc_ref[...]
    vc = vc_ref[...]
    qpos = n * BLK + lax.broadcasted_iota(jnp.int32, (GQA * BLK, 1), 0) % BLK
    kpos = ws + lax.broadcasted_iota(jnp.int32, (1, nk), 1)
    valid = jnp.abs(kpos - qpos) <= WINDOW
    outs = []
    for g in range(N_KV):
        sl = slice(g * HD, (g + 1) * HD)
        qg = _group_rows(q, g).astype(BF16)
        s_loc = jnp.where(valid, _dot_nt(qg, kl[:, sl].astype(BF16)), NEG_INF)
        s_ctx = _dot_nt(qg, kc[:, sl].astype(BF16))
        sink = _sink_column(sink_ref, g, BLK)
        m = jnp.maximum(jnp.maximum(jnp.max(s_loc, axis=-1, keepdims=True),
                                    jnp.max(s_ctx, axis=-1, keepdims=True)), sink)
        e_loc = jnp.exp(s_loc - m)
        e_ctx = jnp.exp(s_ctx - m)
        den = (jnp.sum(e_loc, axis=-1, keepdims=True) + jnp.sum(e_ctx, axis=-1, keepdims=True)
               + jnp.exp(sink - m))
        o = _dot(e_loc.astype(BF16), vl[:, sl].astype(BF16)) + _dot(e_ctx.astype(BF16), vc[:, sl].astype(BF16))
        outs.append(o / den)
    o_ref[...] = _ungroup(outs, BLK).astype(o_ref.dtype)


def _rope_tables(L):
    rows = L // GRID_W
    row = jnp.repeat(jnp.arange(rows, dtype=F32), GRID_W)
    col = jnp.tile(jnp.arange(GRID_W, dtype=F32), rows)
    pairs = HD // 4
    inv = ROPE_BASE ** (-jnp.arange(pairs, dtype=F32) / pairs)
    ang = jnp.concatenate([row[:, None] * inv, col[:, None] * inv], axis=-1)
    cos = jnp.cos(ang)
    sin = jnp.sin(ang)
    return jnp.concatenate([cos, cos], axis=-1), jnp.concatenate([-sin, sin], axis=-1)


def _latent_attention(q, k, v, kc, vc, sink, tables, t_ctx, B, L):
    assert t_ctx % L == 0
    P = kc.shape[1]
    cq, sq, ck, sk = tables
    nbk = L // BLK
    seq = pl.BlockSpec((L, KV_W), lambda b, n: (t_ctx // L + b, 0))
    ctx = pl.BlockSpec((None, P, KV_W), lambda b, n: (b, 0, 0))
    return pl.pallas_call(
        _lat_attn_kernel,
        grid=(B, nbk),
        in_specs=[
            pl.BlockSpec(memory_space=pltpu.SMEM),
            pl.BlockSpec((BLK, ATT_W), lambda b, n: (t_ctx // BLK + b * nbk + n, 0)),
            seq, seq, ctx, ctx,
            pl.BlockSpec((BLK, ATT_W), lambda b, n: (n, 0)),
            pl.BlockSpec((BLK, ATT_W), lambda b, n: (n, 0)),
            pl.BlockSpec((L, KV_W), lambda b, n: (0, 0)),
            pl.BlockSpec((L, KV_W), lambda b, n: (0, 0)),
        ],
        out_specs=pl.BlockSpec((BLK, ATT_W), lambda b, n: (b * nbk + n, 0)),
        out_shape=jax.ShapeDtypeStruct((B * L, ATT_W), BF16),
        compiler_params=_cp(("arbitrary", "arbitrary")),
        name="latent_attention",
    )(sink, q, k, v, kc, vc, cq, sq, ck, sk)


def _conv3(u, prev_row, next_row, w):
    tl = u.shape[0]
    ridx = lax.broadcasted_iota(jnp.int32, (tl, 1), 0)
    dn = jnp.where(ridx == 0, prev_row, pltpu.roll(u, 1, 0))
    up = jnp.where(ridx == tl - 1, next_row, pltpu.roll(u, tl - 1, 0))
    return dn * w[0:1] + u * w[1:2] + up * w[2:3]


def _store_time_major_inner(dst_ref, src, scr_ref):
    n2 = dst_ref.shape[0]
    rows = src.shape[0] // n2
    for h in range(scr_ref.shape[0]):
        scr_ref[h] = src[:, h * 128:(h + 1) * 128]
    for j in range(n2):
        for h in range(scr_ref.shape[0]):
            dst_ref[j, :, h * 128:(h + 1) * 128] = scr_ref[h, pl.ds(j, rows, stride=n2), :]


def _mixprep_kernel(uh_ref, uhp_ref, uhn_ref, us_ref, usp_ref, usn_ref, hw_ref, hb_ref, sw_ref,
                    x0_ref, g_ref, gt_ref, osc_ref, scr_ref):
    t = pl.program_id(1)
    first = t == 0
    last = t == pl.num_programs(1) - 1
    c = HY_W
    hw = hw_ref[...]
    halo = uhp_ref.shape[0]
    f32 = lambda a: a.astype(F32)
    uh = _conv3(f32(uh_ref[...]), jnp.where(first, 0.0, f32(uhp_ref[halo - 1:halo])),
                jnp.where(last, 0.0, f32(uhn_ref[0:1])), hw)
    uh = uh + hb_ref[...]
    x0_ref[...] = uh[:, :c]
    g = uh[:, c:2 * c] * uh[:, 2 * c:]
    g_ref[...] = g
    _store_time_major_inner(gt_ref, g, scr_ref)
    us = f32(us_ref[...])
    usp = f32(usp_ref[halo - 1:halo])
    usn = f32(usn_ref[0:1])
    prod = us[:, c:2 * c] * us[:, 2 * c:]
    pprev = jnp.where(first, 0.0, usp[:, c:2 * c] * usp[:, 2 * c:])
    pnext = jnp.where(last, 0.0, usn[:, c:2 * c] * usn[:, 2 * c:])
    osc_ref[...] = (us[:, :c] * _conv3(prod, pprev, pnext, sw_ref[...])).astype(osc_ref.dtype)


def _mixer_prep(u_hy, u_sc, hy_conv_w, hy_conv_b, sc_conv_w, row0, B, L):
    W = u_hy.shape[1]
    n1, n2 = _fft_split(L)
    tl = 8 * n2
    assert row0 % tl == 0 and L % tl == 0
    nt = L // tl
    halo = 16
    rh = tl // halo
    base = lambda b: row0 // halo + b * (L // halo)
    main = pl.BlockSpec((tl, W), lambda b, t: (row0 // tl + b * nt + t, 0))
    prev = pl.BlockSpec((halo, W), lambda b, t: (base(b) + jnp.maximum(t * rh - 1, 0), 0))
    nxt = pl.BlockSpec((halo, W), lambda b, t: (base(b) + jnp.minimum((t + 1) * rh, L // halo - 1), 0))
    const = lambda shape: pl.BlockSpec(shape, lambda b, t: (0, 0))
    out = pl.BlockSpec((tl, HY_W), lambda b, t: (b * nt + t, 0))
    flat = jax.ShapeDtypeStruct((B * L, HY_W), F32)
    return pl.pallas_call(
        _mixprep_kernel,
        grid=(B, nt),
        in_specs=[main, prev, nxt, main, prev, nxt, const((3, W)), const((1, W)), const((3, SC_W))],
        out_specs=[out, out, pl.BlockSpec((None, n2, 8, HY_W), lambda b, t: (b, 0, t, 0)), out],
        out_shape=[flat, flat, jax.ShapeDtypeStruct((B, n2, n1 // 2, HY_W), F32),
                   jax.ShapeDtypeStruct((B * L, SC_W), BF16)],
        scratch_shapes=[pltpu.VMEM((HY_W // 128, tl, 128), F32)],
        compiler_params=_cp(("arbitrary", "arbitrary")),
        name="mixer_prep",
    )(u_hy, u_hy, u_hy, u_sc, u_sc, u_sc, hy_conv_w, hy_conv_b.reshape(1, W), sc_conv_w)


def _filter_mlp(z, w1_ref, b1_ref, f1_ref, w2_ref, b2_ref, f2_ref, w3_ref, dl_ref, row0):
    tl = z.shape[0]
    h = jnp.sin(f1_ref[...] * (_dot(z, w1_ref[...], HI) + b1_ref[...]))
    h = jnp.sin(f2_ref[...] * (_dot(h, w2_ref[...], HI) + b2_ref[...]))
    h = _dot(h, w3_ref[...], HI) * jnp.exp(-z[:, 0:1] * jnp.abs(dl_ref[...]))
    row = row0 + lax.broadcasted_iota(jnp.int32, (tl, 1), 0)
    return h[:, :HY_W], jnp.where(row == 0, 0.0, h[:, HY_W:])


def _filter_kernel(z_ref, w1_ref, b1_ref, f1_ref, w2_ref, b2_ref, f2_ref, w3_ref, dl_ref, o_ref, scr_ref):
    hf, hb = _filter_mlp(z_ref[...], w1_ref, b1_ref, f1_ref, w2_ref, b2_ref, f2_ref, w3_ref, dl_ref,
                         pl.program_id(0) * z_ref.shape[0])
    _store_time_major_inner(o_ref.at[0], hf, scr_ref)
    _store_time_major_inner(o_ref.at[1], hb, scr_ref)


def _filter_features(L):
    t = jnp.linspace(0.0, 1.0, L, dtype=F32)[:, None]
    w = 2.0 * math.pi * jnp.arange(L, dtype=F32)[:, None] / L
    bands = jnp.linspace(1e-4, FILTER_BANDS - 1, FILTER_BANDS, dtype=F32)[None, :]
    z = jnp.concatenate([t, jnp.cos(bands * w), -jnp.sin(bands * w)], axis=-1)
    return jnp.pad(z, ((0, 0), (0, 128 - FILTER_EMB)))


def _filter_mlp_operands(w1, b1, f1, w2, b2, f2, w3, deltas):
    H = FILTER_HIDDEN
    ops = (jnp.pad(w1, ((0, 128 - FILTER_EMB), (0, 0))), b1.reshape(1, H), f1.reshape(1, H), w2, b2.reshape(1, H),
           f2.reshape(1, H), w3, deltas.reshape(1, 2 * HY_W))
    specs = [pl.BlockSpec(o.shape, lambda *_: (0, 0)) for o in ops]
    return ops, specs


def _hyena_filters(L, mlp):
    n1, n2 = _fft_split(L)
    tl = 8 * n2
    ops, specs = _filter_mlp_operands(*mlp)
    return pl.pallas_call(
        _filter_kernel,
        grid=(L // tl,),
        in_specs=[pl.BlockSpec((tl, 128), lambda i: (i, 0)), *specs],
        out_specs=pl.BlockSpec((2, n2, 8, HY_W), lambda i: (0, 0, i, 0)),
        out_shape=jax.ShapeDtypeStruct((2, n2, n1 // 2, HY_W), F32),
        scratch_shapes=[pltpu.VMEM((HY_W // 128, tl, 128), F32)],
        compiler_params=_cp(("arbitrary",)),
        name="hyena_filters",
    )(_filter_features(L), *ops)


def _dense_dft_constants(L):
    n = 2 * L
    a = (jnp.arange(n, dtype=jnp.int32)[:, None] * jnp.arange(L, dtype=jnp.int32)[None, :]) % n
    th = (2.0 * math.pi / n) * a.astype(F32)
    fwd = jnp.concatenate([jnp.cos(th), -jnp.sin(th)], axis=0)
    inv = jnp.concatenate([jnp.cos(th).T, -jnp.sin(th).T], axis=1) * (1.0 / n)
    return fwd.astype(BF16), inv.astype(BF16)


def _ctx_mixers_kernel(uh_ref, us_ref, hw_ref, hb_ref, sw_ref, bias_ref, fwd_ref, inv_ref, z_ref,
                       w1_ref, b1_ref, f1_ref, w2_ref, b2_ref, f2_ref, w3_ref, dl_ref, ohy_ref, osc_ref, kf_ref):
    c = HY_W
    nf = kf_ref.shape[1]

    @pl.when(pl.program_id(0) == 0)
    def _():
        hf, hb = _filter_mlp(z_ref[...], w1_ref, b1_ref, f1_ref, w2_ref, b2_ref, f2_ref, w3_ref, dl_ref, 0)
        xf = _dot(fwd_ref[...], hf.astype(BF16))
        xb = _dot(fwd_ref[...], hb.astype(BF16))
        kf_ref[0] = xf[:nf] + xb[:nf]
        kf_ref[1] = xf[nf:] - xb[nf:]

    zero = jnp.zeros((1, 1), F32)
    uh = _conv3(uh_ref[...].astype(F32), zero, zero, hw_ref[...]) + hb_ref[...]
    x0, g = uh[:, :c], uh[:, c:2 * c] * uh[:, 2 * c:]
    s = _dot(fwd_ref[...], g.astype(BF16))
    sr, si = s[:nf], s[nf:]
    kr, ki = kf_ref[0], kf_ref[1]
    y = _dot(inv_ref[...], jnp.concatenate([sr * kr - si * ki, sr * ki + si * kr], axis=0).astype(BF16))
    ohy_ref[...] = (x0 * (y + g * bias_ref[...])).astype(ohy_ref.dtype)
    us = us_ref[...].astype(F32)
    osc_ref[...] = (us[:, :c] * _conv3(us[:, c:2 * c] * us[:, 2 * c:], zero, zero, sw_ref[...])).astype(osc_ref.dtype)


def _ctx_mixers(u_hy, u_sc, hy_conv_w, hy_conv_b, sc_conv_w, hy_bias, mlp, B, L):
    W = u_hy.shape[1]
    fwd, inv = _dense_dft_constants(L)
    ops, specs = _filter_mlp_operands(*mlp)
    const = lambda shape: pl.BlockSpec(shape, lambda b: (0, 0))
    seq = lambda w: pl.BlockSpec((L, w), lambda b: (b, 0))
    return pl.pallas_call(
        _ctx_mixers_kernel,
        grid=(B,),
        in_specs=[seq(W), seq(W), const((3, W)), const((1, W)), const((3, SC_W)), const((1, HY_W)),
                  const(fwd.shape), const(inv.shape), const((L, 128)), *specs],
        out_specs=[seq(HY_W), seq(SC_W)],
        out_shape=[jax.ShapeDtypeStruct((B * L, HY_W), BF16), jax.ShapeDtypeStruct((B * L, SC_W), BF16)],
        scratch_shapes=[pltpu.VMEM((2, 2 * L, HY_W), F32)],
        compiler_params=_cp(("arbitrary",)),
        name="context_mixers",
    )(u_hy, u_sc, hy_conv_w, hy_conv_b.reshape(1, W), sc_conv_w, hy_bias.reshape(1, HY_W), fwd, inv,
      _filter_features(L), *ops)


def _fft_split(L):
    n = 2 * L
    n1 = 128 if n >= 8192 else 32
    return n1, n // n1


def _fft_constants(L):
    n1, n2 = _fft_split(L)
    n = n1 * n2
    k1 = jnp.arange(n1, dtype=jnp.int32)
    a = (k1[:, None] * jnp.arange(n1 // 2, dtype=jnp.int32)[None, :]) % n1
    th = (2.0 * math.pi / n1) * a.astype(F32)
    f1 = jnp.concatenate([jnp.cos(th), -jnp.sin(th)], axis=0)
    f3 = jnp.concatenate([jnp.cos(th).T, -jnp.sin(th).T], axis=1) * (1.0 / n)
    k = k1[:, None, None] + n1 * jnp.arange(n2, dtype=jnp.int32)[None, :, None]
    ph = (k * jnp.arange(n2, dtype=jnp.int32)[None, None, :]) % n
    ph = (2.0 * math.pi / n) * ph.astype(F32)
    mr, mi = jnp.cos(ph), -jnp.sin(ph)
    mf = jnp.concatenate([jnp.concatenate([mr, -mi], axis=2), jnp.concatenate([mi, mr], axis=2)], axis=1)
    mrt, mit = jnp.swapaxes(mr, 1, 2), jnp.swapaxes(mi, 1, 2)
    mb = jnp.concatenate([jnp.concatenate([mrt, mit], axis=2), jnp.concatenate([-mit, mrt], axis=2)], axis=1)
    return tuple(c.astype(BF16) for c in (f1, f3, mf, mb))


FFT_GROUP = 8


def _dft_rows_kernel(f_ref, x_ref, o_ref):
    for i in range(FFT_GROUP):
        o_ref[:, i, :] = _dot(f_ref[...], x_ref[i].astype(BF16))


def _dft_rows(f1, xt):
    B, n2, kh, C = xt.shape
    m = f1.shape[0]
    return pl.pallas_call(
        _dft_rows_kernel,
        grid=(B, n2 // FFT_GROUP),
        in_specs=[pl.BlockSpec((m, kh), lambda b, j: (0, 0)),
                  pl.BlockSpec((None, FFT_GROUP, kh, C), lambda b, j: (b, j, 0, 0))],
        out_specs=pl.BlockSpec((None, m, FFT_GROUP, C), lambda b, j: (b, 0, j, 0)),
        out_shape=jax.ShapeDtypeStruct((B, m, n2, C), F32),
        compiler_params=_cp(("arbitrary", "arbitrary")),
        name="fft_rows",
    )(f1, xt)


def _fft_mid_kernel(a_ref, hf_ref, hb_ref, mf_ref, mb_ref, o_ref, kf_ref):
    n2 = a_ref.shape[2]

    @pl.when(pl.program_id(1) == 0)
    def _():
        for i in range(FFT_GROUP):
            xf = _dot(mf_ref[i], jnp.concatenate([hf_ref[0, i], hf_ref[1, i]], axis=0).astype(BF16))
            xb = _dot(mf_ref[i], jnp.concatenate([hb_ref[0, i], hb_ref[1, i]], axis=0).astype(BF16))
            kf_ref[i, 0] = xf[:n2] + xb[:n2]
            kf_ref[i, 1] = xf[n2:] - xb[n2:]

    for i in range(FFT_GROUP):
        x = _dot(mf_ref[i], jnp.concatenate([a_ref[0, i], a_ref[1, i]], axis=0).astype(BF16))
        xr, xi = x[:n2], x[n2:]
        kr, ki = kf_ref[i, 0], kf_ref[i, 1]
        y = jnp.concatenate([xr * kr - xi * ki, xr * ki + xi * kr], axis=0)
        o = _dot(mb_ref[i], y.astype(BF16))
        o_ref[:, 0, i, :] = o[:n2]
        o_ref[:, 1, i, :] = o[n2:]


def _fft_mid(a, ah, mf, mb):
    B, _, n1, n2, C = a.shape
    blk = (None, 2, FFT_GROUP, n2, C)
    return pl.pallas_call(
        _fft_mid_kernel,
        grid=(n1 // FFT_GROUP, B),
        in_specs=[
            pl.BlockSpec(blk, lambda k, b: (b, 0, k, 0, 0)),
            pl.BlockSpec(blk, lambda k, b: (0, 0, k, 0, 0)),
            pl.BlockSpec(blk, lambda k, b: (1, 0, k, 0, 0)),
            pl.BlockSpec((FFT_GROUP, 2 * n2, 2 * n2), lambda k, b: (k, 0, 0)),
            pl.BlockSpec((FFT_GROUP, 2 * n2, 2 * n2), lambda k, b: (k, 0, 0)),
        ],
        out_specs=pl.BlockSpec((None, n2, 2, FFT_GROUP, C), lambda k, b: (b, 0, 0, k, 0)),
        out_shape=jax.ShapeDtypeStruct((B, n2, 2, n1, C), F32),
        scratch_shapes=[pltpu.VMEM((FFT_GROUP, 2, n2, C), F32)],
        compiler_params=_cp(("arbitrary", "arbitrary")),
        name="fft_mid",
    )(a, ah, ah, mf, mb)


def _idft_rows_kernel(f_ref, b_ref, x0_ref, g_ref, bias_ref, o_ref):
    for i in range(FFT_GROUP):
        y = _dot(f_ref[...], b_ref[i].astype(BF16))
        o_ref[:, i, :] = x0_ref[:, i, :] * (y + g_ref[:, i, :] * bias_ref[...])


def _idft_rows_gate(f3, bo, x0, g, bias):
    B, n2, m2, C = bo.shape
    kh = f3.shape[0]
    tile = pl.BlockSpec((None, kh, FFT_GROUP, C), lambda b, j: (b, 0, j, 0))
    return pl.pallas_call(
        _idft_rows_kernel,
        grid=(B, n2 // FFT_GROUP),
        in_specs=[pl.BlockSpec((kh, m2), lambda b, j: (0, 0)),
                  pl.BlockSpec((None, FFT_GROUP, m2, C), lambda b, j: (b, j, 0, 0)),
                  tile, tile, pl.BlockSpec((1, C), lambda b, j: (0, 0))],
        out_specs=tile,
        out_shape=jax.ShapeDtypeStruct((B, kh, n2, C), F32),
        compiler_params=_cp(("arbitrary", "arbitrary")),
        name="ifft_rows_gate",
    )(f3, bo, x0, g, bias)


def _hyena_long_conv(x0, g, gt, filt_t, hy_bias, consts, B, L):
    C = HY_W
    n1, n2 = _fft_split(L)
    f1, f3, mf, mb = consts
    a = _dft_rows(f1, gt).reshape(B, 2, n1, n2, C)
    ah = _dft_rows(f1, filt_t).reshape(2, 2, n1, n2, C)
    bo = _fft_mid(a, ah, mf, mb).reshape(B, n2, 2 * n1, C)
    nat = lambda z: z.reshape(B, n1 // 2, n2, C)
    out = _idft_rows_gate(f3, bo, nat(x0), nat(g), hy_bias.reshape(1, C))
    return out.reshape(B * L, C)


def _outproj_kernel(oac_ref, oal_ref, ohc_ref, ohl_ref, osc_ref, osl_ref, xc_ref, xl_ref, mod_ref, w_ref,
                    lg_ref, lb_ref, rw_ref, rb_ref, x1_ref, h2_ref, meta_ref, cnt_ref, *, nct):
    m = mod_ref[0, 0]
    w = w_ref[...]
    is_ctx = pl.program_id(0) < nct
    pick = lambda c_ref, l_ref, dt: jnp.where(is_ctx, c_ref[...].astype(dt), l_ref[...].astype(dt))
    o = (_dot(pick(oac_ref, oal_ref, BF16), w[:ATT_W])
         + _dot(pick(ohc_ref, ohl_ref, BF16), w[ATT_W:ATT_W + HY_W])
         + _dot(pick(osc_ref, osl_ref, BF16), w[ATT_W + HY_W:]))
    x1 = _layer_norm(DN_ALPHA * pick(xc_ref, xl_ref, F32) + m[2:3] * o, lg_ref[...], lb_ref[...])
    x1_ref[...] = x1
    h2 = x1 * (1.0 + m[4:5]) + m[3:4]
    h2_hi = h2.astype(BF16)
    h2_ref[...] = h2_hi
    lane = lax.broadcasted_iota(jnp.int32, (h2.shape[0], 128), 1)
    h2_lo = (h2 - h2_hi.astype(F32)).astype(BF16)
    logits = _dot(h2_hi, rw_ref[0]) + _dot(h2_lo, rw_ref[0]) + _dot(h2_hi, rw_ref[1]) + rb_ref[...]
    logits = jnp.where(lane < N_EXP, logits, NEG_INF)
    picks = []
    vals = []
    for _ in range(TOP_K):
        v = jnp.max(logits, axis=-1, keepdims=True)
        idx = jnp.min(jnp.where(logits == v, lane, 128), axis=-1, keepdims=True)
        hit = lane == idx
        picks.append(hit)
        vals.append(v)
        logits = jnp.where(hit, NEG_INF, logits)
    es = [jnp.exp(v - vals[0]) for v in vals]
    den = es[0] + es[1] + es[2] + es[3]
    tb = h2.shape[0]
    msel = jnp.zeros(logits.shape, F32)
    for hit in picks:
        msel = msel + jnp.where(hit, 1.0, 0.0)
    cnt = jnp.sum(msel, axis=0, keepdims=True)
    pc = jnp.floor((cnt + (ROW_PAD - 1.0)) * (1.0 / ROW_PAD)) * ROW_PAD
    upper = jnp.where(lax.broadcasted_iota(jnp.int32, (128, 128), 0)
                      < lax.broadcasted_iota(jnp.int32, (128, 128), 1), 1.0, 0.0)
    lo = _dot(jnp.broadcast_to(pc, (8, 128)), upper, HI)[0:1]
    lower = jnp.where(lax.broadcasted_iota(jnp.int32, (tb, tb), 1)
                      < lax.broadcasted_iota(jnp.int32, (tb, tb), 0), 1.0, 0.0).astype(BF16)
    pos = lo + _dot(lower, msel.astype(BF16))
    meta = jnp.zeros(logits.shape, F32)
    for kk, (hit, e) in enumerate(zip(picks, es)):
        slot = jnp.sum(jnp.where(hit, pos, 0.0), axis=-1, keepdims=True)
        meta = jnp.where(lane == kk, slot, meta)
        meta = jnp.where(lane == TOP_K + kk, e / den, meta)
    meta_ref[...] = meta
    cnt_ref[0] = jnp.broadcast_to(cnt, (8, 128))


def _out_projection(o_attn, o_hy, o_sc, x, mod, w_out_bf, ln_g, ln_b, router_w, router_b, layer, dec_seq):
    t_ctx = x[0].shape[0]
    T = t_ctx + x[1].shape[0]
    tm = TOKEN_BLOCK
    nct = t_ctx // tm
    row = lambda i: (layer, _mod_row(i, tm, t_ctx, dec_seq), 0, 0)
    tok = lambda w: pl.BlockSpec((tm, w), lambda i: (i, 0))
    const = lambda shape: pl.BlockSpec(shape, lambda i: (0, 0))
    rw = jnp.pad(router_w, ((0, 0), (0, 128 - N_EXP)))
    rw_hi = rw.astype(BF16)
    rw = jnp.stack([rw_hi, (rw - rw_hi.astype(F32)).astype(BF16)])
    rb = jnp.pad(router_b, (0, 128 - N_EXP)).reshape(1, 128)
    return pl.pallas_call(
        functools.partial(_outproj_kernel, nct=nct),
        grid=(T // tm,),
        in_specs=[*_ctx_lat_specs((tm, ATT_W), nct), *_ctx_lat_specs((tm, HY_W), nct),
                  *_ctx_lat_specs((tm, SC_W), nct), *_ctx_lat_specs((tm, D), nct),
                  pl.BlockSpec((1, 1, 6, D), row),
                  const((D, D)), const((1, D)), const((1, D)),
                  pl.BlockSpec((2, D, 128), lambda i: (0, 0, 0)), const((1, 128))],
        out_specs=[tok(D), tok(D), tok(128), pl.BlockSpec((1, 8, 128), lambda i: (i, 0, 0))],
        out_shape=[jax.ShapeDtypeStruct((T, D), F32), jax.ShapeDtypeStruct((T, D), BF16),
                   jax.ShapeDtypeStruct((T, 128), F32), jax.ShapeDtypeStruct((T // tm, 8, 128), F32)],
        compiler_params=_cp(("arbitrary",)),
        name="out_projection_router",
    )(*o_attn, *o_hy, *o_sc, *x, mod, w_out_bf, ln_g.reshape(1, D), ln_b.reshape(1, D), rw, rb)


def _routing_tables(cnt):
    nb = cnt.shape[0]
    pc = (cnt + ROW_PAD - 1) // ROW_PAD * ROW_PAD
    lo = jnp.cumsum(pc, axis=1) - pc
    tot = jnp.sum(pc, axis=0)
    tot_pad = (tot + EXPERT_TILE - 1) // EXPERT_TILE * EXPERT_TILE
    off = jnp.cumsum(tot_pad) - tot_pad
    gstart = off[None, :] + jnp.cumsum(pc, axis=0) - pc
    ntile = tot_pad // EXPERT_TILE
    cum = jnp.cumsum(ntile)
    n_tiles = _max_expert_tiles(nb)
    i = jnp.arange(n_tiles, dtype=jnp.int32)
    ic = jnp.maximum(jnp.minimum(i, cum[-1] - 1), 0)
    te = jnp.sum((cum[None, :] <= ic[:, None]).astype(jnp.int32), axis=1)
    first = ic == (cum - ntile)[te]
    flags = jnp.where(i < cum[-1], 1 + 2 * first.astype(jnp.int32), 0)
    ex = jnp.arange(N_EXP, dtype=jnp.int32)
    later = (ex[None, :] > ex[:, None]) & (ntile[None, :] > 0)
    next_e = jnp.min(jnp.where(later, ex[None, :], N_EXP), axis=1)
    next_e = jnp.where(next_e < N_EXP, next_e, -1)
    i32 = lambda a: a.astype(jnp.int32).reshape(-1)
    return dict(pc=i32(pc), lo=i32(lo), gstart=i32(gstart), tail_start=i32(off + tot), tail_len=i32(tot_pad - tot),
                rows=i32(jnp.sum(pc, axis=1)), tile_expert=i32(te), tile_row=ic, tile_flags=flags,
                tile_next=i32(next_e[te]))


def _max_expert_tiles(nb):
    rows = TOP_K * nb * TOKEN_BLOCK + nb * N_EXP * (ROW_PAD - 1) + N_EXP * (EXPERT_TILE - 1)
    return rows // EXPERT_TILE + 1


def _run_copies(n, src_ref, src0, dst_ref, dst0, sem, wait):
    done = jnp.int32(0)
    for p in RUN_PIECES:
        take = (n & p) != 0

        @pl.when(take)
        def _():
            cp = pltpu.make_async_copy(src_ref.at[pl.ds(pl.multiple_of(src0 + done, ROW_PAD), p), :],
                                       dst_ref.at[pl.ds(pl.multiple_of(dst0 + done, ROW_PAD), p), :], sem)
            if wait:
                cp.wait()
            else:
                cp.start()

        done = done + jnp.where(take, p, 0)


def _wait_rows(n, vmem_ref, hbm_ref, sem, to_vmem):
    base = TOP_K * TOKEN_BLOCK

    def wait(p):
        src, dst = (hbm_ref, vmem_ref) if to_vmem else (vmem_ref, hbm_ref)
        pltpu.make_async_copy(src.at[pl.ds(0, p), :], dst.at[pl.ds(0, p), :], sem).wait()

    wait(base)
    for p in RUN_PIECES[1:]:
        pl.when(((n - base) & p) != 0)(functools.partial(wait, p))


SORTED_W = D + 128


def _dispatch_kernel(pc_ref, lo_ref, gs_ref, ts_ref, tl_ref, rows_ref, h_ref, meta_ref, xs_ref,
                     buf_ref, zero_ref, sem, zsem):
    b = pl.program_id(0)
    nb = pl.num_programs(0)
    s = buf_ref.shape[1]
    slot = b % 2

    @pl.when(b == 0)
    def _():
        zero_ref[...] = jnp.zeros_like(zero_ref)
        for wait in (False, True):
            def tail(e, carry):
                _run_copies(tl_ref[e], zero_ref, 0, xs_ref, ts_ref[e], zsem, wait)
                return carry
            lax.fori_loop(0, N_EXP, tail, 0)

    mt = meta_ref[...].T
    srow = lax.broadcasted_iota(jnp.int32, (s, 1), 0).astype(F32)
    perm = jnp.zeros((s, TOKEN_BLOCK), F32)
    gperm = jnp.zeros((s, TOKEN_BLOCK), F32)
    for k in range(TOP_K):
        hit = srow == mt[k:k + 1]
        perm = perm + jnp.where(hit, 1.0, 0.0)
        gperm = gperm + jnp.where(hit, mt[TOP_K + k:TOP_K + k + 1], 0.0)
    sorted_rows = _dot(perm.astype(BF16), h_ref[...])
    gate_col = jnp.sum(gperm, axis=-1, keepdims=True)

    @pl.when(b >= 2)
    def _():
        _wait_rows(rows_ref[b - 2], buf_ref.at[slot], xs_ref, sem.at[slot], to_vmem=False)

    buf = buf_ref.at[slot]
    buf[:, :D] = sorted_rows
    buf[:, D:] = jnp.broadcast_to(gate_col, (s, 128))

    def run(e, carry):
        j = b * N_EXP + e
        _run_copies(pc_ref[j], buf, lo_ref[j], xs_ref, gs_ref[j], sem.at[slot], wait=False)
        return carry
    lax.fori_loop(0, N_EXP, run, 0)

    @pl.when(b == nb - 1)
    def _():
        @pl.when(b >= 1)
        def _():
            _wait_rows(rows_ref[b - 1], buf_ref.at[1 - slot], xs_ref, sem.at[1 - slot], to_vmem=False)
        _wait_rows(rows_ref[b], buf, xs_ref, sem.at[slot], to_vmem=False)


def _dispatch(tabs, h2, meta):
    T = h2.shape[0]
    nb = T // TOKEN_BLOCK
    rows = _max_expert_tiles(nb) * EXPERT_TILE
    tok = lambda w: pl.BlockSpec((TOKEN_BLOCK, w), lambda b, *_: (b, 0))
    return pl.pallas_call(
        _dispatch_kernel,
        grid_spec=pltpu.PrefetchScalarGridSpec(
            num_scalar_prefetch=6,
            grid=(nb,),
            in_specs=[tok(D), tok(128)],
            out_specs=pl.BlockSpec(memory_space=pl.ANY),
            scratch_shapes=[pltpu.VMEM((2, LOCAL_SLOTS, SORTED_W), F32), pltpu.VMEM((EXPERT_TILE, SORTED_W), F32),
                            pltpu.SemaphoreType.DMA((2,)), pltpu.SemaphoreType.DMA(())],
        ),
        out_shape=jax.ShapeDtypeStruct((rows, SORTED_W), F32),
        compiler_params=_cp(("arbitrary",)),
        name="moe_dispatch",
    )(tabs["pc"], tabs["lo"], tabs["gstart"], tabs["tail_start"], tabs["tail_len"], tabs["rows"], h2, meta)


def _expert_ffn_kernel(te_ref, tr_ref, fl_ref, nx_ref, x_ref, wg_hbm, bg_ref, wu_hbm, bu_ref, wd_hbm, bd_ref, y_ref,
                       wf_ref, wbf_ref, sem, *, layer):
    i = pl.program_id(0)
    fl = fl_ref[i]
    weights = (wg_hbm, wu_hbm, wd_hbm)

    def weight_copy(j, expert):
        return pltpu.make_async_copy(weights[j].at[layer, expert], wf_ref.at[j], sem.at[j])

    @pl.when((fl & 1) != 0)
    def _():
        @pl.when((fl & 2) != 0)
        def _():
            @pl.when(i == 0)
            def _():
                for j in range(3):
                    weight_copy(j, te_ref[i]).start()

            for j in range(3):
                weight_copy(j, te_ref[i]).wait()
                wbf_ref[j] = wf_ref[j].astype(BF16)

            @pl.when(nx_ref[i] >= 0)
            def _():
                for j in range(3):
                    weight_copy(j, nx_ref[i]).start()

        x = x_ref[:, :D].astype(BF16)
        gate = x_ref[:, D:D + 1]
        g = jnp.minimum(_dot(x, wbf_ref[0]) + bg_ref[0], SWIGLU_LIMIT)
        u = jnp.clip(_dot(x, wbf_ref[1]) + bu_ref[0], -SWIGLU_LIMIT, SWIGLU_LIMIT)
        a = (u + 1.0) * (g * jax.nn.sigmoid(SWIGLU_ALPHA * g))
        y_ref[...] = gate * (_dot(a.astype(BF16), wbf_ref[2]) + bd_ref[0])


def _expert_ffn(tabs, xs, w_gate, b_gate, w_up, b_up, w_down, b_down, layer):
    rows = xs.shape[0]
    hbm = pl.BlockSpec(memory_space=pl.ANY)
    bspec = pl.BlockSpec((1, 1, D), lambda i, te, *_: (te[i], 0, 0))
    return pl.pallas_call(
        functools.partial(_expert_ffn_kernel, layer=layer),
        grid_spec=pltpu.PrefetchScalarGridSpec(
            num_scalar_prefetch=4,
            grid=(rows // EXPERT_TILE,),
            in_specs=[pl.BlockSpec((EXPERT_TILE, SORTED_W), lambda i, te, tr, *_: (tr[i], 0)),
                      hbm, bspec, hbm, bspec, hbm, bspec],
            out_specs=pl.BlockSpec((EXPERT_TILE, D), lambda i, te, tr, *_: (tr[i], 0)),
            scratch_shapes=[pltpu.VMEM((3, D, D), F32), pltpu.VMEM((3, D, D), BF16), pltpu.SemaphoreType.DMA((3,))],
        ),
        out_shape=jax.ShapeDtypeStruct((rows, D), F32),
        compiler_params=_cp(("arbitrary",)),
        name="expert_ffn",
    )(tabs["tile_expert"], tabs["tile_row"], tabs["tile_flags"], tabs["tile_next"], xs,
      w_gate, b_gate[layer].reshape(N_EXP, 1, D), w_up, b_up[layer].reshape(N_EXP, 1, D),
      w_down, b_down[layer].reshape(N_EXP, 1, D))


def _combine_kernel(pc_ref, lo_ref, gs_ref, rows_ref, ys_ref, meta_ref, x1_ref, mod_ref, lg_ref, lb_ref,
                    oc_ref, ol_ref, buf_ref, sem, *, nct):
    b = pl.program_id(0)
    nb = pl.num_programs(0)
    s = buf_ref.shape[1]
    slot = b % 2

    def fetch(blk, to_slot):
        def run(e, carry):
            j = blk * N_EXP + e
            _run_copies(pc_ref[j], ys_ref, gs_ref[j], buf_ref.at[to_slot], lo_ref[j], sem.at[to_slot], wait=False)
            return carry
        lax.fori_loop(0, N_EXP, run, 0)

    @pl.when(b == 0)
    def _():
        buf_ref[...] = jnp.zeros_like(buf_ref)
        fetch(b, slot)

    @pl.when(b + 1 < nb)
    def _():
        fetch(b + 1, 1 - slot)

    _wait_rows(rows_ref[b], buf_ref.at[slot], ys_ref, sem.at[slot], to_vmem=True)

    meta = meta_ref[...]
    scol = lax.broadcasted_iota(jnp.int32, (1, s), 1).astype(F32)
    w = jnp.zeros((TOKEN_BLOCK, s), F32)
    for k in range(TOP_K):
        w = w + jnp.where(meta[:, k:k + 1] == scol, 1.0, 0.0)
    w = w.astype(BF16)
    moe = _dot(w, buf_ref[slot].astype(BF16))
    m = mod_ref[0, 0]
    out = _layer_norm(DN_ALPHA * x1_ref[...] + m[5:6] * moe, lg_ref[...], lb_ref[...])

    @pl.when(b < nct)
    def _():
        oc_ref[...] = out

    @pl.when(b >= nct)
    def _():
        ol_ref[...] = out


def _combine_ln2(tabs, ys, meta, x1, mod, ln_g, ln_b, layer, t_ctx, dec_seq):
    T = x1.shape[0]
    nct = t_ctx // TOKEN_BLOCK
    row = lambda b, *_: (layer, _mod_row(b, TOKEN_BLOCK, t_ctx, dec_seq), 0, 0)
    tok = lambda w: pl.BlockSpec((TOKEN_BLOCK, w), lambda b, *_: (b, 0))
    const = lambda shape: pl.BlockSpec(shape, lambda b, *_: (0, 0))
    return pl.pallas_call(
        functools.partial(_combine_kernel, nct=nct),
        grid_spec=pltpu.PrefetchScalarGridSpec(
            num_scalar_prefetch=4,
            grid=(T // TOKEN_BLOCK,),
            in_specs=[pl.BlockSpec(memory_space=pl.ANY), tok(128), tok(D), pl.BlockSpec((1, 1, 6, D), row),
                      const((1, D)), const((1, D))],
            out_specs=list(_ctx_lat_specs((TOKEN_BLOCK, D), nct)),
            scratch_shapes=[pltpu.VMEM((2, LOCAL_SLOTS, D), F32), pltpu.SemaphoreType.DMA((2,))],
        ),
        out_shape=[jax.ShapeDtypeStruct((t_ctx, D), F32), jax.ShapeDtypeStruct((T - t_ctx, D), F32)],
        compiler_params=_cp(("arbitrary",)),
        name="moe_combine_ln2",
    )(tabs["pc"], tabs["lo"], tabs["gstart"], tabs["rows"], ys, meta, x1, mod, ln_g.reshape(1, D),
      ln_b.reshape(1, D))


def _moe_ln2(h2, meta, cnt, x1, mod, w_gate, b_gate, w_up, b_up, w_down, b_down, ln_g, ln_b, layer, t_ctx, dec_seq):
    tabs = _routing_tables(cnt[:, 0, :N_EXP].astype(jnp.int32))
    xs = _dispatch(tabs, h2, meta)
    ys = _expert_ffn(tabs, xs, w_gate, b_gate, w_up, b_up, w_down, b_down, layer)
    return _combine_ln2(tabs, ys, meta, x1, mod, ln_g, ln_b, layer, t_ctx, dec_seq)


def kernel(x_prompt, x_sample, cache_k, cache_v, c, c_ctx, w_mod, b_mod, w_in, w_out, attn_sink, hy_conv_w, hy_conv_b, hy_w1, hy_b1, hy_f1, hy_w2, hy_b2, hy_f2, hy_w3, hy_deltas, hy_bias, sc_conv_w, ln1_g, ln1_b, router_w, router_b, w_gate, b_gate, w_up, b_up, w_down, b_down, ln2_g, ln2_b):
    nb, seq, _ = x_prompt.shape
    db, dseq, _ = x_sample.shape
    past = cache_k.shape[2]
    t_ctx = nb * seq
    x = (x_prompt.reshape(t_ctx, D), x_sample.reshape(db * dseq, D))

    cond = jnp.zeros((8, D), F32).at[0].set(c_ctx).at[1:1 + db].set(c)
    mod = _modulation(cond, w_mod, b_mod)

    cos64, sin64 = _rope_tables(dseq)
    tables = (jnp.tile(cos64, (1, N_HEADS)), jnp.tile(sin64, (1, N_HEADS)),
              jnp.tile(cos64, (1, N_KV)), jnp.tile(sin64, (1, N_KV)))
    fft_c = _fft_constants(dseq)
    w_in_bf = w_in.astype(BF16)
    w_out_bf = w_out.astype(BF16)

    new_k, new_v = [], []
    for l in range(DEPTH):
        q, k, v, u_hy, u_sc = _in_projection(x[0], x[1], mod, w_in_bf[l], l, dseq)
        new_k.append(k[:t_ctx].reshape(nb, seq, N_KV, HD))
        new_v.append(v[:t_ctx].reshape(nb, seq, N_KV, HD))
        oa_ctx = _context_attention(q, k, v, attn_sink[l], nb, seq)
        oa_lat = _latent_attention(q, k, v, cache_k[:, l].reshape(db, past, KV_W),
                                   cache_v[:, l].reshape(db, past, KV_W), attn_sink[l], tables, t_ctx, db, dseq)
        mlp = (hy_w1[l], hy_b1[l], hy_f1[l], hy_w2[l], hy_b2[l], hy_f2[l], hy_w3[l], hy_deltas[l])
        ohy_ctx, osc_ctx = _ctx_mixers(u_hy, u_sc, hy_conv_w[l], hy_conv_b[l], sc_conv_w[l], hy_bias[l], mlp,
                                       nb, seq)
        x0, g, gt, osc_lat = _mixer_prep(u_hy, u_sc, hy_conv_w[l], hy_conv_b[l], sc_conv_w[l], t_ctx, db, dseq)
        ohy_lat = _hyena_long_conv(x0, g, gt, _hyena_filters(dseq, mlp), hy_bias[l], fft_c, db, dseq)
        x1, h2, meta, cnt = _out_projection((oa_ctx, oa_lat), (ohy_ctx, ohy_lat), (osc_ctx, osc_lat), x, mod,
                                            w_out_bf[l], ln1_g[l], ln1_b[l], router_w[l], router_b[l], l, dseq)
        x = _moe_ln2(h2, meta, cnt, x1, mod, w_gate, b_gate, w_up, b_up, w_down, b_down, ln2_g[l], ln2_b[l],
                     l, t_ctx, dseq)

    y_prompt = x[0].reshape(nb, seq, D)
    y_sample = x[1].reshape(db, dseq, D)
    return (y_prompt, y_sample, jnp.stack(new_k, axis=1), jnp.stack(new_v, axis=1))
```

```python
import functools
import math

import jax
import jax.numpy as jnp
from jax import lax
from jax.experimental import pallas as pl
from jax.experimental.pallas import tpu as pltpu

F32 = jnp.float32
BF16 = jnp.bfloat16
HI = lax.Precision.HIGHEST

D = 1024
DEPTH = 2
N_HEADS = 8
N_KV = 2
HD = 64
GQA = N_HEADS // N_KV
ATT_W = N_HEADS * HD
KV_W = N_KV * HD
WINDOW = 128
BLK = 128
GRID_W = 64
ROPE_BASE = 10000.0
HY_W = 256
SC_W = 256
FILTER_EMB = 33
FILTER_BANDS = 16
FILTER_HIDDEN = 64
IN_W = ATT_W + 2 * KV_W + 3 * HY_W + 3 * SC_W
N_EXP = 32
TOP_K = 4
SWIGLU_LIMIT = 7.0
SWIGLU_ALPHA = 1.702
DN_ALPHA = (2 * DEPTH) ** 0.25
LN_EPS = 1e-5
NEG_INF = float("-inf")

VMEM_LIMIT = 56 * 1024 * 1024

TOKEN_BLOCK = 256
ROW_PAD = 8
RUN_PIECES = (256, 128, 64, 32, 16, 8)
EXPERT_TILE = 512
LOCAL_SLOTS = TOP_K * TOKEN_BLOCK + N_EXP * ROW_PAD


def _cp(sem):
    return pltpu.CompilerParams(dimension_semantics=sem, vmem_limit_bytes=VMEM_LIMIT)


def _dot(a, b, precision=None):
    return jnp.dot(a, b, preferred_element_type=F32, precision=precision)


def _dot_nt(a, b):
    return lax.dot_general(a, b, (((1,), (1,)), ((), ())), preferred_element_type=F32)


def _layer_norm(z, g, b):
    mu = jnp.mean(z, axis=-1, keepdims=True)
    zc = z - mu
    var = jnp.mean(zc * zc, axis=-1, keepdims=True)
    return zc * lax.rsqrt(var + LN_EPS) * g + b


def _mod_kernel(cond_ref, w_ref, b_ref, o_ref):
    c = cond_ref[...]
    s = c * jax.nn.sigmoid(c)
    o_ref[0] = _dot(s, w_ref[0], HI) + b_ref[0]


def _modulation(cond, w_mod, b_mod):
    tn = 1024
    out = pl.pallas_call(
        _mod_kernel,
        grid=(DEPTH, 6 * D // tn),
        in_specs=[
            pl.BlockSpec((8, D), lambda l, j: (0, 0)),
            pl.BlockSpec((1, D, tn), lambda l, j: (l, 0, j)),
            pl.BlockSpec((1, 1, tn), lambda l, j: (l, 0, j)),
        ],
        out_specs=pl.BlockSpec((1, 8, tn), lambda l, j: (l, 0, j)),
        out_shape=jax.ShapeDtypeStruct((DEPTH, 8, 6 * D), F32),
        compiler_params=_cp(("arbitrary", "arbitrary")),
        name="modulation",
    )(cond, w_mod, b_mod.reshape(DEPTH, 1, 6 * D))
    return out.reshape(DEPTH, 8, 6, D)


def _mod_row(i, tm, t_ctx, dec_seq):
    nct = t_ctx // tm
    return jnp.where(i < nct, 0, 1 + (i - nct) // (dec_seq // tm))


def _ctx_lat_specs(shape, nct, ctx_offset=0, lat_offset=0):
    ctx = pl.BlockSpec(shape, lambda i, *_: (jnp.minimum(i, nct - 1) + ctx_offset, 0))
    lat = pl.BlockSpec(shape, lambda i, *_: (jnp.maximum(i - nct, 0) + lat_offset, 0))
    return ctx, lat


def _inproj_kernel(xc_ref, xl_ref, mod_ref, w_ref, q_ref, k_ref, v_ref, uh_ref, us_ref, *, nct):
    m = mod_ref[0, 0]
    x = jnp.where(pl.program_id(0) < nct, xc_ref[...], xl_ref[...])
    h = x * (1.0 + m[1:2]) + m[0:1]
    y = _dot(h.astype(BF16), w_ref[...])
    o0 = ATT_W
    o1 = o0 + KV_W
    o2 = o1 + KV_W
    o3 = o2 + 3 * HY_W
    q_ref[...] = y[:, :o0].astype(q_ref.dtype)
    k_ref[...] = y[:, o0:o1]
    v_ref[...] = y[:, o1:o2]
    uh_ref[...] = y[:, o2:o3].astype(uh_ref.dtype)
    us_ref[...] = y[:, o3:].astype(us_ref.dtype)


def _in_projection(xc, xl, mod, w_in_bf, layer, dec_seq):
    t_ctx = xc.shape[0]
    T = t_ctx + xl.shape[0]
    tm = 512
    row = lambda i: (layer, _mod_row(i, tm, t_ctx, dec_seq), 0, 0)
    tok = lambda w: pl.BlockSpec((tm, w), lambda i: (i, 0))
    widths = (ATT_W, KV_W, KV_W, 3 * HY_W, 3 * SC_W)
    return pl.pallas_call(
        functools.partial(_inproj_kernel, nct=t_ctx // tm),
        grid=(T // tm,),
        in_specs=[
            *_ctx_lat_specs((tm, D), t_ctx // tm),
            pl.BlockSpec((1, 1, 6, D), row),
            pl.BlockSpec((D, IN_W), lambda i: (0, 0)),
        ],
        out_specs=[tok(w) for w in widths],
        out_shape=[jax.ShapeDtypeStruct((T, w), dt) for w, dt in zip(widths, (BF16, F32, F32, BF16, BF16))],
        compiler_params=_cp(("arbitrary",)),
        name="in_projection",
    )(xc, xl, mod, w_in_bf)


def _swap_halves(x):
    w = x.shape[-1]
    lane = lax.broadcasted_iota(jnp.int32, x.shape, 1)
    return jnp.where((lane % HD) < HD // 2, pltpu.roll(x, w - HD // 2, 1), pltpu.roll(x, HD // 2, 1))


def _rope(x, cos, sin_signed):
    return x * cos + _swap_halves(x) * sin_signed


def _group_rows(q, g):
    return jnp.concatenate([q[:, (GQA * g + r) * HD:(GQA * g + r + 1) * HD] for r in range(GQA)], axis=0)


def _sink_column(sink_ref, g, rows):
    ridx = lax.broadcasted_iota(jnp.int32, (GQA * rows, 1), 0)
    col = jnp.full((GQA * rows, 1), sink_ref[GQA * g + GQA - 1], F32)
    for r in range(GQA - 2, -1, -1):
        col = jnp.where(ridx < (r + 1) * rows, sink_ref[GQA * g + r], col)
    return col


def _ungroup(outs, rows):
    return jnp.concatenate([o[r * rows:(r + 1) * rows] for o in outs for r in range(GQA)], axis=1)


def _ctx_attn_kernel(sink_ref, q_ref, k_ref, v_ref, o_ref):
    rows = q_ref.shape[0]
    q = q_ref[...].astype(F32) * (HD ** -0.5)
    k = k_ref[...]
    v = v_ref[...]
    outs = []
    for g in range(N_KV):
        qg = _group_rows(q, g).astype(BF16)
        kg = k[:, g * HD:(g + 1) * HD].astype(BF16)
        vg = v[:, g * HD:(g + 1) * HD].astype(BF16)
        s = _dot_nt(qg, kg)
        sink = _sink_column(sink_ref, g, rows)
        m = jnp.maximum(jnp.max(s, axis=-1, keepdims=True), sink)
        e = jnp.exp(s - m)
        den = jnp.sum(e, axis=-1, keepdims=True) + jnp.exp(sink - m)
        outs.append(_dot(e.astype(BF16), vg) / den)
    o_ref[...] = _ungroup(outs, rows).astype(o_ref.dtype)


def _context_attention(q, k, v, sink, B, L):
    tok = lambda w: pl.BlockSpec((L, w), lambda b: (b, 0))
    return pl.pallas_call(
        _ctx_attn_kernel,
        grid=(B,),
        in_specs=[pl.BlockSpec(memory_space=pltpu.SMEM), tok(ATT_W), tok(KV_W), tok(KV_W)],
        out_specs=tok(ATT_W),
        out_shape=jax.ShapeDtypeStruct((B * L, ATT_W), BF16),
        compiler_params=_cp(("arbitrary",)),
        name="context_attention",
    )(sink, q, k, v)


def _lat_attn_kernel(sink_ref, q_ref, k_ref, v_ref, kc_ref, vc_ref, cq_ref, sq_ref, ck_ref, sk_ref, o_ref):
    n = pl.program_id(1)
    L = k_ref.shape[0]
    nk = 3 * BLK
    ws = pl.multiple_of(jnp.clip((n - 1) * BLK, 0, L - nk), BLK)
    q = _rope(q_ref[...].astype(F32), cq_ref[...], sq_ref[...]) * (HD ** -0.5)
    kl = _rope(k_ref[pl.ds(ws, nk), :], ck_ref[pl.ds(ws, nk), :], sk_ref[pl.ds(ws, nk), :])
    vl = v_ref[pl.ds(ws, nk), :]
    kc = kc_ref[...]
    vc = vc_ref[...]
    qpos = n * BLK + lax.broadcasted_iota(jnp.int32, (GQA * BLK, 1), 0) % BLK
    kpos = ws + lax.broadcasted_iota(jnp.int32, (1, nk), 1)
    valid = jnp.abs(kpos - qpos) <= WINDOW
    outs = []
    for g in range(N_KV):
        sl = slice(g * HD, (g + 1) * HD)
        qg = _group_rows(q, g).astype(BF16)
        s_loc = jnp.where(valid, _dot_nt(qg, kl[:, sl].astype(BF16)), NEG_INF)
        s_ctx = _dot_nt(qg, kc[:, sl].astype(BF16))
        sink = _sink_column(sink_ref, g, BLK)
        m = jnp.maximum(jnp.maximum(jnp.max(s_loc, axis=-1, keepdims=True),
                                    jnp.max(s_ctx, axis=-1, keepdims=True)), sink)
        e_loc = jnp.exp(s_loc - m)
        e_ctx = jnp.exp(s_ctx - m)
        den = (jnp.sum(e_loc, axis=-1, keepdims=True) + jnp.sum(e_ctx, axis=-1, keepdims=True)
               + jnp.exp(sink - m))
        o = _dot(e_loc.astype(BF16), vl[:, sl].astype(BF16)) + _dot(e_ctx.astype(BF16), vc[:, sl].astype(BF16))
        outs.append(o / den)
    o_ref[...] = _ungroup(outs, BLK).astype(o_ref.dtype)


def _rope_tables(L):
    rows = L // GRID_W
    row = jnp.repeat(jnp.arange(rows, dtype=F32), GRID_W)
    col = jnp.tile(jnp.arange(GRID_W, dtype=F32), rows)
    pairs = HD // 4
    inv = ROPE_BASE ** (-jnp.arange(pairs, dtype=F32) / pairs)
    ang = jnp.concatenate([row[:, None] * inv, col[:, None] * inv], axis=-1)
    cos = jnp.cos(ang)
    sin = jnp.sin(ang)
    return jnp.concatenate([cos, cos], axis=-1), jnp.concatenate([-sin, sin], axis=-1)


def _latent_attention(q, k, v, kc, vc, sink, tables, t_ctx, B, L):
    assert t_ctx % L == 0
    P = kc.shape[1]
    cq, sq, ck, sk = tables
    nbk = L // BLK
    seq = pl.BlockSpec((L, KV_W), lambda b, n: (t_ctx // L + b, 0))
    ctx = pl.BlockSpec((None, P, KV_W), lambda b, n: (b, 0, 0))
    return pl.pallas_call(
        _lat_attn_kernel,
        grid=(B, nbk),
        in_specs=[
            pl.BlockSpec(memory_space=pltpu.SMEM),
            pl.BlockSpec((BLK, ATT_W), lambda b, n: (t_ctx // BLK + b * nbk + n, 0)),
            seq, seq, ctx, ctx,
            pl.BlockSpec((BLK, ATT_W), lambda b, n: (n, 0)),
            pl.BlockSpec((BLK, ATT_W), lambda b, n: (n, 0)),
            pl.BlockSpec((L, KV_W), lambda b, n: (0, 0)),
            pl.BlockSpec((L, KV_W), lambda b, n: (0, 0)),
        ],
        out_specs=pl.BlockSpec((BLK, ATT_W), lambda b, n: (b * nbk + n, 0)),
        out_shape=jax.ShapeDtypeStruct((B * L, ATT_W), BF16),
        compiler_params=_cp(("arbitrary", "arbitrary")),
        name="latent_attention",
    )(sink, q, k, v, kc, vc, cq, sq, ck, sk)


def _conv3(u, prev_row, next_row, w):
    tl = u.shape[0]
    ridx = lax.broadcasted_iota(jnp.int32, (tl, 1), 0)
    dn = jnp.where(ridx == 0, prev_row, pltpu.roll(u, 1, 0))
    up = jnp.where(ridx == tl - 1, next_row, pltpu.roll(u, tl - 1, 0))
    return dn * w[0:1] + u * w[1:2] + up * w[2:3]


def _store_time_major_inner(dst_ref, src, scr_ref):
    n2 = dst_ref.shape[0]
    rows = src.shape[0] // n2
    for h in range(scr_ref.shape[0]):
        scr_ref[h] = src[:, h * 128:(h + 1) * 128]
    for j in range(n2):
        for h in range(scr_ref.shape[0]):
            dst_ref[j, :, h * 128:(h + 1) * 128] = scr_ref[h, pl.ds(j, rows, stride=n2), :]


def _mixprep_kernel(uh_ref, uhp_ref, uhn_ref, us_ref, usp_ref, usn_ref, hw_ref, hb_ref, sw_ref,
                    x0_ref, g_ref, gt_ref, osc_ref, scr_ref):
    t = pl.program_id(1)
    first = t == 0
    last = t == pl.num_programs(1) - 1
    c = HY_W
    hw = hw_ref[...]
    halo = uhp_ref.shape[0]
    f32 = lambda a: a.astype(F32)
    uh = _conv3(f32(uh_ref[...]), jnp.where(first, 0.0, f32(uhp_ref[halo - 1:halo])),
                jnp.where(last, 0.0, f32(uhn_ref[0:1])), hw)
    uh = uh + hb_ref[...]
    x0_ref[...] = uh[:, :c]
    g = uh[:, c:2 * c] * uh[:, 2 * c:]
    g_ref[...] = g
    _store_time_major_inner(gt_ref, g, scr_ref)
    us = f32(us_ref[...])
    usp = f32(usp_ref[halo - 1:halo])
    usn = f32(usn_ref[0:1])
    prod = us[:, c:2 * c] * us[:, 2 * c:]
    pprev = jnp.where(first, 0.0, usp[:, c:2 * c] * usp[:, 2 * c:])
    pnext = jnp.where(last, 0.0, usn[:, c:2 * c] * usn[:, 2 * c:])
    osc_ref[...] = (us[:, :c] * _conv3(prod, pprev, pnext, sw_ref[...])).astype(osc_ref.dtype)


def _mixer_prep(u_hy, u_sc, hy_conv_w, hy_conv_b, sc_conv_w, row0, B, L):
    W = u_hy.shape[1]
    n1, n2 = _fft_split(L)
    tl = 8 * n2
    assert row0 % tl == 0 and L % tl == 0
    nt = L // tl
    halo = 16
    rh = tl // halo
    base = lambda b: row0 // halo + b * (L // halo)
    main = pl.BlockSpec((tl, W), lambda b, t: (row0 // tl + b * nt + t, 0))
    prev = pl.BlockSpec((halo, W), lambda b, t: (base(b) + jnp.maximum(t * rh - 1, 0), 0))
    nxt = pl.BlockSpec((halo, W), lambda b, t: (base(b) + jnp.minimum((t + 1) * rh, L // halo - 1), 0))
    const = lambda shape: pl.BlockSpec(shape, lambda b, t: (0, 0))
    out = pl.BlockSpec((tl, HY_W), lambda b, t: (b * nt + t, 0))
    flat = jax.ShapeDtypeStruct((B * L, HY_W), F32)
    return pl.pallas_call(
        _mixprep_kernel,
        grid=(B, nt),
        in_specs=[main, prev, nxt, main, prev, nxt, const((3, W)), const((1, W)), const((3, SC_W))],
        out_specs=[out, out, pl.BlockSpec((None, n2, 8, HY_W), lambda b, t: (b, 0, t, 0)), out],
        out_shape=[flat, flat, jax.ShapeDtypeStruct((B, n2, n1 // 2, HY_W), F32),
                   jax.ShapeDtypeStruct((B * L, SC_W), BF16)],
        scratch_shapes=[pltpu.VMEM((HY_W // 128, tl, 128), F32)],
        compiler_params=_cp(("arbitrary", "arbitrary")),
        name="mixer_prep",
    )(u_hy, u_hy, u_hy, u_sc, u_sc, u_sc, hy_conv_w, hy_conv_b.reshape(1, W), sc_conv_w)


def _filter_mlp(z, w1_ref, b1_ref, f1_ref, w2_ref, b2_ref, f2_ref, w3_ref, dl_ref, row0):
    tl = z.shape[0]
    h = jnp.sin(f1_ref[...] * (_dot(z, w1_ref[...], HI) + b1_ref[...]))
    h = jnp.sin(f2_ref[...] * (_dot(h, w2_ref[...], HI) + b2_ref[...]))
    h = _dot(h, w3_ref[...], HI) * jnp.exp(-z[:, 0:1] * jnp.abs(dl_ref[...]))
    row = row0 + lax.broadcasted_iota(jnp.int32, (tl, 1), 0)
    return h[:, :HY_W], jnp.where(row == 0, 0.0, h[:, HY_W:])


def _filter_kernel(z_ref, w1_ref, b1_ref, f1_ref, w2_ref, b2_ref, f2_ref, w3_ref, dl_ref, o_ref, scr_ref):
    hf, hb = _filter_mlp(z_ref[...], w1_ref, b1_ref, f1_ref, w2_ref, b2_ref, f2_ref, w3_ref, dl_ref,
                         pl.program_id(0) * z_ref.shape[0])
    _store_time_major_inner(o_ref.at[0], hf, scr_ref)
    _store_time_major_inner(o_ref.at[1], hb, scr_ref)


def _filter_features(L):
    t = jnp.linspace(0.0, 1.0, L, dtype=F32)[:, None]
    w = 2.0 * math.pi * jnp.arange(L, dtype=F32)[:, None] / L
    bands = jnp.linspace(1e-4, FILTER_BANDS - 1, FILTER_BANDS, dtype=F32)[None, :]
    z = jnp.concatenate([t, jnp.cos(bands * w), -jnp.sin(bands * w)], axis=-1)
    return jnp.pad(z, ((0, 0), (0, 128 - FILTER_EMB)))


def _filter_mlp_operands(w1, b1, f1, w2, b2, f2, w3, deltas):
    H = FILTER_HIDDEN
    ops = (jnp.pad(w1, ((0, 128 - FILTER_EMB), (0, 0))), b1.reshape(1, H), f1.reshape(1, H), w2, b2.reshape(1, H),
           f2.reshape(1, H), w3, deltas.reshape(1, 2 * HY_W))
    specs = [pl.BlockSpec(o.shape, lambda *_: (0, 0)) for o in ops]
    return ops, specs


def _hyena_filters(L, mlp):
    n1, n2 = _fft_split(L)
    tl = 8 * n2
    ops, specs = _filter_mlp_operands(*mlp)
    return pl.pallas_call(
        _filter_kernel,
        grid=(L // tl,),
        in_specs=[pl.BlockSpec((tl, 128), lambda i: (i, 0)), *specs],
        out_specs=pl.BlockSpec((2, n2, 8, HY_W), lambda i: (0, 0, i, 0)),
        out_shape=jax.ShapeDtypeStruct((2, n2, n1 // 2, HY_W), F32),
        scratch_shapes=[pltpu.VMEM((HY_W // 128, tl, 128), F32)],
        compiler_params=_cp(("arbitrary",)),
        name="hyena_filters",
    )(_filter_features(L), *ops)


def _dense_dft_constants(L):
    n = 2 * L
    a = (jnp.arange(n, dtype=jnp.int32)[:, None] * jnp.arange(L, dtype=jnp.int32)[None, :]) % n
    th = (2.0 * math.pi / n) * a.astype(F32)
    fwd = jnp.concatenate([jnp.cos(th), -jnp.sin(th)], axis=0)
    inv = jnp.concatenate([jnp.cos(th).T, -jnp.sin(th).T], axis=1) * (1.0 / n)
    return fwd.astype(BF16), inv.astype(BF16)


def _ctx_mixers_kernel(uh_ref, us_ref, hw_ref, hb_ref, sw_ref, bias_ref, fwd_ref, inv_ref, z_ref,
                       w1_ref, b1_ref, f1_ref, w2_ref, b2_ref, f2_ref, w3_ref, dl_ref, ohy_ref, osc_ref, kf_ref):
    c = HY_W
    nf = kf_ref.shape[1]

    @pl.when(pl.program_id(0) == 0)
    def _():
        hf, hb = _filter_mlp(z_ref[...], w1_ref, b1_ref, f1_ref, w2_ref, b2_ref, f2_ref, w3_ref, dl_ref, 0)
        xf = _dot(fwd_ref[...], hf.astype(BF16))
        xb = _dot(fwd_ref[...], hb.astype(BF16))
        kf_ref[0] = xf[:nf] + xb[:nf]
        kf_ref[1] = xf[nf:] - xb[nf:]

    zero = jnp.zeros((1, 1), F32)
    uh = _conv3(uh_ref[...].astype(F32), zero, zero, hw_ref[...]) + hb_ref[...]
    x0, g = uh[:, :c], uh[:, c:2 * c] * uh[:, 2 * c:]
    s = _dot(fwd_ref[...], g.astype(BF16))
    sr, si = s[:nf], s[nf:]
    kr, ki = kf_ref[0], kf_ref[1]
    y = _dot(inv_ref[...], jnp.concatenate([sr * kr - si * ki, sr * ki + si * kr], axis=0).astype(BF16))
    ohy_ref[...] = (x0 * (y + g * bias_ref[...])).astype(ohy_ref.dtype)
    us = us_ref[...].astype(F32)
    osc_ref[...] = (us[:, :c] * _conv3(us[:, c:2 * c] * us[:, 2 * c:], zero, zero, sw_ref[...])).astype(osc_ref.dtype)


def _ctx_mixers(u_hy, u_sc, hy_conv_w, hy_conv_b, sc_conv_w, hy_bias, mlp, B, L):
    W = u_hy.shape[1]
    fwd, inv = _dense_dft_constants(L)
    ops, specs = _filter_mlp_operands(*mlp)
    const = lambda shape: pl.BlockSpec(shape, lambda b: (0, 0))
    seq = lambda w: pl.BlockSpec((L, w), lambda b: (b, 0))
    return pl.pallas_call(
        _ctx_mixers_kernel,
        grid=(B,),
        in_specs=[seq(W), seq(W), const((3, W)), const((1, W)), const((3, SC_W)), const((1, HY_W)),
                  const(fwd.shape), const(inv.shape), const((L, 128)), *specs],
        out_specs=[seq(HY_W), seq(SC_W)],
        out_shape=[jax.ShapeDtypeStruct((B * L, HY_W), BF16), jax.ShapeDtypeStruct((B * L, SC_W), BF16)],
        scratch_shapes=[pltpu.VMEM((2, 2 * L, HY_W), F32)],
        compiler_params=_cp(("arbitrary",)),
        name="context_mixers",
    )(u_hy, u_sc, hy_conv_w, hy_conv_b.reshape(1, W), sc_conv_w, hy_bias.reshape(1, HY_W), fwd, inv,
      _filter_features(L), *ops)


def _fft_split(L):
    n = 2 * L
    n1 = 128 if n >= 8192 else 32
    return n1, n // n1


def _fft_constants(L):
    n1, n2 = _fft_split(L)
    n = n1 * n2
    k1 = jnp.arange(n1, dtype=jnp.int32)
    a = (k1[:, None] * jnp.arange(n1 // 2, dtype=jnp.int32)[None, :]) % n1
    th = (2.0 * math.pi / n1) * a.astype(F32)
    f1 = jnp.concatenate([jnp.cos(th), -jnp.sin(th)], axis=0)
    f3 = jnp.concatenate([jnp.cos(th).T, -jnp.sin(th).T], axis=1) * (1.0 / n)
    k = k1[:, None, None] + n1 * jnp.arange(n2, dtype=jnp.int32)[None, :, None]
    ph = (k * jnp.arange(n2, dtype=jnp.int32)[None, None, :]) % n
    ph = (2.0 * math.pi / n) * ph.astype(F32)
    mr, mi = jnp.cos(ph), -jnp.sin(ph)
    mf = jnp.concatenate([jnp.concatenate([mr, -mi], axis=2), jnp.concatenate([mi, mr], axis=2)], axis=1)
    mrt, mit = jnp.swapaxes(mr, 1, 2), jnp.swapaxes(mi, 1, 2)
    mb = jnp.concatenate([jnp.concatenate([mrt, mit], axis=2), jnp.concatenate([-mit, mrt], axis=2)], axis=1)
    return tuple(c.astype(BF16) for c in (f1, f3, mf, mb))


FFT_GROUP = 8


def _dft_rows_kernel(f_ref, x_ref, o_ref):
    for i in range(FFT_GROUP):
        o_ref[:, i, :] = _dot(f_ref[...], x_ref[i].astype(BF16))


def _dft_rows(f1, xt):
    B, n2, kh, C = xt.shape
    m = f1.shape[0]
    return pl.pallas_call(
        _dft_rows_kernel,
        grid=(B, n2 // FFT_GROUP),
        in_specs=[pl.BlockSpec((m, kh), lambda b, j: (0, 0)),
                  pl.BlockSpec((None, FFT_GROUP, kh, C), lambda b, j: (b, j, 0, 0))],
        out_specs=pl.BlockSpec((None, m, FFT_GROUP, C), lambda b, j: (b, 0, j, 0)),
        out_shape=jax.ShapeDtypeStruct((B, m, n2, C), F32),
        compiler_params=_cp(("arbitrary", "arbitrary")),
        name="fft_rows",
    )(f1, xt)


def _fft_mid_kernel(a_ref, hf_ref, hb_ref, mf_ref, mb_ref, o_ref, kf_ref):
    n2 = a_ref.shape[2]

    @pl.when(pl.program_id(1) == 0)
    def _():
        for i in range(FFT_GROUP):
            xf = _dot(mf_ref[i], jnp.concatenate([hf_ref[0, i], hf_ref[1, i]], axis=0).astype(BF16))
            xb = _dot(mf_ref[i], jnp.concatenate([hb_ref[0, i], hb_ref[1, i]], axis=0).astype(BF16))
            kf_ref[i, 0] = xf[:n2] + xb[:n2]
            kf_ref[i, 1] = xf[n2:] - xb[n2:]

    for i in range(FFT_GROUP):
        x = _dot(mf_ref[i], jnp.concatenate([a_ref[0, i], a_ref[1, i]], axis=0).astype(BF16))
        xr, xi = x[:n2], x[n2:]
        kr, ki = kf_ref[i, 0], kf_ref[i, 1]
        y = jnp.concatenate([xr * kr - xi * ki, xr * ki + xi * kr], axis=0)
        o = _dot(mb_ref[i], y.astype(BF16))
        o_ref[:, 0, i, :] = o[:n2]
        o_ref[:, 1, i, :] = o[n2:]


def _fft_mid(a, ah, mf, mb):
    B, _, n1, n2, C = a.shape
    blk = (None, 2, FFT_GROUP, n2, C)
    return pl.pallas_call(
        _fft_mid_kernel,
        grid=(n1 // FFT_GROUP, B),
        in_specs=[
            pl.BlockSpec(blk, lambda k, b: (b, 0, k, 0, 0)),
            pl.BlockSpec(blk, lambda k, b: (0, 0, k, 0, 0)),
            pl.BlockSpec(blk, lambda k, b: (1, 0, k, 0, 0)),
            pl.BlockSpec((FFT_GROUP, 2 * n2, 2 * n2), lambda k, b: (k, 0, 0)),
            pl.BlockSpec((FFT_GROUP, 2 * n2, 2 * n2), lambda k, b: (k, 0, 0)),
        ],
        out_specs=pl.BlockSpec((None, n2, 2, FFT_GROUP, C), lambda k, b: (b, 0, 0, k, 0)),
        out_shape=jax.ShapeDtypeStruct((B, n2, 2, n1, C), F32),
        scratch_shapes=[pltpu.VMEM((FFT_GROUP, 2, n2, C), F32)],
        compiler_params=_cp(("arbitrary", "arbitrary")),
        name="fft_mid",
    )(a, ah, ah, mf, mb)


def _idft_rows_kernel(f_ref, b_ref, x0_ref, g_ref, bias_ref, o_ref):
    for i in range(FFT_GROUP):
        o_ref[:, i, :] = _dot(f_ref[...], b_ref[i].astype(BF16))
    o_ref[...] = x0_ref[...] * (o_ref[...] + g_ref[...] * bias_ref[...])


def _idft_rows_gate(f3, bo, x0, g, bias):
    B, n2, m2, C = bo.shape
    kh = f3.shape[0]
    tile = pl.BlockSpec((None, kh, FFT_GROUP, C), lambda b, j: (b, 0, j, 0))
    return pl.pallas_call(
        _idft_rows_kernel,
        grid=(B, n2 // FFT_GROUP),
        in_specs=[pl.BlockSpec((kh, m2), lambda b, j: (0, 0)),
                  pl.BlockSpec((None, FFT_GROUP, m2, C), lambda b, j: (b, j, 0, 0)),
                  tile, tile, pl.BlockSpec((1, C), lambda b, j: (0, 0))],
        out_specs=tile,
        out_shape=jax.ShapeDtypeStruct((B, kh, n2, C), F32),
        compiler_params=_cp(("arbitrary", "arbitrary")),
        name="ifft_rows_gate",
    )(f3, bo, x0, g, bias)


def _hyena_long_conv(x0, g, gt, filt_t, hy_bias, consts, B, L):
    C = HY_W
    n1, n2 = _fft_split(L)
    f1, f3, mf, mb = consts
    a = _dft_rows(f1, gt).reshape(B, 2, n1, n2, C)
    ah = _dft_rows(f1, filt_t).reshape(2, 2, n1, n2, C)
    bo = _fft_mid(a, ah, mf, mb).reshape(B, n2, 2 * n1, C)
    nat = lambda z: z.reshape(B, n1 // 2, n2, C)
    out = _idft_rows_gate(f3, bo, nat(x0), nat(g), hy_bias.reshape(1, C))
    return out.reshape(B * L, C)


def _outproj_kernel(oac_ref, oal_ref, ohc_ref, ohl_ref, osc_ref, osl_ref, xc_ref, xl_ref, mod_ref, w_ref,
                    lg_ref, lb_ref, rw_ref, rb_ref, x1_ref, h2_ref, meta_ref, cnt_ref, *, nct):
    m = mod_ref[0, 0]
    w = w_ref[...]
    is_ctx = pl.program_id(0) < nct
    pick = lambda c_ref, l_ref, dt: jnp.where(is_ctx, c_ref[...].astype(dt), l_ref[...].astype(dt))
    o = (_dot(pick(oac_ref, oal_ref, BF16), w[:ATT_W])
         + _dot(pick(ohc_ref, ohl_ref, BF16), w[ATT_W:ATT_W + HY_W])
         + _dot(pick(osc_ref, osl_ref, BF16), w[ATT_W + HY_W:]))
    x1 = _layer_norm(DN_ALPHA * pick(xc_ref, xl_ref, F32) + m[2:3] * o, lg_ref[...], lb_ref[...])
    x1_ref[...] = x1
    h2 = x1 * (1.0 + m[4:5]) + m[3:4]
    h2_hi = h2.astype(BF16)
    h2_ref[...] = h2_hi
    h2_lo = (h2 - h2_hi.astype(F32)).astype(BF16)
    logits = _dot(h2_hi, rw_ref[0]) + _dot(h2_lo, rw_ref[0]) + _dot(h2_hi, rw_ref[1]) + rb_ref[...]
    for sb in range(h2.shape[0] // TOKEN_BLOCK):
        rows = slice(sb * TOKEN_BLOCK, (sb + 1) * TOKEN_BLOCK)
        meta, cnt = _route_block(logits[rows])
        meta_ref[rows, :] = meta
        cnt_ref[sb] = jnp.broadcast_to(cnt, (8, 128))


def _route_block(logits):
    lane = lax.broadcasted_iota(jnp.int32, logits.shape, 1)
    logits = jnp.where(lane < N_EXP, logits, NEG_INF)
    picks = []
    vals = []
    for _ in range(TOP_K):
        v = jnp.max(logits, axis=-1, keepdims=True)
        idx = jnp.min(jnp.where(logits == v, lane, 128), axis=-1, keepdims=True)
        hit = lane == idx
        picks.append(hit)
        vals.append(v)
        logits = jnp.where(hit, NEG_INF, logits)
    es = [jnp.exp(v - vals[0]) for v in vals]
    den = es[0] + es[1] + es[2] + es[3]
    tb = logits.shape[0]
    msel = jnp.zeros(logits.shape, F32)
    for hit in picks:
        msel = msel + jnp.where(hit, 1.0, 0.0)
    cnt = jnp.sum(msel, axis=0, keepdims=True)
    pc = jnp.floor((cnt + (ROW_PAD - 1.0)) * (1.0 / ROW_PAD)) * ROW_PAD
    upper = jnp.where(lax.broadcasted_iota(jnp.int32, (128, 128), 0)
                      < lax.broadcasted_iota(jnp.int32, (128, 128), 1), 1.0, 0.0)
    lo = _dot(jnp.broadcast_to(pc, (8, 128)), upper, HI)[0:1]
    lower = jnp.where(lax.broadcasted_iota(jnp.int32, (tb, tb), 1)
                      < lax.broadcasted_iota(jnp.int32, (tb, tb), 0), 1.0, 0.0).astype(BF16)
    pos = lo + _dot(lower, msel.astype(BF16))
    meta = jnp.zeros(logits.shape, F32)
    for kk, (hit, e) in enumerate(zip(picks, es)):
        slot = jnp.sum(jnp.where(hit, pos, 0.0), axis=-1, keepdims=True)
        meta = jnp.where(lane == kk, slot, meta)
        meta = jnp.where(lane == TOP_K + kk, e / den, meta)
    return meta, cnt


def _out_projection(o_attn, o_hy, o_sc, x, mod, w_out_bf, ln_g, ln_b, router_w, router_b, layer, dec_seq):
    t_ctx = x[0].shape[0]
    T = t_ctx + x[1].shape[0]
    blocks_per_step = 2
    tm = blocks_per_step * TOKEN_BLOCK
    nct = t_ctx // tm
    row = lambda i: (layer, _mod_row(i, tm, t_ctx, dec_seq), 0, 0)
    tok = lambda w: pl.BlockSpec((tm, w), lambda i: (i, 0))
    const = lambda shape: pl.BlockSpec(shape, lambda i: (0, 0))
    rw = jnp.pad(router_w, ((0, 0), (0, 128 - N_EXP)))
    rw_hi = rw.astype(BF16)
    rw = jnp.stack([rw_hi, (rw - rw_hi.astype(F32)).astype(BF16)])
    rb = jnp.pad(router_b, (0, 128 - N_EXP)).reshape(1, 128)
    return pl.pallas_call(
        functools.partial(_outproj_kernel, nct=nct),
        grid=(T // tm,),
        in_specs=[*_ctx_lat_specs((tm, ATT_W), nct), *_ctx_lat_specs((tm, HY_W), nct),
                  *_ctx_lat_specs((tm, SC_W), nct), *_ctx_lat_specs((tm, D), nct),
                  pl.BlockSpec((1, 1, 6, D), row),
                  const((D, D)), const((1, D)), const((1, D)),
                  pl.BlockSpec((2, D, 128), lambda i: (0, 0, 0)), const((1, 128))],
        out_specs=[tok(D), tok(D), tok(128), pl.BlockSpec((blocks_per_step, 8, 128), lambda i: (i, 0, 0))],
        out_shape=[jax.ShapeDtypeStruct((T, D), F32), jax.ShapeDtypeStruct((T, D), BF16),
                   jax.ShapeDtypeStruct((T, 128), F32), jax.ShapeDtypeStruct((T // TOKEN_BLOCK, 8, 128), F32)],
        compiler_params=_cp(("arbitrary",)),
        name="out_projection_router",
    )(*o_attn, *o_hy, *o_sc, *x, mod, w_out_bf, ln_g.reshape(1, D), ln_b.reshape(1, D), rw, rb)


def _routing_tables(cnt):
    nb = cnt.shape[0]
    pc = (cnt + ROW_PAD - 1) // ROW_PAD * ROW_PAD
    lo = jnp.cumsum(pc, axis=1) - pc
    tot = jnp.sum(pc, axis=0)
    tot_pad = (tot + EXPERT_TILE - 1) // EXPERT_TILE * EXPERT_TILE
    off = jnp.cumsum(tot_pad) - tot_pad
    gstart = off[None, :] + jnp.cumsum(pc, axis=0) - pc
    ntile = tot_pad // EXPERT_TILE
    cum = jnp.cumsum(ntile)
    n_tiles = _max_expert_tiles(nb)
    i = jnp.arange(n_tiles, dtype=jnp.int32)
    ic = jnp.maximum(jnp.minimum(i, cum[-1] - 1), 0)
    te = jnp.sum((cum[None, :] <= ic[:, None]).astype(jnp.int32), axis=1)
    first = ic == (cum - ntile)[te]
    flags = jnp.where(i < cum[-1], 1 + 2 * first.astype(jnp.int32), 0)
    ex = jnp.arange(N_EXP, dtype=jnp.int32)
    later = (ex[None, :] > ex[:, None]) & (ntile[None, :] > 0)
    next_e = jnp.min(jnp.where(later, ex[None, :], N_EXP), axis=1)
    next_e = jnp.where(next_e < N_EXP, next_e, -1)
    i32 = lambda a: a.astype(jnp.int32).reshape(-1)
    return dict(pc=i32(pc), lo=i32(lo), gstart=i32(gstart), tail_start=i32(off + tot), tail_len=i32(tot_pad - tot),
                rows=i32(jnp.sum(pc, axis=1)), tile_expert=i32(te), tile_row=ic, tile_flags=flags,
                tile_next=i32(next_e[te]))


def _max_expert_tiles(nb):
    rows = TOP_K * nb * TOKEN_BLOCK + nb * N_EXP * (ROW_PAD - 1) + N_EXP * (EXPERT_TILE - 1)
    return rows // EXPERT_TILE + 1


def _run_copies(n, src_ref, src0, dst_ref, dst0, sem, wait):
    done = jnp.int32(0)
    for p in RUN_PIECES:
        take = (n & p) != 0

        @pl.when(take)
        def _():
            cp = pltpu.make_async_copy(src_ref.at[pl.ds(pl.multiple_of(src0 + done, ROW_PAD), p), :],
                                       dst_ref.at[pl.ds(pl.multiple_of(dst0 + done, ROW_PAD), p), :], sem)
            if wait:
                cp.wait()
            else:
                cp.start()

        done = done + jnp.where(take, p, 0)


def _wait_rows(n, vmem_ref, hbm_ref, sem, to_vmem):
    base = TOP_K * TOKEN_BLOCK

    def wait(p):
        src, dst = (hbm_ref, vmem_ref) if to_vmem else (vmem_ref, hbm_ref)
        pltpu.make_async_copy(src.at[pl.ds(0, p), :], dst.at[pl.ds(0, p), :], sem).wait()

    wait(base)
    for p in RUN_PIECES[1:]:
        pl.when(((n - base) & p) != 0)(functools.partial(wait, p))


SORTED_W = D + 128


def _dispatch_kernel(pc_ref, lo_ref, gs_ref, ts_ref, tl_ref, rows_ref, h_ref, meta_ref, xs_ref,
                     buf_ref, zero_ref, sem, zsem):
    b = pl.program_id(0)
    nb = pl.num_programs(0)
    s = buf_ref.shape[1]
    slot = b % 2

    @pl.when(b == 0)
    def _():
        zero_ref[...] = jnp.zeros_like(zero_ref)
        for wait in (False, True):
            def tail(e, carry):
                _run_copies(tl_ref[e], zero_ref, 0, xs_ref, ts_ref[e], zsem, wait)
                return carry
            lax.fori_loop(0, N_EXP, tail, 0)

    mt = meta_ref[...].T
    srow = lax.broadcasted_iota(jnp.int32, (s, 1), 0).astype(F32)
    perm = jnp.zeros((s, TOKEN_BLOCK), F32)
    gperm = jnp.zeros((s, TOKEN_BLOCK), F32)
    for k in range(TOP_K):
        hit = srow == mt[k:k + 1]
        perm = jnp.where(hit, 1.0, perm)
        gperm = jnp.where(hit, mt[TOP_K + k:TOP_K + k + 1], gperm)
    sorted_rows = _dot(perm.astype(BF16), h_ref[...])
    gate_col = jnp.sum(gperm, axis=-1, keepdims=True)

    @pl.when(b >= 2)
    def _():
        _wait_rows(rows_ref[b - 2], buf_ref.at[slot], xs_ref, sem.at[slot], to_vmem=False)

    buf = buf_ref.at[slot]
    buf[:, :D] = sorted_rows
    buf[:, D:] = jnp.broadcast_to(gate_col, (s, 128))

    def run(e, carry):
        j = b * N_EXP + e
        _run_copies(pc_ref[j], buf, lo_ref[j], xs_ref, gs_ref[j], sem.at[slot], wait=False)
        return carry
    lax.fori_loop(0, N_EXP, run, 0)

    @pl.when(b == nb - 1)
    def _():
        @pl.when(b >= 1)
        def _():
            _wait_rows(rows_ref[b - 1], buf_ref.at[1 - slot], xs_ref, sem.at[1 - slot], to_vmem=False)
        _wait_rows(rows_ref[b], buf, xs_ref, sem.at[slot], to_vmem=False)


def _dispatch(tabs, h2, meta):
    T = h2.shape[0]
    nb = T // TOKEN_BLOCK
    rows = _max_expert_tiles(nb) * EXPERT_TILE
    tok = lambda w: pl.BlockSpec((TOKEN_BLOCK, w), lambda b, *_: (b, 0))
    return pl.pallas_call(
        _dispatch_kernel,
        grid_spec=pltpu.PrefetchScalarGridSpec(
            num_scalar_prefetch=6,
            grid=(nb,),
            in_specs=[tok(D), tok(128)],
            out_specs=pl.BlockSpec(memory_space=pl.ANY),
            scratch_shapes=[pltpu.VMEM((2, LOCAL_SLOTS, SORTED_W), F32), pltpu.VMEM((EXPERT_TILE, SORTED_W), F32),
                            pltpu.SemaphoreType.DMA((2,)), pltpu.SemaphoreType.DMA(())],
        ),
        out_shape=jax.ShapeDtypeStruct((rows, SORTED_W), F32),
        compiler_params=_cp(("arbitrary",)),
        name="moe_dispatch",
    )(tabs["pc"], tabs["lo"], tabs["gstart"], tabs["tail_start"], tabs["tail_len"], tabs["rows"], h2, meta)


def _expert_ffn_kernel(te_ref, tr_ref, fl_ref, nx_ref, x_ref, wg_hbm, bg_ref, wu_hbm, bu_ref, wd_hbm, bd_ref, y_ref,
                       wf_ref, wbf_ref, sem, *, layer):
    i = pl.program_id(0)
    fl = fl_ref[i]
    weights = (wg_hbm, wu_hbm, wd_hbm)

    def weight_copy(j, expert):
        return pltpu.make_async_copy(weights[j].at[layer, expert], wf_ref.at[j], sem.at[j])

    @pl.when((fl & 1) != 0)
    def _():
        @pl.when((fl & 2) != 0)
        def _():
            @pl.when(i == 0)
            def _():
                for j in range(3):
                    weight_copy(j, te_ref[i]).start()

            for j in range(3):
                weight_copy(j, te_ref[i]).wait()
                wbf_ref[j] = wf_ref[j].astype(BF16)

            @pl.when(nx_ref[i] >= 0)
            def _():
                for j in range(3):
                    weight_copy(j, nx_ref[i]).start()

        x = x_ref[:, :D].astype(BF16)
        gate = x_ref[:, D:D + 1]
        g = jnp.minimum(_dot(x, wbf_ref[0]) + bg_ref[0], SWIGLU_LIMIT)
        u = jnp.clip(_dot(x, wbf_ref[1]) + bu_ref[0], -SWIGLU_LIMIT, SWIGLU_LIMIT)
        a = (u + 1.0) * (g * jax.nn.sigmoid(SWIGLU_ALPHA * g))
        y_ref[...] = gate * (_dot(a.astype(BF16), wbf_ref[2]) + bd_ref[0])


def _expert_ffn(tabs, xs, w_gate, b_gate, w_up, b_up, w_down, b_down, layer):
    rows = xs.shape[0]
    hbm = pl.BlockSpec(memory_space=pl.ANY)
    bspec = pl.BlockSpec((1, 1, D), lambda i, te, *_: (te[i], 0, 0))
    return pl.pallas_call(
        functools.partial(_expert_ffn_kernel, layer=layer),
        grid_spec=pltpu.PrefetchScalarGridSpec(
            num_scalar_prefetch=4,
            grid=(rows // EXPERT_TILE,),
            in_specs=[pl.BlockSpec((EXPERT_TILE, SORTED_W), lambda i, te, tr, *_: (tr[i], 0)),
                      hbm, bspec, hbm, bspec, hbm, bspec],
            out_specs=pl.BlockSpec((EXPERT_TILE, D), lambda i, te, tr, *_: (tr[i], 0)),
            scratch_shapes=[pltpu.VMEM((3, D, D), F32), pltpu.VMEM((3, D, D), BF16), pltpu.SemaphoreType.DMA((3,))],
        ),
        out_shape=jax.ShapeDtypeStruct((rows, D), F32),
        compiler_params=_cp(("arbitrary",)),
        name="expert_ffn",
    )(tabs["tile_expert"], tabs["tile_row"], tabs["tile_flags"], tabs["tile_next"], xs,
      w_gate, b_gate[layer].reshape(N_EXP, 1, D), w_up, b_up[layer].reshape(N_EXP, 1, D),
      w_down, b_down[layer].reshape(N_EXP, 1, D))


def _combine_kernel(pc_ref, lo_ref, gs_ref, rows_ref, ys_ref, meta_ref, x1_ref, mod_ref, lg_ref, lb_ref,
                    oc_ref, ol_ref, buf_ref, sem, *, nct):
    b = pl.program_id(0)
    nb = pl.num_programs(0)
    s = buf_ref.shape[1]
    slot = b % 2

    def fetch(blk, to_slot):
        def run(e, carry):
            j = blk * N_EXP + e
            _run_copies(pc_ref[j], ys_ref, gs_ref[j], buf_ref.at[to_slot], lo_ref[j], sem.at[to_slot], wait=False)
            return carry
        lax.fori_loop(0, N_EXP, run, 0)

    @pl.when(b == 0)
    def _():
        buf_ref[...] = jnp.zeros_like(buf_ref)
        fetch(b, slot)

    @pl.when(b + 1 < nb)
    def _():
        fetch(b + 1, 1 - slot)

    _wait_rows(rows_ref[b], buf_ref.at[slot], ys_ref, sem.at[slot], to_vmem=True)

    meta = meta_ref[...]
    scol = lax.broadcasted_iota(jnp.int32, (1, s), 1).astype(F32)
    w = jnp.zeros((TOKEN_BLOCK, s), F32)
    for k in range(TOP_K):
        w = jnp.where(meta[:, k:k + 1] == scol, 1.0, w)
    w = w.astype(BF16)
    moe = _dot(w, buf_ref[slot].astype(BF16))
    m = mod_ref[0, 0]
    out = _layer_norm(DN_ALPHA * x1_ref[...] + m[5:6] * moe, lg_ref[...], lb_ref[...])

    @pl.when(b < nct)
    def _():
        oc_ref[...] = out

    @pl.when(b >= nct)
    def _():
        ol_ref[...] = out


def _combine_ln2(tabs, ys, meta, x1, mod, ln_g, ln_b, layer, t_ctx, dec_seq):
    T = x1.shape[0]
    nct = t_ctx // TOKEN_BLOCK
    row = lambda b, *_: (layer, _mod_row(b, TOKEN_BLOCK, t_ctx, dec_seq), 0, 0)
    tok = lambda w: pl.BlockSpec((TOKEN_BLOCK, w), lambda b, *_: (b, 0))
    const = lambda shape: pl.BlockSpec(shape, lambda b, *_: (0, 0))
    return pl.pallas_call(
        functools.partial(_combine_kernel, nct=nct),
        grid_spec=pltpu.PrefetchScalarGridSpec(
            num_scalar_prefetch=4,
            grid=(T // TOKEN_BLOCK,),
            in_specs=[pl.BlockSpec(memory_space=pl.ANY), tok(128), tok(D), pl.BlockSpec((1, 1, 6, D), row),
                      const((1, D)), const((1, D))],
            out_specs=list(_ctx_lat_specs((TOKEN_BLOCK, D), nct)),
            scratch_shapes=[pltpu.VMEM((2, LOCAL_SLOTS, D), F32), pltpu.SemaphoreType.DMA((2,))],
        ),
        out_shape=[jax.ShapeDtypeStruct((t_ctx, D), F32), jax.ShapeDtypeStruct((T - t_ctx, D), F32)],
        compiler_params=_cp(("arbitrary",)),
        name="moe_combine_ln2",
    )(tabs["pc"], tabs["lo"], tabs["gstart"], tabs["rows"], ys, meta, x1, mod, ln_g.reshape(1, D),
      ln_b.reshape(1, D))


def _moe_ln2(h2, meta, cnt, x1, mod, w_gate, b_gate, w_up, b_up, w_down, b_down, ln_g, ln_b, layer, t_ctx, dec_seq):
    tabs = _routing_tables(cnt[:, 0, :N_EXP].astype(jnp.int32))
    xs = _dispatch(tabs, h2, meta)
    ys = _expert_ffn(tabs, xs, w_gate, b_gate, w_up, b_up, w_down, b_down, layer)
    return _combine_ln2(tabs, ys, meta, x1, mod, ln_g, ln_b, layer, t_ctx, dec_seq)


def kernel(x_prompt, x_sample, cache_k, cache_v, c, c_ctx, w_mod, b_mod, w_in, w_out, attn_sink, hy_conv_w, hy_conv_b, hy_w1, hy_b1, hy_f1, hy_w2, hy_b2, hy_f2, hy_w3, hy_deltas, hy_bias, sc_conv_w, ln1_g, ln1_b, router_w, router_b, w_gate, b_gate, w_up, b_up, w_down, b_down, ln2_g, ln2_b):
    nb, seq, _ = x_prompt.shape
    db, dseq, _ = x_sample.shape
    past = cache_k.shape[2]
    t_ctx = nb * seq
    x = (x_prompt.reshape(t_ctx, D), x_sample.reshape(db * dseq, D))

    cond = jnp.zeros((8, D), F32).at[0].set(c_ctx).at[1:1 + db].set(c)
    mod = _modulation(cond, w_mod, b_mod)

    cos64, sin64 = _rope_tables(dseq)
    tables = (jnp.tile(cos64, (1, N_HEADS)), jnp.tile(sin64, (1, N_HEADS)),
              jnp.tile(cos64, (1, N_KV)), jnp.tile(sin64, (1, N_KV)))
    fft_c = _fft_constants(dseq)
    w_in_bf = w_in.astype(BF16)
    w_out_bf = w_out.astype(BF16)

    new_k, new_v = [], []
    for l in range(DEPTH):
        q, k, v, u_hy, u_sc = _in_projection(x[0], x[1], mod, w_in_bf[l], l, dseq)
        new_k.append(k[:t_ctx].reshape(nb, seq, N_KV, HD))
        new_v.append(v[:t_ctx].reshape(nb, seq, N_KV, HD))
        oa_ctx = _context_attention(q, k, v, attn_sink[l], nb, seq)
        oa_lat = _latent_attention(q, k, v, cache_k[:, l].reshape(db, past, KV_W),
                                   cache_v[:, l].reshape(db, past, KV_W), attn_sink[l], tables, t_ctx, db, dseq)
        mlp = (hy_w1[l], hy_b1[l], hy_f1[l], hy_w2[l], hy_b2[l], hy_f2[l], hy_w3[l], hy_deltas[l])
        ohy_ctx, osc_ctx = _ctx_mixers(u_hy, u_sc, hy_conv_w[l], hy_conv_b[l], sc_conv_w[l], hy_bias[l], mlp,
                                       nb, seq)
        x0, g, gt, osc_lat = _mixer_prep(u_hy, u_sc, hy_conv_w[l], hy_conv_b[l], sc_conv_w[l], t_ctx, db, dseq)
        ohy_lat = _hyena_long_conv(x0, g, gt, _hyena_filters(dseq, mlp), hy_bias[l], fft_c, db, dseq)
        x1, h2, meta, cnt = _out_projection((oa_ctx, oa_lat), (ohy_ctx, ohy_lat), (osc_ctx, osc_lat), x, mod,
                                            w_out_bf[l], ln1_g[l], ln1_b[l], router_w[l], router_b[l], l, dseq)
        x = _moe_ln2(h2, meta, cnt, x1, mod, w_gate, b_gate, w_up, b_up, w_down, b_down, ln2_g[l], ln2_b[l],
                     l, t_ctx, dseq)

    y_prompt = x[0].reshape(nb, seq, D)
    y_sample = x[1].reshape(db, dseq, D)
    return (y_prompt, y_sample, jnp.stack(new_k, axis=1), jnp.stack(new_v, axis=1))
```

```python
import functools
import math

import jax
import jax.numpy as jnp
from jax import lax
from jax.experimental import pallas as pl
from jax.experimental.pallas import tpu as pltpu

F32 = jnp.float32
BF16 = jnp.bfloat16
HI = lax.Precision.HIGHEST

D = 1024
DEPTH = 2
N_HEADS = 8
N_KV = 2
HD = 64
GQA = N_HEADS // N_KV
ATT_W = N_HEADS * HD
KV_W = N_KV * HD
WINDOW = 128
BLK = 128
GRID_W = 64
ROPE_BASE = 10000.0
HY_W = 256
SC_W = 256
FILTER_EMB = 33
FILTER_BANDS = 16
FILTER_HIDDEN = 64
IN_W = ATT_W + 2 * KV_W + 3 * HY_W + 3 * SC_W
N_EXP = 32
TOP_K = 4
SWIGLU_LIMIT = 7.0
SWIGLU_ALPHA = 1.702
DN_ALPHA = (2 * DEPTH) ** 0.25
LN_EPS = 1e-5
NEG_INF = float("-inf")

VMEM_LIMIT = 56 * 1024 * 1024

TOKEN_BLOCK = 256
ROW_PAD = 8
RUN_PIECES = (256, 128, 64, 32, 16, 8)
EXPERT_TILE = 512
LOCAL_SLOTS = TOP_K * TOKEN_BLOCK + N_EXP * ROW_PAD


def _cp(sem):
    return pltpu.CompilerParams(dimension_semantics=sem, vmem_limit_bytes=VMEM_LIMIT)


def _dot(a, b, precision=None):
    return jnp.dot(a, b, preferred_element_type=F32, precision=precision)


def _dot_nt(a, b):
    return lax.dot_general(a, b, (((1,), (1,)), ((), ())), preferred_element_type=F32)


def _layer_norm(z, g, b):
    mu = jnp.mean(z, axis=-1, keepdims=True)
    zc = z - mu
    var = jnp.mean(zc * zc, axis=-1, keepdims=True)
    return zc * lax.rsqrt(var + LN_EPS) * g + b


def _mod_kernel(cond_ref, w_ref, b_ref, o_ref):
    c = cond_ref[...]
    s = c * jax.nn.sigmoid(c)
    o_ref[0] = _dot(s, w_ref[0], HI) + b_ref[0]


def _modulation(cond, w_mod, b_mod):
    tn = 1024
    out = pl.pallas_call(
        _mod_kernel,
        grid=(DEPTH, 6 * D // tn),
        in_specs=[
            pl.BlockSpec((8, D), lambda l, j: (0, 0)),
            pl.BlockSpec((1, D, tn), lambda l, j: (l, 0, j)),
            pl.BlockSpec((1, 1, tn), lambda l, j: (l, 0, j)),
        ],
        out_specs=pl.BlockSpec((1, 8, tn), lambda l, j: (l, 0, j)),
        out_shape=jax.ShapeDtypeStruct((DEPTH, 8, 6 * D), F32),
        compiler_params=_cp(("arbitrary", "arbitrary")),
        name="modulation",
    )(cond, w_mod, b_mod.reshape(DEPTH, 1, 6 * D))
    return out.reshape(DEPTH, 8, 6, D)


def _mod_row(i, tm, t_ctx, dec_seq):
    nct = t_ctx // tm
    return jnp.where(i < nct, 0, 1 + (i - nct) // (dec_seq // tm))


def _ctx_lat_specs(shape, nct, ctx_offset=0, lat_offset=0):
    ctx = pl.BlockSpec(shape, lambda i, *_: (jnp.minimum(i, nct - 1) + ctx_offset, 0))
    lat = pl.BlockSpec(shape, lambda i, *_: (jnp.maximum(i - nct, 0) + lat_offset, 0))
    return ctx, lat


def _inproj_kernel(xc_ref, xl_ref, mod_ref, w_ref, q_ref, k_ref, v_ref, uh_ref, us_ref, wbf_ref, *, nct):
    @pl.when(pl.program_id(0) == 0)
    def _():
        wbf_ref[...] = w_ref[...].astype(BF16)

    m = mod_ref[0, 0]
    x = jnp.where(pl.program_id(0) < nct, xc_ref[...], xl_ref[...])
    h = x * (1.0 + m[1:2]) + m[0:1]
    y = _dot(h.astype(BF16), wbf_ref[...])
    o0 = ATT_W
    o1 = o0 + KV_W
    o2 = o1 + KV_W
    o3 = o2 + 3 * HY_W
    q_ref[...] = y[:, :o0].astype(q_ref.dtype)
    k_ref[...] = y[:, o0:o1]
    v_ref[...] = y[:, o1:o2]
    uh_ref[...] = y[:, o2:o3].astype(uh_ref.dtype)
    us_ref[...] = y[:, o3:].astype(us_ref.dtype)


def _in_projection(xc, xl, mod, w_in, layer, dec_seq):
    t_ctx = xc.shape[0]
    T = t_ctx + xl.shape[0]
    tm = 512
    row = lambda i: (layer, _mod_row(i, tm, t_ctx, dec_seq), 0, 0)
    tok = lambda w: pl.BlockSpec((tm, w), lambda i: (i, 0))
    widths = (ATT_W, KV_W, KV_W, 3 * HY_W, 3 * SC_W)
    return pl.pallas_call(
        functools.partial(_inproj_kernel, nct=t_ctx // tm),
        grid=(T // tm,),
        in_specs=[
            *_ctx_lat_specs((tm, D), t_ctx // tm),
            pl.BlockSpec((1, 1, 6, D), row),
            pl.BlockSpec((None, D, IN_W), lambda i: (layer, 0, 0)),
        ],
        out_specs=[tok(w) for w in widths],
        out_shape=[jax.ShapeDtypeStruct((T, w), dt) for w, dt in zip(widths, (BF16, F32, F32, BF16, BF16))],
        scratch_shapes=[pltpu.VMEM((D, IN_W), BF16)],
        compiler_params=_cp(("arbitrary",)),
        name="in_projection",
    )(xc, xl, mod, w_in)


def _swap_halves(x):
    w = x.shape[-1]
    lane = lax.broadcasted_iota(jnp.int32, x.shape, 1)
    return jnp.where((lane % HD) < HD // 2, pltpu.roll(x, w - HD // 2, 1), pltpu.roll(x, HD // 2, 1))


def _rope(x, cos, sin_signed):
    return x * cos + _swap_halves(x) * sin_signed


def _group_rows(q, g):
    return jnp.concatenate([q[:, (GQA * g + r) * HD:(GQA * g + r + 1) * HD] for r in range(GQA)], axis=0)


def _sink_column(sink_ref, g, rows):
    ridx = lax.broadcasted_iota(jnp.int32, (GQA * rows, 1), 0)
    col = jnp.full((GQA * rows, 1), sink_ref[GQA * g + GQA - 1], F32)
    for r in range(GQA - 2, -1, -1):
        col = jnp.where(ridx < (r + 1) * rows, sink_ref[GQA * g + r], col)
    return col


def _ungroup(outs, rows):
    return jnp.concatenate([o[r * rows:(r + 1) * rows] for o in outs for r in range(GQA)], axis=1)


def _ctx_attn_kernel(sink_ref, q_ref, k_ref, v_ref, o_ref):
    rows = q_ref.shape[0]
    q = q_ref[...].astype(F32) * (HD ** -0.5)
    k = k_ref[...]
    v = v_ref[...]
    outs = []
    for g in range(N_KV):
        qg = _group_rows(q, g).astype(BF16)
        kg = k[:, g * HD:(g + 1) * HD].astype(BF16)
        vg = v[:, g * HD:(g + 1) * HD].astype(BF16)
        s = _dot_nt(qg, kg)
        sink = _sink_column(sink_ref, g, rows)
        m = jnp.maximum(jnp.max(s, axis=-1, keepdims=True), sink)
        e = jnp.exp(s - m)
        den = jnp.sum(e, axis=-1, keepdims=True) + jnp.exp(sink - m)
        outs.append(_dot(e.astype(BF16), vg) / den)
    o_ref[...] = _ungroup(outs, rows).astype(o_ref.dtype)


def _context_attention(q, k, v, sink, B, L):
    tok = lambda w: pl.BlockSpec((L, w), lambda b: (b, 0))
    return pl.pallas_call(
        _ctx_attn_kernel,
        grid=(B,),
        in_specs=[pl.BlockSpec(memory_space=pltpu.SMEM), tok(ATT_W), tok(KV_W), tok(KV_W)],
        out_specs=tok(ATT_W),
        out_shape=jax.ShapeDtypeStruct((B * L, ATT_W), BF16),
        compiler_params=_cp(("arbitrary",)),
        name="context_attention",
    )(sink, q, k, v)


def _lat_attn_kernel(sink_ref, q_ref, k_ref, v_ref, kc_ref, vc_ref, ck_ref, sk_ref, o_ref,
                     kr_ref, vb_ref, kcb_ref, vcb_ref):
    n = pl.program_id(1)
    L = k_ref.shape[0]
    nk = 3 * BLK

    @pl.when(n == 0)
    def _():
        kr_ref[...] = _rope(k_ref[...], ck_ref[...], sk_ref[...]).astype(BF16)
        vb_ref[...] = v_ref[...].astype(BF16)
        kcb_ref[...] = kc_ref[...].astype(BF16)
        vcb_ref[...] = vc_ref[...].astype(BF16)

    ws = pl.multiple_of(jnp.clip((n - 1) * BLK, 0, L - nk), BLK)
    qs = pl.multiple_of(n * BLK, BLK)
    heads_per_table = N_HEADS // N_KV
    cq = jnp.tile(ck_ref[pl.ds(qs, BLK), :], (1, heads_per_table))
    sq = jnp.tile(sk_ref[pl.ds(qs, BLK), :], (1, heads_per_table))
    q = _rope(q_ref[...].astype(F32), cq, sq) * (HD ** -0.5)
    kl = kr_ref[pl.ds(ws, nk), :]
    vl = vb_ref[pl.ds(ws, nk), :]
    kc = kcb_ref[...]
    vc = vcb_ref[...]
    qpos = n * BLK + lax.broadcasted_iota(jnp.int32, (GQA * BLK, 1), 0) % BLK
    kpos = ws + lax.broadcasted_iota(jnp.int32, (1, nk), 1)
    valid = jnp.abs(kpos - qpos) <= WINDOW
    outs = []
    for g in range(N_KV):
        sl = slice(g * HD, (g + 1) * HD)
        qg = _group_rows(q, g).astype(BF16)
        s_loc = jnp.where(valid, _dot_nt(qg, kl[:, sl]), NEG_INF)
        s_ctx = _dot_nt(qg, kc[:, sl])
        sink = _sink_column(sink_ref, g, BLK)
        m = jnp.maximum(jnp.maximum(jnp.max(s_loc, axis=-1, keepdims=True),
                                    jnp.max(s_ctx, axis=-1, keepdims=True)), sink)
        e_loc = jnp.exp(s_loc - m)
        e_ctx = jnp.exp(s_ctx - m)
        den = (jnp.sum(e_loc, axis=-1, keepdims=True) + jnp.sum(e_ctx, axis=-1, keepdims=True)
               + jnp.exp(sink - m))
        o = _dot(e_loc.astype(BF16), vl[:, sl]) + _dot(e_ctx.astype(BF16), vc[:, sl])
        outs.append(o / den)
    o_ref[...] = _ungroup(outs, BLK).astype(o_ref.dtype)


def _rope_tables(L):
    rows = L // GRID_W
    row = jnp.repeat(jnp.arange(rows, dtype=F32), GRID_W)
    col = jnp.tile(jnp.arange(GRID_W, dtype=F32), rows)
    pairs = HD // 4
    inv = ROPE_BASE ** (-jnp.arange(pairs, dtype=F32) / pairs)
    ang = jnp.concatenate([row[:, None] * inv, col[:, None] * inv], axis=-1)
    cos = jnp.cos(ang)
    sin = jnp.sin(ang)
    return (jnp.tile(jnp.concatenate([cos, cos], axis=-1), (1, N_KV)),
            jnp.tile(jnp.concatenate([-sin, sin], axis=-1), (1, N_KV)))


def _latent_attention(q, k, v, kc, vc, sink, tables, t_ctx, B, L):
    assert t_ctx % L == 0
    P = kc.shape[1]
    ck, sk = tables
    nbk = L // BLK
    seq = pl.BlockSpec((L, KV_W), lambda b, n: (t_ctx // L + b, 0))
    ctx = pl.BlockSpec((None, P, KV_W), lambda b, n: (b, 0, 0))
    return pl.pallas_call(
        _lat_attn_kernel,
        grid=(B, nbk),
        in_specs=[
            pl.BlockSpec(memory_space=pltpu.SMEM),
            pl.BlockSpec((BLK, ATT_W), lambda b, n: (t_ctx // BLK + b * nbk + n, 0)),
            seq, seq, ctx, ctx,
            pl.BlockSpec((L, KV_W), lambda b, n: (0, 0)),
            pl.BlockSpec((L, KV_W), lambda b, n: (0, 0)),
        ],
        out_specs=pl.BlockSpec((BLK, ATT_W), lambda b, n: (b * nbk + n, 0)),
        out_shape=jax.ShapeDtypeStruct((B * L, ATT_W), BF16),
        scratch_shapes=[pltpu.VMEM((L, KV_W), BF16), pltpu.VMEM((L, KV_W), BF16),
                        pltpu.VMEM((P, KV_W), BF16), pltpu.VMEM((P, KV_W), BF16)],
        compiler_params=_cp(("arbitrary", "arbitrary")),
        name="latent_attention",
    )(sink, q, k, v, kc, vc, ck, sk)


def _conv3(u, prev_row, next_row, w):
    tl = u.shape[0]
    ridx = lax.broadcasted_iota(jnp.int32, (tl, 1), 0)
    dn = jnp.where(ridx == 0, prev_row, pltpu.roll(u, 1, 0))
    up = jnp.where(ridx == tl - 1, next_row, pltpu.roll(u, tl - 1, 0))
    return dn * w[0:1] + u * w[1:2] + up * w[2:3]


def _store_time_major_inner(dst_ref, src, scr_ref):
    n2 = dst_ref.shape[0]
    rows = src.shape[0] // n2
    for h in range(scr_ref.shape[0]):
        scr_ref[h] = src[:, h * 128:(h + 1) * 128]
    for j in range(n2):
        for h in range(scr_ref.shape[0]):
            dst_ref[j, :, h * 128:(h + 1) * 128] = scr_ref[h, pl.ds(j, rows, stride=n2), :]


def _mixprep_kernel(uh_ref, uhp_ref, uhn_ref, us_ref, usp_ref, usn_ref, hw_ref, hb_ref, sw_ref,
                    x0_ref, g_ref, gt_ref, osc_ref, scr_ref):
    t = pl.program_id(1)
    first = t == 0
    last = t == pl.num_programs(1) - 1
    c = HY_W
    hw = hw_ref[...]
    halo = uhp_ref.shape[0]
    f32 = lambda a: a.astype(F32)
    uh = _conv3(f32(uh_ref[...]), jnp.where(first, 0.0, f32(uhp_ref[halo - 1:halo])),
                jnp.where(last, 0.0, f32(uhn_ref[0:1])), hw)
    uh = uh + hb_ref[...]
    x0_ref[...] = uh[:, :c]
    g = uh[:, c:2 * c] * uh[:, 2 * c:]
    g_ref[...] = g
    _store_time_major_inner(gt_ref, g, scr_ref)
    us = f32(us_ref[...])
    usp = f32(usp_ref[halo - 1:halo])
    usn = f32(usn_ref[0:1])
    prod = us[:, c:2 * c] * us[:, 2 * c:]
    pprev = jnp.where(first, 0.0, usp[:, c:2 * c] * usp[:, 2 * c:])
    pnext = jnp.where(last, 0.0, usn[:, c:2 * c] * usn[:, 2 * c:])
    osc_ref[...] = (us[:, :c] * _conv3(prod, pprev, pnext, sw_ref[...])).astype(osc_ref.dtype)


def _mixer_prep(u_hy, u_sc, hy_conv_w, hy_conv_b, sc_conv_w, row0, B, L):
    W = u_hy.shape[1]
    n1, n2 = _fft_split(L)
    tl = 8 * n2
    assert row0 % tl == 0 and L % tl == 0
    nt = L // tl
    halo = 16
    rh = tl // halo
    base = lambda b: row0 // halo + b * (L // halo)
    main = pl.BlockSpec((tl, W), lambda b, t: (row0 // tl + b * nt + t, 0))
    prev = pl.BlockSpec((halo, W), lambda b, t: (base(b) + jnp.maximum(t * rh - 1, 0), 0))
    nxt = pl.BlockSpec((halo, W), lambda b, t: (base(b) + jnp.minimum((t + 1) * rh, L // halo - 1), 0))
    const = lambda shape: pl.BlockSpec(shape, lambda b, t: (0, 0))
    out = pl.BlockSpec((tl, HY_W), lambda b, t: (b * nt + t, 0))
    flat = jax.ShapeDtypeStruct((B * L, HY_W), F32)
    return pl.pallas_call(
        _mixprep_kernel,
        grid=(B, nt),
        in_specs=[main, prev, nxt, main, prev, nxt, const((3, W)), const((1, W)), const((3, SC_W))],
        out_specs=[out, out, pl.BlockSpec((None, n2, 8, HY_W), lambda b, t: (b, 0, t, 0)), out],
        out_shape=[flat, flat, jax.ShapeDtypeStruct((B, n2, n1 // 2, HY_W), F32),
                   jax.ShapeDtypeStruct((B * L, SC_W), BF16)],
        scratch_shapes=[pltpu.VMEM((HY_W // 128, tl, 128), F32)],
        compiler_params=_cp(("arbitrary", "arbitrary")),
        name="mixer_prep",
    )(u_hy, u_hy, u_hy, u_sc, u_sc, u_sc, hy_conv_w, hy_conv_b.reshape(1, W), sc_conv_w)


def _filter_mlp(z, w1_ref, b1_ref, f1_ref, w2_ref, b2_ref, f2_ref, w3_ref, dl_ref, row0):
    tl = z.shape[0]
    h = jnp.sin(f1_ref[...] * (_dot(z, w1_ref[...], HI) + b1_ref[...]))
    h = jnp.sin(f2_ref[...] * (_dot(h, w2_ref[...], HI) + b2_ref[...]))
    h = _dot(h, w3_ref[...], HI) * jnp.exp(-z[:, 0:1] * jnp.abs(dl_ref[...]))
    row = row0 + lax.broadcasted_iota(jnp.int32, (tl, 1), 0)
    return h[:, :HY_W], jnp.where(row == 0, 0.0, h[:, HY_W:])


def _filter_kernel(z_ref, w1_ref, b1_ref, f1_ref, w2_ref, b2_ref, f2_ref, w3_ref, dl_ref, o_ref, scr_ref):
    hf, hb = _filter_mlp(z_ref[...], w1_ref, b1_ref, f1_ref, w2_ref, b2_ref, f2_ref, w3_ref, dl_ref,
                         pl.program_id(0) * z_ref.shape[0])
    _store_time_major_inner(o_ref.at[0], hf, scr_ref)
    _store_time_major_inner(o_ref.at[1], hb, scr_ref)


def _filter_features(L):
    t = jnp.linspace(0.0, 1.0, L, dtype=F32)[:, None]
    w = 2.0 * math.pi * jnp.arange(L, dtype=F32)[:, None] / L
    bands = jnp.linspace(1e-4, FILTER_BANDS - 1, FILTER_BANDS, dtype=F32)[None, :]
    z = jnp.concatenate([t, jnp.cos(bands * w), -jnp.sin(bands * w)], axis=-1)
    return jnp.pad(z, ((0, 0), (0, 128 - FILTER_EMB)))


def _filter_mlp_operands(w1, b1, f1, w2, b2, f2, w3, deltas):
    H = FILTER_HIDDEN
    ops = (jnp.pad(w1, ((0, 128 - FILTER_EMB), (0, 0))), b1.reshape(1, H), f1.reshape(1, H), w2, b2.reshape(1, H),
           f2.reshape(1, H), w3, deltas.reshape(1, 2 * HY_W))
    specs = [pl.BlockSpec(o.shape, lambda *_: (0, 0)) for o in ops]
    return ops, specs


def _hyena_filters(L, mlp):
    n1, n2 = _fft_split(L)
    tl = 8 * n2
    ops, specs = _filter_mlp_operands(*mlp)
    return pl.pallas_call(
        _filter_kernel,
        grid=(L // tl,),
        in_specs=[pl.BlockSpec((tl, 128), lambda i: (i, 0)), *specs],
        out_specs=pl.BlockSpec((2, n2, 8, HY_W), lambda i: (0, 0, i, 0)),
        out_shape=jax.ShapeDtypeStruct((2, n2, n1 // 2, HY_W), F32),
        scratch_shapes=[pltpu.VMEM((HY_W // 128, tl, 128), F32)],
        compiler_params=_cp(("arbitrary",)),
        name="hyena_filters",
    )(_filter_features(L), *ops)


def _dense_dft_constants(L):
    n = 2 * L
    a = (jnp.arange(n, dtype=jnp.int32)[:, None] * jnp.arange(L, dtype=jnp.int32)[None, :]) % n
    th = (2.0 * math.pi / n) * a.astype(F32)
    fwd = jnp.concatenate([jnp.cos(th), -jnp.sin(th)], axis=0)
    inv = jnp.concatenate([jnp.cos(th).T, -jnp.sin(th).T], axis=1) * (1.0 / n)
    return fwd.astype(BF16), inv.astype(BF16)


def _ctx_mixers_kernel(uh_ref, us_ref, hw_ref, hb_ref, sw_ref, bias_ref, fwd_ref, inv_ref, z_ref,
                       w1_ref, b1_ref, f1_ref, w2_ref, b2_ref, f2_ref, w3_ref, dl_ref, ohy_ref, osc_ref, kf_ref):
    c = HY_W
    nf = kf_ref.shape[1]

    @pl.when(pl.program_id(0) == 0)
    def _():
        hf, hb = _filter_mlp(z_ref[...], w1_ref, b1_ref, f1_ref, w2_ref, b2_ref, f2_ref, w3_ref, dl_ref, 0)
        xf = _dot(fwd_ref[...], hf.astype(BF16))
        xb = _dot(fwd_ref[...], hb.astype(BF16))
        kf_ref[0] = xf[:nf] + xb[:nf]
        kf_ref[1] = xf[nf:] - xb[nf:]

    zero = jnp.zeros((1, 1), F32)
    uh = _conv3(uh_ref[...].astype(F32), zero, zero, hw_ref[...]) + hb_ref[...]
    x0, g = uh[:, :c], uh[:, c:2 * c] * uh[:, 2 * c:]
    s = _dot(fwd_ref[...], g.astype(BF16))
    sr, si = s[:nf], s[nf:]
    kr, ki = kf_ref[0], kf_ref[1]
    y = _dot(inv_ref[...], jnp.concatenate([sr * kr - si * ki, sr * ki + si * kr], axis=0).astype(BF16))
    ohy_ref[...] = (x0 * (y + g * bias_ref[...])).astype(ohy_ref.dtype)
    us = us_ref[...].astype(F32)
    osc_ref[...] = (us[:, :c] * _conv3(us[:, c:2 * c] * us[:, 2 * c:], zero, zero, sw_ref[...])).astype(osc_ref.dtype)


def _ctx_mixers(u_hy, u_sc, hy_conv_w, hy_conv_b, sc_conv_w, hy_bias, mlp, B, L):
    W = u_hy.shape[1]
    fwd, inv = _dense_dft_constants(L)
    ops, specs = _filter_mlp_operands(*mlp)
    const = lambda shape: pl.BlockSpec(shape, lambda b: (0, 0))
    seq = lambda w: pl.BlockSpec((L, w), lambda b: (b, 0))
    return pl.pallas_call(
        _ctx_mixers_kernel,
        grid=(B,),
        in_specs=[seq(W), seq(W), const((3, W)), const((1, W)), const((3, SC_W)), const((1, HY_W)),
                  const(fwd.shape), const(inv.shape), const((L, 128)), *specs],
        out_specs=[seq(HY_W), seq(SC_W)],
        out_shape=[jax.ShapeDtypeStruct((B * L, HY_W), BF16), jax.ShapeDtypeStruct((B * L, SC_W), BF16)],
        scratch_shapes=[pltpu.VMEM((2, 2 * L, HY_W), F32)],
        compiler_params=_cp(("arbitrary",)),
        name="context_mixers",
    )(u_hy, u_sc, hy_conv_w, hy_conv_b.reshape(1, W), sc_conv_w, hy_bias.reshape(1, HY_W), fwd, inv,
      _filter_features(L), *ops)


def _fft_split(L):
    n = 2 * L
    n1 = 128 if n >= 8192 else 32
    return n1, n // n1


def _fft_constants(L):
    n1, n2 = _fft_split(L)
    n = n1 * n2
    k1 = jnp.arange(n1, dtype=jnp.int32)
    a = (k1[:, None] * jnp.arange(n1 // 2, dtype=jnp.int32)[None, :]) % n1
    th = (2.0 * math.pi / n1) * a.astype(F32)
    f1 = jnp.concatenate([jnp.cos(th), -jnp.sin(th)], axis=0)
    f3 = jnp.concatenate([jnp.cos(th).T, -jnp.sin(th).T], axis=1) * (1.0 / n)
    k = k1[:, None, None] + n1 * jnp.arange(n2, dtype=jnp.int32)[None, :, None]
    ph = (k * jnp.arange(n2, dtype=jnp.int32)[None, None, :]) % n
    ph = (2.0 * math.pi / n) * ph.astype(F32)
    mr, mi = jnp.cos(ph), -jnp.sin(ph)
    mf = jnp.concatenate([jnp.concatenate([mr, -mi], axis=2), jnp.concatenate([mi, mr], axis=2)], axis=1)
    mrt, mit = jnp.swapaxes(mr, 1, 2), jnp.swapaxes(mi, 1, 2)
    mb = jnp.concatenate([jnp.concatenate([mrt, mit], axis=2), jnp.concatenate([-mit, mrt], axis=2)], axis=1)
    return tuple(c.astype(BF16) for c in (f1, f3, mf, mb))


FFT_GROUP = 8


def _dft_rows_kernel(f_ref, x_ref, o_ref):
    for i in range(FFT_GROUP):
        o_ref[:, i, :] = _dot(f_ref[...], x_ref[i].astype(BF16))


def _dft_rows(f1, xt):
    B, n2, kh, C = xt.shape
    m = f1.shape[0]
    return pl.pallas_call(
        _dft_rows_kernel,
        grid=(B, n2 // FFT_GROUP),
        in_specs=[pl.BlockSpec((m, kh), lambda b, j: (0, 0)),
                  pl.BlockSpec((None, FFT_GROUP, kh, C), lambda b, j: (b, j, 0, 0))],
        out_specs=pl.BlockSpec((None, m, FFT_GROUP, C), lambda b, j: (b, 0, j, 0)),
        out_shape=jax.ShapeDtypeStruct((B, m, n2, C), F32),
        compiler_params=_cp(("arbitrary", "arbitrary")),
        name="fft_rows",
    )(f1, xt)


def _fft_mid_kernel(a_ref, hf_ref, hb_ref, mf_ref, mb_ref, o_ref, kf_ref):
    n2 = a_ref.shape[2]

    @pl.when(pl.program_id(1) == 0)
    def _():
        for i in range(FFT_GROUP):
            xf = _dot(mf_ref[i], jnp.concatenate([hf_ref[0, i], hf_ref[1, i]], axis=0).astype(BF16))
            xb = _dot(mf_ref[i], jnp.concatenate([hb_ref[0, i], hb_ref[1, i]], axis=0).astype(BF16))
            kf_ref[i, 0] = xf[:n2] + xb[:n2]
            kf_ref[i, 1] = xf[n2:] - xb[n2:]

    for i in range(FFT_GROUP):
        x = _dot(mf_ref[i], jnp.concatenate([a_ref[0, i], a_ref[1, i]], axis=0).astype(BF16))
        xr, xi = x[:n2], x[n2:]
        kr, ki = kf_ref[i, 0], kf_ref[i, 1]
        y = jnp.concatenate([xr * kr - xi * ki, xr * ki + xi * kr], axis=0)
        o = _dot(mb_ref[i], y.astype(BF16))
        o_ref[:, 0, i, :] = o[:n2]
        o_ref[:, 1, i, :] = o[n2:]


def _fft_mid(a, ah, mf, mb):
    B, _, n1, n2, C = a.shape
    blk = (None, 2, FFT_GROUP, n2, C)
    return pl.pallas_call(
        _fft_mid_kernel,
        grid=(n1 // FFT_GROUP, B),
        in_specs=[
            pl.BlockSpec(blk, lambda k, b: (b, 0, k, 0, 0)),
            pl.BlockSpec(blk, lambda k, b: (0, 0, k, 0, 0)),
            pl.BlockSpec(blk, lambda k, b: (1, 0, k, 0, 0)),
            pl.BlockSpec((FFT_GROUP, 2 * n2, 2 * n2), lambda k, b: (k, 0, 0)),
            pl.BlockSpec((FFT_GROUP, 2 * n2, 2 * n2), lambda k, b: (k, 0, 0)),
        ],
        out_specs=pl.BlockSpec((None, n2, 2, FFT_GROUP, C), lambda k, b: (b, 0, 0, k, 0)),
        out_shape=jax.ShapeDtypeStruct((B, n2, 2, n1, C), F32),
        scratch_shapes=[pltpu.VMEM((FFT_GROUP, 2, n2, C), F32)],
        compiler_params=_cp(("arbitrary", "arbitrary")),
        name="fft_mid",
    )(a, ah, ah, mf, mb)


def _idft_rows_kernel(f_ref, b_ref, x0_ref, g_ref, bias_ref, o_ref):
    for i in range(FFT_GROUP):
        o_ref[:, i, :] = _dot(f_ref[...], b_ref[i].astype(BF16))
    o_ref[...] = x0_ref[...] * (o_ref[...] + g_ref[...] * bias_ref[...])


def _idft_rows_gate(f3, bo, x0, g, bias):
    B, n2, m2, C = bo.shape
    kh = f3.shape[0]
    tile = pl.BlockSpec((None, kh, FFT_GROUP, C), lambda b, j: (b, 0, j, 0))
    return pl.pallas_call(
        _idft_rows_kernel,
        grid=(B, n2 // FFT_GROUP),
        in_specs=[pl.BlockSpec((kh, m2), lambda b, j: (0, 0)),
                  pl.BlockSpec((None, FFT_GROUP, m2, C), lambda b, j: (b, j, 0, 0)),
                  tile, tile, pl.BlockSpec((1, C), lambda b, j: (0, 0))],
        out_specs=tile,
        out_shape=jax.ShapeDtypeStruct((B, kh, n2, C), F32),
        compiler_params=_cp(("arbitrary", "arbitrary")),
        name="ifft_rows_gate",
    )(f3, bo, x0, g, bias)


def _hyena_long_conv(x0, g, gt, filt_t, hy_bias, consts, B, L):
    C = HY_W
    n1, n2 = _fft_split(L)
    f1, f3, mf, mb = consts
    a = _dft_rows(f1, gt).reshape(B, 2, n1, n2, C)
    ah = _dft_rows(f1, filt_t).reshape(2, 2, n1, n2, C)
    bo = _fft_mid(a, ah, mf, mb).reshape(B, n2, 2 * n1, C)
    nat = lambda z: z.reshape(B, n1 // 2, n2, C)
    out = _idft_rows_gate(f3, bo, nat(x0), nat(g), hy_bias.reshape(1, C))
    return out.reshape(B * L, C)


def _outproj_kernel(oac_ref, oal_ref, ohc_ref, ohl_ref, osc_ref, osl_ref, xc_ref, xl_ref, mod_ref, w_ref,
                    lg_ref, lb_ref, rw_ref, rb_ref, x1_ref, h2_ref, meta_ref, cnt_ref, wbf_ref, *, nct):
    @pl.when(pl.program_id(0) == 0)
    def _():
        wbf_ref[...] = w_ref[...].astype(BF16)

    m = mod_ref[0, 0]
    w = wbf_ref[...]
    is_ctx = pl.program_id(0) < nct
    pick = lambda c_ref, l_ref, dt: jnp.where(is_ctx, c_ref[...].astype(dt), l_ref[...].astype(dt))
    o = (_dot(pick(oac_ref, oal_ref, BF16), w[:ATT_W])
         + _dot(pick(ohc_ref, ohl_ref, BF16), w[ATT_W:ATT_W + HY_W])
         + _dot(pick(osc_ref, osl_ref, BF16), w[ATT_W + HY_W:]))
    x1 = _layer_norm(DN_ALPHA * pick(xc_ref, xl_ref, F32) + m[2:3] * o, lg_ref[...], lb_ref[...])
    x1_ref[...] = x1
    h2 = x1 * (1.0 + m[4:5]) + m[3:4]
    h2_hi = h2.astype(BF16)
    h2_ref[...] = h2_hi
    h2_lo = (h2 - h2_hi.astype(F32)).astype(BF16)
    logits = _dot(h2_hi, rw_ref[0]) + _dot(h2_lo, rw_ref[0]) + _dot(h2_hi, rw_ref[1]) + rb_ref[...]
    for sb in range(h2.shape[0] // TOKEN_BLOCK):
        rows = slice(sb * TOKEN_BLOCK, (sb + 1) * TOKEN_BLOCK)
        meta, cnt = _route_block(logits[rows])
        meta_ref[rows, :] = meta
        cnt_ref[sb] = jnp.broadcast_to(cnt, (8, 128))


def _route_block(logits):
    lane = lax.broadcasted_iota(jnp.int32, logits.shape, 1)
    logits = jnp.where(lane < N_EXP, logits, NEG_INF)
    picks = []
    vals = []
    for _ in range(TOP_K):
        v = jnp.max(logits, axis=-1, keepdims=True)
        idx = jnp.min(jnp.where(logits == v, lane, 128), axis=-1, keepdims=True)
        hit = lane == idx
        picks.append(hit)
        vals.append(v)
        logits = jnp.where(hit, NEG_INF, logits)
    es = [jnp.exp(v - vals[0]) for v in vals]
    den = es[0] + es[1] + es[2] + es[3]
    tb = logits.shape[0]
    msel = jnp.zeros(logits.shape, F32)
    for hit in picks:
        msel = msel + jnp.where(hit, 1.0, 0.0)
    cnt = jnp.sum(msel, axis=0, keepdims=True)
    pc = jnp.floor((cnt + (ROW_PAD - 1.0)) * (1.0 / ROW_PAD)) * ROW_PAD
    upper = jnp.where(lax.broadcasted_iota(jnp.int32, (128, 128), 0)
                      < lax.broadcasted_iota(jnp.int32, (128, 128), 1), 1.0, 0.0)
    lo = _dot(jnp.broadcast_to(pc, (8, 128)), upper, HI)[0:1]
    lower = jnp.where(lax.broadcasted_iota(jnp.int32, (tb, tb), 1)
                      < lax.broadcasted_iota(jnp.int32, (tb, tb), 0), 1.0, 0.0).astype(BF16)
    pos = lo + _dot(lower, msel.astype(BF16))
    meta = jnp.zeros(logits.shape, F32)
    for kk, (hit, e) in enumerate(zip(picks, es)):
        slot = jnp.sum(jnp.where(hit, pos, 0.0), axis=-1, keepdims=True)
        meta = jnp.where(lane == kk, slot, meta)
        meta = jnp.where(lane == TOP_K + kk, e / den, meta)
    return meta, cnt


def _out_projection(o_attn, o_hy, o_sc, x, mod, w_out, ln_g, ln_b, router_w, router_b, layer, dec_seq):
    t_ctx = x[0].shape[0]
    T = t_ctx + x[1].shape[0]
    blocks_per_step = 2
    tm = blocks_per_step * TOKEN_BLOCK
    nct = t_ctx // tm
    row = lambda i: (layer, _mod_row(i, tm, t_ctx, dec_seq), 0, 0)
    tok = lambda w: pl.BlockSpec((tm, w), lambda i: (i, 0))
    const = lambda shape: pl.BlockSpec(shape, lambda i: (0, 0))
    rw = jnp.pad(router_w, ((0, 0), (0, 128 - N_EXP)))
    rw_hi = rw.astype(BF16)
    rw = jnp.stack([rw_hi, (rw - rw_hi.astype(F32)).astype(BF16)])
    rb = jnp.pad(router_b, (0, 128 - N_EXP)).reshape(1, 128)
    return pl.pallas_call(
        functools.partial(_outproj_kernel, nct=nct),
        grid=(T // tm,),
        in_specs=[*_ctx_lat_specs((tm, ATT_W), nct), *_ctx_lat_specs((tm, HY_W), nct),
                  *_ctx_lat_specs((tm, SC_W), nct), *_ctx_lat_specs((tm, D), nct),
                  pl.BlockSpec((1, 1, 6, D), row),
                  pl.BlockSpec((None, D, D), lambda i: (layer, 0, 0)), const((1, D)), const((1, D)),
                  pl.BlockSpec((2, D, 128), lambda i: (0, 0, 0)), const((1, 128))],
        out_specs=[tok(D), tok(D), tok(128), pl.BlockSpec((blocks_per_step, 8, 128), lambda i: (i, 0, 0))],
        out_shape=[jax.ShapeDtypeStruct((T, D), F32), jax.ShapeDtypeStruct((T, D), BF16),
                   jax.ShapeDtypeStruct((T, 128), F32), jax.ShapeDtypeStruct((T // TOKEN_BLOCK, 8, 128), F32)],
        scratch_shapes=[pltpu.VMEM((D, D), BF16)],
        compiler_params=_cp(("arbitrary",)),
        name="out_projection_router",
    )(*o_attn, *o_hy, *o_sc, *x, mod, w_out, ln_g.reshape(1, D), ln_b.reshape(1, D), rw, rb)


def _routing_tables(cnt):
    nb = cnt.shape[0]
    pc = (cnt + ROW_PAD - 1) // ROW_PAD * ROW_PAD
    lo = jnp.cumsum(pc, axis=1) - pc
    tot = jnp.sum(pc, axis=0)
    tot_pad = (tot + EXPERT_TILE - 1) // EXPERT_TILE * EXPERT_TILE
    off = jnp.cumsum(tot_pad) - tot_pad
    gstart = off[None, :] + jnp.cumsum(pc, axis=0) - pc
    ntile = tot_pad // EXPERT_TILE
    cum = jnp.cumsum(ntile)
    n_tiles = _max_expert_tiles(nb)
    i = jnp.arange(n_tiles, dtype=jnp.int32)
    ic = jnp.maximum(jnp.minimum(i, cum[-1] - 1), 0)
    te = jnp.sum((cum[None, :] <= ic[:, None]).astype(jnp.int32), axis=1)
    first = ic == (cum - ntile)[te]
    flags = jnp.where(i < cum[-1], 1 + 2 * first.astype(jnp.int32), 0)
    ex = jnp.arange(N_EXP, dtype=jnp.int32)
    later = (ex[None, :] > ex[:, None]) & (ntile[None, :] > 0)
    next_e = jnp.min(jnp.where(later, ex[None, :], N_EXP), axis=1)
    next_e = jnp.where(next_e < N_EXP, next_e, -1)
    i32 = lambda a: a.astype(jnp.int32).reshape(-1)
    return dict(pc=i32(pc), lo=i32(lo), gstart=i32(gstart), tail_start=i32(off + tot), tail_len=i32(tot_pad - tot),
                rows=i32(jnp.sum(pc, axis=1)), tile_expert=i32(te), tile_row=ic, tile_flags=flags,
                tile_next=i32(next_e[te]))


def _max_expert_tiles(nb):
    rows = TOP_K * nb * TOKEN_BLOCK + nb * N_EXP * (ROW_PAD - 1) + N_EXP * (EXPERT_TILE - 1)
    return rows // EXPERT_TILE + 1


def _run_copies(n, src_ref, src0, dst_ref, dst0, sem, wait):
    done = jnp.int32(0)
    for p in RUN_PIECES:
        take = (n & p) != 0

        @pl.when(take)
        def _():
            cp = pltpu.make_async_copy(src_ref.at[pl.ds(pl.multiple_of(src0 + done, ROW_PAD), p), :],
                                       dst_ref.at[pl.ds(pl.multiple_of(dst0 + done, ROW_PAD), p), :], sem)
            if wait:
                cp.wait()
            else:
                cp.start()

        done = done + jnp.where(take, p, 0)


def _wait_rows(n, vmem_ref, hbm_ref, sem, to_vmem):
    base = TOP_K * TOKEN_BLOCK

    def wait(p):
        src, dst = (hbm_ref, vmem_ref) if to_vmem else (vmem_ref, hbm_ref)
        pltpu.make_async_copy(src.at[pl.ds(0, p), :], dst.at[pl.ds(0, p), :], sem).wait()

    wait(base)
    for p in RUN_PIECES[1:]:
        pl.when(((n - base) & p) != 0)(functools.partial(wait, p))


SORTED_W = D + 128


def _dispatch_kernel(pc_ref, lo_ref, gs_ref, ts_ref, tl_ref, rows_ref, h_ref, meta_ref, xs_ref,
                     buf_ref, zero_ref, sem, zsem):
    b = pl.program_id(0)
    nb = pl.num_programs(0)
    s = buf_ref.shape[1]
    slot = b % 2

    @pl.when(b == 0)
    def _():
        zero_ref[...] = jnp.zeros_like(zero_ref)
        for wait in (False, True):
            def tail(e, carry):
                _run_copies(tl_ref[e], zero_ref, 0, xs_ref, ts_ref[e], zsem, wait)
                return carry
            lax.fori_loop(0, N_EXP, tail, 0)

    mt = meta_ref[...].T
    srow = lax.broadcasted_iota(jnp.int32, (s, 1), 0).astype(F32)
    perm = jnp.zeros((s, TOKEN_BLOCK), F32)
    gperm = jnp.zeros((s, TOKEN_BLOCK), F32)
    for k in range(TOP_K):
        hit = srow == mt[k:k + 1]
        perm = jnp.where(hit, 1.0, perm)
        gperm = jnp.where(hit, mt[TOP_K + k:TOP_K + k + 1], gperm)
    sorted_rows = _dot(perm.astype(BF16), h_ref[...])
    gate_col = jnp.sum(gperm, axis=-1, keepdims=True)

    @pl.when(b >= 2)
    def _():
        _wait_rows(rows_ref[b - 2], buf_ref.at[slot], xs_ref, sem.at[slot], to_vmem=False)

    buf = buf_ref.at[slot]
    buf[:, :D] = sorted_rows
    buf[:, D:] = jnp.broadcast_to(gate_col, (s, 128))

    def run(e, carry):
        j = b * N_EXP + e
        _run_copies(pc_ref[j], buf, lo_ref[j], xs_ref, gs_ref[j], sem.at[slot], wait=False)
        return carry
    lax.fori_loop(0, N_EXP, run, 0)

    @pl.when(b == nb - 1)
    def _():
        @pl.when(b >= 1)
        def _():
            _wait_rows(rows_ref[b - 1], buf_ref.at[1 - slot], xs_ref, sem.at[1 - slot], to_vmem=False)
        _wait_rows(rows_ref[b], buf, xs_ref, sem.at[slot], to_vmem=False)


def _dispatch(tabs, h2, meta):
    T = h2.shape[0]
    nb = T // TOKEN_BLOCK
    rows = _max_expert_tiles(nb) * EXPERT_TILE
    tok = lambda w: pl.BlockSpec((TOKEN_BLOCK, w), lambda b, *_: (b, 0))
    return pl.pallas_call(
        _dispatch_kernel,
        grid_spec=pltpu.PrefetchScalarGridSpec(
            num_scalar_prefetch=6,
            grid=(nb,),
            in_specs=[tok(D), tok(128)],
            out_specs=pl.BlockSpec(memory_space=pl.ANY),
            scratch_shapes=[pltpu.VMEM((2, LOCAL_SLOTS, SORTED_W), F32), pltpu.VMEM((EXPERT_TILE, SORTED_W), F32),
                            pltpu.SemaphoreType.DMA((2,)), pltpu.SemaphoreType.DMA(())],
        ),
        out_shape=jax.ShapeDtypeStruct((rows, SORTED_W), F32),
        compiler_params=_cp(("arbitrary",)),
        name="moe_dispatch",
    )(tabs["pc"], tabs["lo"], tabs["gstart"], tabs["tail_start"], tabs["tail_len"], tabs["rows"], h2, meta)


def _expert_ffn_kernel(te_ref, tr_ref, fl_ref, nx_ref, x_ref, wg_hbm, bg_ref, wu_hbm, bu_ref, wd_hbm, bd_ref, y_ref,
                       wf_ref, wbf_ref, sem, *, layer):
    i = pl.program_id(0)
    fl = fl_ref[i]
    weights = (wg_hbm, wu_hbm, wd_hbm)

    def weight_copy(j, expert):
        return pltpu.make_async_copy(weights[j].at[layer, expert], wf_ref.at[j], sem.at[j])

    @pl.when((fl & 1) != 0)
    def _():
        @pl.when((fl & 2) != 0)
        def _():
            @pl.when(i == 0)
            def _():
                for j in range(3):
                    weight_copy(j, te_ref[i]).start()

            for j in range(3):
                weight_copy(j, te_ref[i]).wait()
                wbf_ref[j] = wf_ref[j].astype(BF16)

            @pl.when(nx_ref[i] >= 0)
            def _():
                for j in range(3):
                    weight_copy(j, nx_ref[i]).start()

        x = x_ref[:, :D].astype(BF16)
        gate = x_ref[:, D:D + 1]
        g = jnp.minimum(_dot(x, wbf_ref[0]) + bg_ref[0], SWIGLU_LIMIT)
        u = jnp.clip(_dot(x, wbf_ref[1]) + bu_ref[0], -SWIGLU_LIMIT, SWIGLU_LIMIT)
        a = (u + 1.0) * (g * jax.nn.sigmoid(SWIGLU_ALPHA * g))
        y_ref[...] = gate * (_dot(a.astype(BF16), wbf_ref[2]) + bd_ref[0])


def _expert_ffn(tabs, xs, w_gate, b_gate, w_up, b_up, w_down, b_down, layer):
    rows = xs.shape[0]
    hbm = pl.BlockSpec(memory_space=pl.ANY)
    bspec = pl.BlockSpec((1, 1, D), lambda i, te, *_: (te[i], 0, 0))
    return pl.pallas_call(
        functools.partial(_expert_ffn_kernel, layer=layer),
        grid_spec=pltpu.PrefetchScalarGridSpec(
            num_scalar_prefetch=4,
            grid=(rows // EXPERT_TILE,),
            in_specs=[pl.BlockSpec((EXPERT_TILE, SORTED_W), lambda i, te, tr, *_: (tr[i], 0)),
                      hbm, bspec, hbm, bspec, hbm, bspec],
            out_specs=pl.BlockSpec((EXPERT_TILE, D), lambda i, te, tr, *_: (tr[i], 0)),
            scratch_shapes=[pltpu.VMEM((3, D, D), F32), pltpu.VMEM((3, D, D), BF16), pltpu.SemaphoreType.DMA((3,))],
        ),
        out_shape=jax.ShapeDtypeStruct((rows, D), F32),
        compiler_params=_cp(("arbitrary",)),
        name="expert_ffn",
    )(tabs["tile_expert"], tabs["tile_row"], tabs["tile_flags"], tabs["tile_next"], xs,
      w_gate, b_gate[layer].reshape(N_EXP, 1, D), w_up, b_up[layer].reshape(N_EXP, 1, D),
      w_down, b_down[layer].reshape(N_EXP, 1, D))


def _combine_kernel(pc_ref, lo_ref, gs_ref, rows_ref, ys_ref, meta_ref, x1_ref, mod_ref, lg_ref, lb_ref,
                    oc_ref, ol_ref, buf_ref, sem, *, nct):
    b = pl.program_id(0)
    nb = pl.num_programs(0)
    s = buf_ref.shape[1]
    slot = b % 2

    def fetch(blk, to_slot):
        def run(e, carry):
            j = blk * N_EXP + e
            _run_copies(pc_ref[j], ys_ref, gs_ref[j], buf_ref.at[to_slot], lo_ref[j], sem.at[to_slot], wait=False)
            return carry
        lax.fori_loop(0, N_EXP, run, 0)

    @pl.when(b == 0)
    def _():
        buf_ref[...] = jnp.zeros_like(buf_ref)
        fetch(b, slot)

    @pl.when(b + 1 < nb)
    def _():
        fetch(b + 1, 1 - slot)

    _wait_rows(rows_ref[b], buf_ref.at[slot], ys_ref, sem.at[slot], to_vmem=True)

    meta = meta_ref[...]
    scol = lax.broadcasted_iota(jnp.int32, (1, s), 1).astype(F32)
    w = jnp.zeros((TOKEN_BLOCK, s), F32)
    for k in range(TOP_K):
        w = jnp.where(meta[:, k:k + 1] == scol, 1.0, w)
    w = w.astype(BF16)
    moe = _dot(w, buf_ref[slot].astype(BF16))
    m = mod_ref[0, 0]
    out = _layer_norm(DN_ALPHA * x1_ref[...] + m[5:6] * moe, lg_ref[...], lb_ref[...])

    @pl.when(b < nct)
    def _():
        oc_ref[...] = out

    @pl.when(b >= nct)
    def _():
        ol_ref[...] = out


def _combine_ln2(tabs, ys, meta, x1, mod, ln_g, ln_b, layer, t_ctx, dec_seq):
    T = x1.shape[0]
    nct = t_ctx // TOKEN_BLOCK
    row = lambda b, *_: (layer, _mod_row(b, TOKEN_BLOCK, t_ctx, dec_seq), 0, 0)
    tok = lambda w: pl.BlockSpec((TOKEN_BLOCK, w), lambda b, *_: (b, 0))
    const = lambda shape: pl.BlockSpec(shape, lambda b, *_: (0, 0))
    return pl.pallas_call(
        functools.partial(_combine_kernel, nct=nct),
        grid_spec=pltpu.PrefetchScalarGridSpec(
            num_scalar_prefetch=4,
            grid=(T // TOKEN_BLOCK,),
            in_specs=[pl.BlockSpec(memory_space=pl.ANY), tok(128), tok(D), pl.BlockSpec((1, 1, 6, D), row),
                      const((1, D)), const((1, D))],
            out_specs=list(_ctx_lat_specs((TOKEN_BLOCK, D), nct)),
            scratch_shapes=[pltpu.VMEM((2, LOCAL_SLOTS, D), F32), pltpu.SemaphoreType.DMA((2,))],
        ),
        out_shape=[jax.ShapeDtypeStruct((t_ctx, D), F32), jax.ShapeDtypeStruct((T - t_ctx, D), F32)],
        compiler_params=_cp(("arbitrary",)),
        name="moe_combine_ln2",
    )(tabs["pc"], tabs["lo"], tabs["gstart"], tabs["rows"], ys, meta, x1, mod, ln_g.reshape(1, D),
      ln_b.reshape(1, D))


def _moe_ln2(h2, meta, cnt, x1, mod, w_gate, b_gate, w_up, b_up, w_down, b_down, ln_g, ln_b, layer, t_ctx, dec_seq):
    tabs = _routing_tables(cnt[:, 0, :N_EXP].astype(jnp.int32))
    xs = _dispatch(tabs, h2, meta)
    ys = _expert_ffn(tabs, xs, w_gate, b_gate, w_up, b_up, w_down, b_down, layer)
    return _combine_ln2(tabs, ys, meta, x1, mod, ln_g, ln_b, layer, t_ctx, dec_seq)


def kernel(x_prompt, x_sample, cache_k, cache_v, c, c_ctx, w_mod, b_mod, w_in, w_out, attn_sink, hy_conv_w, hy_conv_b, hy_w1, hy_b1, hy_f1, hy_w2, hy_b2, hy_f2, hy_w3, hy_deltas, hy_bias, sc_conv_w, ln1_g, ln1_b, router_w, router_b, w_gate, b_gate, w_up, b_up, w_down, b_down, ln2_g, ln2_b):
    nb, seq, _ = x_prompt.shape
    db, dseq, _ = x_sample.shape
    past = cache_k.shape[2]
    t_ctx = nb * seq
    x = (x_prompt.reshape(t_ctx, D), x_sample.reshape(db * dseq, D))

    cond = jnp.zeros((8, D), F32).at[0].set(c_ctx).at[1:1 + db].set(c)
    mod = _modulation(cond, w_mod, b_mod)

    tables = _rope_tables(dseq)
    fft_c = _fft_constants(dseq)

    new_k, new_v = [], []
    for l in range(DEPTH):
        q, k, v, u_hy, u_sc = _in_projection(x[0], x[1], mod, w_in, l, dseq)
        new_k.append(k[:t_ctx].reshape(nb, seq, N_KV, HD))
        new_v.append(v[:t_ctx].reshape(nb, seq, N_KV, HD))
        oa_ctx = _context_attention(q, k, v, attn_sink[l], nb, seq)
        oa_lat = _latent_attention(q, k, v, cache_k[:, l].reshape(db, past, KV_W),
                                   cache_v[:, l].reshape(db, past, KV_W), attn_sink[l], tables, t_ctx, db, dseq)
        mlp = (hy_w1[l], hy_b1[l], hy_f1[l], hy_w2[l], hy_b2[l], hy_f2[l], hy_w3[l], hy_deltas[l])
        ohy_ctx, osc_ctx = _ctx_mixers(u_hy, u_sc, hy_conv_w[l], hy_conv_b[l], sc_conv_w[l], hy_bias[l], mlp,
                                       nb, seq)
        x0, g, gt, osc_lat = _mixer_prep(u_hy, u_sc, hy_conv_w[l], hy_conv_b[l], sc_conv_w[l], t_ctx, db, dseq)
        ohy_lat = _hyena_long_conv(x0, g, gt, _hyena_filters(dseq, mlp), hy_bias[l], fft_c, db, dseq)
        x1, h2, meta, cnt = _out_projection((oa_ctx, oa_lat), (ohy_ctx, ohy_lat), (osc_ctx, osc_lat), x, mod,
                                            w_out, ln1_g[l], ln1_b[l], router_w[l], router_b[l], l, dseq)
        x = _moe_ln2(h2, meta, cnt, x1, mod, w_gate, b_gate, w_up, b_up, w_down, b_down, ln2_g[l], ln2_b[l],
                     l, t_ctx, dseq)

    y_prompt = x[0].reshape(nb, seq, D)
    y_sample = x[1].reshape(db, dseq, D)
    return (y_prompt, y_sample, jnp.stack(new_k, axis=1), jnp.stack(new_v, axis=1))
```

```python
import functools
import math

import jax
import jax.numpy as jnp
from jax import lax
from jax.experimental import pallas as pl
from jax.experimental.pallas import tpu as pltpu

F32 = jnp.float32
BF16 = jnp.bfloat16
HI = lax.Precision.HIGHEST

D = 1024
DEPTH = 2
N_HEADS = 8
N_KV = 2
HD = 64
GQA = N_HEADS // N_KV
ATT_W = N_HEADS * HD
KV_W = N_KV * HD
WINDOW = 128
BLK = 128
GRID_W = 64
ROPE_BASE = 10000.0
HY_W = 256
SC_W = 256
FILTER_EMB = 33
FILTER_BANDS = 16
FILTER_HIDDEN = 64
IN_W = ATT_W + 2 * KV_W + 3 * HY_W + 3 * SC_W
N_EXP = 32
TOP_K = 4
SWIGLU_LIMIT = 7.0
SWIGLU_ALPHA = 1.702
DN_ALPHA = (2 * DEPTH) ** 0.25
LN_EPS = 1e-5
NEG_INF = float("-inf")

VMEM_LIMIT = 56 * 1024 * 1024

TOKEN_BLOCK = 256
ROW_PAD = 8
RUN_PIECES = (256, 128, 64, 32, 16, 8)
EXPERT_TILE = 512
LOCAL_SLOTS = TOP_K * TOKEN_BLOCK + N_EXP * ROW_PAD


def _cp(sem):
    return pltpu.CompilerParams(dimension_semantics=sem, vmem_limit_bytes=VMEM_LIMIT)


def _dot(a, b, precision=None):
    return jnp.dot(a, b, preferred_element_type=F32, precision=precision)


def _dot_nt(a, b):
    return lax.dot_general(a, b, (((1,), (1,)), ((), ())), preferred_element_type=F32)


def _layer_norm(z, g, b):
    mu = jnp.mean(z, axis=-1, keepdims=True)
    zc = z - mu
    var = jnp.mean(zc * zc, axis=-1, keepdims=True)
    return zc * lax.rsqrt(var + LN_EPS) * g + b


def _mod_kernel(cond_ref, w_ref, b_ref, o_ref):
    c = cond_ref[...]
    s = c * jax.nn.sigmoid(c)
    o_ref[0] = _dot(s, w_ref[0], HI) + b_ref[0]


def _modulation(cond, w_mod, b_mod):
    tn = 1024
    out = pl.pallas_call(
        _mod_kernel,
        grid=(DEPTH, 6 * D // tn),
        in_specs=[
            pl.BlockSpec((8, D), lambda l, j: (0, 0)),
            pl.BlockSpec((1, D, tn), lambda l, j: (l, 0, j)),
            pl.BlockSpec((1, 1, tn), lambda l, j: (l, 0, j)),
        ],
        out_specs=pl.BlockSpec((1, 8, tn), lambda l, j: (l, 0, j)),
        out_shape=jax.ShapeDtypeStruct((DEPTH, 8, 6 * D), F32),
        compiler_params=_cp(("arbitrary", "arbitrary")),
        name="modulation",
    )(cond, w_mod, b_mod.reshape(DEPTH, 1, 6 * D))
    return out.reshape(DEPTH, 8, 6, D)


def _mod_row(i, tm, t_ctx, dec_seq):
    nct = t_ctx // tm
    return jnp.where(i < nct, 0, 1 + (i - nct) // (dec_seq // tm))


def _ctx_lat_specs(shape, nct, ctx_offset=0, lat_offset=0):
    ctx = pl.BlockSpec(shape, lambda i, *_: (jnp.minimum(i, nct - 1) + ctx_offset, 0))
    lat = pl.BlockSpec(shape, lambda i, *_: (jnp.maximum(i - nct, 0) + lat_offset, 0))
    return ctx, lat


def _inproj_kernel(xc_ref, xl_ref, mod_ref, w_ref, q_ref, k_ref, v_ref, uh_ref, us_ref, wbf_ref, *, nct):
    @pl.when(pl.program_id(0) == 0)
    def _():
        wbf_ref[...] = w_ref[...].astype(BF16)

    m = mod_ref[0, 0]
    x = jnp.where(pl.program_id(0) < nct, xc_ref[...], xl_ref[...])
    h = x * (1.0 + m[1:2]) + m[0:1]
    y = _dot(h.astype(BF16), wbf_ref[...])
    o0 = ATT_W
    o1 = o0 + KV_W
    o2 = o1 + KV_W
    o3 = o2 + 3 * HY_W
    q_ref[...] = y[:, :o0].astype(q_ref.dtype)
    k_ref[...] = y[:, o0:o1]
    v_ref[...] = y[:, o1:o2]
    uh_ref[...] = y[:, o2:o3].astype(uh_ref.dtype)
    us_ref[...] = y[:, o3:].astype(us_ref.dtype)


def _in_projection(xc, xl, mod, w_in, layer, dec_seq):
    t_ctx = xc.shape[0]
    T = t_ctx + xl.shape[0]
    tm = 512
    row = lambda i: (layer, _mod_row(i, tm, t_ctx, dec_seq), 0, 0)
    tok = lambda w: pl.BlockSpec((tm, w), lambda i: (i, 0))
    widths = (ATT_W, KV_W, KV_W, 3 * HY_W, 3 * SC_W)
    return pl.pallas_call(
        functools.partial(_inproj_kernel, nct=t_ctx // tm),
        grid=(T // tm,),
        in_specs=[
            *_ctx_lat_specs((tm, D), t_ctx // tm),
            pl.BlockSpec((1, 1, 6, D), row),
            pl.BlockSpec((None, D, IN_W), lambda i: (layer, 0, 0)),
        ],
        out_specs=[tok(w) for w in widths],
        out_shape=[jax.ShapeDtypeStruct((T, w), dt) for w, dt in zip(widths, (BF16, F32, F32, BF16, BF16))],
        scratch_shapes=[pltpu.VMEM((D, IN_W), BF16)],
        compiler_params=_cp(("arbitrary",)),
        name="in_projection",
    )(xc, xl, mod, w_in)


def _swap_halves(x):
    w = x.shape[-1]
    lane = lax.broadcasted_iota(jnp.int32, x.shape, 1)
    return jnp.where((lane % HD) < HD // 2, pltpu.roll(x, w - HD // 2, 1), pltpu.roll(x, HD // 2, 1))


def _rope(x, cos, sin_signed):
    return x * cos + _swap_halves(x) * sin_signed


def _group_rows(q, g):
    return jnp.concatenate([q[:, (GQA * g + r) * HD:(GQA * g + r + 1) * HD] for r in range(GQA)], axis=0)


def _sink_column(sink_ref, g, rows):
    ridx = lax.broadcasted_iota(jnp.int32, (GQA * rows, 1), 0)
    col = jnp.full((GQA * rows, 1), sink_ref[GQA * g + GQA - 1], F32)
    for r in range(GQA - 2, -1, -1):
        col = jnp.where(ridx < (r + 1) * rows, sink_ref[GQA * g + r], col)
    return col


def _ungroup(outs, rows):
    return jnp.concatenate([o[r * rows:(r + 1) * rows] for o in outs for r in range(GQA)], axis=1)


def _ctx_attn_kernel(sink_ref, q_ref, k_ref, v_ref, o_ref):
    rows = q_ref.shape[0]
    q = q_ref[...].astype(F32) * (HD ** -0.5)
    k = k_ref[...]
    v = v_ref[...]
    outs = []
    for g in range(N_KV):
        qg = _group_rows(q, g).astype(BF16)
        kg = k[:, g * HD:(g + 1) * HD].astype(BF16)
        vg = v[:, g * HD:(g + 1) * HD].astype(BF16)
        s = _dot_nt(qg, kg)
        sink = _sink_column(sink_ref, g, rows)
        m = jnp.maximum(jnp.max(s, axis=-1, keepdims=True), sink)
        e = jnp.exp(s - m)
        den = jnp.sum(e, axis=-1, keepdims=True) + jnp.exp(sink - m)
        outs.append(_dot(e.astype(BF16), vg) / den)
    o_ref[...] = _ungroup(outs, rows).astype(o_ref.dtype)


def _context_attention(q, k, v, sink, B, L):
    tok = lambda w: pl.BlockSpec((L, w), lambda b: (b, 0))
    return pl.pallas_call(
        _ctx_attn_kernel,
        grid=(B,),
        in_specs=[pl.BlockSpec(memory_space=pltpu.SMEM), tok(ATT_W), tok(KV_W), tok(KV_W)],
        out_specs=tok(ATT_W),
        out_shape=jax.ShapeDtypeStruct((B * L, ATT_W), BF16),
        compiler_params=_cp(("arbitrary",)),
        name="context_attention",
    )(sink, q, k, v)


def _lat_attn_kernel(sink_ref, q_ref, k_ref, v_ref, kc_ref, vc_ref, ck_ref, sk_ref, o_ref,
                     kr_ref, vb_ref, kcb_ref, vcb_ref):
    n = pl.program_id(1)
    L = k_ref.shape[0]
    nk = 3 * BLK

    @pl.when(n == 0)
    def _():
        kr_ref[...] = _rope(k_ref[...], ck_ref[...], sk_ref[...]).astype(BF16)
        vb_ref[...] = v_ref[...].astype(BF16)
        kcb_ref[...] = kc_ref[...].astype(BF16)
        vcb_ref[...] = vc_ref[...].astype(BF16)

    ws = pl.multiple_of(jnp.clip((n - 1) * BLK, 0, L - nk), BLK)
    qs = pl.multiple_of(n * BLK, BLK)
    heads_per_table = N_HEADS // N_KV
    cq = jnp.tile(ck_ref[pl.ds(qs, BLK), :], (1, heads_per_table))
    sq = jnp.tile(sk_ref[pl.ds(qs, BLK), :], (1, heads_per_table))
    q = _rope(q_ref[...].astype(F32), cq, sq) * (HD ** -0.5)
    kl = kr_ref[pl.ds(ws, nk), :]
    vl = vb_ref[pl.ds(ws, nk), :]
    kc = kcb_ref[...]
    vc = vcb_ref[...]
    qpos = n * BLK + lax.broadcasted_iota(jnp.int32, (GQA * BLK, 1), 0) % BLK
    kpos = ws + lax.broadcasted_iota(jnp.int32, (1, nk), 1)
    valid = jnp.abs(kpos - qpos) <= WINDOW
    outs = []
    for g in range(N_KV):
        sl = slice(g * HD, (g + 1) * HD)
        qg = _group_rows(q, g).astype(BF16)
        s_loc = jnp.where(valid, _dot_nt(qg, kl[:, sl]), NEG_INF)
        s_ctx = _dot_nt(qg, kc[:, sl])
        sink = _sink_column(sink_ref, g, BLK)
        m = jnp.maximum(jnp.maximum(jnp.max(s_loc, axis=-1, keepdims=True),
                                    jnp.max(s_ctx, axis=-1, keepdims=True)), sink)
        e_loc = jnp.exp(s_loc - m)
        e_ctx = jnp.exp(s_ctx - m)
        den = (jnp.sum(e_loc, axis=-1, keepdims=True) + jnp.sum(e_ctx, axis=-1, keepdims=True)
               + jnp.exp(sink - m))
        o = _dot(e_loc.astype(BF16), vl[:, sl]) + _dot(e_ctx.astype(BF16), vc[:, sl])
        outs.append(o / den)
    o_ref[...] = _ungroup(outs, BLK).astype(o_ref.dtype)


def _rope_tables(L):
    rows = L // GRID_W
    row = jnp.repeat(jnp.arange(rows, dtype=F32), GRID_W)
    col = jnp.tile(jnp.arange(GRID_W, dtype=F32), rows)
    pairs = HD // 4
    inv = ROPE_BASE ** (-jnp.arange(pairs, dtype=F32) / pairs)
    ang = jnp.concatenate([row[:, None] * inv, col[:, None] * inv], axis=-1)
    cos = jnp.cos(ang)
    sin = jnp.sin(ang)
    return (jnp.tile(jnp.concatenate([cos, cos], axis=-1), (1, N_KV)),
            jnp.tile(jnp.concatenate([-sin, sin], axis=-1), (1, N_KV)))


def _latent_attention(q, k, v, kc, vc, sink, tables, t_ctx, B, L):
    assert t_ctx % L == 0
    P = kc.shape[1]
    ck, sk = tables
    nbk = L // BLK
    seq = pl.BlockSpec((L, KV_W), lambda b, n: (t_ctx // L + b, 0))
    ctx = pl.BlockSpec((None, P, KV_W), lambda b, n: (b, 0, 0))
    return pl.pallas_call(
        _lat_attn_kernel,
        grid=(B, nbk),
        in_specs=[
            pl.BlockSpec(memory_space=pltpu.SMEM),
            pl.BlockSpec((BLK, ATT_W), lambda b, n: (t_ctx // BLK + b * nbk + n, 0)),
            seq, seq, ctx, ctx,
            pl.BlockSpec((L, KV_W), lambda b, n: (0, 0)),
            pl.BlockSpec((L, KV_W), lambda b, n: (0, 0)),
        ],
        out_specs=pl.BlockSpec((BLK, ATT_W), lambda b, n: (b * nbk + n, 0)),
        out_shape=jax.ShapeDtypeStruct((B * L, ATT_W), BF16),
        scratch_shapes=[pltpu.VMEM((L, KV_W), BF16), pltpu.VMEM((L, KV_W), BF16),
                        pltpu.VMEM((P, KV_W), BF16), pltpu.VMEM((P, KV_W), BF16)],
        compiler_params=_cp(("arbitrary", "arbitrary")),
        name="latent_attention",
    )(sink, q, k, v, kc, vc, ck, sk)


def _conv3(u, prev_row, next_row, w):
    tl = u.shape[0]
    ridx = lax.broadcasted_iota(jnp.int32, (tl, 1), 0)
    dn = jnp.where(ridx == 0, prev_row, pltpu.roll(u, 1, 0))
    up = jnp.where(ridx == tl - 1, next_row, pltpu.roll(u, tl - 1, 0))
    return dn * w[0:1] + u * w[1:2] + up * w[2:3]


def _store_time_major_inner(dst_ref, src, scr_ref):
    n2 = dst_ref.shape[0]
    rows = src.shape[0] // n2
    for h in range(scr_ref.shape[0]):
        scr_ref[h] = src[:, h * 128:(h + 1) * 128]
    for j in range(n2):
        for h in range(scr_ref.shape[0]):
            dst_ref[j, :, h * 128:(h + 1) * 128] = scr_ref[h, pl.ds(j, rows, stride=n2), :]


def _mixprep_kernel(uh_ref, uhp_ref, uhn_ref, us_ref, usp_ref, usn_ref, hw_ref, hb_ref, sw_ref,
                    x0_ref, g_ref, gt_ref, osc_ref, scr_ref):
    t = pl.program_id(1)
    first = t == 0
    last = t == pl.num_programs(1) - 1
    c = HY_W
    hw = hw_ref[...]
    halo = uhp_ref.shape[0]
    f32 = lambda a: a.astype(F32)
    uh = _conv3(f32(uh_ref[...]), jnp.where(first, 0.0, f32(uhp_ref[halo - 1:halo])),
                jnp.where(last, 0.0, f32(uhn_ref[0:1])), hw)
    uh = uh + hb_ref[...]
    x0_ref[...] = uh[:, :c]
    g = uh[:, c:2 * c] * uh[:, 2 * c:]
    g_ref[...] = g
    _store_time_major_inner(gt_ref, g, scr_ref)
    us = f32(us_ref[...])
    usp = f32(usp_ref[halo - 1:halo])
    usn = f32(usn_ref[0:1])
    prod = us[:, c:2 * c] * us[:, 2 * c:]
    pprev = jnp.where(first, 0.0, usp[:, c:2 * c] * usp[:, 2 * c:])
    pnext = jnp.where(last, 0.0, usn[:, c:2 * c] * usn[:, 2 * c:])
    osc_ref[...] = (us[:, :c] * _conv3(prod, pprev, pnext, sw_ref[...])).astype(osc_ref.dtype)


def _mixer_prep(u_hy, u_sc, hy_conv_w, hy_conv_b, sc_conv_w, row0, B, L):
    W = u_hy.shape[1]
    n1, n2 = _fft_split(L)
    tl = 8 * n2
    assert row0 % tl == 0 and L % tl == 0
    nt = L // tl
    halo = 16
    rh = tl // halo
    base = lambda b: row0 // halo + b * (L // halo)
    main = pl.BlockSpec((tl, W), lambda b, t: (row0 // tl + b * nt + t, 0))
    prev = pl.BlockSpec((halo, W), lambda b, t: (base(b) + jnp.maximum(t * rh - 1, 0), 0))
    nxt = pl.BlockSpec((halo, W), lambda b, t: (base(b) + jnp.minimum((t + 1) * rh, L // halo - 1), 0))
    const = lambda shape: pl.BlockSpec(shape, lambda b, t: (0, 0))
    out = pl.BlockSpec((tl, HY_W), lambda b, t: (b * nt + t, 0))
    flat = jax.ShapeDtypeStruct((B * L, HY_W), F32)
    return pl.pallas_call(
        _mixprep_kernel,
        grid=(B, nt),
        in_specs=[main, prev, nxt, main, prev, nxt, const((3, W)), const((1, W)), const((3, SC_W))],
        out_specs=[out, out, pl.BlockSpec((None, n2, 8, HY_W), lambda b, t: (b, 0, t, 0)), out],
        out_shape=[flat, flat, jax.ShapeDtypeStruct((B, n2, n1 // 2, HY_W), F32),
                   jax.ShapeDtypeStruct((B * L, SC_W), BF16)],
        scratch_shapes=[pltpu.VMEM((HY_W // 128, tl, 128), F32)],
        compiler_params=_cp(("arbitrary", "arbitrary")),
        name="mixer_prep",
    )(u_hy, u_hy, u_hy, u_sc, u_sc, u_sc, hy_conv_w, hy_conv_b.reshape(1, W), sc_conv_w)


def _filter_mlp(z, w1_ref, b1_ref, f1_ref, w2_ref, b2_ref, f2_ref, w3_ref, dl_ref, row0):
    tl = z.shape[0]
    h = jnp.sin(f1_ref[...] * (_dot(z, w1_ref[...], HI) + b1_ref[...]))
    h = jnp.sin(f2_ref[...] * (_dot(h, w2_ref[...], HI) + b2_ref[...]))
    h = _dot(h, w3_ref[...], HI) * jnp.exp(-z[:, 0:1] * jnp.abs(dl_ref[...]))
    row = row0 + lax.broadcasted_iota(jnp.int32, (tl, 1), 0)
    return h[:, :HY_W], jnp.where(row == 0, 0.0, h[:, HY_W:])


def _filter_kernel(z_ref, w1_ref, b1_ref, f1_ref, w2_ref, b2_ref, f2_ref, w3_ref, dl_ref, o_ref, scr_ref):
    hf, hb = _filter_mlp(z_ref[...], w1_ref, b1_ref, f1_ref, w2_ref, b2_ref, f2_ref, w3_ref, dl_ref,
                         pl.program_id(0) * z_ref.shape[0])
    _store_time_major_inner(o_ref.at[0], hf, scr_ref)
    _store_time_major_inner(o_ref.at[1], hb, scr_ref)


def _filter_features(L):
    t = jnp.linspace(0.0, 1.0, L, dtype=F32)[:, None]
    w = 2.0 * math.pi * jnp.arange(L, dtype=F32)[:, None] / L
    bands = jnp.linspace(1e-4, FILTER_BANDS - 1, FILTER_BANDS, dtype=F32)[None, :]
    z = jnp.concatenate([t, jnp.cos(bands * w), -jnp.sin(bands * w)], axis=-1)
    return jnp.pad(z, ((0, 0), (0, 128 - FILTER_EMB)))


def _filter_mlp_operands(w1, b1, f1, w2, b2, f2, w3, deltas):
    H = FILTER_HIDDEN
    ops = (jnp.pad(w1, ((0, 128 - FILTER_EMB), (0, 0))), b1.reshape(1, H), f1.reshape(1, H), w2, b2.reshape(1, H),
           f2.reshape(1, H), w3, deltas.reshape(1, 2 * HY_W))
    specs = [pl.BlockSpec(o.shape, lambda *_: (0, 0)) for o in ops]
    return ops, specs


def _hyena_filters(L, mlp):
    n1, n2 = _fft_split(L)
    tl = 8 * n2
    ops, specs = _filter_mlp_operands(*mlp)
    return pl.pallas_call(
        _filter_kernel,
        grid=(L // tl,),
        in_specs=[pl.BlockSpec((tl, 128), lambda i: (i, 0)), *specs],
        out_specs=pl.BlockSpec((2, n2, 8, HY_W), lambda i: (0, 0, i, 0)),
        out_shape=jax.ShapeDtypeStruct((2, n2, n1 // 2, HY_W), F32),
        scratch_shapes=[pltpu.VMEM((HY_W // 128, tl, 128), F32)],
        compiler_params=_cp(("arbitrary",)),
        name="hyena_filters",
    )(_filter_features(L), *ops)


def _dense_dft_constants(L):
    n = 2 * L
    a = (jnp.arange(n, dtype=jnp.int32)[:, None] * jnp.arange(L, dtype=jnp.int32)[None, :]) % n
    th = (2.0 * math.pi / n) * a.astype(F32)
    fwd = jnp.concatenate([jnp.cos(th), -jnp.sin(th)], axis=0)
    inv = jnp.concatenate([jnp.cos(th).T, -jnp.sin(th).T], axis=1) * (1.0 / n)
    return fwd.astype(BF16), inv.astype(BF16)


def _ctx_mixers_kernel(uh_ref, us_ref, hw_ref, hb_ref, sw_ref, bias_ref, fwd_ref, inv_ref, z_ref,
                       w1_ref, b1_ref, f1_ref, w2_ref, b2_ref, f2_ref, w3_ref, dl_ref, ohy_ref, osc_ref, kf_ref):
    c = HY_W
    nf = kf_ref.shape[1]

    @pl.when(pl.program_id(0) == 0)
    def _():
        hf, hb = _filter_mlp(z_ref[...], w1_ref, b1_ref, f1_ref, w2_ref, b2_ref, f2_ref, w3_ref, dl_ref, 0)
        xf = _dot(fwd_ref[...], hf.astype(BF16))
        xb = _dot(fwd_ref[...], hb.astype(BF16))
        kf_ref[0] = xf[:nf] + xb[:nf]
        kf_ref[1] = xf[nf:] - xb[nf:]

    zero = jnp.zeros((1, 1), F32)
    uh = _conv3(uh_ref[...].astype(F32), zero, zero, hw_ref[...]) + hb_ref[...]
    x0, g = uh[:, :c], uh[:, c:2 * c] * uh[:, 2 * c:]
    s = _dot(fwd_ref[...], g.astype(BF16))
    sr, si = s[:nf], s[nf:]
    kr, ki = kf_ref[0], kf_ref[1]
    y = _dot(inv_ref[...], jnp.concatenate([sr * kr - si * ki, sr * ki + si * kr], axis=0).astype(BF16))
    ohy_ref[...] = (x0 * (y + g * bias_ref[...])).astype(ohy_ref.dtype)
    us = us_ref[...].astype(F32)
    osc_ref[...] = (us[:, :c] * _conv3(us[:, c:2 * c] * us[:, 2 * c:], zero, zero, sw_ref[...])).astype(osc_ref.dtype)


def _ctx_mixers(u_hy, u_sc, hy_conv_w, hy_conv_b, sc_conv_w, hy_bias, mlp, B, L):
    W = u_hy.shape[1]
    fwd, inv = _dense_dft_constants(L)
    ops, specs = _filter_mlp_operands(*mlp)
    const = lambda shape: pl.BlockSpec(shape, lambda b: (0, 0))
    seq = lambda w: pl.BlockSpec((L, w), lambda b: (b, 0))
    return pl.pallas_call(
        _ctx_mixers_kernel,
        grid=(B,),
        in_specs=[seq(W), seq(W), const((3, W)), const((1, W)), const((3, SC_W)), const((1, HY_W)),
                  const(fwd.shape), const(inv.shape), const((L, 128)), *specs],
        out_specs=[seq(HY_W), seq(SC_W)],
        out_shape=[jax.ShapeDtypeStruct((B * L, HY_W), BF16), jax.ShapeDtypeStruct((B * L, SC_W), BF16)],
        scratch_shapes=[pltpu.VMEM((2, 2 * L, HY_W), F32)],
        compiler_params=_cp(("arbitrary",)),
        name="context_mixers",
    )(u_hy, u_sc, hy_conv_w, hy_conv_b.reshape(1, W), sc_conv_w, hy_bias.reshape(1, HY_W), fwd, inv,
      _filter_features(L), *ops)


def _fft_split(L):
    n = 2 * L
    n1 = 128 if n >= 8192 else 32
    return n1, n // n1


def _fft_constants(L):
    n1, n2 = _fft_split(L)
    n = n1 * n2
    k1 = jnp.arange(n1, dtype=jnp.int32)
    a = (k1[:, None] * jnp.arange(n1 // 2, dtype=jnp.int32)[None, :]) % n1
    th = (2.0 * math.pi / n1) * a.astype(F32)
    f1 = jnp.concatenate([jnp.cos(th), -jnp.sin(th)], axis=0)
    f3 = jnp.concatenate([jnp.cos(th).T, -jnp.sin(th).T], axis=1) * (1.0 / n)
    k = k1[:, None, None] + n1 * jnp.arange(n2, dtype=jnp.int32)[None, :, None]
    ph = (k * jnp.arange(n2, dtype=jnp.int32)[None, None, :]) % n
    ph = (2.0 * math.pi / n) * ph.astype(F32)
    mr, mi = jnp.cos(ph), -jnp.sin(ph)
    mf = jnp.concatenate([jnp.concatenate([mr, -mi], axis=2), jnp.concatenate([mi, mr], axis=2)], axis=1)
    mrt, mit = jnp.swapaxes(mr, 1, 2), jnp.swapaxes(mi, 1, 2)
    mb = jnp.concatenate([jnp.concatenate([mrt, mit], axis=2), jnp.concatenate([-mit, mrt], axis=2)], axis=1)
    return tuple(c.astype(BF16) for c in (f1, f3, mf, mb))


FFT_GROUP = 8


def _dft_rows_kernel(f_ref, x_ref, o_ref):
    for i in range(FFT_GROUP):
        o_ref[:, i, :] = _dot(f_ref[...], x_ref[i].astype(BF16))


def _dft_rows(f1, xt):
    B, n2, kh, C = xt.shape
    m = f1.shape[0]
    return pl.pallas_call(
        _dft_rows_kernel,
        grid=(B, n2 // FFT_GROUP),
        in_specs=[pl.BlockSpec((m, kh), lambda b, j: (0, 0)),
                  pl.BlockSpec((None, FFT_GROUP, kh, C), lambda b, j: (b, j, 0, 0))],
        out_specs=pl.BlockSpec((None, m, FFT_GROUP, C), lambda b, j: (b, 0, j, 0)),
        out_shape=jax.ShapeDtypeStruct((B, m, n2, C), F32),
        compiler_params=_cp(("arbitrary", "arbitrary")),
        name="fft_rows",
    )(f1, xt)


def _fft_mid_kernel(a_ref, hf_ref, hb_ref, mf_ref, mb_ref, o_ref, kf_ref):
    n2 = a_ref.shape[2]

    @pl.when(pl.program_id(1) == 0)
    def _():
        for i in range(FFT_GROUP):
            xf = _dot(mf_ref[i], jnp.concatenate([hf_ref[0, i], hf_ref[1, i]], axis=0).astype(BF16))
            xb = _dot(mf_ref[i], jnp.concatenate([hb_ref[0, i], hb_ref[1, i]], axis=0).astype(BF16))
            kf_ref[i, 0] = xf[:n2] + xb[:n2]
            kf_ref[i, 1] = xf[n2:] - xb[n2:]

    for i in range(FFT_GROUP):
        x = _dot(mf_ref[i], jnp.concatenate([a_ref[0, i], a_ref[1, i]], axis=0).astype(BF16))
        xr, xi = x[:n2], x[n2:]
        kr, ki = kf_ref[i, 0], kf_ref[i, 1]
        y = jnp.concatenate([xr * kr - xi * ki, xr * ki + xi * kr], axis=0)
        o = _dot(mb_ref[i], y.astype(BF16))
        o_ref[:, 0, i, :] = o[:n2]
        o_ref[:, 1, i, :] = o[n2:]


def _fft_mid(a, ah, mf, mb):
    B, _, n1, n2, C = a.shape
    blk = (None, 2, FFT_GROUP, n2, C)
    return pl.pallas_call(
        _fft_mid_kernel,
        grid=(n1 // FFT_GROUP, B),
        in_specs=[
            pl.BlockSpec(blk, lambda k, b: (b, 0, k, 0, 0)),
            pl.BlockSpec(blk, lambda k, b: (0, 0, k, 0, 0)),
            pl.BlockSpec(blk, lambda k, b: (1, 0, k, 0, 0)),
            pl.BlockSpec((FFT_GROUP, 2 * n2, 2 * n2), lambda k, b: (k, 0, 0)),
            pl.BlockSpec((FFT_GROUP, 2 * n2, 2 * n2), lambda k, b: (k, 0, 0)),
        ],
        out_specs=pl.BlockSpec((None, n2, 2, FFT_GROUP, C), lambda k, b: (b, 0, 0, k, 0)),
        out_shape=jax.ShapeDtypeStruct((B, n2, 2, n1, C), F32),
        scratch_shapes=[pltpu.VMEM((FFT_GROUP, 2, n2, C), F32)],
        compiler_params=_cp(("arbitrary", "arbitrary")),
        name="fft_mid",
    )(a, ah, ah, mf, mb)


def _idft_rows_kernel(f_ref, b_ref, x0_ref, g_ref, bias_ref, o_ref):
    for i in range(FFT_GROUP):
        o_ref[:, i, :] = _dot(f_ref[...], b_ref[i].astype(BF16))
    o_ref[...] = x0_ref[...] * (o_ref[...] + g_ref[...] * bias_ref[...])


def _idft_rows_gate(f3, bo, x0, g, bias):
    B, n2, m2, C = bo.shape
    kh = f3.shape[0]
    tile = pl.BlockSpec((None, kh, FFT_GROUP, C), lambda b, j: (b, 0, j, 0))
    return pl.pallas_call(
        _idft_rows_kernel,
        grid=(B, n2 // FFT_GROUP),
        in_specs=[pl.BlockSpec((kh, m2), lambda b, j: (0, 0)),
                  pl.BlockSpec((None, FFT_GROUP, m2, C), lambda b, j: (b, j, 0, 0)),
                  tile, tile, pl.BlockSpec((1, C), lambda b, j: (0, 0))],
        out_specs=tile,
        out_shape=jax.ShapeDtypeStruct((B, kh, n2, C), F32),
        compiler_params=_cp(("arbitrary", "arbitrary")),
        name="ifft_rows_gate",
    )(f3, bo, x0, g, bias)


def _hyena_long_conv(x0, g, gt, filt_t, hy_bias, consts, B, L):
    C = HY_W
    n1, n2 = _fft_split(L)
    f1, f3, mf, mb = consts
    a = _dft_rows(f1, gt).reshape(B, 2, n1, n2, C)
    ah = _dft_rows(f1, filt_t).reshape(2, 2, n1, n2, C)
    bo = _fft_mid(a, ah, mf, mb).reshape(B, n2, 2 * n1, C)
    nat = lambda z: z.reshape(B, n1 // 2, n2, C)
    out = _idft_rows_gate(f3, bo, nat(x0), nat(g), hy_bias.reshape(1, C))
    return out.reshape(B * L, C)


def _outproj_kernel(oac_ref, oal_ref, ohc_ref, ohl_ref, osc_ref, osl_ref, xc_ref, xl_ref, mod_ref, w_ref,
                    lg_ref, lb_ref, rw_ref, rb_ref, x1_ref, h2_ref, meta_ref, cnt_ref, wbf_ref, *, nct):
    @pl.when(pl.program_id(0) == 0)
    def _():
        wbf_ref[...] = w_ref[...].astype(BF16)

    m = mod_ref[0, 0]
    w = wbf_ref[...]
    is_ctx = pl.program_id(0) < nct
    pick = lambda c_ref, l_ref, dt: jnp.where(is_ctx, c_ref[...].astype(dt), l_ref[...].astype(dt))
    o = (_dot(pick(oac_ref, oal_ref, BF16), w[:ATT_W])
         + _dot(pick(ohc_ref, ohl_ref, BF16), w[ATT_W:ATT_W + HY_W])
         + _dot(pick(osc_ref, osl_ref, BF16), w[ATT_W + HY_W:]))
    x1 = _layer_norm(DN_ALPHA * pick(xc_ref, xl_ref, F32) + m[2:3] * o, lg_ref[...], lb_ref[...])
    x1_ref[...] = x1
    h2 = x1 * (1.0 + m[4:5]) + m[3:4]
    h2_hi = h2.astype(BF16)
    h2_ref[...] = h2_hi
    h2_lo = (h2 - h2_hi.astype(F32)).astype(BF16)
    logits = _dot(h2_hi, rw_ref[0]) + _dot(h2_lo, rw_ref[0]) + _dot(h2_hi, rw_ref[1]) + rb_ref[...]
    for sb in range(h2.shape[0] // TOKEN_BLOCK):
        rows = slice(sb * TOKEN_BLOCK, (sb + 1) * TOKEN_BLOCK)
        meta, cnt = _route_block(logits[rows])
        meta_ref[rows, :] = meta
        cnt_ref[sb] = jnp.broadcast_to(cnt, (8, 128))


def _route_block(logits):
    lane = lax.broadcasted_iota(jnp.int32, logits.shape, 1)
    logits = jnp.where(lane < N_EXP, logits, NEG_INF)
    picks = []
    vals = []
    for _ in range(TOP_K):
        v = jnp.max(logits, axis=-1, keepdims=True)
        idx = jnp.min(jnp.where(logits == v, lane, 128), axis=-1, keepdims=True)
        hit = lane == idx
        picks.append(hit)
        vals.append(v)
        logits = jnp.where(hit, NEG_INF, logits)
    es = [jnp.exp(v - vals[0]) for v in vals]
    den = es[0] + es[1] + es[2] + es[3]
    tb = logits.shape[0]
    msel = jnp.zeros(logits.shape, F32)
    for hit in picks:
        msel = msel + jnp.where(hit, 1.0, 0.0)
    cnt = jnp.sum(msel, axis=0, keepdims=True)
    pc = jnp.floor((cnt + (ROW_PAD - 1.0)) * (1.0 / ROW_PAD)) * ROW_PAD
    upper = jnp.where(lax.broadcasted_iota(jnp.int32, (128, 128), 0)
                      < lax.broadcasted_iota(jnp.int32, (128, 128), 1), 1.0, 0.0)
    lo = _dot(jnp.broadcast_to(pc, (8, 128)), upper, HI)[0:1]
    lower = jnp.where(lax.broadcasted_iota(jnp.int32, (tb, tb), 1)
                      < lax.broadcasted_iota(jnp.int32, (tb, tb), 0), 1.0, 0.0).astype(BF16)
    pos = lo + _dot(lower, msel.astype(BF16))
    meta = jnp.zeros(logits.shape, F32)
    for kk, (hit, e) in enumerate(zip(picks, es)):
        slot = jnp.sum(jnp.where(hit, pos, 0.0), axis=-1, keepdims=True)
        meta = jnp.where(lane == kk, slot, meta)
        meta = jnp.where(lane == TOP_K + kk, e / den, meta)
    return meta, cnt


def _out_projection(o_attn, o_hy, o_sc, x, mod, w_out, ln_g, ln_b, router_w, router_b, layer, dec_seq):
    t_ctx = x[0].shape[0]
    T = t_ctx + x[1].shape[0]
    blocks_per_step = 2
    tm = blocks_per_step * TOKEN_BLOCK
    nct = t_ctx // tm
    row = lambda i: (layer, _mod_row(i, tm, t_ctx, dec_seq), 0, 0)
    tok = lambda w: pl.BlockSpec((tm, w), lambda i: (i, 0))
    const = lambda shape: pl.BlockSpec(shape, lambda i: (0, 0))
    rw = jnp.pad(router_w, ((0, 0), (0, 128 - N_EXP)))
    rw_hi = rw.astype(BF16)
    rw = jnp.stack([rw_hi, (rw - rw_hi.astype(F32)).astype(BF16)])
    rb = jnp.pad(router_b, (0, 128 - N_EXP)).reshape(1, 128)
    return pl.pallas_call(
        functools.partial(_outproj_kernel, nct=nct),
        grid=(T // tm,),
        in_specs=[*_ctx_lat_specs((tm, ATT_W), nct), *_ctx_lat_specs((tm, HY_W), nct),
                  *_ctx_lat_specs((tm, SC_W), nct), *_ctx_lat_specs((tm, D), nct),
                  pl.BlockSpec((1, 1, 6, D), row),
                  pl.BlockSpec((None, D, D), lambda i: (layer, 0, 0)), const((1, D)), const((1, D)),
                  pl.BlockSpec((2, D, 128), lambda i: (0, 0, 0)), const((1, 128))],
        out_specs=[tok(D), tok(D), tok(128), pl.BlockSpec((blocks_per_step, 8, 128), lambda i: (i, 0, 0))],
        out_shape=[jax.ShapeDtypeStruct((T, D), F32), jax.ShapeDtypeStruct((T, D), BF16),
                   jax.ShapeDtypeStruct((T, 128), F32), jax.ShapeDtypeStruct((T // TOKEN_BLOCK, 8, 128), F32)],
        scratch_shapes=[pltpu.VMEM((D, D), BF16)],
        compiler_params=_cp(("arbitrary",)),
        name="out_projection_router",
    )(*o_attn, *o_hy, *o_sc, *x, mod, w_out, ln_g.reshape(1, D), ln_b.reshape(1, D), rw, rb)


def _routing_tables(cnt):
    nb = cnt.shape[0]
    def tri_dot(a, b):
        return jnp.dot(a.astype(F32), b.astype(F32), precision=HI).astype(jnp.int32)
    ex = jnp.arange(N_EXP, dtype=jnp.int32)
    before_e = ex[:, None] < ex[None, :]
    bl = jnp.arange(nb, dtype=jnp.int32)
    before_b = bl[None, :] < bl[:, None]
    pc = (cnt + ROW_PAD - 1) // ROW_PAD * ROW_PAD
    lo = tri_dot(pc, before_e)
    tot = jnp.sum(pc, axis=0)
    tot_pad = (tot + EXPERT_TILE - 1) // EXPERT_TILE * EXPERT_TILE
    off = tri_dot(tot_pad[None, :], before_e)[0]
    gstart = off[None, :] + tri_dot(before_b, pc)
    ntile = tot_pad // EXPERT_TILE
    cum = tri_dot(ntile[None, :], before_e)[0] + ntile
    n_tiles = _max_expert_tiles(nb)
    i = jnp.arange(n_tiles, dtype=jnp.int32)
    ic = jnp.maximum(jnp.minimum(i, cum[-1] - 1), 0)
    te = jnp.sum((cum[None, :] <= ic[:, None]).astype(jnp.int32), axis=1)
    first = ic == (cum - ntile)[te]
    used = jnp.clip(tot[te] - (ic - (cum - ntile)[te]) * EXPERT_TILE, 0, EXPERT_TILE)
    half = used <= EXPERT_TILE // 2
    flags = jnp.where(i < cum[-1], 1 + 2 * first.astype(jnp.int32) + 4 * half.astype(jnp.int32), 0)
    later =(ex[None, :] > ex[:, None]) & (ntile[None, :] > 0)
    next_e = jnp.min(jnp.where(later, ex[None, :], N_EXP), axis=1)
    next_e = jnp.where(next_e < N_EXP, next_e, -1)
    i32 = lambda a: a.astype(jnp.int32).reshape(-1)
    return dict(pc=i32(pc), lo=i32(lo), gstart=i32(gstart), tail_start=i32(off + tot), tail_len=i32(tot_pad - tot),
                rows=i32(jnp.sum(pc, axis=1)), tile_expert=i32(te), tile_row=ic, tile_flags=flags,
                tile_next=i32(next_e[te]))


def _max_expert_tiles(nb):
    rows = TOP_K * nb * TOKEN_BLOCK + nb * N_EXP * (ROW_PAD - 1) + N_EXP * (EXPERT_TILE - 1)
    return rows // EXPERT_TILE + 1


def _run_copies(n, src_ref, src0, dst_ref, dst0, sem, wait):
    done = jnp.int32(0)
    for p in RUN_PIECES:
        take = (n & p) != 0

        @pl.when(take)
        def _():
            cp = pltpu.make_async_copy(src_ref.at[pl.ds(pl.multiple_of(src0 + done, ROW_PAD), p), :],
                                       dst_ref.at[pl.ds(pl.multiple_of(dst0 + done, ROW_PAD), p), :], sem)
            if wait:
                cp.wait()
            else:
                cp.start()

        done = done + jnp.where(take, p, 0)


def _wait_rows(n, vmem_ref, hbm_ref, sem, to_vmem):
    base = TOP_K * TOKEN_BLOCK

    def wait(p):
        src, dst = (hbm_ref, vmem_ref) if to_vmem else (vmem_ref, hbm_ref)
        pltpu.make_async_copy(src.at[pl.ds(0, p), :], dst.at[pl.ds(0, p), :], sem).wait()

    wait(base)
    for p in RUN_PIECES[1:]:
        pl.when(((n - base) & p) != 0)(functools.partial(wait, p))


SORTED_W = D + 128


def _dispatch_kernel(pc_ref, lo_ref, gs_ref, ts_ref, tl_ref, rows_ref, h_ref, meta_ref, xs_ref,
                     buf_ref, zero_ref, sem, zsem):
    b = pl.program_id(0)
    nb = pl.num_programs(0)
    s = buf_ref.shape[1]
    slot = b % 2

    @pl.when(b == 0)
    def _():
        zero_ref[...] = jnp.zeros_like(zero_ref)
        for wait in (False, True):
            def tail(e, carry):
                _run_copies(tl_ref[e], zero_ref, 0, xs_ref, ts_ref[e], zsem, wait)
                return carry
            lax.fori_loop(0, N_EXP, tail, 0)

    mt = meta_ref[...].T
    srow = lax.broadcasted_iota(jnp.int32, (s, 1), 0).astype(F32)
    perm = jnp.zeros((s, TOKEN_BLOCK), F32)
    gperm = jnp.zeros((s, TOKEN_BLOCK), F32)
    for k in range(TOP_K):
        hit = srow == mt[k:k + 1]
        perm = jnp.where(hit, 1.0, perm)
        gperm = jnp.where(hit, mt[TOP_K + k:TOP_K + k + 1], gperm)
    sorted_rows = _dot(perm.astype(BF16), h_ref[...])
    gate_col = jnp.sum(gperm, axis=-1, keepdims=True)

    @pl.when(b >= 2)
    def _():
        _wait_rows(rows_ref[b - 2], buf_ref.at[slot], xs_ref, sem.at[slot], to_vmem=False)

    buf = buf_ref.at[slot]
    buf[:, :D] = sorted_rows
    buf[:, D:] = jnp.broadcast_to(gate_col, (s, 128))

    def run(e, carry):
        j = b * N_EXP + e
        _run_copies(pc_ref[j], buf, lo_ref[j], xs_ref, gs_ref[j], sem.at[slot], wait=False)
        return carry
    lax.fori_loop(0, N_EXP, run, 0)

    @pl.when(b == nb - 1)
    def _():
        @pl.when(b >= 1)
        def _():
            _wait_rows(rows_ref[b - 1], buf_ref.at[1 - slot], xs_ref, sem.at[1 - slot], to_vmem=False)
        _wait_rows(rows_ref[b], buf, xs_ref, sem.at[slot], to_vmem=False)


def _dispatch(tabs, h2, meta):
    T = h2.shape[0]
    nb = T // TOKEN_BLOCK
    rows = _max_expert_tiles(nb) * EXPERT_TILE
    tok = lambda w: pl.BlockSpec((TOKEN_BLOCK, w), lambda b, *_: (b, 0))
    return pl.pallas_call(
        _dispatch_kernel,
        grid_spec=pltpu.PrefetchScalarGridSpec(
            num_scalar_prefetch=6,
            grid=(nb,),
            in_specs=[tok(D), tok(128)],
            out_specs=pl.BlockSpec(memory_space=pl.ANY),
            scratch_shapes=[pltpu.VMEM((2, LOCAL_SLOTS, SORTED_W), F32), pltpu.VMEM((EXPERT_TILE, SORTED_W), F32),
                            pltpu.SemaphoreType.DMA((2,)), pltpu.SemaphoreType.DMA(())],
        ),
        out_shape=jax.ShapeDtypeStruct((rows, SORTED_W), F32),
        compiler_params=_cp(("arbitrary",)),
        name="moe_dispatch",
    )(tabs["pc"], tabs["lo"], tabs["gstart"], tabs["tail_start"], tabs["tail_len"], tabs["rows"], h2, meta)


def _expert_ffn_kernel(te_ref, tr_ref, fl_ref, nx_ref, x_ref, wg_hbm, bg_ref, wu_hbm, bu_ref, wd_hbm, bd_ref, y_ref,
                       wf_ref, wbf_ref, sem, *, layer):
    i = pl.program_id(0)
    fl = fl_ref[i]
    weights = (wg_hbm, wu_hbm, wd_hbm)

    def weight_copy(j, expert):
        return pltpu.make_async_copy(weights[j].at[layer, expert], wf_ref.at[j], sem.at[j])

    @pl.when((fl & 1) != 0)
    def _():
        @pl.when((fl & 2) != 0)
        def _():
            @pl.when(i == 0)
            def _():
                for j in range(3):
                    weight_copy(j, te_ref[i]).start()

            for j in range(3):
                weight_copy(j, te_ref[i]).wait()
                wbf_ref[j] = wf_ref[j].astype(BF16)

            @pl.when(nx_ref[i] >= 0)
            def _():
                for j in range(3):
                    weight_copy(j, nx_ref[i]).start()

        def ffn(rows):
            x = x_ref[:rows, :D].astype(BF16)
            gate = x_ref[:rows, D:D + 1]
            g = jnp.minimum(_dot(x, wbf_ref[0]) + bg_ref[0], SWIGLU_LIMIT)
            u = jnp.clip(_dot(x, wbf_ref[1]) + bu_ref[0], -SWIGLU_LIMIT, SWIGLU_LIMIT)
            a = (u + 1.0) * (g * jax.nn.sigmoid(SWIGLU_ALPHA * g))
            y_ref[:rows, :] = gate * (_dot(a.astype(BF16), wbf_ref[2]) + bd_ref[0])

        pl.when((fl & 4) == 0)(functools.partial(ffn, EXPERT_TILE))
        pl.when((fl & 4) != 0)(functools.partial(ffn, EXPERT_TILE // 2))


def _expert_ffn(tabs, xs, w_gate, b_gate, w_up, b_up, w_down, b_down, layer):
    rows = xs.shape[0]
    hbm = pl.BlockSpec(memory_space=pl.ANY)
    bspec = pl.BlockSpec((1, 1, D), lambda i, te, *_: (te[i], 0, 0))
    return pl.pallas_call(
        functools.partial(_expert_ffn_kernel, layer=layer),
        grid_spec=pltpu.PrefetchScalarGridSpec(
            num_scalar_prefetch=4,
            grid=(rows // EXPERT_TILE,),
            in_specs=[pl.BlockSpec((EXPERT_TILE, SORTED_W), lambda i, te, tr, *_: (tr[i], 0)),
                      hbm, bspec, hbm, bspec, hbm, bspec],
            out_specs=pl.BlockSpec((EXPERT_TILE, D), lambda i, te, tr, *_: (tr[i], 0)),
            scratch_shapes=[pltpu.VMEM((3, D, D), F32), pltpu.VMEM((3, D, D), BF16), pltpu.SemaphoreType.DMA((3,))],
        ),
        out_shape=jax.ShapeDtypeStruct((rows, D), F32),
        compiler_params=_cp(("arbitrary",)),
        name="expert_ffn",
    )(tabs["tile_expert"], tabs["tile_row"], tabs["tile_flags"], tabs["tile_next"], xs,
      w_gate, b_gate[layer].reshape(N_EXP, 1, D), w_up, b_up[layer].reshape(N_EXP, 1, D),
      w_down, b_down[layer].reshape(N_EXP, 1, D))


def _combine_kernel(pc_ref, lo_ref, gs_ref, rows_ref, ys_ref, meta_ref, x1_ref, mod_ref, lg_ref, lb_ref,
                    oc_ref, ol_ref, buf_ref, sem, *, nct):
    b = pl.program_id(0)
    nb = pl.num_programs(0)
    s = buf_ref.shape[1]
    slot = b % 2

    def fetch(blk, to_slot):
        def run(e, carry):
            j = blk * N_EXP + e
            _run_copies(pc_ref[j], ys_ref, gs_ref[j], buf_ref.at[to_slot], lo_ref[j], sem.at[to_slot], wait=False)
            return carry
        lax.fori_loop(0, N_EXP, run, 0)

    @pl.when(b == 0)
    def _():
        buf_ref[...] = jnp.zeros_like(buf_ref)
        fetch(b, slot)

    @pl.when(b + 1 < nb)
    def _():
        fetch(b + 1, 1 - slot)

    _wait_rows(rows_ref[b], buf_ref.at[slot], ys_ref, sem.at[slot], to_vmem=True)

    meta = meta_ref[...]
    scol = lax.broadcasted_iota(jnp.int32, (1, s), 1).astype(F32)
    w = jnp.zeros((TOKEN_BLOCK, s), F32)
    for k in range(TOP_K):
        w = jnp.where(meta[:, k:k + 1] == scol, 1.0, w)
    w = w.astype(BF16)
    moe = _dot(w, buf_ref[slot].astype(BF16))
    m = mod_ref[0, 0]
    out = _layer_norm(DN_ALPHA * x1_ref[...] + m[5:6] * moe, lg_ref[...], lb_ref[...])

    @pl.when(b < nct)
    def _():
        oc_ref[...] = out

    @pl.when(b >= nct)
    def _():
        ol_ref[...] = out


def _combine_ln2(tabs, ys, meta, x1, mod, ln_g, ln_b, layer, t_ctx, dec_seq):
    T = x1.shape[0]
    nct = t_ctx // TOKEN_BLOCK
    row = lambda b, *_: (layer, _mod_row(b, TOKEN_BLOCK, t_ctx, dec_seq), 0, 0)
    tok = lambda w: pl.BlockSpec((TOKEN_BLOCK, w), lambda b, *_: (b, 0))
    const = lambda shape: pl.BlockSpec(shape, lambda b, *_: (0, 0))
    return pl.pallas_call(
        functools.partial(_combine_kernel, nct=nct),
        grid_spec=pltpu.PrefetchScalarGridSpec(
            num_scalar_prefetch=4,
            grid=(T // TOKEN_BLOCK,),
            in_specs=[pl.BlockSpec(memory_space=pl.ANY), tok(128), tok(D), pl.BlockSpec((1, 1, 6, D), row),
                      const((1, D)), const((1, D))],
            out_specs=list(_ctx_lat_specs((TOKEN_BLOCK, D), nct)),
            scratch_shapes=[pltpu.VMEM((2, LOCAL_SLOTS, D), F32), pltpu.SemaphoreType.DMA((2,))],
        ),
        out_shape=[jax.ShapeDtypeStruct((t_ctx, D), F32), jax.ShapeDtypeStruct((T - t_ctx, D), F32)],
        compiler_params=_cp(("arbitrary",)),
        name="moe_combine_ln2",
    )(tabs["pc"], tabs["lo"], tabs["gstart"], tabs["rows"], ys, meta, x1, mod, ln_g.reshape(1, D),
      ln_b.reshape(1, D))


def _moe_ln2(h2, meta, cnt, x1, mod, w_gate, b_gate, w_up, b_up, w_down, b_down, ln_g, ln_b, layer, t_ctx, dec_seq):
    tabs = _routing_tables(cnt[:, 0, :N_EXP].astype(jnp.int32))
    xs = _dispatch(tabs, h2, meta)
    ys = _expert_ffn(tabs, xs, w_gate, b_gate, w_up, b_up, w_down, b_down, layer)
    return _combine_ln2(tabs, ys, meta, x1, mod, ln_g, ln_b, layer, t_ctx, dec_seq)


def kernel(x_prompt, x_sample, cache_k, cache_v, c, c_ctx, w_mod, b_mod, w_in, w_out, attn_sink, hy_conv_w, hy_conv_b, hy_w1, hy_b1, hy_f1, hy_w2, hy_b2, hy_f2, hy_w3, hy_deltas, hy_bias, sc_conv_w, ln1_g, ln1_b, router_w, router_b, w_gate, b_gate, w_up, b_up, w_down, b_down, ln2_g, ln2_b):
    nb, seq, _ = x_prompt.shape
    db, dseq, _ = x_sample.shape
    past = cache_k.shape[2]
    t_ctx = nb * seq
    x = (x_prompt.reshape(t_ctx, D), x_sample.reshape(db * dseq, D))

    cond = jnp.zeros((8, D), F32).at[0].set(c_ctx).at[1:1 + db].set(c)
    mod = _modulation(cond, w_mod, b_mod)

    tables = _rope_tables(dseq)
    fft_c = _fft_constants(dseq)

    new_k, new_v = [], []
    for l in range(DEPTH):
        q, k, v, u_hy, u_sc = _in_projection(x[0], x[1], mod, w_in, l, dseq)
        new_k.append(k[:t_ctx].reshape(nb, seq, N_KV, HD))
        new_v.append(v[:t_ctx].reshape(nb, seq, N_KV, HD))
        oa_ctx = _context_attention(q, k, v, attn_sink[l], nb, seq)
        oa_lat = _latent_attention(q, k, v, cache_k[:, l].reshape(db, past, KV_W),
                                   cache_v[:, l].reshape(db, past, KV_W), attn_sink[l], tables, t_ctx, db, dseq)
        mlp = (hy_w1[l], hy_b1[l], hy_f1[l], hy_w2[l], hy_b2[l], hy_f2[l], hy_w3[l], hy_deltas[l])
        ohy_ctx, osc_ctx = _ctx_mixers(u_hy, u_sc, hy_conv_w[l], hy_conv_b[l], sc_conv_w[l], hy_bias[l], mlp,
                                       nb, seq)
        x0, g, gt, osc_lat = _mixer_prep(u_hy, u_sc, hy_conv_w[l], hy_conv_b[l], sc_conv_w[l], t_ctx, db, dseq)
        ohy_lat = _hyena_long_conv(x0, g, gt, _hyena_filters(dseq, mlp), hy_bias[l], fft_c, db, dseq)
        x1, h2, meta, cnt = _out_projection((oa_ctx, oa_lat), (ohy_ctx, ohy_lat), (osc_ctx, osc_lat), x, mod,
                                            w_out, ln1_g[l], ln1_b[l], router_w[l], router_b[l], l, dseq)
        x = _moe_ln2(h2, meta, cnt, x1, mod, w_gate, b_gate, w_up, b_up, w_down, b_down, ln2_g[l], ln2_b[l],
                     l, t_ctx, dseq)

    y_prompt = x[0].reshape(nb, seq, D)
    y_sample = x[1].reshape(db, dseq, D)
    return (y_prompt, y_sample, jnp.stack(new_k, axis=1), jnp.stack(new_v, axis=1))
```

```python
import functools
import math

import jax
import jax.numpy as jnp
from jax import lax
from jax.experimental import pallas as pl
from jax.experimental.pallas import tpu as pltpu

F32 = jnp.float32
BF16 = jnp.bfloat16
HI = lax.Precision.HIGHEST

D = 1024
DEPTH = 2
N_HEADS = 8
N_KV = 2
HD = 64
GQA = N_HEADS // N_KV
ATT_W = N_HEADS * HD
KV_W = N_KV * HD
WINDOW = 128
BLK = 128
GRID_W = 64
ROPE_BASE = 10000.0
HY_W = 256
SC_W = 256
FILTER_EMB = 33
FILTER_BANDS = 16
FILTER_HIDDEN = 64
IN_W = ATT_W + 2 * KV_W + 3 * HY_W + 3 * SC_W
N_EXP = 32
TOP_K = 4
SWIGLU_LIMIT = 7.0
SWIGLU_ALPHA = 1.702
DN_ALPHA = (2 * DEPTH) ** 0.25
LN_EPS = 1e-5
NEG_INF = float("-inf")

VMEM_LIMIT = 56 * 1024 * 1024

TOKEN_BLOCK = 256
ROW_PAD = 8
RUN_PIECES = (256, 128, 64, 32, 16, 8)
EXPERT_TILE = 512
LOCAL_SLOTS = TOP_K * TOKEN_BLOCK + N_EXP * ROW_PAD


def _cp(sem):
    return pltpu.CompilerParams(dimension_semantics=sem, vmem_limit_bytes=VMEM_LIMIT)


def _dot(a, b, precision=None):
    return jnp.dot(a, b, preferred_element_type=F32, precision=precision)


def _dot_nt(a, b):
    return lax.dot_general(a, b, (((1,), (1,)), ((), ())), preferred_element_type=F32)


def _layer_norm(z, g, b):
    mu = jnp.mean(z, axis=-1, keepdims=True)
    zc = z - mu
    var = jnp.mean(zc * zc, axis=-1, keepdims=True)
    return zc * lax.rsqrt(var + LN_EPS) * g + b


def _mod_kernel(cond_ref, w_ref, b_ref, o_ref):
    c = cond_ref[...]
    s = c * jax.nn.sigmoid(c)
    o_ref[0] = _dot(s, w_ref[0], HI) + b_ref[0]


def _modulation(cond, w_mod, b_mod):
    tn = 1024
    out = pl.pallas_call(
        _mod_kernel,
        grid=(DEPTH, 6 * D // tn),
        in_specs=[
            pl.BlockSpec((8, D), lambda l, j: (0, 0)),
            pl.BlockSpec((1, D, tn), lambda l, j: (l, 0, j)),
            pl.BlockSpec((1, 1, tn), lambda l, j: (l, 0, j)),
        ],
        out_specs=pl.BlockSpec((1, 8, tn), lambda l, j: (l, 0, j)),
        out_shape=jax.ShapeDtypeStruct((DEPTH, 8, 6 * D), F32),
        compiler_params=_cp(("arbitrary", "arbitrary")),
        name="modulation",
    )(cond, w_mod, b_mod.reshape(DEPTH, 1, 6 * D))
    return out.reshape(DEPTH, 8, 6, D)


def _mod_row(i, tm, t_ctx, dec_seq):
    nct = t_ctx // tm
    return jnp.where(i < nct, 0, 1 + (i - nct) // (dec_seq // tm))


def _ctx_lat_specs(shape, nct, ctx_offset=0, lat_offset=0):
    ctx = pl.BlockSpec(shape, lambda i, *_: (jnp.minimum(i, nct - 1) + ctx_offset, 0))
    lat = pl.BlockSpec(shape, lambda i, *_: (jnp.maximum(i - nct, 0) + lat_offset, 0))
    return ctx, lat


def _inproj_kernel(xc_ref, xl_ref, mod_ref, w_ref, q_ref, k_ref, v_ref, uh_ref, us_ref, wbf_ref, *, nct):
    @pl.when(pl.program_id(0) == 0)
    def _():
        wbf_ref[...] = w_ref[...].astype(BF16)

    m = mod_ref[0, 0]
    x = jnp.where(pl.program_id(0) < nct, xc_ref[...], xl_ref[...])
    h = x * (1.0 + m[1:2]) + m[0:1]
    y = _dot(h.astype(BF16), wbf_ref[...])
    o0 = ATT_W
    o1 = o0 + KV_W
    o2 = o1 + KV_W
    o3 = o2 + 3 * HY_W
    q_ref[...] = y[:, :o0].astype(q_ref.dtype)
    k_ref[...] = y[:, o0:o1]
    v_ref[...] = y[:, o1:o2]
    uh_ref[...] = y[:, o2:o3].astype(uh_ref.dtype)
    us_ref[...] = y[:, o3:].astype(us_ref.dtype)


def _in_projection(xc, xl, mod, w_in, layer, dec_seq):
    t_ctx = xc.shape[0]
    T = t_ctx + xl.shape[0]
    tm = 512
    row = lambda i: (layer, _mod_row(i, tm, t_ctx, dec_seq), 0, 0)
    tok = lambda w: pl.BlockSpec((tm, w), lambda i: (i, 0))
    widths = (ATT_W, KV_W, KV_W, 3 * HY_W, 3 * SC_W)
    return pl.pallas_call(
        functools.partial(_inproj_kernel, nct=t_ctx // tm),
        grid=(T // tm,),
        in_specs=[
            *_ctx_lat_specs((tm, D), t_ctx // tm),
            pl.BlockSpec((1, 1, 6, D), row),
            pl.BlockSpec((None, D, IN_W), lambda i: (layer, 0, 0)),
        ],
        out_specs=[tok(w) for w in widths],
        out_shape=[jax.ShapeDtypeStruct((T, w), dt) for w, dt in zip(widths, (BF16, F32, F32, BF16, BF16))],
        scratch_shapes=[pltpu.VMEM((D, IN_W), BF16)],
        compiler_params=_cp(("arbitrary",)),
        name="in_projection",
    )(xc, xl, mod, w_in)


def _swap_halves(x):
    w = x.shape[-1]
    lane = lax.broadcasted_iota(jnp.int32, x.shape, 1)
    return jnp.where((lane % HD) < HD // 2, pltpu.roll(x, w - HD // 2, 1), pltpu.roll(x, HD // 2, 1))


def _rope(x, cos, sin_signed):
    return x * cos + _swap_halves(x) * sin_signed


def _group_rows(q, g):
    return jnp.concatenate([q[:, (GQA * g + r) * HD:(GQA * g + r + 1) * HD] for r in range(GQA)], axis=0)


def _sink_column(sink_ref, g, rows):
    ridx = lax.broadcasted_iota(jnp.int32, (GQA * rows, 1), 0)
    col = jnp.full((GQA * rows, 1), sink_ref[GQA * g + GQA - 1], F32)
    for r in range(GQA - 2, -1, -1):
        col = jnp.where(ridx < (r + 1) * rows, sink_ref[GQA * g + r], col)
    return col


def _ungroup(outs, rows):
    return jnp.concatenate([o[r * rows:(r + 1) * rows] for o in outs for r in range(GQA)], axis=1)


def _ctx_attn_kernel(sink_ref, q_ref, k_ref, v_ref, o_ref):
    rows = q_ref.shape[0]
    q = q_ref[...].astype(F32) * (HD ** -0.5)
    k = k_ref[...]
    v = v_ref[...]
    outs = []
    for g in range(N_KV):
        qg = _group_rows(q, g).astype(BF16)
        kg = k[:, g * HD:(g + 1) * HD].astype(BF16)
        vg = v[:, g * HD:(g + 1) * HD].astype(BF16)
        s = _dot_nt(qg, kg)
        sink = _sink_column(sink_ref, g, rows)
        m = jnp.maximum(jnp.max(s, axis=-1, keepdims=True), sink)
        e = jnp.exp(s - m)
        den = jnp.sum(e, axis=-1, keepdims=True) + jnp.exp(sink - m)
        outs.append(_dot(e.astype(BF16), vg) / den)
    o_ref[...] = _ungroup(outs, rows).astype(o_ref.dtype)


def _context_attention(q, k, v, sink, B, L):
    tok = lambda w: pl.BlockSpec((L, w), lambda b: (b, 0))
    return pl.pallas_call(
        _ctx_attn_kernel,
        grid=(B,),
        in_specs=[pl.BlockSpec(memory_space=pltpu.SMEM), tok(ATT_W), tok(KV_W), tok(KV_W)],
        out_specs=tok(ATT_W),
        out_shape=jax.ShapeDtypeStruct((B * L, ATT_W), BF16),
        compiler_params=_cp(("arbitrary",)),
        name="context_attention",
    )(sink, q, k, v)


def _lat_attn_kernel(sink_ref, q_ref, k_ref, v_ref, kc_ref, vc_ref, ck_ref, sk_ref, o_ref,
                     kr_ref, vb_ref, kcb_ref, vcb_ref):
    n = pl.program_id(1)
    L = k_ref.shape[0]
    nk = 3 * BLK

    @pl.when(n == 0)
    def _():
        kr_ref[...] = _rope(k_ref[...], ck_ref[...], sk_ref[...]).astype(BF16)
        vb_ref[...] = v_ref[...].astype(BF16)
        kcb_ref[...] = kc_ref[...].astype(BF16)
        vcb_ref[...] = vc_ref[...].astype(BF16)

    ws = pl.multiple_of(jnp.clip((n - 1) * BLK, 0, L - nk), BLK)
    qs = pl.multiple_of(n * BLK, BLK)
    heads_per_table = N_HEADS // N_KV
    cq = jnp.tile(ck_ref[pl.ds(qs, BLK), :], (1, heads_per_table))
    sq = jnp.tile(sk_ref[pl.ds(qs, BLK), :], (1, heads_per_table))
    q = _rope(q_ref[...].astype(F32), cq, sq) * (HD ** -0.5)
    kl = kr_ref[pl.ds(ws, nk), :]
    vl = vb_ref[pl.ds(ws, nk), :]
    kc = kcb_ref[...]
    vc = vcb_ref[...]
    qpos = n * BLK + lax.broadcasted_iota(jnp.int32, (GQA * BLK, 1), 0) % BLK
    kpos = ws + lax.broadcasted_iota(jnp.int32, (1, nk), 1)
    valid = jnp.abs(kpos - qpos) <= WINDOW
    outs = []
    for g in range(N_KV):
        sl = slice(g * HD, (g + 1) * HD)
        qg = _group_rows(q, g).astype(BF16)
        s_loc = jnp.where(valid, _dot_nt(qg, kl[:, sl]), NEG_INF)
        s_ctx = _dot_nt(qg, kc[:, sl])
        sink = _sink_column(sink_ref, g, BLK)
        m = jnp.maximum(jnp.maximum(jnp.max(s_loc, axis=-1, keepdims=True),
                                    jnp.max(s_ctx, axis=-1, keepdims=True)), sink)
        e_loc = jnp.exp(s_loc - m)
        e_ctx = jnp.exp(s_ctx - m)
        den = (jnp.sum(e_loc, axis=-1, keepdims=True) + jnp.sum(e_ctx, axis=-1, keepdims=True)
               + jnp.exp(sink - m))
        o = _dot(e_loc.astype(BF16), vl[:, sl]) + _dot(e_ctx.astype(BF16), vc[:, sl])
        outs.append(o / den)
    o_ref[...] = _ungroup(outs, BLK).astype(o_ref.dtype)


def _rope_tables(L):
    rows = L // GRID_W
    row = jnp.repeat(jnp.arange(rows, dtype=F32), GRID_W)
    col = jnp.tile(jnp.arange(GRID_W, dtype=F32), rows)
    pairs = HD // 4
    inv = ROPE_BASE ** (-jnp.arange(pairs, dtype=F32) / pairs)
    ang = jnp.concatenate([row[:, None] * inv, col[:, None] * inv], axis=-1)
    cos = jnp.cos(ang)
    sin = jnp.sin(ang)
    return (jnp.tile(jnp.concatenate([cos, cos], axis=-1), (1, N_KV)),
            jnp.tile(jnp.concatenate([-sin, sin], axis=-1), (1, N_KV)))


def _latent_attention(q, k, v, kc, vc, sink, tables, t_ctx, B, L):
    assert t_ctx % L == 0
    P = kc.shape[1]
    ck, sk = tables
    nbk = L // BLK
    seq = pl.BlockSpec((L, KV_W), lambda b, n: (t_ctx // L + b, 0))
    ctx = pl.BlockSpec((None, P, KV_W), lambda b, n: (b, 0, 0))
    return pl.pallas_call(
        _lat_attn_kernel,
        grid=(B, nbk),
        in_specs=[
            pl.BlockSpec(memory_space=pltpu.SMEM),
            pl.BlockSpec((BLK, ATT_W), lambda b, n: (t_ctx // BLK + b * nbk + n, 0)),
            seq, seq, ctx, ctx,
            pl.BlockSpec((L, KV_W), lambda b, n: (0, 0)),
            pl.BlockSpec((L, KV_W), lambda b, n: (0, 0)),
        ],
        out_specs=pl.BlockSpec((BLK, ATT_W), lambda b, n: (b * nbk + n, 0)),
        out_shape=jax.ShapeDtypeStruct((B * L, ATT_W), BF16),
        scratch_shapes=[pltpu.VMEM((L, KV_W), BF16), pltpu.VMEM((L, KV_W), BF16),
                        pltpu.VMEM((P, KV_W), BF16), pltpu.VMEM((P, KV_W), BF16)],
        compiler_params=_cp(("arbitrary", "arbitrary")),
        name="latent_attention",
    )(sink, q, k, v, kc, vc, ck, sk)


def _conv3(u, prev_row, next_row, w):
    tl = u.shape[0]
    ridx = lax.broadcasted_iota(jnp.int32, (tl, 1), 0)
    dn = jnp.where(ridx == 0, prev_row, pltpu.roll(u, 1, 0))
    up = jnp.where(ridx == tl - 1, next_row, pltpu.roll(u, tl - 1, 0))
    return dn * w[0:1] + u * w[1:2] + up * w[2:3]


def _store_time_major_inner(dst_ref, src, scr_ref):
    n2 = dst_ref.shape[0]
    rows = src.shape[0] // n2
    for h in range(scr_ref.shape[0]):
        scr_ref[h] = src[:, h * 128:(h + 1) * 128]
    for j in range(n2):
        for h in range(scr_ref.shape[0]):
            dst_ref[j, :, h * 128:(h + 1) * 128] = scr_ref[h, pl.ds(j, rows, stride=n2), :]


def _mixprep_kernel(uh_ref, uhp_ref, uhn_ref, us_ref, usp_ref, usn_ref, hw_ref, hb_ref, sw_ref,
                    x0_ref, g_ref, gt_ref, osc_ref, scr_ref):
    t = pl.program_id(1)
    first = t == 0
    last = t == pl.num_programs(1) - 1
    c = HY_W
    hw = hw_ref[...]
    halo = uhp_ref.shape[0]
    f32 = lambda a: a.astype(F32)
    uh = _conv3(f32(uh_ref[...]), jnp.where(first, 0.0, f32(uhp_ref[halo - 1:halo])),
                jnp.where(last, 0.0, f32(uhn_ref[0:1])), hw)
    uh = uh + hb_ref[...]
    x0_ref[...] = uh[:, :c]
    g = uh[:, c:2 * c] * uh[:, 2 * c:]
    g_ref[...] = g
    _store_time_major_inner(gt_ref, g, scr_ref)
    us = f32(us_ref[...])
    usp = f32(usp_ref[halo - 1:halo])
    usn = f32(usn_ref[0:1])
    prod = us[:, c:2 * c] * us[:, 2 * c:]
    pprev = jnp.where(first, 0.0, usp[:, c:2 * c] * usp[:, 2 * c:])
    pnext = jnp.where(last, 0.0, usn[:, c:2 * c] * usn[:, 2 * c:])
    osc_ref[...] = (us[:, :c] * _conv3(prod, pprev, pnext, sw_ref[...])).astype(osc_ref.dtype)


def _mixer_prep(u_hy, u_sc, hy_conv_w, hy_conv_b, sc_conv_w, row0, B, L):
    W = u_hy.shape[1]
    n1, n2 = _fft_split(L)
    tl = 8 * n2
    assert row0 % tl == 0 and L % tl == 0
    nt = L // tl
    halo = 16
    rh = tl // halo
    base = lambda b: row0 // halo + b * (L // halo)
    main = pl.BlockSpec((tl, W), lambda b, t: (row0 // tl + b * nt + t, 0))
    prev = pl.BlockSpec((halo, W), lambda b, t: (base(b) + jnp.maximum(t * rh - 1, 0), 0))
    nxt = pl.BlockSpec((halo, W), lambda b, t: (base(b) + jnp.minimum((t + 1) * rh, L // halo - 1), 0))
    const = lambda shape: pl.BlockSpec(shape, lambda b, t: (0, 0))
    out = pl.BlockSpec((tl, HY_W), lambda b, t: (b * nt + t, 0))
    flat = jax.ShapeDtypeStruct((B * L, HY_W), F32)
    return pl.pallas_call(
        _mixprep_kernel,
        grid=(B, nt),
        in_specs=[main, prev, nxt, main, prev, nxt, const((3, W)), const((1, W)), const((3, SC_W))],
        out_specs=[out, out, pl.BlockSpec((None, n2, 8, HY_W), lambda b, t: (b, 0, t, 0)), out],
        out_shape=[flat, flat, jax.ShapeDtypeStruct((B, n2, n1 // 2, HY_W), F32),
                   jax.ShapeDtypeStruct((B * L, SC_W), BF16)],
        scratch_shapes=[pltpu.VMEM((HY_W // 128, tl, 128), F32)],
        compiler_params=_cp(("arbitrary", "arbitrary")),
        name="mixer_prep",
    )(u_hy, u_hy, u_hy, u_sc, u_sc, u_sc, hy_conv_w, hy_conv_b.reshape(1, W), sc_conv_w)


def _filter_mlp(z, w1_ref, b1_ref, f1_ref, w2_ref, b2_ref, f2_ref, w3_ref, dl_ref, row0):
    tl = z.shape[0]
    h = jnp.sin(f1_ref[...] * (_dot(z, w1_ref[...], HI) + b1_ref[...]))
    h = jnp.sin(f2_ref[...] * (_dot(h, w2_ref[...], HI) + b2_ref[...]))
    h = _dot(h, w3_ref[...], HI) * jnp.exp(-z[:, 0:1] * jnp.abs(dl_ref[...]))
    row = row0 + lax.broadcasted_iota(jnp.int32, (tl, 1), 0)
    return h[:, :HY_W], jnp.where(row == 0, 0.0, h[:, HY_W:])


def _filter_kernel(z_ref, w1_ref, b1_ref, f1_ref, w2_ref, b2_ref, f2_ref, w3_ref, dl_ref, o_ref, scr_ref):
    hf, hb = _filter_mlp(z_ref[...], w1_ref, b1_ref, f1_ref, w2_ref, b2_ref, f2_ref, w3_ref, dl_ref,
                         pl.program_id(0) * z_ref.shape[0])
    _store_time_major_inner(o_ref.at[0], hf, scr_ref)
    _store_time_major_inner(o_ref.at[1], hb, scr_ref)


def _filter_features(L):
    t = jnp.linspace(0.0, 1.0, L, dtype=F32)[:, None]
    w = 2.0 * math.pi * jnp.arange(L, dtype=F32)[:, None] / L
    bands = jnp.linspace(1e-4, FILTER_BANDS - 1, FILTER_BANDS, dtype=F32)[None, :]
    z = jnp.concatenate([t, jnp.cos(bands * w), -jnp.sin(bands * w)], axis=-1)
    return jnp.pad(z, ((0, 0), (0, 128 - FILTER_EMB)))


def _filter_mlp_operands(w1, b1, f1, w2, b2, f2, w3, deltas):
    H = FILTER_HIDDEN
    ops = (jnp.pad(w1, ((0, 128 - FILTER_EMB), (0, 0))), b1.reshape(1, H), f1.reshape(1, H), w2, b2.reshape(1, H),
           f2.reshape(1, H), w3, deltas.reshape(1, 2 * HY_W))
    specs = [pl.BlockSpec(o.shape, lambda *_: (0, 0)) for o in ops]
    return ops, specs


def _hyena_filters(L, mlp):
    n1, n2 = _fft_split(L)
    tl = 8 * n2
    ops, specs = _filter_mlp_operands(*mlp)
    return pl.pallas_call(
        _filter_kernel,
        grid=(L // tl,),
        in_specs=[pl.BlockSpec((tl, 128), lambda i: (i, 0)), *specs],
        out_specs=pl.BlockSpec((2, n2, 8, HY_W), lambda i: (0, 0, i, 0)),
        out_shape=jax.ShapeDtypeStruct((2, n2, n1 // 2, HY_W), F32),
        scratch_shapes=[pltpu.VMEM((HY_W // 128, tl, 128), F32)],
        compiler_params=_cp(("arbitrary",)),
        name="hyena_filters",
    )(_filter_features(L), *ops)


def _dense_dft_constants(L):
    n = 2 * L
    a = (jnp.arange(n, dtype=jnp.int32)[:, None] * jnp.arange(L, dtype=jnp.int32)[None, :]) % n
    th = (2.0 * math.pi / n) * a.astype(F32)
    fwd = jnp.concatenate([jnp.cos(th), -jnp.sin(th)], axis=0)
    inv = jnp.concatenate([jnp.cos(th).T, -jnp.sin(th).T], axis=1) * (1.0 / n)
    return fwd.astype(BF16), inv.astype(BF16)


def _ctx_mixers_kernel(uh_ref, us_ref, hw_ref, hb_ref, sw_ref, bias_ref, fwd_ref, inv_ref, z_ref,
                       w1_ref, b1_ref, f1_ref, w2_ref, b2_ref, f2_ref, w3_ref, dl_ref, ohy_ref, osc_ref, kf_ref):
    c = HY_W
    nf = kf_ref.shape[1]

    @pl.when(pl.program_id(0) == 0)
    def _():
        hf, hb = _filter_mlp(z_ref[...], w1_ref, b1_ref, f1_ref, w2_ref, b2_ref, f2_ref, w3_ref, dl_ref, 0)
        xf = _dot(fwd_ref[...], hf.astype(BF16))
        xb = _dot(fwd_ref[...], hb.astype(BF16))
        kf_ref[0] = xf[:nf] + xb[:nf]
        kf_ref[1] = xf[nf:] - xb[nf:]

    zero = jnp.zeros((1, 1), F32)
    uh = _conv3(uh_ref[...].astype(F32), zero, zero, hw_ref[...]) + hb_ref[...]
    x0, g = uh[:, :c], uh[:, c:2 * c] * uh[:, 2 * c:]
    s = _dot(fwd_ref[...], g.astype(BF16))
    sr, si = s[:nf], s[nf:]
    kr, ki = kf_ref[0], kf_ref[1]
    y = _dot(inv_ref[...], jnp.concatenate([sr * kr - si * ki, sr * ki + si * kr], axis=0).astype(BF16))
    ohy_ref[...] = (x0 * (y + g * bias_ref[...])).astype(ohy_ref.dtype)
    us = us_ref[...].astype(F32)
    osc_ref[...] = (us[:, :c] * _conv3(us[:, c:2 * c] * us[:, 2 * c:], zero, zero, sw_ref[...])).astype(osc_ref.dtype)


def _ctx_mixers(u_hy, u_sc, hy_conv_w, hy_conv_b, sc_conv_w, hy_bias, mlp, B, L):
    W = u_hy.shape[1]
    fwd, inv = _dense_dft_constants(L)
    ops, specs = _filter_mlp_operands(*mlp)
    const = lambda shape: pl.BlockSpec(shape, lambda b: (0, 0))
    seq = lambda w: pl.BlockSpec((L, w), lambda b: (b, 0))
    return pl.pallas_call(
        _ctx_mixers_kernel,
        grid=(B,),
        in_specs=[seq(W), seq(W), const((3, W)), const((1, W)), const((3, SC_W)), const((1, HY_W)),
                  const(fwd.shape), const(inv.shape), const((L, 128)), *specs],
        out_specs=[seq(HY_W), seq(SC_W)],
        out_shape=[jax.ShapeDtypeStruct((B * L, HY_W), BF16), jax.ShapeDtypeStruct((B * L, SC_W), BF16)],
        scratch_shapes=[pltpu.VMEM((2, 2 * L, HY_W), F32)],
        compiler_params=_cp(("arbitrary",)),
        name="context_mixers",
    )(u_hy, u_sc, hy_conv_w, hy_conv_b.reshape(1, W), sc_conv_w, hy_bias.reshape(1, HY_W), fwd, inv,
      _filter_features(L), *ops)


def _fft_split(L):
    n = 2 * L
    n1 = 128 if n >= 8192 else 32
    return n1, n // n1


def _fft_constants(L):
    n1, n2 = _fft_split(L)
    n = n1 * n2
    k1 = jnp.arange(n1, dtype=jnp.int32)
    a = (k1[:, None] * jnp.arange(n1 // 2, dtype=jnp.int32)[None, :]) % n1
    th = (2.0 * math.pi / n1) * a.astype(F32)
    f1 = jnp.concatenate([jnp.cos(th), -jnp.sin(th)], axis=0)
    f3 = jnp.concatenate([jnp.cos(th).T, -jnp.sin(th).T], axis=1) * (1.0 / n)
    k = k1[:, None, None] + n1 * jnp.arange(n2, dtype=jnp.int32)[None, :, None]
    ph = (k * jnp.arange(n2, dtype=jnp.int32)[None, None, :]) % n
    ph = (2.0 * math.pi / n) * ph.astype(F32)
    mr, mi = jnp.cos(ph), -jnp.sin(ph)
    mf = jnp.concatenate([jnp.concatenate([mr, -mi], axis=2), jnp.concatenate([mi, mr], axis=2)], axis=1)
    mrt, mit = jnp.swapaxes(mr, 1, 2), jnp.swapaxes(mi, 1, 2)
    mb = jnp.concatenate([jnp.concatenate([mrt, mit], axis=2), jnp.concatenate([-mit, mrt], axis=2)], axis=1)
    return tuple(c.astype(BF16) for c in (f1, f3, mf, mb))


FFT_GROUP = 8


def _dft_rows_kernel(f_ref, x_ref, o_ref):
    for i in range(FFT_GROUP):
        o_ref[:, i, :] = _dot(f_ref[...], x_ref[i].astype(BF16))


def _dft_rows(f1, xt):
    B, n2, kh, C = xt.shape
    m = f1.shape[0]
    return pl.pallas_call(
        _dft_rows_kernel,
        grid=(B, n2 // FFT_GROUP),
        in_specs=[pl.BlockSpec((m, kh), lambda b, j: (0, 0)),
                  pl.BlockSpec((None, FFT_GROUP, kh, C), lambda b, j: (b, j, 0, 0))],
        out_specs=pl.BlockSpec((None, m, FFT_GROUP, C), lambda b, j: (b, 0, j, 0)),
        out_shape=jax.ShapeDtypeStruct((B, m, n2, C), F32),
        compiler_params=_cp(("arbitrary", "arbitrary")),
        name="fft_rows",
    )(f1, xt)


def _fft_mid_kernel(a_ref, hf_ref, hb_ref, mf_ref, mb_ref, o_ref, kf_ref):
    n2 = a_ref.shape[2]

    @pl.when(pl.program_id(1) == 0)
    def _():
        for i in range(FFT_GROUP):
            xf = _dot(mf_ref[i], jnp.concatenate([hf_ref[0, i], hf_ref[1, i]], axis=0).astype(BF16))
            xb = _dot(mf_ref[i], jnp.concatenate([hb_ref[0, i], hb_ref[1, i]], axis=0).astype(BF16))
            kf_ref[i, 0] = xf[:n2] + xb[:n2]
            kf_ref[i, 1] = xf[n2:] - xb[n2:]

    for i in range(FFT_GROUP):
        x = _dot(mf_ref[i], jnp.concatenate([a_ref[0, i], a_ref[1, i]], axis=0).astype(BF16))
        xr, xi = x[:n2], x[n2:]
        kr, ki = kf_ref[i, 0], kf_ref[i, 1]
        y = jnp.concatenate([xr * kr - xi * ki, xr * ki + xi * kr], axis=0)
        o = _dot(mb_ref[i], y.astype(BF16))
        o_ref[:, 0, i, :] = o[:n2]
        o_ref[:, 1, i, :] = o[n2:]


def _fft_mid(a, ah, mf, mb):
    B, _, n1, n2, C = a.shape
    blk = (None, 2, FFT_GROUP, n2, C)
    return pl.pallas_call(
        _fft_mid_kernel,
        grid=(n1 // FFT_GROUP, B),
        in_specs=[
            pl.BlockSpec(blk, lambda k, b: (b, 0, k, 0, 0)),
            pl.BlockSpec(blk, lambda k, b: (0, 0, k, 0, 0)),
            pl.BlockSpec(blk, lambda k, b: (1, 0, k, 0, 0)),
            pl.BlockSpec((FFT_GROUP, 2 * n2, 2 * n2), lambda k, b: (k, 0, 0)),
            pl.BlockSpec((FFT_GROUP, 2 * n2, 2 * n2), lambda k, b: (k, 0, 0)),
        ],
        out_specs=pl.BlockSpec((None, n2, 2, FFT_GROUP, C), lambda k, b: (b, 0, 0, k, 0)),
        out_shape=jax.ShapeDtypeStruct((B, n2, 2, n1, C), F32),
        scratch_shapes=[pltpu.VMEM((FFT_GROUP, 2, n2, C), F32)],
        compiler_params=_cp(("arbitrary", "arbitrary")),
        name="fft_mid",
    )(a, ah, ah, mf, mb)


def _idft_rows_kernel(f_ref, b_ref, x0_ref, g_ref, bias_ref, o_ref):
    for i in range(FFT_GROUP):
        o_ref[:, i, :] = _dot(f_ref[...], b_ref[i].astype(BF16))
    o_ref[...] = x0_ref[...] * (o_ref[...] + g_ref[...] * bias_ref[...])


def _idft_rows_gate(f3, bo, x0, g, bias):
    B, n2, m2, C = bo.shape
    kh = f3.shape[0]
    tile = pl.BlockSpec((None, kh, FFT_GROUP, C), lambda b, j: (b, 0, j, 0))
    return pl.pallas_call(
        _idft_rows_kernel,
        grid=(B, n2 // FFT_GROUP),
        in_specs=[pl.BlockSpec((kh, m2), lambda b, j: (0, 0)),
                  pl.BlockSpec((None, FFT_GROUP, m2, C), lambda b, j: (b, j, 0, 0)),
                  tile, tile, pl.BlockSpec((1, C), lambda b, j: (0, 0))],
        out_specs=tile,
        out_shape=jax.ShapeDtypeStruct((B, kh, n2, C), F32),
        compiler_params=_cp(("arbitrary", "arbitrary")),
        name="ifft_rows_gate",
    )(f3, bo, x0, g, bias)


def _hyena_long_conv(x0, g, gt, filt_t, hy_bias, consts, B, L):
    C = HY_W
    n1, n2 = _fft_split(L)
    f1, f3, mf, mb = consts
    a = _dft_rows(f1, gt).reshape(B, 2, n1, n2, C)
    ah = _dft_rows(f1, filt_t).reshape(2, 2, n1, n2, C)
    bo = _fft_mid(a, ah, mf, mb).reshape(B, n2, 2 * n1, C)
    nat = lambda z: z.reshape(B, n1 // 2, n2, C)
    out = _idft_rows_gate(f3, bo, nat(x0), nat(g), hy_bias.reshape(1, C))
    return out.reshape(B * L, C)


def _outproj_kernel(oac_ref, oal_ref, ohc_ref, ohl_ref, osc_ref, osl_ref, xc_ref, xl_ref, mod_ref, w_ref,
                    lg_ref, lb_ref, rw_ref, rb_ref, x1_ref, h2_ref, meta_ref, cnt_ref, wbf_ref, *, nct):
    @pl.when(pl.program_id(0) == 0)
    def _():
        wbf_ref[...] = w_ref[...].astype(BF16)

    m = mod_ref[0, 0]
    w = wbf_ref[...]
    is_ctx = pl.program_id(0) < nct
    pick = lambda c_ref, l_ref, dt: jnp.where(is_ctx, c_ref[...].astype(dt), l_ref[...].astype(dt))
    o = (_dot(pick(oac_ref, oal_ref, BF16), w[:ATT_W])
         + _dot(pick(ohc_ref, ohl_ref, BF16), w[ATT_W:ATT_W + HY_W])
         + _dot(pick(osc_ref, osl_ref, BF16), w[ATT_W + HY_W:]))
    x1 = _layer_norm(DN_ALPHA * pick(xc_ref, xl_ref, F32) + m[2:3] * o, lg_ref[...], lb_ref[...])
    x1_ref[...] = x1
    h2 = x1 * (1.0 + m[4:5]) + m[3:4]
    h2_hi = h2.astype(BF16)
    h2_ref[...] = h2_hi
    h2_lo = (h2 - h2_hi.astype(F32)).astype(BF16)
    logits = _dot(h2_hi, rw_ref[0]) + _dot(h2_lo, rw_ref[0]) + _dot(h2_hi, rw_ref[1]) + rb_ref[...]
    for sb in range(h2.shape[0] // TOKEN_BLOCK):
        rows = slice(sb * TOKEN_BLOCK, (sb + 1) * TOKEN_BLOCK)
        meta, cnt = _route_block(logits[rows])
        meta_ref[rows, :] = meta
        cnt_ref[sb] = jnp.broadcast_to(cnt, (8, 128))


def _route_block(logits):
    lane = lax.broadcasted_iota(jnp.int32, logits.shape, 1)
    logits = jnp.where(lane < N_EXP, logits, NEG_INF)
    picks = []
    vals = []
    for _ in range(TOP_K):
        v = jnp.max(logits, axis=-1, keepdims=True)
        idx = jnp.min(jnp.where(logits == v, lane, 128), axis=-1, keepdims=True)
        hit = lane == idx
        picks.append(hit)
        vals.append(v)
        logits = jnp.where(hit, NEG_INF, logits)
    es = [jnp.exp(v - vals[0]) for v in vals]
    den = es[0] + es[1] + es[2] + es[3]
    tb = logits.shape[0]
    msel = jnp.zeros(logits.shape, F32)
    for hit in picks:
        msel = msel + jnp.where(hit, 1.0, 0.0)
    cnt = jnp.sum(msel, axis=0, keepdims=True)
    pc = jnp.floor((cnt + (ROW_PAD - 1.0)) * (1.0 / ROW_PAD)) * ROW_PAD
    upper = jnp.where(lax.broadcasted_iota(jnp.int32, (128, 128), 0)
                      < lax.broadcasted_iota(jnp.int32, (128, 128), 1), 1.0, 0.0)
    lo = _dot(jnp.broadcast_to(pc, (8, 128)), upper, HI)[0:1]
    lower = jnp.where(lax.broadcasted_iota(jnp.int32, (tb, tb), 1)
                      < lax.broadcasted_iota(jnp.int32, (tb, tb), 0), 1.0, 0.0).astype(BF16)
    pos = lo + _dot(lower, msel.astype(BF16))
    meta = jnp.zeros(logits.shape, F32)
    for kk, (hit, e) in enumerate(zip(picks, es)):
        slot = jnp.sum(jnp.where(hit, pos, 0.0), axis=-1, keepdims=True)
        meta = jnp.where(lane == kk, slot, meta)
        meta = jnp.where(lane == TOP_K + kk, e / den, meta)
    return meta, cnt


def _out_projection(o_attn, o_hy, o_sc, x, mod, w_out, ln_g, ln_b, router_w, router_b, layer, dec_seq):
    t_ctx = x[0].shape[0]
    T = t_ctx + x[1].shape[0]
    blocks_per_step = 2
    tm = blocks_per_step * TOKEN_BLOCK
    nct = t_ctx // tm
    row = lambda i: (layer, _mod_row(i, tm, t_ctx, dec_seq), 0, 0)
    tok = lambda w: pl.BlockSpec((tm, w), lambda i: (i, 0))
    const = lambda shape: pl.BlockSpec(shape, lambda i: (0, 0))
    rw = jnp.pad(router_w, ((0, 0), (0, 128 - N_EXP)))
    rw_hi = rw.astype(BF16)
    rw = jnp.stack([rw_hi, (rw - rw_hi.astype(F32)).astype(BF16)])
    rb = jnp.pad(router_b, (0, 128 - N_EXP)).reshape(1, 128)
    return pl.pallas_call(
        functools.partial(_outproj_kernel, nct=nct),
        grid=(T // tm,),
        in_specs=[*_ctx_lat_specs((tm, ATT_W), nct), *_ctx_lat_specs((tm, HY_W), nct),
                  *_ctx_lat_specs((tm, SC_W), nct), *_ctx_lat_specs((tm, D), nct),
                  pl.BlockSpec((1, 1, 6, D), row),
                  pl.BlockSpec((None, D, D), lambda i: (layer, 0, 0)), const((1, D)), const((1, D)),
                  pl.BlockSpec((2, D, 128), lambda i: (0, 0, 0)), const((1, 128))],
        out_specs=[tok(D), tok(D), tok(128), pl.BlockSpec((blocks_per_step, 8, 128), lambda i: (i, 0, 0))],
        out_shape=[jax.ShapeDtypeStruct((T, D), F32), jax.ShapeDtypeStruct((T, D), BF16),
                   jax.ShapeDtypeStruct((T, 128), F32), jax.ShapeDtypeStruct((T // TOKEN_BLOCK, 8, 128), F32)],
        scratch_shapes=[pltpu.VMEM((D, D), BF16)],
        compiler_params=_cp(("arbitrary",)),
        name="out_projection_router",
    )(*o_attn, *o_hy, *o_sc, *x, mod, w_out, ln_g.reshape(1, D), ln_b.reshape(1, D), rw, rb)


def _routing_tables(cnt):
    nb = cnt.shape[0]
    ex = jnp.arange(N_EXP, dtype=jnp.int32)
    before_e = (ex[:, None] < ex[None, :]).astype(jnp.int32)
    bl = jnp.arange(nb, dtype=jnp.int32)
    before_b = (bl[None, :] < bl[:, None]).astype(jnp.int32)
    pc = (cnt + ROW_PAD - 1) // ROW_PAD * ROW_PAD
    lo = jnp.sum(pc[:, :, None] * before_e[None], axis=1)
    tot = jnp.sum(pc, axis=0)
    tot_pad = (tot + EXPERT_TILE - 1) // EXPERT_TILE * EXPERT_TILE
    off = jnp.sum(tot_pad[:, None] * before_e, axis=0)
    gstart = off[None, :] + jnp.sum(before_b[:, :, None] * pc[None], axis=1)
    ntile = tot_pad // EXPERT_TILE
    cum = jnp.sum(ntile[:, None] * before_e, axis=0) + ntile
    n_tiles = _max_expert_tiles(nb)
    i = jnp.arange(n_tiles, dtype=jnp.int32)
    ic = jnp.maximum(jnp.minimum(i, cum[-1] - 1), 0)
    te = jnp.sum((cum[None, :] <= ic[:, None]).astype(jnp.int32), axis=1)
    owner = (te[:, None] == ex[None, :]).astype(jnp.int32)
    pick = lambda per_expert: jnp.sum(owner * per_expert[None, :], axis=1)
    first_tile = pick(cum - ntile)
    first = ic == first_tile
    used = jnp.clip(pick(tot) - (ic - first_tile) * EXPERT_TILE, 0, EXPERT_TILE)
    half = used <= EXPERT_TILE // 2
    flags = jnp.where(i < cum[-1], 1 + 2 * first.astype(jnp.int32) + 4 * half.astype(jnp.int32), 0)
    later = (ex[None, :] > ex[:, None]) & (ntile[None, :] > 0)
    next_e = jnp.min(jnp.where(later, ex[None, :], N_EXP), axis=1)
    next_e = jnp.where(next_e < N_EXP, next_e, -1)
    i32 = lambda a: a.astype(jnp.int32).reshape(-1)
    return dict(pc=i32(pc), lo=i32(lo), gstart=i32(gstart), tail_start=i32(off + tot), tail_len=i32(tot_pad - tot),
                rows=i32(jnp.sum(pc, axis=1)), tile_expert=i32(te), tile_row=ic, tile_flags=flags,
                tile_next=i32(pick(next_e)))


def _max_expert_tiles(nb):
    rows = TOP_K * nb * TOKEN_BLOCK + nb * N_EXP * (ROW_PAD - 1) + N_EXP * (EXPERT_TILE - 1)
    return rows // EXPERT_TILE + 1


def _run_copies(n, src_ref, src0, dst_ref, dst0, sem, wait):
    done = jnp.int32(0)
    for p in RUN_PIECES:
        take = (n & p) != 0

        @pl.when(take)
        def _():
            cp = pltpu.make_async_copy(src_ref.at[pl.ds(pl.multiple_of(src0 + done, ROW_PAD), p), :],
                                       dst_ref.at[pl.ds(pl.multiple_of(dst0 + done, ROW_PAD), p), :], sem)
            if wait:
                cp.wait()
            else:
                cp.start()

        done = done + jnp.where(take, p, 0)


def _wait_rows(n, vmem_ref, hbm_ref, sem, to_vmem):
    base = TOP_K * TOKEN_BLOCK

    def wait(p):
        src, dst = (hbm_ref, vmem_ref) if to_vmem else (vmem_ref, hbm_ref)
        pltpu.make_async_copy(src.at[pl.ds(0, p), :], dst.at[pl.ds(0, p), :], sem).wait()

    wait(base)
    for p in RUN_PIECES[1:]:
        pl.when(((n - base) & p) != 0)(functools.partial(wait, p))


SORTED_W = D + 128


def _dispatch_kernel(pc_ref, lo_ref, gs_ref, ts_ref, tl_ref, rows_ref, h_ref, meta_ref, xs_ref,
                     buf_ref, zero_ref, sem, zsem):
    b = pl.program_id(0)
    nb = pl.num_programs(0)
    s = buf_ref.shape[1]
    slot = b % 2

    @pl.when(b == 0)
    def _():
        zero_ref[...] = jnp.zeros_like(zero_ref)
        for wait in (False, True):
            def tail(e, carry):
                _run_copies(tl_ref[e], zero_ref, 0, xs_ref, ts_ref[e], zsem, wait)
                return carry
            lax.fori_loop(0, N_EXP, tail, 0)

    mt = meta_ref[...].T
    srow = lax.broadcasted_iota(jnp.int32, (s, 1), 0).astype(F32)
    perm = jnp.zeros((s, TOKEN_BLOCK), F32)
    gperm = jnp.zeros((s, TOKEN_BLOCK), F32)
    for k in range(TOP_K):
        hit = srow == mt[k:k + 1]
        perm = jnp.where(hit, 1.0, perm)
        gperm = jnp.where(hit, mt[TOP_K + k:TOP_K + k + 1], gperm)
    sorted_rows = _dot(perm.astype(BF16), h_ref[...])
    gate_col = jnp.sum(gperm, axis=-1, keepdims=True)

    @pl.when(b >= 2)
    def _():
        _wait_rows(rows_ref[b - 2], buf_ref.at[slot], xs_ref, sem.at[slot], to_vmem=False)

    buf = buf_ref.at[slot]
    buf[:, :D] = sorted_rows
    buf[:, D:] = jnp.broadcast_to(gate_col, (s, 128))

    def run(e, carry):
        j = b * N_EXP + e
        _run_copies(pc_ref[j], buf, lo_ref[j], xs_ref, gs_ref[j], sem.at[slot], wait=False)
        return carry
    lax.fori_loop(0, N_EXP, run, 0)

    @pl.when(b == nb - 1)
    def _():
        @pl.when(b >= 1)
        def _():
            _wait_rows(rows_ref[b - 1], buf_ref.at[1 - slot], xs_ref, sem.at[1 - slot], to_vmem=False)
        _wait_rows(rows_ref[b], buf, xs_ref, sem.at[slot], to_vmem=False)


def _dispatch(tabs, h2, meta):
    T = h2.shape[0]
    nb = T // TOKEN_BLOCK
    rows = _max_expert_tiles(nb) * EXPERT_TILE
    tok = lambda w: pl.BlockSpec((TOKEN_BLOCK, w), lambda b, *_: (b, 0))
    return pl.pallas_call(
        _dispatch_kernel,
        grid_spec=pltpu.PrefetchScalarGridSpec(
            num_scalar_prefetch=6,
            grid=(nb,),
            in_specs=[tok(D), tok(128)],
            out_specs=pl.BlockSpec(memory_space=pl.ANY),
            scratch_shapes=[pltpu.VMEM((2, LOCAL_SLOTS, SORTED_W), F32), pltpu.VMEM((EXPERT_TILE, SORTED_W), F32),
                            pltpu.SemaphoreType.DMA((2,)), pltpu.SemaphoreType.DMA(())],
        ),
        out_shape=jax.ShapeDtypeStruct((rows, SORTED_W), F32),
        compiler_params=_cp(("arbitrary",)),
        name="moe_dispatch",
    )(tabs["pc"], tabs["lo"], tabs["gstart"], tabs["tail_start"], tabs["tail_len"], tabs["rows"], h2, meta)


def _expert_ffn_kernel(te_ref, tr_ref, fl_ref, nx_ref, x_ref, wg_hbm, bg_ref, wu_hbm, bu_ref, wd_hbm, bd_ref, y_ref,
                       wf_ref, wbf_ref, sem, *, layer):
    i = pl.program_id(0)
    fl = fl_ref[i]
    weights = (wg_hbm, wu_hbm, wd_hbm)

    def weight_copy(j, expert):
        return pltpu.make_async_copy(weights[j].at[layer, expert], wf_ref.at[j], sem.at[j])

    @pl.when((fl & 1) != 0)
    def _():
        @pl.when((fl & 2) != 0)
        def _():
            @pl.when(i == 0)
            def _():
                for j in range(3):
                    weight_copy(j, te_ref[i]).start()

            for j in range(3):
                weight_copy(j, te_ref[i]).wait()
                wbf_ref[j] = wf_ref[j].astype(BF16)

            @pl.when(nx_ref[i] >= 0)
            def _():
                for j in range(3):
                    weight_copy(j, nx_ref[i]).start()

        def ffn(rows):
            x = x_ref[:rows, :D].astype(BF16)
            gate = x_ref[:rows, D:D + 1]
            g = jnp.minimum(_dot(x, wbf_ref[0]) + bg_ref[0], SWIGLU_LIMIT)
            u = jnp.clip(_dot(x, wbf_ref[1]) + bu_ref[0], -SWIGLU_LIMIT, SWIGLU_LIMIT)
            a = (u + 1.0) * (g * jax.nn.sigmoid(SWIGLU_ALPHA * g))
            y_ref[:rows, :] = gate * (_dot(a.astype(BF16), wbf_ref[2]) + bd_ref[0])

        pl.when((fl & 4) == 0)(functools.partial(ffn, EXPERT_TILE))
        pl.when((fl & 4) != 0)(functools.partial(ffn, EXPERT_TILE // 2))


def _expert_ffn(tabs, xs, w_gate, b_gate, w_up, b_up, w_down, b_down, layer):
    rows = xs.shape[0]
    hbm = pl.BlockSpec(memory_space=pl.ANY)
    bspec = pl.BlockSpec((1, 1, D), lambda i, te, *_: (te[i], 0, 0))
    return pl.pallas_call(
        functools.partial(_expert_ffn_kernel, layer=layer),
        grid_spec=pltpu.PrefetchScalarGridSpec(
            num_scalar_prefetch=4,
            grid=(rows // EXPERT_TILE,),
            in_specs=[pl.BlockSpec((EXPERT_TILE, SORTED_W), lambda i, te, tr, *_: (tr[i], 0)),
                      hbm, bspec, hbm, bspec, hbm, bspec],
            out_specs=pl.BlockSpec((EXPERT_TILE, D), lambda i, te, tr, *_: (tr[i], 0)),
            scratch_shapes=[pltpu.VMEM((3, D, D), F32), pltpu.VMEM((3, D, D), BF16), pltpu.SemaphoreType.DMA((3,))],
        ),
        out_shape=jax.ShapeDtypeStruct((rows, D), F32),
        compiler_params=_cp(("arbitrary",)),
        name="expert_ffn",
    )(tabs["tile_expert"], tabs["tile_row"], tabs["tile_flags"], tabs["tile_next"], xs,
      w_gate, b_gate[layer].reshape(N_EXP, 1, D), w_up, b_up[layer].reshape(N_EXP, 1, D),
      w_down, b_down[layer].reshape(N_EXP, 1, D))


def _combine_kernel(pc_ref, lo_ref, gs_ref, rows_ref, ys_ref, meta_ref, x1_ref, mod_ref, lg_ref, lb_ref,
                    oc_ref, ol_ref, buf_ref, sem, *, nct):
    b = pl.program_id(0)
    nb = pl.num_programs(0)
    s = buf_ref.shape[1]
    slot = b % 2

    def fetch(blk, to_slot):
        def run(e, carry):
            j = blk * N_EXP + e
            _run_copies(pc_ref[j], ys_ref, gs_ref[j], buf_ref.at[to_slot], lo_ref[j], sem.at[to_slot], wait=False)
            return carry
        lax.fori_loop(0, N_EXP, run, 0)

    @pl.when(b == 0)
    def _():
        buf_ref[...] = jnp.zeros_like(buf_ref)
        fetch(b, slot)

    @pl.when(b + 1 < nb)
    def _():
        fetch(b + 1, 1 - slot)

    _wait_rows(rows_ref[b], buf_ref.at[slot], ys_ref, sem.at[slot], to_vmem=True)

    meta = meta_ref[...]
    scol = lax.broadcasted_iota(jnp.int32, (1, s), 1).astype(F32)
    w = jnp.zeros((TOKEN_BLOCK, s), F32)
    for k in range(TOP_K):
        w = jnp.where(meta[:, k:k + 1] == scol, 1.0, w)
    w = w.astype(BF16)
    moe = _dot(w, buf_ref[slot].astype(BF16))
    m = mod_ref[0, 0]
    out = _layer_norm(DN_ALPHA * x1_ref[...] + m[5:6] * moe, lg_ref[...], lb_ref[...])

    @pl.when(b < nct)
    def _():
        oc_ref[...] = out

    @pl.when(b >= nct)
    def _():
        ol_ref[...] = out


def _combine_ln2(tabs, ys, meta, x1, mod, ln_g, ln_b, layer, t_ctx, dec_seq):
    T = x1.shape[0]
    nct = t_ctx // TOKEN_BLOCK
    row = lambda b, *_: (layer, _mod_row(b, TOKEN_BLOCK, t_ctx, dec_seq), 0, 0)
    tok = lambda w: pl.BlockSpec((TOKEN_BLOCK, w), lambda b, *_: (b, 0))
    const = lambda shape: pl.BlockSpec(shape, lambda b, *_: (0, 0))
    return pl.pallas_call(
        functools.partial(_combine_kernel, nct=nct),
        grid_spec=pltpu.PrefetchScalarGridSpec(
            num_scalar_prefetch=4,
            grid=(T // TOKEN_BLOCK,),
            in_specs=[pl.BlockSpec(memory_space=pl.ANY), tok(128), tok(D), pl.BlockSpec((1, 1, 6, D), row),
                      const((1, D)), const((1, D))],
            out_specs=list(_ctx_lat_specs((TOKEN_BLOCK, D), nct)),
            scratch_shapes=[pltpu.VMEM((2, LOCAL_SLOTS, D), F32), pltpu.SemaphoreType.DMA((2,))],
        ),
        out_shape=[jax.ShapeDtypeStruct((t_ctx, D), F32), jax.ShapeDtypeStruct((T - t_ctx, D), F32)],
        compiler_params=_cp(("arbitrary",)),
        name="moe_combine_ln2",
    )(tabs["pc"], tabs["lo"], tabs["gstart"], tabs["rows"], ys, meta, x1, mod, ln_g.reshape(1, D),
      ln_b.reshape(1, D))


def _moe_ln2(h2, meta, cnt, x1, mod, w_gate, b_gate, w_up, b_up, w_down, b_down, ln_g, ln_b, layer, t_ctx, dec_seq):
    tabs = _routing_tables(cnt[:, 0, :N_EXP].astype(jnp.int32))
    xs = _dispatch(tabs, h2, meta)
    ys = _expert_ffn(tabs, xs, w_gate, b_gate, w_up, b_up, w_down, b_down, layer)
    return _combine_ln2(tabs, ys, meta, x1, mod, ln_g, ln_b, layer, t_ctx, dec_seq)


def kernel(x_prompt, x_sample, cache_k, cache_v, c, c_ctx, w_mod, b_mod, w_in, w_out, attn_sink, hy_conv_w, hy_conv_b, hy_w1, hy_b1, hy_f1, hy_w2, hy_b2, hy_f2, hy_w3, hy_deltas, hy_bias, sc_conv_w, ln1_g, ln1_b, router_w, router_b, w_gate, b_gate, w_up, b_up, w_down, b_down, ln2_g, ln2_b):
    nb, seq, _ = x_prompt.shape
    db, dseq, _ = x_sample.shape
    past = cache_k.shape[2]
    t_ctx = nb * seq
    x = (x_prompt.reshape(t_ctx, D), x_sample.reshape(db * dseq, D))

    cond = jnp.zeros((8, D), F32).at[0].set(c_ctx).at[1:1 + db].set(c)
    mod = _modulation(cond, w_mod, b_mod)

    tables = _rope_tables(dseq)
    fft_c = _fft_constants(dseq)

    new_k, new_v = [], []
    for l in range(DEPTH):
        q, k, v, u_hy, u_sc = _in_projection(x[0], x[1], mod, w_in, l, dseq)
        new_k.append(k[:t_ctx].reshape(nb, seq, N_KV, HD))
        new_v.append(v[:t_ctx].reshape(nb, seq, N_KV, HD))
        oa_ctx = _context_attention(q, k, v, attn_sink[l], nb, seq)
        oa_lat = _latent_attention(q, k, v, cache_k[:, l].reshape(db, past, KV_W),
                                   cache_v[:, l].reshape(db, past, KV_W), attn_sink[l], tables, t_ctx, db, dseq)
        mlp = (hy_w1[l], hy_b1[l], hy_f1[l], hy_w2[l], hy_b2[l], hy_f2[l], hy_w3[l], hy_deltas[l])
        ohy_ctx, osc_ctx = _ctx_mixers(u_hy, u_sc, hy_conv_w[l], hy_conv_b[l], sc_conv_w[l], hy_bias[l], mlp,
                                       nb, seq)
        x0, g, gt, osc_lat = _mixer_prep(u_hy, u_sc, hy_conv_w[l], hy_conv_b[l], sc_conv_w[l], t_ctx, db, dseq)
        ohy_lat = _hyena_long_conv(x0, g, gt, _hyena_filters(dseq, mlp), hy_bias[l], fft_c, db, dseq)
        x1, h2, meta, cnt = _out_projection((oa_ctx, oa_lat), (ohy_ctx, ohy_lat), (osc_ctx, osc_lat), x, mod,
                                            w_out, ln1_g[l], ln1_b[l], router_w[l], router_b[l], l, dseq)
        x = _moe_ln2(h2, meta, cnt, x1, mod, w_gate, b_gate, w_up, b_up, w_down, b_down, ln2_g[l], ln2_b[l],
                     l, t_ctx, dseq)

    y_prompt = x[0].reshape(nb, seq, D)
    y_sample = x[1].reshape(db, dseq, D)
    return (y_prompt, y_sample, jnp.stack(new_k, axis=1), jnp.stack(new_v, axis=1))
```

```python
import functools
import math

import jax
import jax.numpy as jnp
from jax import lax
from jax.experimental import pallas as pl
from jax.experimental.pallas import tpu as pltpu

F32 = jnp.float32
BF16 = jnp.bfloat16
HI = lax.Precision.HIGHEST

D = 1024
DEPTH = 2
N_HEADS = 8
N_KV = 2
HD = 64
GQA = N_HEADS // N_KV
ATT_W = N_HEADS * HD
KV_W = N_KV * HD
WINDOW = 128
BLK = 128
GRID_W = 64
ROPE_BASE = 10000.0
HY_W = 256
SC_W = 256
FILTER_EMB = 33
FILTER_BANDS = 16
FILTER_HIDDEN = 64
IN_W = ATT_W + 2 * KV_W + 3 * HY_W + 3 * SC_W
N_EXP = 32
TOP_K = 4
SWIGLU_LIMIT = 7.0
SWIGLU_ALPHA = 1.702
DN_ALPHA = (2 * DEPTH) ** 0.25
LN_EPS = 1e-5
NEG_INF = float("-inf")

VMEM_LIMIT = 56 * 1024 * 1024

TOKEN_BLOCK = 256
ROW_PAD = 8
RUN_PIECES = (256, 128, 64, 32, 16, 8)
EXPERT_TILE = 512
TILE_PARTS = 4
LOCAL_SLOTS = TOP_K * TOKEN_BLOCK + N_EXP * ROW_PAD


def _cp(sem):
    return pltpu.CompilerParams(dimension_semantics=sem, vmem_limit_bytes=VMEM_LIMIT)


def _dot(a, b, precision=None):
    return jnp.dot(a, b, preferred_element_type=F32, precision=precision)


def _dot_nt(a, b):
    return lax.dot_general(a, b, (((1,), (1,)), ((), ())), preferred_element_type=F32)


def _layer_norm(z, g, b):
    mu = jnp.mean(z, axis=-1, keepdims=True)
    zc = z - mu
    var = jnp.mean(zc * zc, axis=-1, keepdims=True)
    return zc * lax.rsqrt(var + LN_EPS) * g + b


def _mod_kernel(cond_ref, w_ref, b_ref, o_ref):
    c = cond_ref[...]
    s = c * jax.nn.sigmoid(c)
    o_ref[0] = _dot(s, w_ref[0], HI) + b_ref[0]


def _modulation(cond, w_mod, b_mod):
    tn = 1024
    out = pl.pallas_call(
        _mod_kernel,
        grid=(DEPTH, 6 * D // tn),
        in_specs=[
            pl.BlockSpec((8, D), lambda l, j: (0, 0)),
            pl.BlockSpec((1, D, tn), lambda l, j: (l, 0, j)),
            pl.BlockSpec((1, 1, tn), lambda l, j: (l, 0, j)),
        ],
        out_specs=pl.BlockSpec((1, 8, tn), lambda l, j: (l, 0, j)),
        out_shape=jax.ShapeDtypeStruct((DEPTH, 8, 6 * D), F32),
        compiler_params=_cp(("arbitrary", "arbitrary")),
        name="modulation",
    )(cond, w_mod, b_mod.reshape(DEPTH, 1, 6 * D))
    return out.reshape(DEPTH, 8, 6, D)


def _mod_row(i, tm, t_ctx, dec_seq):
    nct = t_ctx // tm
    return jnp.where(i < nct, 0, 1 + (i - nct) // (dec_seq // tm))


def _ctx_lat_specs(shape, nct, ctx_offset=0, lat_offset=0):
    ctx = pl.BlockSpec(shape, lambda i, *_: (jnp.minimum(i, nct - 1) + ctx_offset, 0))
    lat = pl.BlockSpec(shape, lambda i, *_: (jnp.maximum(i - nct, 0) + lat_offset, 0))
    return ctx, lat


def _inproj_kernel(xc_ref, xl_ref, mod_ref, w_ref, q_ref, k_ref, v_ref, uh_ref, us_ref, wbf_ref, *, nct):
    @pl.when(pl.program_id(0) == 0)
    def _():
        wbf_ref[...] = w_ref[...].astype(BF16)

    m = mod_ref[0, 0]
    x = jnp.where(pl.program_id(0) < nct, xc_ref[...], xl_ref[...])
    h = x * (1.0 + m[1:2]) + m[0:1]
    y = _dot(h.astype(BF16), wbf_ref[...])
    o0 = ATT_W
    o1 = o0 + KV_W
    o2 = o1 + KV_W
    o3 = o2 + 3 * HY_W
    q_ref[...] = y[:, :o0].astype(q_ref.dtype)
    k_ref[...] = y[:, o0:o1]
    v_ref[...] = y[:, o1:o2]
    uh_ref[...] = y[:, o2:o3].astype(uh_ref.dtype)
    us_ref[...] = y[:, o3:].astype(us_ref.dtype)


def _in_projection(xc, xl, mod, w_in, layer, dec_seq):
    t_ctx = xc.shape[0]
    T = t_ctx + xl.shape[0]
    tm = 512
    row = lambda i: (layer, _mod_row(i, tm, t_ctx, dec_seq), 0, 0)
    tok = lambda w: pl.BlockSpec((tm, w), lambda i: (i, 0))
    widths = (ATT_W, KV_W, KV_W, 3 * HY_W, 3 * SC_W)
    return pl.pallas_call(
        functools.partial(_inproj_kernel, nct=t_ctx // tm),
        grid=(T // tm,),
        in_specs=[
            *_ctx_lat_specs((tm, D), t_ctx // tm),
            pl.BlockSpec((1, 1, 6, D), row),
            pl.BlockSpec((None, D, IN_W), lambda i: (layer, 0, 0)),
        ],
        out_specs=[tok(w) for w in widths],
        out_shape=[jax.ShapeDtypeStruct((T, w), dt) for w, dt in zip(widths, (BF16, F32, F32, BF16, BF16))],
        scratch_shapes=[pltpu.VMEM((D, IN_W), BF16)],
        compiler_params=_cp(("arbitrary",)),
        name="in_projection",
    )(xc, xl, mod, w_in)


def _swap_halves(x):
    w = x.shape[-1]
    lane = lax.broadcasted_iota(jnp.int32, x.shape, 1)
    return jnp.where((lane % HD) < HD // 2, pltpu.roll(x, w - HD // 2, 1), pltpu.roll(x, HD // 2, 1))


def _rope(x, cos, sin_signed):
    return x * cos + _swap_halves(x) * sin_signed


def _group_rows(q, g):
    return jnp.concatenate([q[:, (GQA * g + r) * HD:(GQA * g + r + 1) * HD] for r in range(GQA)], axis=0)


def _sink_column(sink_ref, g, rows):
    ridx = lax.broadcasted_iota(jnp.int32, (GQA * rows, 1), 0)
    col = jnp.full((GQA * rows, 1), sink_ref[GQA * g + GQA - 1], F32)
    for r in range(GQA - 2, -1, -1):
        col = jnp.where(ridx < (r + 1) * rows, sink_ref[GQA * g + r], col)
    return col


def _ungroup(outs, rows):
    return jnp.concatenate([o[r * rows:(r + 1) * rows] for o in outs for r in range(GQA)], axis=1)


def _ctx_attn_kernel(sink_ref, q_ref, k_ref, v_ref, o_ref):
    rows = q_ref.shape[0]
    q = q_ref[...].astype(F32) * (HD ** -0.5)
    k = k_ref[...]
    v = v_ref[...]
    outs = []
    for g in range(N_KV):
        qg = _group_rows(q, g).astype(BF16)
        kg = k[:, g * HD:(g + 1) * HD].astype(BF16)
        vg = v[:, g * HD:(g + 1) * HD].astype(BF16)
        s = _dot_nt(qg, kg)
        sink = _sink_column(sink_ref, g, rows)
        m = jnp.maximum(jnp.max(s, axis=-1, keepdims=True), sink)
        e = jnp.exp(s - m)
        den = jnp.sum(e, axis=-1, keepdims=True) + jnp.exp(sink - m)
        outs.append(_dot(e.astype(BF16), vg) / den)
    o_ref[...] = _ungroup(outs, rows).astype(o_ref.dtype)


def _context_attention(q, k, v, sink, B, L):
    tok = lambda w: pl.BlockSpec((L, w), lambda b: (b, 0))
    return pl.pallas_call(
        _ctx_attn_kernel,
        grid=(B,),
        in_specs=[pl.BlockSpec(memory_space=pltpu.SMEM), tok(ATT_W), tok(KV_W), tok(KV_W)],
        out_specs=tok(ATT_W),
        out_shape=jax.ShapeDtypeStruct((B * L, ATT_W), BF16),
        compiler_params=_cp(("arbitrary",)),
        name="context_attention",
    )(sink, q, k, v)


def _lat_attn_kernel(sink_ref, q_ref, k_ref, v_ref, kc_ref, vc_ref, ck_ref, sk_ref, o_ref,
                     kr_ref, vb_ref, kcb_ref, vcb_ref):
    n = pl.program_id(1)
    L = k_ref.shape[0]
    nk = 3 * BLK

    @pl.when(n == 0)
    def _():
        kr_ref[...] = _rope(k_ref[...], ck_ref[...], sk_ref[...]).astype(BF16)
        vb_ref[...] = v_ref[...].astype(BF16)
        kcb_ref[...] = kc_ref[...].astype(BF16)
        vcb_ref[...] = vc_ref[...].astype(BF16)

    ws = pl.multiple_of(jnp.clip((n - 1) * BLK, 0, L - nk), BLK)
    qs = pl.multiple_of(n * BLK, BLK)
    heads_per_table = N_HEADS // N_KV
    cq = jnp.tile(ck_ref[pl.ds(qs, BLK), :], (1, heads_per_table))
    sq = jnp.tile(sk_ref[pl.ds(qs, BLK), :], (1, heads_per_table))
    q = _rope(q_ref[...].astype(F32), cq, sq) * (HD ** -0.5)
    kl = kr_ref[pl.ds(ws, nk), :]
    vl = vb_ref[pl.ds(ws, nk), :]
    kc = kcb_ref[...]
    vc = vcb_ref[...]
    qpos = n * BLK + lax.broadcasted_iota(jnp.int32, (GQA * BLK, 1), 0) % BLK
    kpos = ws + lax.broadcasted_iota(jnp.int32, (1, nk), 1)
    valid = jnp.abs(kpos - qpos) <= WINDOW
    outs = []
    for g in range(N_KV):
        sl = slice(g * HD, (g + 1) * HD)
        qg = _group_rows(q, g).astype(BF16)
        s_loc = jnp.where(valid, _dot_nt(qg, kl[:, sl]), NEG_INF)
        s_ctx = _dot_nt(qg, kc[:, sl])
        sink = _sink_column(sink_ref, g, BLK)
        m = jnp.maximum(jnp.maximum(jnp.max(s_loc, axis=-1, keepdims=True),
                                    jnp.max(s_ctx, axis=-1, keepdims=True)), sink)
        e_loc = jnp.exp(s_loc - m)
        e_ctx = jnp.exp(s_ctx - m)
        den = (jnp.sum(e_loc, axis=-1, keepdims=True) + jnp.sum(e_ctx, axis=-1, keepdims=True)
               + jnp.exp(sink - m))
        o = _dot(e_loc.astype(BF16), vl[:, sl]) + _dot(e_ctx.astype(BF16), vc[:, sl])
        outs.append(o / den)
    o_ref[...] = _ungroup(outs, BLK).astype(o_ref.dtype)


def _rope_tables(L):
    rows = L // GRID_W
    row = jnp.repeat(jnp.arange(rows, dtype=F32), GRID_W)
    col = jnp.tile(jnp.arange(GRID_W, dtype=F32), rows)
    pairs = HD // 4
    inv = ROPE_BASE ** (-jnp.arange(pairs, dtype=F32) / pairs)
    ang = jnp.concatenate([row[:, None] * inv, col[:, None] * inv], axis=-1)
    cos = jnp.cos(ang)
    sin = jnp.sin(ang)
    return (jnp.tile(jnp.concatenate([cos, cos], axis=-1), (1, N_KV)),
            jnp.tile(jnp.concatenate([-sin, sin], axis=-1), (1, N_KV)))


def _latent_attention(q, k, v, kc, vc, sink, tables, t_ctx, B, L):
    assert t_ctx % L == 0
    P = kc.shape[1]
    ck, sk = tables
    nbk = L // BLK
    seq = pl.BlockSpec((L, KV_W), lambda b, n: (t_ctx // L + b, 0))
    ctx = pl.BlockSpec((None, P, KV_W), lambda b, n: (b, 0, 0))
    return pl.pallas_call(
        _lat_attn_kernel,
        grid=(B, nbk),
        in_specs=[
            pl.BlockSpec(memory_space=pltpu.SMEM),
            pl.BlockSpec((BLK, ATT_W), lambda b, n: (t_ctx // BLK + b * nbk + n, 0)),
            seq, seq, ctx, ctx,
            pl.BlockSpec((L, KV_W), lambda b, n: (0, 0)),
            pl.BlockSpec((L, KV_W), lambda b, n: (0, 0)),
        ],
        out_specs=pl.BlockSpec((BLK, ATT_W), lambda b, n: (b * nbk + n, 0)),
        out_shape=jax.ShapeDtypeStruct((B * L, ATT_W), BF16),
        scratch_shapes=[pltpu.VMEM((L, KV_W), BF16), pltpu.VMEM((L, KV_W), BF16),
                        pltpu.VMEM((P, KV_W), BF16), pltpu.VMEM((P, KV_W), BF16)],
        compiler_params=_cp(("arbitrary", "arbitrary")),
        name="latent_attention",
    )(sink, q, k, v, kc, vc, ck, sk)


def _conv3(u, prev_row, next_row, w):
    tl = u.shape[0]
    ridx = lax.broadcasted_iota(jnp.int32, (tl, 1), 0)
    dn = jnp.where(ridx == 0, prev_row, pltpu.roll(u, 1, 0))
    up = jnp.where(ridx == tl - 1, next_row, pltpu.roll(u, tl - 1, 0))
    return dn * w[0:1] + u * w[1:2] + up * w[2:3]


def _store_time_major_inner(dst_ref, src, scr_ref):
    n2 = dst_ref.shape[0]
    rows = src.shape[0] // n2
    for h in range(scr_ref.shape[0]):
        scr_ref[h] = src[:, h * 128:(h + 1) * 128]
    for j in range(n2):
        for h in range(scr_ref.shape[0]):
            dst_ref[j, :, h * 128:(h + 1) * 128] = scr_ref[h, pl.ds(j, rows, stride=n2), :]


def _mixprep_kernel(uh_ref, uhp_ref, uhn_ref, us_ref, usp_ref, usn_ref, hw_ref, hb_ref, sw_ref,
                    x0_ref, g_ref, gt_ref, osc_ref, scr_ref):
    t = pl.program_id(1)
    first = t == 0
    last = t == pl.num_programs(1) - 1
    c = HY_W
    hw = hw_ref[...]
    halo = uhp_ref.shape[0]
    f32 = lambda a: a.astype(F32)
    uh = _conv3(f32(uh_ref[...]), jnp.where(first, 0.0, f32(uhp_ref[halo - 1:halo])),
                jnp.where(last, 0.0, f32(uhn_ref[0:1])), hw)
    uh = uh + hb_ref[...]
    x0_ref[...] = uh[:, :c]
    g = uh[:, c:2 * c] * uh[:, 2 * c:]
    g_ref[...] = g
    _store_time_major_inner(gt_ref, g, scr_ref)
    us = f32(us_ref[...])
    usp = f32(usp_ref[halo - 1:halo])
    usn = f32(usn_ref[0:1])
    prod = us[:, c:2 * c] * us[:, 2 * c:]
    pprev = jnp.where(first, 0.0, usp[:, c:2 * c] * usp[:, 2 * c:])
    pnext = jnp.where(last, 0.0, usn[:, c:2 * c] * usn[:, 2 * c:])
    osc_ref[...] = (us[:, :c] * _conv3(prod, pprev, pnext, sw_ref[...])).astype(osc_ref.dtype)


def _mixer_prep(u_hy, u_sc, hy_conv_w, hy_conv_b, sc_conv_w, row0, B, L):
    W = u_hy.shape[1]
    n1, n2 = _fft_split(L)
    tl = 8 * n2
    assert row0 % tl == 0 and L % tl == 0
    nt = L // tl
    halo = 16
    rh = tl // halo
    base = lambda b: row0 // halo + b * (L // halo)
    main = pl.BlockSpec((tl, W), lambda b, t: (row0 // tl + b * nt + t, 0))
    prev = pl.BlockSpec((halo, W), lambda b, t: (base(b) + jnp.maximum(t * rh - 1, 0), 0))
    nxt = pl.BlockSpec((halo, W), lambda b, t: (base(b) + jnp.minimum((t + 1) * rh, L // halo - 1), 0))
    const = lambda shape: pl.BlockSpec(shape, lambda b, t: (0, 0))
    out = pl.BlockSpec((tl, HY_W), lambda b, t: (b * nt + t, 0))
    flat = jax.ShapeDtypeStruct((B * L, HY_W), F32)
    return pl.pallas_call(
        _mixprep_kernel,
        grid=(B, nt),
        in_specs=[main, prev, nxt, main, prev, nxt, const((3, W)), const((1, W)), const((3, SC_W))],
        out_specs=[out, out, pl.BlockSpec((None, n2, 8, HY_W), lambda b, t: (b, 0, t, 0)), out],
        out_shape=[flat, flat, jax.ShapeDtypeStruct((B, n2, n1 // 2, HY_W), F32),
                   jax.ShapeDtypeStruct((B * L, SC_W), BF16)],
        scratch_shapes=[pltpu.VMEM((HY_W // 128, tl, 128), F32)],
        compiler_params=_cp(("arbitrary", "arbitrary")),
        name="mixer_prep",
    )(u_hy, u_hy, u_hy, u_sc, u_sc, u_sc, hy_conv_w, hy_conv_b.reshape(1, W), sc_conv_w)


def _filter_mlp(z, w1_ref, b1_ref, f1_ref, w2_ref, b2_ref, f2_ref, w3_ref, dl_ref, row0):
    tl = z.shape[0]
    h = jnp.sin(f1_ref[...] * (_dot(z, w1_ref[...], HI) + b1_ref[...]))
    h = jnp.sin(f2_ref[...] * (_dot(h, w2_ref[...], HI) + b2_ref[...]))
    h = _dot(h, w3_ref[...], HI) * jnp.exp(-z[:, 0:1] * jnp.abs(dl_ref[...]))
    row = row0 + lax.broadcasted_iota(jnp.int32, (tl, 1), 0)
    return h[:, :HY_W], jnp.where(row == 0, 0.0, h[:, HY_W:])


def _filter_kernel(z_ref, w1_ref, b1_ref, f1_ref, w2_ref, b2_ref, f2_ref, w3_ref, dl_ref, o_ref, scr_ref):
    hf, hb = _filter_mlp(z_ref[...], w1_ref, b1_ref, f1_ref, w2_ref, b2_ref, f2_ref, w3_ref, dl_ref,
                         pl.program_id(0) * z_ref.shape[0])
    _store_time_major_inner(o_ref.at[0], hf, scr_ref)
    _store_time_major_inner(o_ref.at[1], hb, scr_ref)


def _filter_features(L):
    t = jnp.linspace(0.0, 1.0, L, dtype=F32)[:, None]
    w = 2.0 * math.pi * jnp.arange(L, dtype=F32)[:, None] / L
    bands = jnp.linspace(1e-4, FILTER_BANDS - 1, FILTER_BANDS, dtype=F32)[None, :]
    z = jnp.concatenate([t, jnp.cos(bands * w), -jnp.sin(bands * w)], axis=-1)
    return jnp.pad(z, ((0, 0), (0, 128 - FILTER_EMB)))


def _filter_mlp_operands(w1, b1, f1, w2, b2, f2, w3, deltas):
    H = FILTER_HIDDEN
    ops = (jnp.pad(w1, ((0, 128 - FILTER_EMB), (0, 0))), b1.reshape(1, H), f1.reshape(1, H), w2, b2.reshape(1, H),
           f2.reshape(1, H), w3, deltas.reshape(1, 2 * HY_W))
    specs = [pl.BlockSpec(o.shape, lambda *_: (0, 0)) for o in ops]
    return ops, specs


def _hyena_filters(L, mlp):
    n1, n2 = _fft_split(L)
    tl = 8 * n2
    ops, specs = _filter_mlp_operands(*mlp)
    return pl.pallas_call(
        _filter_kernel,
        grid=(L // tl,),
        in_specs=[pl.BlockSpec((tl, 128), lambda i: (i, 0)), *specs],
        out_specs=pl.BlockSpec((2, n2, 8, HY_W), lambda i: (0, 0, i, 0)),
        out_shape=jax.ShapeDtypeStruct((2, n2, n1 // 2, HY_W), F32),
        scratch_shapes=[pltpu.VMEM((HY_W // 128, tl, 128), F32)],
        compiler_params=_cp(("arbitrary",)),
        name="hyena_filters",
    )(_filter_features(L), *ops)


def _dense_dft_constants(L):
    n = 2 * L
    a = (jnp.arange(n, dtype=jnp.int32)[:, None] * jnp.arange(L, dtype=jnp.int32)[None, :]) % n
    th = (2.0 * math.pi / n) * a.astype(F32)
    fwd = jnp.concatenate([jnp.cos(th), -jnp.sin(th)], axis=0)
    inv = jnp.concatenate([jnp.cos(th).T, -jnp.sin(th).T], axis=1) * (1.0 / n)
    return fwd.astype(BF16), inv.astype(BF16)


def _ctx_mixers_kernel(uh_ref, us_ref, hw_ref, hb_ref, sw_ref, bias_ref, fwd_ref, inv_ref, z_ref,
                       w1_ref, b1_ref, f1_ref, w2_ref, b2_ref, f2_ref, w3_ref, dl_ref, ohy_ref, osc_ref, kf_ref):
    c = HY_W
    nf = kf_ref.shape[1]

    @pl.when(pl.program_id(0) == 0)
    def _():
        hf, hb = _filter_mlp(z_ref[...], w1_ref, b1_ref, f1_ref, w2_ref, b2_ref, f2_ref, w3_ref, dl_ref, 0)
        xf = _dot(fwd_ref[...], hf.astype(BF16))
        xb = _dot(fwd_ref[...], hb.astype(BF16))
        kf_ref[0] = xf[:nf] + xb[:nf]
        kf_ref[1] = xf[nf:] - xb[nf:]

    zero = jnp.zeros((1, 1), F32)
    uh = _conv3(uh_ref[...].astype(F32), zero, zero, hw_ref[...]) + hb_ref[...]
    x0, g = uh[:, :c], uh[:, c:2 * c] * uh[:, 2 * c:]
    s = _dot(fwd_ref[...], g.astype(BF16))
    sr, si = s[:nf], s[nf:]
    kr, ki = kf_ref[0], kf_ref[1]
    y = _dot(inv_ref[...], jnp.concatenate([sr * kr - si * ki, sr * ki + si * kr], axis=0).astype(BF16))
    ohy_ref[...] = (x0 * (y + g * bias_ref[...])).astype(ohy_ref.dtype)
    us = us_ref[...].astype(F32)
    osc_ref[...] = (us[:, :c] * _conv3(us[:, c:2 * c] * us[:, 2 * c:], zero, zero, sw_ref[...])).astype(osc_ref.dtype)


def _ctx_mixers(u_hy, u_sc, hy_conv_w, hy_conv_b, sc_conv_w, hy_bias, mlp, B, L):
    W = u_hy.shape[1]
    fwd, inv = _dense_dft_constants(L)
    ops, specs = _filter_mlp_operands(*mlp)
    const = lambda shape: pl.BlockSpec(shape, lambda b: (0, 0))
    seq = lambda w: pl.BlockSpec((L, w), lambda b: (b, 0))
    return pl.pallas_call(
        _ctx_mixers_kernel,
        grid=(B,),
        in_specs=[seq(W), seq(W), const((3, W)), const((1, W)), const((3, SC_W)), const((1, HY_W)),
                  const(fwd.shape), const(inv.shape), const((L, 128)), *specs],
        out_specs=[seq(HY_W), seq(SC_W)],
        out_shape=[jax.ShapeDtypeStruct((B * L, HY_W), BF16), jax.ShapeDtypeStruct((B * L, SC_W), BF16)],
        scratch_shapes=[pltpu.VMEM((2, 2 * L, HY_W), F32)],
        compiler_params=_cp(("arbitrary",)),
        name="context_mixers",
    )(u_hy, u_sc, hy_conv_w, hy_conv_b.reshape(1, W), sc_conv_w, hy_bias.reshape(1, HY_W), fwd, inv,
      _filter_features(L), *ops)


def _fft_split(L):
    n = 2 * L
    n1 = 128 if n >= 8192 else 32
    return n1, n // n1


def _fft_constants(L):
    n1, n2 = _fft_split(L)
    n = n1 * n2
    k1 = jnp.arange(n1, dtype=jnp.int32)
    a = (k1[:, None] * jnp.arange(n1 // 2, dtype=jnp.int32)[None, :]) % n1
    th = (2.0 * math.pi / n1) * a.astype(F32)
    f1 = jnp.concatenate([jnp.cos(th), -jnp.sin(th)], axis=0)
    f3 = jnp.concatenate([jnp.cos(th).T, -jnp.sin(th).T], axis=1) * (1.0 / n)
    k = k1[:, None, None] + n1 * jnp.arange(n2, dtype=jnp.int32)[None, :, None]
    ph = (k * jnp.arange(n2, dtype=jnp.int32)[None, None, :]) % n
    ph = (2.0 * math.pi / n) * ph.astype(F32)
    mr, mi = jnp.cos(ph), -jnp.sin(ph)
    mf = jnp.concatenate([jnp.concatenate([mr, -mi], axis=2), jnp.concatenate([mi, mr], axis=2)], axis=1)
    mrt, mit = jnp.swapaxes(mr, 1, 2), jnp.swapaxes(mi, 1, 2)
    mb = jnp.concatenate([jnp.concatenate([mrt, mit], axis=2), jnp.concatenate([-mit, mrt], axis=2)], axis=1)
    return tuple(c.astype(BF16) for c in (f1, f3, mf, mb))


FFT_GROUP = 8
FFT_MID_GROUP = 2 * FFT_GROUP


def _dft_rows_kernel(f_ref, x_ref, o_ref):
    for i in range(FFT_GROUP):
        o_ref[:, i, :] = _dot(f_ref[...], x_ref[i].astype(BF16))


def _dft_rows(f1, xt):
    B, n2, kh, C = xt.shape
    m = f1.shape[0]
    return pl.pallas_call(
        _dft_rows_kernel,
        grid=(B, n2 // FFT_GROUP),
        in_specs=[pl.BlockSpec((m, kh), lambda b, j: (0, 0)),
                  pl.BlockSpec((None, FFT_GROUP, kh, C), lambda b, j: (b, j, 0, 0))],
        out_specs=pl.BlockSpec((None, m, FFT_GROUP, C), lambda b, j: (b, 0, j, 0)),
        out_shape=jax.ShapeDtypeStruct((B, m, n2, C), F32),
        compiler_params=_cp(("arbitrary", "arbitrary")),
        name="fft_rows",
    )(f1, xt)


def _fft_mid_kernel(a_ref, hf_ref, hb_ref, mf_ref, mb_ref, o_ref, kf_ref):
    n2 = a_ref.shape[2]

    @pl.when(pl.program_id(1) == 0)
    def _():
        for i in range(FFT_MID_GROUP):
            xf = _dot(mf_ref[i], jnp.concatenate([hf_ref[0, i], hf_ref[1, i]], axis=0).astype(BF16))
            xb = _dot(mf_ref[i], jnp.concatenate([hb_ref[0, i], hb_ref[1, i]], axis=0).astype(BF16))
            kf_ref[i, 0] = xf[:n2] + xb[:n2]
            kf_ref[i, 1] = xf[n2:] - xb[n2:]

    for i in range(FFT_MID_GROUP):
        x = _dot(mf_ref[i], jnp.concatenate([a_ref[0, i], a_ref[1, i]], axis=0).astype(BF16))
        xr, xi = x[:n2], x[n2:]
        kr, ki = kf_ref[i, 0], kf_ref[i, 1]
        y = jnp.concatenate([xr * kr - xi * ki, xr * ki + xi * kr], axis=0)
        o = _dot(mb_ref[i], y.astype(BF16))
        o_ref[:, 0, i, :] = o[:n2]
        o_ref[:, 1, i, :] = o[n2:]


def _fft_mid(a, ah, mf, mb):
    B, _, n1, n2, C = a.shape
    blk = (None, 2, FFT_MID_GROUP, n2, C)
    return pl.pallas_call(
        _fft_mid_kernel,
        grid=(n1 // FFT_MID_GROUP, B),
        in_specs=[
            pl.BlockSpec(blk, lambda k, b: (b, 0, k, 0, 0)),
            pl.BlockSpec(blk, lambda k, b: (0, 0, k, 0, 0)),
            pl.BlockSpec(blk, lambda k, b: (1, 0, k, 0, 0)),
            pl.BlockSpec((FFT_MID_GROUP, 2 * n2, 2 * n2), lambda k, b: (k, 0, 0)),
            pl.BlockSpec((FFT_MID_GROUP, 2 * n2, 2 * n2), lambda k, b: (k, 0, 0)),
        ],
        out_specs=pl.BlockSpec((None, n2, 2, FFT_MID_GROUP, C), lambda k, b: (b, 0, 0, k, 0)),
        out_shape=jax.ShapeDtypeStruct((B, n2, 2, n1, C), F32),
        scratch_shapes=[pltpu.VMEM((FFT_MID_GROUP, 2, n2, C), F32)],
        compiler_params=_cp(("arbitrary", "arbitrary")),
        name="fft_mid",
    )(a, ah, ah, mf, mb)


def _idft_rows_kernel(f_ref, b_ref, x0_ref, g_ref, bias_ref, o_ref):
    for i in range(FFT_GROUP):
        o_ref[:, i, :] = _dot(f_ref[...], b_ref[i].astype(BF16))
    o_ref[...] = x0_ref[...] * (o_ref[...] + g_ref[...] * bias_ref[...])


def _idft_rows_gate(f3, bo, x0, g, bias):
    B, n2, m2, C = bo.shape
    kh = f3.shape[0]
    tile = pl.BlockSpec((None, kh, FFT_GROUP, C), lambda b, j: (b, 0, j, 0))
    return pl.pallas_call(
        _idft_rows_kernel,
        grid=(B, n2 // FFT_GROUP),
        in_specs=[pl.BlockSpec((kh, m2), lambda b, j: (0, 0)),
                  pl.BlockSpec((None, FFT_GROUP, m2, C), lambda b, j: (b, j, 0, 0)),
                  tile, tile, pl.BlockSpec((1, C), lambda b, j: (0, 0))],
        out_specs=tile,
        out_shape=jax.ShapeDtypeStruct((B, kh, n2, C), F32),
        compiler_params=_cp(("arbitrary", "arbitrary")),
        name="ifft_rows_gate",
    )(f3, bo, x0, g, bias)


def _hyena_long_conv(x0, g, gt, filt_t, hy_bias, consts, B, L):
    C = HY_W
    n1, n2 = _fft_split(L)
    f1, f3, mf, mb = consts
    a = _dft_rows(f1, gt).reshape(B, 2, n1, n2, C)
    ah = _dft_rows(f1, filt_t).reshape(2, 2, n1, n2, C)
    bo = _fft_mid(a, ah, mf, mb).reshape(B, n2, 2 * n1, C)
    nat = lambda z: z.reshape(B, n1 // 2, n2, C)
    out = _idft_rows_gate(f3, bo, nat(x0), nat(g), hy_bias.reshape(1, C))
    return out.reshape(B * L, C)


def _outproj_kernel(oac_ref, oal_ref, ohc_ref, ohl_ref, osc_ref, osl_ref, xc_ref, xl_ref, mod_ref, w_ref,
                    lg_ref, lb_ref, rw_ref, rb_ref, x1_ref, h2_ref, meta_ref, cnt_ref, wbf_ref, *, nct):
    @pl.when(pl.program_id(0) == 0)
    def _():
        wbf_ref[...] = w_ref[...].astype(BF16)

    m = mod_ref[0, 0]
    w = wbf_ref[...]
    is_ctx = pl.program_id(0) < nct
    pick = lambda c_ref, l_ref, dt: jnp.where(is_ctx, c_ref[...].astype(dt), l_ref[...].astype(dt))
    o = (_dot(pick(oac_ref, oal_ref, BF16), w[:ATT_W])
         + _dot(pick(ohc_ref, ohl_ref, BF16), w[ATT_W:ATT_W + HY_W])
         + _dot(pick(osc_ref, osl_ref, BF16), w[ATT_W + HY_W:]))
    x1 = _layer_norm(DN_ALPHA * pick(xc_ref, xl_ref, F32) + m[2:3] * o, lg_ref[...], lb_ref[...])
    x1_ref[...] = x1
    h2 = x1 * (1.0 + m[4:5]) + m[3:4]
    h2_hi = h2.astype(BF16)
    h2_ref[...] = h2_hi
    h2_lo = (h2 - h2_hi.astype(F32)).astype(BF16)
    logits = _dot(h2_hi, rw_ref[0]) + _dot(h2_lo, rw_ref[0]) + _dot(h2_hi, rw_ref[1]) + rb_ref[...]
    for sb in range(h2.shape[0] // TOKEN_BLOCK):
        rows = slice(sb * TOKEN_BLOCK, (sb + 1) * TOKEN_BLOCK)
        meta, cnt = _route_block(logits[rows])
        meta_ref[rows, :] = meta
        cnt_ref[sb] = jnp.broadcast_to(cnt, (8, 128))


def _route_block(logits):
    lane = lax.broadcasted_iota(jnp.int32, logits.shape, 1)
    logits = jnp.where(lane < N_EXP, logits, NEG_INF)
    picks = []
    vals = []
    for _ in range(TOP_K):
        v = jnp.max(logits, axis=-1, keepdims=True)
        idx = jnp.min(jnp.where(logits == v, lane, 128), axis=-1, keepdims=True)
        hit = lane == idx
        picks.append(hit)
        vals.append(v)
        logits = jnp.where(hit, NEG_INF, logits)
    es = [jnp.exp(v - vals[0]) for v in vals]
    den = es[0] + es[1] + es[2] + es[3]
    tb = logits.shape[0]
    msel = jnp.zeros(logits.shape, F32)
    for hit in picks:
        msel = msel + jnp.where(hit, 1.0, 0.0)
    cnt = jnp.sum(msel, axis=0, keepdims=True)
    pc = jnp.floor((cnt + (ROW_PAD - 1.0)) * (1.0 / ROW_PAD)) * ROW_PAD
    upper = jnp.where(lax.broadcasted_iota(jnp.int32, (128, 128), 0)
                      < lax.broadcasted_iota(jnp.int32, (128, 128), 1), 1.0, 0.0)
    lo = _dot(jnp.broadcast_to(pc, (8, 128)), upper, HI)[0:1]
    lower = jnp.where(lax.broadcasted_iota(jnp.int32, (tb, tb), 1)
                      < lax.broadcasted_iota(jnp.int32, (tb, tb), 0), 1.0, 0.0).astype(BF16)
    pos = lo + _dot(lower, msel.astype(BF16))
    meta = jnp.zeros(logits.shape, F32)
    for kk, (hit, e) in enumerate(zip(picks, es)):
        slot = jnp.sum(jnp.where(hit, pos, 0.0), axis=-1, keepdims=True)
        meta = jnp.where(lane == kk, slot, meta)
        meta = jnp.where(lane == TOP_K + kk, e / den, meta)
    return meta, cnt


def _out_projection(o_attn, o_hy, o_sc, x, mod, w_out, ln_g, ln_b, router_w, router_b, layer, dec_seq):
    t_ctx = x[0].shape[0]
    T = t_ctx + x[1].shape[0]
    blocks_per_step = 2
    tm = blocks_per_step * TOKEN_BLOCK
    nct = t_ctx // tm
    row = lambda i: (layer, _mod_row(i, tm, t_ctx, dec_seq), 0, 0)
    tok = lambda w: pl.BlockSpec((tm, w), lambda i: (i, 0))
    const = lambda shape: pl.BlockSpec(shape, lambda i: (0, 0))
    rw = jnp.pad(router_w, ((0, 0), (0, 128 - N_EXP)))
    rw_hi = rw.astype(BF16)
    rw = jnp.stack([rw_hi, (rw - rw_hi.astype(F32)).astype(BF16)])
    rb = jnp.pad(router_b, (0, 128 - N_EXP)).reshape(1, 128)
    return pl.pallas_call(
        functools.partial(_outproj_kernel, nct=nct),
        grid=(T // tm,),
        in_specs=[*_ctx_lat_specs((tm, ATT_W), nct), *_ctx_lat_specs((tm, HY_W), nct),
                  *_ctx_lat_specs((tm, SC_W), nct), *_ctx_lat_specs((tm, D), nct),
                  pl.BlockSpec((1, 1, 6, D), row),
                  pl.BlockSpec((None, D, D), lambda i: (layer, 0, 0)), const((1, D)), const((1, D)),
                  pl.BlockSpec((2, D, 128), lambda i: (0, 0, 0)), const((1, 128))],
        out_specs=[tok(D), tok(D), tok(128), pl.BlockSpec((blocks_per_step, 8, 128), lambda i: (i, 0, 0))],
        out_shape=[jax.ShapeDtypeStruct((T, D), F32), jax.ShapeDtypeStruct((T, D), BF16),
                   jax.ShapeDtypeStruct((T, 128), F32), jax.ShapeDtypeStruct((T // TOKEN_BLOCK, 8, 128), F32)],
        scratch_shapes=[pltpu.VMEM((D, D), BF16)],
        compiler_params=_cp(("arbitrary",)),
        name="out_projection_router",
    )(*o_attn, *o_hy, *o_sc, *x, mod, w_out, ln_g.reshape(1, D), ln_b.reshape(1, D), rw, rb)


def _routing_tables(cnt):
    nb = cnt.shape[0]
    ex = jnp.arange(N_EXP, dtype=jnp.int32)
    before_e = (ex[:, None] < ex[None, :]).astype(jnp.int32)
    bl = jnp.arange(nb, dtype=jnp.int32)
    before_b = (bl[None, :] < bl[:, None]).astype(jnp.int32)
    pc = (cnt + ROW_PAD - 1) // ROW_PAD * ROW_PAD
    lo = jnp.sum(pc[:, :, None] * before_e[None], axis=1)
    tot = jnp.sum(pc, axis=0)
    tot_pad = (tot + EXPERT_TILE - 1) // EXPERT_TILE * EXPERT_TILE
    off = jnp.sum(tot_pad[:, None] * before_e, axis=0)
    gstart = off[None, :] + jnp.sum(before_b[:, :, None] * pc[None], axis=1)
    ntile = tot_pad // EXPERT_TILE
    cum = jnp.sum(ntile[:, None] * before_e, axis=0) + ntile
    n_tiles = _max_expert_tiles(nb)
    i = jnp.arange(n_tiles, dtype=jnp.int32)
    ic = jnp.maximum(jnp.minimum(i, cum[-1] - 1), 0)
    te = jnp.sum((cum[None, :] <= ic[:, None]).astype(jnp.int32), axis=1)
    owner = (te[:, None] == ex[None, :]).astype(jnp.int32)
    pick = lambda per_expert: jnp.sum(owner * per_expert[None, :], axis=1)
    first_tile = pick(cum - ntile)
    first = ic == first_tile
    used = jnp.clip(pick(tot) - (ic - first_tile) * EXPERT_TILE, 0, EXPERT_TILE)
    part = EXPERT_TILE // TILE_PARTS
    parts = jnp.clip((used + part - 1) // part, 1, TILE_PARTS)
    flags = jnp.where(i < cum[-1], 1 + 2 * first.astype(jnp.int32) + 4 * parts, 0)
    later = (ex[None, :] > ex[:, None]) & (ntile[None, :] > 0)
    next_e = jnp.min(jnp.where(later, ex[None, :], N_EXP), axis=1)
    next_e = jnp.where(next_e < N_EXP, next_e, -1)
    i32 = lambda a: a.astype(jnp.int32).reshape(-1)
    return dict(pc=i32(pc), lo=i32(lo), gstart=i32(gstart), tail_start=i32(off + tot), tail_len=i32(tot_pad - tot),
                rows=i32(jnp.sum(pc, axis=1)), tile_expert=i32(te), tile_row=ic, tile_flags=flags,
                tile_next=i32(pick(next_e)))


def _max_expert_tiles(nb):
    rows = TOP_K * nb * TOKEN_BLOCK + nb * N_EXP * (ROW_PAD - 1) + N_EXP * (EXPERT_TILE - 1)
    return rows // EXPERT_TILE + 1


def _run_copies(n, src_ref, src0, dst_ref, dst0, sem, wait):
    done = jnp.int32(0)
    for p in RUN_PIECES:
        take = (n & p) != 0

        @pl.when(take)
        def _():
            cp = pltpu.make_async_copy(src_ref.at[pl.ds(pl.multiple_of(src0 + done, ROW_PAD), p), :],
                                       dst_ref.at[pl.ds(pl.multiple_of(dst0 + done, ROW_PAD), p), :], sem)
            if wait:
                cp.wait()
            else:
                cp.start()

        done = done + jnp.where(take, p, 0)


def _wait_rows(n, vmem_ref, hbm_ref, sem, to_vmem):
    base = TOP_K * TOKEN_BLOCK

    def wait(p):
        src, dst = (hbm_ref, vmem_ref) if to_vmem else (vmem_ref, hbm_ref)
        pltpu.make_async_copy(src.at[pl.ds(0, p), :], dst.at[pl.ds(0, p), :], sem).wait()

    wait(base)
    for p in RUN_PIECES[1:]:
        pl.when(((n - base) & p) != 0)(functools.partial(wait, p))


SORTED_W = D + 128


def _dispatch_kernel(pc_ref, lo_ref, gs_ref, ts_ref, tl_ref, rows_ref, h_ref, meta_ref, xs_ref,
                     buf_ref, zero_ref, sem, zsem):
    b = pl.program_id(0)
    nb = pl.num_programs(0)
    s = buf_ref.shape[1]
    slot = b % 2

    @pl.when(b == 0)
    def _():
        zero_ref[...] = jnp.zeros_like(zero_ref)
        for wait in (False, True):
            def tail(e, carry):
                _run_copies(tl_ref[e], zero_ref, 0, xs_ref, ts_ref[e], zsem, wait)
                return carry
            lax.fori_loop(0, N_EXP, tail, 0)

    mt = meta_ref[...].T
    srow = lax.broadcasted_iota(jnp.int32, (s, 1), 0).astype(F32)
    perm = jnp.zeros((s, TOKEN_BLOCK), F32)
    gperm = jnp.zeros((s, TOKEN_BLOCK), F32)
    for k in range(TOP_K):
        hit = srow == mt[k:k + 1]
        perm = jnp.where(hit, 1.0, perm)
        gperm = jnp.where(hit, mt[TOP_K + k:TOP_K + k + 1], gperm)
    sorted_rows = _dot(perm.astype(BF16), h_ref[...])
    gate_col = jnp.sum(gperm, axis=-1, keepdims=True)

    @pl.when(b >= 2)
    def _():
        _wait_rows(rows_ref[b - 2], buf_ref.at[slot], xs_ref, sem.at[slot], to_vmem=False)

    buf = buf_ref.at[slot]
    buf[:, :D] = sorted_rows
    buf[:, D:] = jnp.broadcast_to(gate_col, (s, 128))

    def run(e, carry):
        j = b * N_EXP + e
        _run_copies(pc_ref[j], buf, lo_ref[j], xs_ref, gs_ref[j], sem.at[slot], wait=False)
        return carry
    lax.fori_loop(0, N_EXP, run, 0)

    @pl.when(b == nb - 1)
    def _():
        @pl.when(b >= 1)
        def _():
            _wait_rows(rows_ref[b - 1], buf_ref.at[1 - slot], xs_ref, sem.at[1 - slot], to_vmem=False)
        _wait_rows(rows_ref[b], buf, xs_ref, sem.at[slot], to_vmem=False)


def _dispatch(tabs, h2, meta):
    T = h2.shape[0]
    nb = T // TOKEN_BLOCK
    rows = _max_expert_tiles(nb) * EXPERT_TILE
    tok = lambda w: pl.BlockSpec((TOKEN_BLOCK, w), lambda b, *_: (b, 0))
    return pl.pallas_call(
        _dispatch_kernel,
        grid_spec=pltpu.PrefetchScalarGridSpec(
            num_scalar_prefetch=6,
            grid=(nb,),
            in_specs=[tok(D), tok(128)],
            out_specs=pl.BlockSpec(memory_space=pl.ANY),
            scratch_shapes=[pltpu.VMEM((2, LOCAL_SLOTS, SORTED_W), F32), pltpu.VMEM((EXPERT_TILE, SORTED_W), F32),
                            pltpu.SemaphoreType.DMA((2,)), pltpu.SemaphoreType.DMA(())],
        ),
        out_shape=jax.ShapeDtypeStruct((rows, SORTED_W), F32),
        compiler_params=_cp(("arbitrary",)),
        name="moe_dispatch",
    )(tabs["pc"], tabs["lo"], tabs["gstart"], tabs["tail_start"], tabs["tail_len"], tabs["rows"], h2, meta)


def _expert_ffn_kernel(te_ref, tr_ref, fl_ref, nx_ref, x_ref, wg_hbm, bg_ref, wu_hbm, bu_ref, wd_hbm, bd_ref, y_ref,
                       wf_ref, wbf_ref, sem, *, layer):
    i = pl.program_id(0)
    fl = fl_ref[i]
    weights = (wg_hbm, wu_hbm, wd_hbm)

    def weight_copy(j, expert):
        return pltpu.make_async_copy(weights[j].at[layer, expert], wf_ref.at[j], sem.at[j])

    @pl.when((fl & 1) != 0)
    def _():
        @pl.when((fl & 2) != 0)
        def _():
            @pl.when(i == 0)
            def _():
                for j in range(3):
                    weight_copy(j, te_ref[i]).start()

            for j in range(3):
                weight_copy(j, te_ref[i]).wait()
                wbf_ref[j] = wf_ref[j].astype(BF16)

            @pl.when(nx_ref[i] >= 0)
            def _():
                for j in range(3):
                    weight_copy(j, nx_ref[i]).start()

        def ffn(rows):
            x = x_ref[:rows, :D].astype(BF16)
            gate = x_ref[:rows, D:D + 1]
            g = jnp.minimum(_dot(x, wbf_ref[0]) + bg_ref[0], SWIGLU_LIMIT)
            u = jnp.clip(_dot(x, wbf_ref[1]) + bu_ref[0], -SWIGLU_LIMIT, SWIGLU_LIMIT)
            a = (u + 1.0) * (g * jax.nn.sigmoid(SWIGLU_ALPHA * g))
            y_ref[:rows, :] = gate * (_dot(a.astype(BF16), wbf_ref[2]) + bd_ref[0])

        quarters = fl >> 2
        for q in range(1, TILE_PARTS + 1):
            pl.when(quarters == q)(functools.partial(ffn, q * (EXPERT_TILE // TILE_PARTS)))


def _expert_ffn(tabs, xs, w_gate, b_gate, w_up, b_up, w_down, b_down, layer):
    rows = xs.shape[0]
    hbm = pl.BlockSpec(memory_space=pl.ANY)
    bspec = pl.BlockSpec((1, 1, D), lambda i, te, *_: (te[i], 0, 0))
    return pl.pallas_call(
        functools.partial(_expert_ffn_kernel, layer=layer),
        grid_spec=pltpu.PrefetchScalarGridSpec(
            num_scalar_prefetch=4,
            grid=(rows // EXPERT_TILE,),
            in_specs=[pl.BlockSpec((EXPERT_TILE, SORTED_W), lambda i, te, tr, *_: (tr[i], 0)),
                      hbm, bspec, hbm, bspec, hbm, bspec],
            out_specs=pl.BlockSpec((EXPERT_TILE, D), lambda i, te, tr, *_: (tr[i], 0)),
            scratch_shapes=[pltpu.VMEM((3, D, D), F32), pltpu.VMEM((3, D, D), BF16), pltpu.SemaphoreType.DMA((3,))],
        ),
        out_shape=jax.ShapeDtypeStruct((rows, D), F32),
        compiler_params=_cp(("arbitrary",)),
        name="expert_ffn",
    )(tabs["tile_expert"], tabs["tile_row"], tabs["tile_flags"], tabs["tile_next"], xs,
      w_gate, b_gate[layer].reshape(N_EXP, 1, D), w_up, b_up[layer].reshape(N_EXP, 1, D),
      w_down, b_down[layer].reshape(N_EXP, 1, D))


def _combine_kernel(pc_ref, lo_ref, gs_ref, rows_ref, ys_ref, meta_ref, x1_ref, mod_ref, lg_ref, lb_ref,
                    oc_ref, ol_ref, buf_ref, sem, *, nct):
    b = pl.program_id(0)
    nb = pl.num_programs(0)
    s = buf_ref.shape[1]
    slot = b % 2

    def fetch(blk, to_slot):
        def run(e, carry):
            j = blk * N_EXP + e
            _run_copies(pc_ref[j], ys_ref, gs_ref[j], buf_ref.at[to_slot], lo_ref[j], sem.at[to_slot], wait=False)
            return carry
        lax.fori_loop(0, N_EXP, run, 0)

    @pl.when(b == 0)
    def _():
        buf_ref[...] = jnp.zeros_like(buf_ref)
        fetch(b, slot)

    @pl.when(b + 1 < nb)
    def _():
        fetch(b + 1, 1 - slot)

    _wait_rows(rows_ref[b], buf_ref.at[slot], ys_ref, sem.at[slot], to_vmem=True)

    meta = meta_ref[...]
    scol = lax.broadcasted_iota(jnp.int32, (1, s), 1).astype(F32)
    w = jnp.zeros((TOKEN_BLOCK, s), F32)
    for k in range(TOP_K):
        w = jnp.where(meta[:, k:k + 1] == scol, 1.0, w)
    w = w.astype(BF16)
    moe = _dot(w, buf_ref[slot].astype(BF16))
    m = mod_ref[0, 0]
    out = _layer_norm(DN_ALPHA * x1_ref[...] + m[5:6] * moe, lg_ref[...], lb_ref[...])

    @pl.when(b < nct)
    def _():
        oc_ref[...] = out

    @pl.when(b >= nct)
    def _():
        ol_ref[...] = out


def _combine_ln2(tabs, ys, meta, x1, mod, ln_g, ln_b, layer, t_ctx, dec_seq):
    T = x1.shape[0]
    nct = t_ctx // TOKEN_BLOCK
    row = lambda b, *_: (layer, _mod_row(b, TOKEN_BLOCK, t_ctx, dec_seq), 0, 0)
    tok = lambda w: pl.BlockSpec((TOKEN_BLOCK, w), lambda b, *_: (b, 0))
    const = lambda shape: pl.BlockSpec(shape, lambda b, *_: (0, 0))
    return pl.pallas_call(
        functools.partial(_combine_kernel, nct=nct),
        grid_spec=pltpu.PrefetchScalarGridSpec(
            num_scalar_prefetch=4,
            grid=(T // TOKEN_BLOCK,),
            in_specs=[pl.BlockSpec(memory_space=pl.ANY), tok(128), tok(D), pl.BlockSpec((1, 1, 6, D), row),
                      const((1, D)), const((1, D))],
            out_specs=list(_ctx_lat_specs((TOKEN_BLOCK, D), nct)),
            scratch_shapes=[pltpu.VMEM((2, LOCAL_SLOTS, D), F32), pltpu.SemaphoreType.DMA((2,))],
        ),
        out_shape=[jax.ShapeDtypeStruct((t_ctx, D), F32), jax.ShapeDtypeStruct((T - t_ctx, D), F32)],
        compiler_params=_cp(("arbitrary",)),
        name="moe_combine_ln2",
    )(tabs["pc"], tabs["lo"], tabs["gstart"], tabs["rows"], ys, meta, x1, mod, ln_g.reshape(1, D),
      ln_b.reshape(1, D))


def _moe_ln2(h2, meta, cnt, x1, mod, w_gate, b_gate, w_up, b_up, w_down, b_down, ln_g, ln_b, layer, t_ctx, dec_seq):
    tabs = _routing_tables(cnt[:, 0, :N_EXP].astype(jnp.int32))
    xs = _dispatch(tabs, h2, meta)
    ys = _expert_ffn(tabs, xs, w_gate, b_gate, w_up, b_up, w_down, b_down, layer)
    return _combine_ln2(tabs, ys, meta, x1, mod, ln_g, ln_b, layer, t_ctx, dec_seq)


def kernel(x_prompt, x_sample, cache_k, cache_v, c, c_ctx, w_mod, b_mod, w_in, w_out, attn_sink, hy_conv_w, hy_conv_b, hy_w1, hy_b1, hy_f1, hy_w2, hy_b2, hy_f2, hy_w3, hy_deltas, hy_bias, sc_conv_w, ln1_g, ln1_b, router_w, router_b, w_gate, b_gate, w_up, b_up, w_down, b_down, ln2_g, ln2_b):
    nb, seq, _ = x_prompt.shape
    db, dseq, _ = x_sample.shape
    past = cache_k.shape[2]
    t_ctx = nb * seq
    x = (x_prompt.reshape(t_ctx, D), x_sample.reshape(db * dseq, D))

    cond = jnp.zeros((8, D), F32).at[0].set(c_ctx).at[1:1 + db].set(c)
    mod = _modulation(cond, w_mod, b_mod)

    tables = _rope_tables(dseq)
    fft_c = _fft_constants(dseq)

    new_k, new_v = [], []
    for l in range(DEPTH):
        q, k, v, u_hy, u_sc = _in_projection(x[0], x[1], mod, w_in, l, dseq)
        new_k.append(k[:t_ctx].reshape(nb, seq, N_KV, HD))
        new_v.append(v[:t_ctx].reshape(nb, seq, N_KV, HD))
        oa_ctx = _context_attention(q, k, v, attn_sink[l], nb, seq)
        oa_lat = _latent_attention(q, k, v, cache_k[:, l].reshape(db, past, KV_W),
                                   cache_v[:, l].reshape(db, past, KV_W), attn_sink[l], tables, t_ctx, db, dseq)
        mlp = (hy_w1[l], hy_b1[l], hy_f1[l], hy_w2[l], hy_b2[l], hy_f2[l], hy_w3[l], hy_deltas[l])
        ohy_ctx, osc_ctx = _ctx_mixers(u_hy, u_sc, hy_conv_w[l], hy_conv_b[l], sc_conv_w[l], hy_bias[l], mlp,
                                       nb, seq)
        x0, g, gt, osc_lat = _mixer_prep(u_hy, u_sc, hy_conv_w[l], hy_conv_b[l], sc_conv_w[l], t_ctx, db, dseq)
        ohy_lat = _hyena_long_conv(x0, g, gt, _hyena_filters(dseq, mlp), hy_bias[l], fft_c, db, dseq)
        x1, h2, meta, cnt = _out_projection((oa_ctx, oa_lat), (ohy_ctx, ohy_lat), (osc_ctx, osc_lat), x, mod,
                                            w_out, ln1_g[l], ln1_b[l], router_w[l], router_b[l], l, dseq)
        x = _moe_ln2(h2, meta, cnt, x1, mod, w_gate, b_gate, w_up, b_up, w_down, b_down, ln2_g[l], ln2_b[l],
                     l, t_ctx, dseq)

    y_prompt = x[0].reshape(nb, seq, D)
    y_sample = x[1].reshape(db, dseq, D)
    return (y_prompt, y_sample, jnp.stack(new_k, axis=1), jnp.stack(new_v, axis=1))
```

```python
import functools
import math

import jax
import jax.numpy as jnp
from jax import lax
from jax.experimental import pallas as pl
from jax.experimental.pallas import tpu as pltpu

F32 = jnp.float32
BF16 = jnp.bfloat16
HI = lax.Precision.HIGHEST

D = 1024
DEPTH = 2
N_HEADS = 8
N_KV = 2
HD = 64
GQA = N_HEADS // N_KV
ATT_W = N_HEADS * HD
KV_W = N_KV * HD
WINDOW = 128
BLK = 128
GRID_W = 64
ROPE_BASE = 10000.0
HY_W = 256
SC_W = 256
FILTER_EMB = 33
FILTER_BANDS = 16
FILTER_HIDDEN = 64
IN_W = ATT_W + 2 * KV_W + 3 * HY_W + 3 * SC_W
N_EXP = 32
TOP_K = 4
SWIGLU_LIMIT = 7.0
SWIGLU_ALPHA = 1.702
DN_ALPHA = (2 * DEPTH) ** 0.25
LN_EPS = 1e-5
NEG_INF = float("-inf")

VMEM_LIMIT = 56 * 1024 * 1024

TOKEN_BLOCK = 256
ROW_PAD = 8
RUN_PIECES = (256, 128, 64, 32, 16, 8)
EXPERT_TILE = 512
TILE_PARTS = 4
LOCAL_SLOTS = TOP_K * TOKEN_BLOCK + N_EXP * ROW_PAD


def _cp(sem):
    return pltpu.CompilerParams(dimension_semantics=sem, vmem_limit_bytes=VMEM_LIMIT)


def _dot(a, b, precision=None):
    return jnp.dot(a, b, preferred_element_type=F32, precision=precision)


def _dot_nt(a, b):
    return lax.dot_general(a, b, (((1,), (1,)), ((), ())), preferred_element_type=F32)


def _layer_norm(z, g, b):
    mu = jnp.mean(z, axis=-1, keepdims=True)
    zc = z - mu
    var = jnp.mean(zc * zc, axis=-1, keepdims=True)
    return zc * lax.rsqrt(var + LN_EPS) * g + b


def _mod_kernel(cond_ref, w_ref, b_ref, o_ref):
    c = cond_ref[...]
    s = c * jax.nn.sigmoid(c)
    o_ref[0] = _dot(s, w_ref[0], HI) + b_ref[0]


def _modulation(cond, w_mod, b_mod):
    tn = 1024
    out = pl.pallas_call(
        _mod_kernel,
        grid=(DEPTH, 6 * D // tn),
        in_specs=[
            pl.BlockSpec((8, D), lambda l, j: (0, 0)),
            pl.BlockSpec((1, D, tn), lambda l, j: (l, 0, j)),
            pl.BlockSpec((1, 1, tn), lambda l, j: (l, 0, j)),
        ],
        out_specs=pl.BlockSpec((1, 8, tn), lambda l, j: (l, 0, j)),
        out_shape=jax.ShapeDtypeStruct((DEPTH, 8, 6 * D), F32),
        compiler_params=_cp(("arbitrary", "arbitrary")),
        name="modulation",
    )(cond, w_mod, b_mod.reshape(DEPTH, 1, 6 * D))
    return out.reshape(DEPTH, 8, 6, D)


def _mod_row(i, tm, t_ctx, dec_seq):
    nct = t_ctx // tm
    return jnp.where(i < nct, 0, 1 + (i - nct) // (dec_seq // tm))


def _ctx_lat_specs(shape, nct, ctx_offset=0, lat_offset=0):
    ctx = pl.BlockSpec(shape, lambda i, *_: (jnp.minimum(i, nct - 1) + ctx_offset, 0))
    lat = pl.BlockSpec(shape, lambda i, *_: (jnp.maximum(i - nct, 0) + lat_offset, 0))
    return ctx, lat


def _inproj_kernel(xc_ref, xl_ref, mod_ref, w_ref, q_ref, k_ref, v_ref, uh_ref, us_ref, wbf_ref, *, nct):
    @pl.when(pl.program_id(0) == 0)
    def _():
        wbf_ref[...] = w_ref[...].astype(BF16)

    m = mod_ref[0, 0]
    x = jnp.where(pl.program_id(0) < nct, xc_ref[...], xl_ref[...])
    h = x * (1.0 + m[1:2]) + m[0:1]
    y = _dot(h.astype(BF16), wbf_ref[...])
    o0 = ATT_W
    o1 = o0 + KV_W
    o2 = o1 + KV_W
    o3 = o2 + 3 * HY_W
    q_ref[...] = y[:, :o0].astype(q_ref.dtype)
    k_ref[...] = y[:, o0:o1]
    v_ref[...] = y[:, o1:o2]
    uh_ref[...] = y[:, o2:o3].astype(uh_ref.dtype)
    us_ref[...] = y[:, o3:].astype(us_ref.dtype)


def _in_projection(xc, xl, mod, w_in, layer, dec_seq):
    t_ctx = xc.shape[0]
    T = t_ctx + xl.shape[0]
    tm = 512
    row = lambda i: (layer, _mod_row(i, tm, t_ctx, dec_seq), 0, 0)
    tok = lambda w: pl.BlockSpec((tm, w), lambda i: (i, 0))
    widths = (ATT_W, KV_W, KV_W, 3 * HY_W, 3 * SC_W)
    return pl.pallas_call(
        functools.partial(_inproj_kernel, nct=t_ctx // tm),
        grid=(T // tm,),
        in_specs=[
            *_ctx_lat_specs((tm, D), t_ctx // tm),
            pl.BlockSpec((1, 1, 6, D), row),
            pl.BlockSpec((None, D, IN_W), lambda i: (layer, 0, 0)),
        ],
        out_specs=[tok(w) for w in widths],
        out_shape=[jax.ShapeDtypeStruct((T, w), dt) for w, dt in zip(widths, (BF16, F32, F32, BF16, BF16))],
        scratch_shapes=[pltpu.VMEM((D, IN_W), BF16)],
        compiler_params=_cp(("arbitrary",)),
        name="in_projection",
    )(xc, xl, mod, w_in)


def _swap_halves(x):
    w = x.shape[-1]
    lane = lax.broadcasted_iota(jnp.int32, x.shape, 1)
    return jnp.where((lane % HD) < HD // 2, pltpu.roll(x, w - HD // 2, 1), pltpu.roll(x, HD // 2, 1))


def _rope(x, cos, sin_signed):
    return x * cos + _swap_halves(x) * sin_signed


def _group_rows(q, g):
    return jnp.concatenate([q[:, (GQA * g + r) * HD:(GQA * g + r + 1) * HD] for r in range(GQA)], axis=0)


def _sink_column(sink_ref, g, rows):
    ridx = lax.broadcasted_iota(jnp.int32, (GQA * rows, 1), 0)
    col = jnp.full((GQA * rows, 1), sink_ref[GQA * g + GQA - 1], F32)
    for r in range(GQA - 2, -1, -1):
        col = jnp.where(ridx < (r + 1) * rows, sink_ref[GQA * g + r], col)
    return col


def _ungroup(outs, rows):
    return jnp.concatenate([o[r * rows:(r + 1) * rows] for o in outs for r in range(GQA)], axis=1)


def _ctx_attn_kernel(sink_ref, q_ref, k_ref, v_ref, o_ref):
    rows = q_ref.shape[0]
    q = q_ref[...].astype(F32) * (HD ** -0.5)
    k = k_ref[...]
    v = v_ref[...]
    outs = []
    for g in range(N_KV):
        qg = _group_rows(q, g).astype(BF16)
        kg = k[:, g * HD:(g + 1) * HD].astype(BF16)
        vg = v[:, g * HD:(g + 1) * HD].astype(BF16)
        s = _dot_nt(qg, kg)
        sink = _sink_column(sink_ref, g, rows)
        m = jnp.maximum(jnp.max(s, axis=-1, keepdims=True), sink)
        e = jnp.exp(s - m)
        den = jnp.sum(e, axis=-1, keepdims=True) + jnp.exp(sink - m)
        outs.append(_dot(e.astype(BF16), vg) / den)
    o_ref[...] = _ungroup(outs, rows).astype(o_ref.dtype)


def _context_attention(q, k, v, sink, B, L):
    tok = lambda w: pl.BlockSpec((L, w), lambda b: (b, 0))
    return pl.pallas_call(
        _ctx_attn_kernel,
        grid=(B,),
        in_specs=[pl.BlockSpec(memory_space=pltpu.SMEM), tok(ATT_W), tok(KV_W), tok(KV_W)],
        out_specs=tok(ATT_W),
        out_shape=jax.ShapeDtypeStruct((B * L, ATT_W), BF16),
        compiler_params=_cp(("arbitrary",)),
        name="context_attention",
    )(sink, q, k, v)


def _lat_attn_kernel(sink_ref, q_ref, k_ref, v_ref, kc_ref, vc_ref, ck_ref, sk_ref, o_ref,
                     kr_ref, vb_ref, kcb_ref, vcb_ref):
    n = pl.program_id(1)
    L = k_ref.shape[0]
    nk = 3 * BLK

    @pl.when(n == 0)
    def _():
        kr_ref[...] = _rope(k_ref[...], ck_ref[...], sk_ref[...]).astype(BF16)
        vb_ref[...] = v_ref[...].astype(BF16)
        kcb_ref[...] = kc_ref[...].astype(BF16)
        vcb_ref[...] = vc_ref[...].astype(BF16)

    ws = pl.multiple_of(jnp.clip((n - 1) * BLK, 0, L - nk), BLK)
    qs = pl.multiple_of(n * BLK, BLK)
    heads_per_table = N_HEADS // N_KV
    cq = jnp.tile(ck_ref[pl.ds(qs, BLK), :], (1, heads_per_table))
    sq = jnp.tile(sk_ref[pl.ds(qs, BLK), :], (1, heads_per_table))
    q = _rope(q_ref[...].astype(F32), cq, sq) * (HD ** -0.5)
    kl = kr_ref[pl.ds(ws, nk), :]
    vl = vb_ref[pl.ds(ws, nk), :]
    kc = kcb_ref[...]
    vc = vcb_ref[...]
    qpos = n * BLK + lax.broadcasted_iota(jnp.int32, (GQA * BLK, 1), 0) % BLK
    kpos = ws + lax.broadcasted_iota(jnp.int32, (1, nk), 1)
    valid = jnp.abs(kpos - qpos) <= WINDOW
    outs = []
    for g in range(N_KV):
        sl = slice(g * HD, (g + 1) * HD)
        qg = _group_rows(q, g).astype(BF16)
        s_loc = jnp.where(valid, _dot_nt(qg, kl[:, sl]), NEG_INF)
        s_ctx = _dot_nt(qg, kc[:, sl])
        sink = _sink_column(sink_ref, g, BLK)
        m = jnp.maximum(jnp.maximum(jnp.max(s_loc, axis=-1, keepdims=True),
                                    jnp.max(s_ctx, axis=-1, keepdims=True)), sink)
        e_loc = jnp.exp(s_loc - m)
        e_ctx = jnp.exp(s_ctx - m)
        den = (jnp.sum(e_loc, axis=-1, keepdims=True) + jnp.sum(e_ctx, axis=-1, keepdims=True)
               + jnp.exp(sink - m))
        o = _dot(e_loc.astype(BF16), vl[:, sl]) + _dot(e_ctx.astype(BF16), vc[:, sl])
        outs.append(o / den)
    o_ref[...] = _ungroup(outs, BLK).astype(o_ref.dtype)


def _rope_tables(L):
    rows = L // GRID_W
    row = jnp.repeat(jnp.arange(rows, dtype=F32), GRID_W)
    col = jnp.tile(jnp.arange(GRID_W, dtype=F32), rows)
    pairs = HD // 4
    inv = ROPE_BASE ** (-jnp.arange(pairs, dtype=F32) / pairs)
    ang = jnp.concatenate([row[:, None] * inv, col[:, None] * inv], axis=-1)
    cos = jnp.cos(ang)
    sin = jnp.sin(ang)
    return (jnp.tile(jnp.concatenate([cos, cos], axis=-1), (1, N_KV)),
            jnp.tile(jnp.concatenate([-sin, sin], axis=-1), (1, N_KV)))


def _latent_attention(q, k, v, kc, vc, sink, tables, t_ctx, B, L):
    assert t_ctx % L == 0
    P = kc.shape[1]
    ck, sk = tables
    nbk = L // BLK
    seq = pl.BlockSpec((L, KV_W), lambda b, n: (t_ctx // L + b, 0))
    ctx = pl.BlockSpec((None, P, KV_W), lambda b, n: (b, 0, 0))
    return pl.pallas_call(
        _lat_attn_kernel,
        grid=(B, nbk),
        in_specs=[
            pl.BlockSpec(memory_space=pltpu.SMEM),
            pl.BlockSpec((BLK, ATT_W), lambda b, n: (t_ctx // BLK + b * nbk + n, 0)),
            seq, seq, ctx, ctx,
            pl.BlockSpec((L, KV_W), lambda b, n: (0, 0)),
            pl.BlockSpec((L, KV_W), lambda b, n: (0, 0)),
        ],
        out_specs=pl.BlockSpec((BLK, ATT_W), lambda b, n: (b * nbk + n, 0)),
        out_shape=jax.ShapeDtypeStruct((B * L, ATT_W), BF16),
        scratch_shapes=[pltpu.VMEM((L, KV_W), BF16), pltpu.VMEM((L, KV_W), BF16),
                        pltpu.VMEM((P, KV_W), BF16), pltpu.VMEM((P, KV_W), BF16)],
        compiler_params=_cp(("arbitrary", "arbitrary")),
        name="latent_attention",
    )(sink, q, k, v, kc, vc, ck, sk)


def _conv3(u, prev_row, next_row, w):
    tl = u.shape[0]
    ridx = lax.broadcasted_iota(jnp.int32, (tl, 1), 0)
    dn = jnp.where(ridx == 0, prev_row, pltpu.roll(u, 1, 0))
    up = jnp.where(ridx == tl - 1, next_row, pltpu.roll(u, tl - 1, 0))
    return dn * w[0:1] + u * w[1:2] + up * w[2:3]


def _store_time_major_inner(dst_ref, src, scr_ref):
    n2 = dst_ref.shape[0]
    rows = src.shape[0] // n2
    for h in range(scr_ref.shape[0]):
        scr_ref[h] = src[:, h * 128:(h + 1) * 128]
    for j in range(n2):
        for h in range(scr_ref.shape[0]):
            dst_ref[j, :, h * 128:(h + 1) * 128] = scr_ref[h, pl.ds(j, rows, stride=n2), :]


def _mixprep_kernel(uh_ref, uhp_ref, uhn_ref, us_ref, usp_ref, usn_ref, hw_ref, hb_ref, sw_ref,
                    x0_ref, g_ref, gt_ref, osc_ref, scr_ref):
    t = pl.program_id(1)
    first = t == 0
    last = t == pl.num_programs(1) - 1
    c = HY_W
    hw = hw_ref[...]
    halo = uhp_ref.shape[0]
    f32 = lambda a: a.astype(F32)
    uh = _conv3(f32(uh_ref[...]), jnp.where(first, 0.0, f32(uhp_ref[halo - 1:halo])),
                jnp.where(last, 0.0, f32(uhn_ref[0:1])), hw)
    uh = uh + hb_ref[...]
    x0_ref[...] = uh[:, :c]
    g = uh[:, c:2 * c] * uh[:, 2 * c:]
    g_ref[...] = g
    _store_time_major_inner(gt_ref, g, scr_ref)
    us = f32(us_ref[...])
    usp = f32(usp_ref[halo - 1:halo])
    usn = f32(usn_ref[0:1])
    prod = us[:, c:2 * c] * us[:, 2 * c:]
    pprev = jnp.where(first, 0.0, usp[:, c:2 * c] * usp[:, 2 * c:])
    pnext = jnp.where(last, 0.0, usn[:, c:2 * c] * usn[:, 2 * c:])
    osc_ref[...] = (us[:, :c] * _conv3(prod, pprev, pnext, sw_ref[...])).astype(osc_ref.dtype)


def _mixer_prep(u_hy, u_sc, hy_conv_w, hy_conv_b, sc_conv_w, row0, B, L):
    W = u_hy.shape[1]
    n1, n2 = _fft_split(L)
    tl = 8 * n2
    assert row0 % tl == 0 and L % tl == 0
    nt = L // tl
    halo = 16
    rh = tl // halo
    base = lambda b: row0 // halo + b * (L // halo)
    main = pl.BlockSpec((tl, W), lambda b, t: (row0 // tl + b * nt + t, 0))
    prev = pl.BlockSpec((halo, W), lambda b, t: (base(b) + jnp.maximum(t * rh - 1, 0), 0))
    nxt = pl.BlockSpec((halo, W), lambda b, t: (base(b) + jnp.minimum((t + 1) * rh, L // halo - 1), 0))
    const = lambda shape: pl.BlockSpec(shape, lambda b, t: (0, 0))
    out = pl.BlockSpec((tl, HY_W), lambda b, t: (b * nt + t, 0))
    flat = jax.ShapeDtypeStruct((B * L, HY_W), F32)
    return pl.pallas_call(
        _mixprep_kernel,
        grid=(B, nt),
        in_specs=[main, prev, nxt, main, prev, nxt, const((3, W)), const((1, W)), const((3, SC_W))],
        out_specs=[out, out, pl.BlockSpec((None, n2, 8, HY_W), lambda b, t: (b, 0, t, 0)), out],
        out_shape=[flat, flat, jax.ShapeDtypeStruct((B, n2, n1 // 2, HY_W), F32),
                   jax.ShapeDtypeStruct((B * L, SC_W), BF16)],
        scratch_shapes=[pltpu.VMEM((HY_W // 128, tl, 128), F32)],
        compiler_params=_cp(("arbitrary", "arbitrary")),
        name="mixer_prep",
    )(u_hy, u_hy, u_hy, u_sc, u_sc, u_sc, hy_conv_w, hy_conv_b.reshape(1, W), sc_conv_w)


def _filter_mlp(z, w1_ref, b1_ref, f1_ref, w2_ref, b2_ref, f2_ref, w3_ref, dl_ref, row0):
    tl = z.shape[0]
    h = jnp.sin(f1_ref[...] * (_dot(z, w1_ref[...], HI) + b1_ref[...]))
    h = jnp.sin(f2_ref[...] * (_dot(h, w2_ref[...], HI) + b2_ref[...]))
    h = _dot(h, w3_ref[...], HI) * jnp.exp(-z[:, 0:1] * jnp.abs(dl_ref[...]))
    row = row0 + lax.broadcasted_iota(jnp.int32, (tl, 1), 0)
    return h[:, :HY_W], jnp.where(row == 0, 0.0, h[:, HY_W:])


def _filter_kernel(z_ref, w1_ref, b1_ref, f1_ref, w2_ref, b2_ref, f2_ref, w3_ref, dl_ref, o_ref, scr_ref):
    hf, hb = _filter_mlp(z_ref[...], w1_ref, b1_ref, f1_ref, w2_ref, b2_ref, f2_ref, w3_ref, dl_ref,
                         pl.program_id(0) * z_ref.shape[0])
    _store_time_major_inner(o_ref.at[0], hf, scr_ref)
    _store_time_major_inner(o_ref.at[1], hb, scr_ref)


def _filter_features(L):
    t = jnp.linspace(0.0, 1.0, L, dtype=F32)[:, None]
    w = 2.0 * math.pi * jnp.arange(L, dtype=F32)[:, None] / L
    bands = jnp.linspace(1e-4, FILTER_BANDS - 1, FILTER_BANDS, dtype=F32)[None, :]
    z = jnp.concatenate([t, jnp.cos(bands * w), -jnp.sin(bands * w)], axis=-1)
    return jnp.pad(z, ((0, 0), (0, 128 - FILTER_EMB)))


def _filter_mlp_operands(w1, b1, f1, w2, b2, f2, w3, deltas):
    H = FILTER_HIDDEN
    ops = (jnp.pad(w1, ((0, 128 - FILTER_EMB), (0, 0))), b1.reshape(1, H), f1.reshape(1, H), w2, b2.reshape(1, H),
           f2.reshape(1, H), w3, deltas.reshape(1, 2 * HY_W))
    specs = [pl.BlockSpec(o.shape, lambda *_: (0, 0)) for o in ops]
    return ops, specs


def _hyena_filters(L, mlp):
    n1, n2 = _fft_split(L)
    tl = 8 * n2
    ops, specs = _filter_mlp_operands(*mlp)
    return pl.pallas_call(
        _filter_kernel,
        grid=(L // tl,),
        in_specs=[pl.BlockSpec((tl, 128), lambda i: (i, 0)), *specs],
        out_specs=pl.BlockSpec((2, n2, 8, HY_W), lambda i: (0, 0, i, 0)),
        out_shape=jax.ShapeDtypeStruct((2, n2, n1 // 2, HY_W), F32),
        scratch_shapes=[pltpu.VMEM((HY_W // 128, tl, 128), F32)],
        compiler_params=_cp(("arbitrary",)),
        name="hyena_filters",
    )(_filter_features(L), *ops)


def _dense_dft_constants(L):
    n = 2 * L
    a = (jnp.arange(n, dtype=jnp.int32)[:, None] * jnp.arange(L, dtype=jnp.int32)[None, :]) % n
    th = (2.0 * math.pi / n) * a.astype(F32)
    fwd = jnp.concatenate([jnp.cos(th), -jnp.sin(th)], axis=0)
    inv = jnp.concatenate([jnp.cos(th).T, -jnp.sin(th).T], axis=1) * (1.0 / n)
    return fwd.astype(BF16), inv.astype(BF16)


def _ctx_mixers_kernel(uh_ref, us_ref, hw_ref, hb_ref, sw_ref, bias_ref, fwd_ref, inv_ref, z_ref,
                       w1_ref, b1_ref, f1_ref, w2_ref, b2_ref, f2_ref, w3_ref, dl_ref, ohy_ref, osc_ref, kf_ref):
    c = HY_W
    nf = kf_ref.shape[1]

    @pl.when(pl.program_id(0) == 0)
    def _():
        hf, hb = _filter_mlp(z_ref[...], w1_ref, b1_ref, f1_ref, w2_ref, b2_ref, f2_ref, w3_ref, dl_ref, 0)
        xf = _dot(fwd_ref[...], hf.astype(BF16))
        xb = _dot(fwd_ref[...], hb.astype(BF16))
        kf_ref[0] = xf[:nf] + xb[:nf]
        kf_ref[1] = xf[nf:] - xb[nf:]

    zero = jnp.zeros((1, 1), F32)
    uh = _conv3(uh_ref[...].astype(F32), zero, zero, hw_ref[...]) + hb_ref[...]
    x0, g = uh[:, :c], uh[:, c:2 * c] * uh[:, 2 * c:]
    s = _dot(fwd_ref[...], g.astype(BF16))
    sr, si = s[:nf], s[nf:]
    kr, ki = kf_ref[0], kf_ref[1]
    y = _dot(inv_ref[...], jnp.concatenate([sr * kr - si * ki, sr * ki + si * kr], axis=0).astype(BF16))
    ohy_ref[...] = (x0 * (y + g * bias_ref[...])).astype(ohy_ref.dtype)
    us = us_ref[...].astype(F32)
    osc_ref[...] = (us[:, :c] * _conv3(us[:, c:2 * c] * us[:, 2 * c:], zero, zero, sw_ref[...])).astype(osc_ref.dtype)


def _ctx_mixers(u_hy, u_sc, hy_conv_w, hy_conv_b, sc_conv_w, hy_bias, mlp, B, L):
    W = u_hy.shape[1]
    fwd, inv = _dense_dft_constants(L)
    ops, specs = _filter_mlp_operands(*mlp)
    const = lambda shape: pl.BlockSpec(shape, lambda b: (0, 0))
    seq = lambda w: pl.BlockSpec((L, w), lambda b: (b, 0))
    return pl.pallas_call(
        _ctx_mixers_kernel,
        grid=(B,),
        in_specs=[seq(W), seq(W), const((3, W)), const((1, W)), const((3, SC_W)), const((1, HY_W)),
                  const(fwd.shape), const(inv.shape), const((L, 128)), *specs],
        out_specs=[seq(HY_W), seq(SC_W)],
        out_shape=[jax.ShapeDtypeStruct((B * L, HY_W), BF16), jax.ShapeDtypeStruct((B * L, SC_W), BF16)],
        scratch_shapes=[pltpu.VMEM((2, 2 * L, HY_W), F32)],
        compiler_params=_cp(("arbitrary",)),
        name="context_mixers",
    )(u_hy, u_sc, hy_conv_w, hy_conv_b.reshape(1, W), sc_conv_w, hy_bias.reshape(1, HY_W), fwd, inv,
      _filter_features(L), *ops)


def _fft_split(L):
    n = 2 * L
    n1 = 128 if n >= 8192 else 32
    return n1, n // n1


def _fft_constants(L):
    n1, n2 = _fft_split(L)
    n = n1 * n2
    k1 = jnp.arange(n1, dtype=jnp.int32)
    a = (k1[:, None] * jnp.arange(n1 // 2, dtype=jnp.int32)[None, :]) % n1
    th = (2.0 * math.pi / n1) * a.astype(F32)
    f1 = jnp.concatenate([jnp.cos(th), -jnp.sin(th)], axis=0)
    f3 = jnp.concatenate([jnp.cos(th).T, -jnp.sin(th).T], axis=1) * (1.0 / n)
    k = k1[:, None, None] + n1 * jnp.arange(n2, dtype=jnp.int32)[None, :, None]
    ph = (k * jnp.arange(n2, dtype=jnp.int32)[None, None, :]) % n
    ph = (2.0 * math.pi / n) * ph.astype(F32)
    mr, mi = jnp.cos(ph), -jnp.sin(ph)
    mf = jnp.concatenate([jnp.concatenate([mr, -mi], axis=2), jnp.concatenate([mi, mr], axis=2)], axis=1)
    return tuple(c.astype(BF16) for c in (f1, f3, mf))


FFT_GROUP = 8
FFT_MID_GROUP = 2 * FFT_GROUP


def _dft_rows_kernel(f_ref, x_ref, o_ref):
    for i in range(FFT_GROUP):
        o_ref[:, i, :] = _dot(f_ref[...], x_ref[i].astype(BF16))


def _dft_rows(f1, xt):
    B, n2, kh, C = xt.shape
    m = f1.shape[0]
    return pl.pallas_call(
        _dft_rows_kernel,
        grid=(B, n2 // FFT_GROUP),
        in_specs=[pl.BlockSpec((m, kh), lambda b, j: (0, 0)),
                  pl.BlockSpec((None, FFT_GROUP, kh, C), lambda b, j: (b, j, 0, 0))],
        out_specs=pl.BlockSpec((None, m, FFT_GROUP, C), lambda b, j: (b, 0, j, 0)),
        out_shape=jax.ShapeDtypeStruct((B, m, n2, C), F32),
        compiler_params=_cp(("arbitrary", "arbitrary")),
        name="fft_rows",
    )(f1, xt)


def _fft_mid_kernel(a_ref, hf_ref, hb_ref, mf_ref, o_ref, kf_ref):
    n2 = a_ref.shape[2]

    @pl.when(pl.program_id(1) == 0)
    def _():
        for i in range(FFT_MID_GROUP):
            xf = _dot(mf_ref[i], jnp.concatenate([hf_ref[0, i], hf_ref[1, i]], axis=0).astype(BF16))
            xb = _dot(mf_ref[i], jnp.concatenate([hb_ref[0, i], hb_ref[1, i]], axis=0).astype(BF16))
            kf_ref[i, 0] = xf[:n2] + xb[:n2]
            kf_ref[i, 1] = xf[n2:] - xb[n2:]

    for i in range(FFT_MID_GROUP):
        x = _dot(mf_ref[i], jnp.concatenate([a_ref[0, i], a_ref[1, i]], axis=0).astype(BF16))
        xr, xi = x[:n2], x[n2:]
        kr, ki = kf_ref[i, 0], kf_ref[i, 1]
        y = jnp.concatenate([xr * kr - xi * ki, xr * ki + xi * kr], axis=0)
        o = lax.dot_general(mf_ref[i], y.astype(BF16), (((0,), (0,)), ((), ())), preferred_element_type=F32)
        o_ref[:, 0, i, :] = o[:n2]
        o_ref[:, 1, i, :] = o[n2:]


def _fft_mid(a, ah, mf):
    B, _, n1, n2, C = a.shape
    blk = (None, 2, FFT_MID_GROUP, n2, C)
    return pl.pallas_call(
        _fft_mid_kernel,
        grid=(n1 // FFT_MID_GROUP, B),
        in_specs=[
            pl.BlockSpec(blk, lambda k, b: (b, 0, k, 0, 0)),
            pl.BlockSpec(blk, lambda k, b: (0, 0, k, 0, 0)),
            pl.BlockSpec(blk, lambda k, b: (1, 0, k, 0, 0)),
            pl.BlockSpec((FFT_MID_GROUP, 2 * n2, 2 * n2), lambda k, b: (k, 0, 0)),
        ],
        out_specs=pl.BlockSpec((None, n2, 2, FFT_MID_GROUP, C), lambda k, b: (b, 0, 0, k, 0)),
        out_shape=jax.ShapeDtypeStruct((B, n2, 2, n1, C), F32),
        scratch_shapes=[pltpu.VMEM((FFT_MID_GROUP, 2, n2, C), F32)],
        compiler_params=_cp(("arbitrary", "arbitrary")),
        name="fft_mid",
    )(a, ah, ah, mf)


def _idft_rows_kernel(f_ref, b_ref, x0_ref, g_ref, bias_ref, o_ref):
    for i in range(FFT_GROUP):
        o_ref[:, i, :] = _dot(f_ref[...], b_ref[i].astype(BF16))
    o_ref[...] = x0_ref[...] * (o_ref[...] + g_ref[...] * bias_ref[...])


def _idft_rows_gate(f3, bo, x0, g, bias):
    B, n2, m2, C = bo.shape
    kh = f3.shape[0]
    tile = pl.BlockSpec((None, kh, FFT_GROUP, C), lambda b, j: (b, 0, j, 0))
    return pl.pallas_call(
        _idft_rows_kernel,
        grid=(B, n2 // FFT_GROUP),
        in_specs=[pl.BlockSpec((kh, m2), lambda b, j: (0, 0)),
                  pl.BlockSpec((None, FFT_GROUP, m2, C), lambda b, j: (b, j, 0, 0)),
                  tile, tile, pl.BlockSpec((1, C), lambda b, j: (0, 0))],
        out_specs=tile,
        out_shape=jax.ShapeDtypeStruct((B, kh, n2, C), F32),
        compiler_params=_cp(("arbitrary", "arbitrary")),
        name="ifft_rows_gate",
    )(f3, bo, x0, g, bias)


def _hyena_long_conv(x0, g, gt, filt_t, hy_bias, consts, B, L):
    C = HY_W
    n1, n2 = _fft_split(L)
    f1, f3, mf = consts
    a = _dft_rows(f1, gt).reshape(B, 2, n1, n2, C)
    ah = _dft_rows(f1, filt_t).reshape(2, 2, n1, n2, C)
    bo = _fft_mid(a, ah, mf).reshape(B, n2, 2 * n1, C)
    nat = lambda z: z.reshape(B, n1 // 2, n2, C)
    out = _idft_rows_gate(f3, bo, nat(x0), nat(g), hy_bias.reshape(1, C))
    return out.reshape(B * L, C)


def _outproj_kernel(oac_ref, oal_ref, ohc_ref, ohl_ref, osc_ref, osl_ref, xc_ref, xl_ref, mod_ref, w_ref,
                    lg_ref, lb_ref, rw_ref, rb_ref, x1_ref, h2_ref, meta_ref, cnt_ref, wbf_ref, *, nct):
    @pl.when(pl.program_id(0) == 0)
    def _():
        wbf_ref[...] = w_ref[...].astype(BF16)

    m = mod_ref[0, 0]
    w = wbf_ref[...]
    is_ctx = pl.program_id(0) < nct
    pick = lambda c_ref, l_ref, dt: jnp.where(is_ctx, c_ref[...].astype(dt), l_ref[...].astype(dt))
    o = (_dot(pick(oac_ref, oal_ref, BF16), w[:ATT_W])
         + _dot(pick(ohc_ref, ohl_ref, BF16), w[ATT_W:ATT_W + HY_W])
         + _dot(pick(osc_ref, osl_ref, BF16), w[ATT_W + HY_W:]))
    x1 = _layer_norm(DN_ALPHA * pick(xc_ref, xl_ref, F32) + m[2:3] * o, lg_ref[...], lb_ref[...])
    x1_ref[...] = x1
    h2 = x1 * (1.0 + m[4:5]) + m[3:4]
    h2_hi = h2.astype(BF16)
    h2_ref[...] = h2_hi
    h2_lo = (h2 - h2_hi.astype(F32)).astype(BF16)
    logits = _dot(h2_hi, rw_ref[0]) + _dot(h2_lo, rw_ref[0]) + _dot(h2_hi, rw_ref[1]) + rb_ref[...]
    for sb in range(h2.shape[0] // TOKEN_BLOCK):
        rows = slice(sb * TOKEN_BLOCK, (sb + 1) * TOKEN_BLOCK)
        meta, cnt = _route_block(logits[rows])
        meta_ref[rows, :] = meta
        cnt_ref[sb] = jnp.broadcast_to(cnt, (8, 128))


def _route_block(logits):
    lane = lax.broadcasted_iota(jnp.int32, logits.shape, 1)
    logits = jnp.where(lane < N_EXP, logits, NEG_INF)
    picks = []
    vals = []
    for _ in range(TOP_K):
        v = jnp.max(logits, axis=-1, keepdims=True)
        idx = jnp.min(jnp.where(logits == v, lane, 128), axis=-1, keepdims=True)
        hit = lane == idx
        picks.append(hit)
        vals.append(v)
        logits = jnp.where(hit, NEG_INF, logits)
    es = [jnp.exp(v - vals[0]) for v in vals]
    den = es[0] + es[1] + es[2] + es[3]
    tb = logits.shape[0]
    msel = jnp.zeros(logits.shape, F32)
    for hit in picks:
        msel = msel + jnp.where(hit, 1.0, 0.0)
    cnt = jnp.sum(msel, axis=0, keepdims=True)
    pc = jnp.floor((cnt + (ROW_PAD - 1.0)) * (1.0 / ROW_PAD)) * ROW_PAD
    upper = jnp.where(lax.broadcasted_iota(jnp.int32, (128, 128), 0)
                      < lax.broadcasted_iota(jnp.int32, (128, 128), 1), 1.0, 0.0)
    lo = _dot(jnp.broadcast_to(pc, (8, 128)), upper, HI)[0:1]
    lower = jnp.where(lax.broadcasted_iota(jnp.int32, (tb, tb), 1)
                      < lax.broadcasted_iota(jnp.int32, (tb, tb), 0), 1.0, 0.0).astype(BF16)
    pos = lo + _dot(lower, msel.astype(BF16))
    meta = jnp.zeros(logits.shape, F32)
    for kk, (hit, e) in enumerate(zip(picks, es)):
        slot = jnp.sum(jnp.where(hit, pos, 0.0), axis=-1, keepdims=True)
        meta = jnp.where(lane == kk, slot, meta)
        meta = jnp.where(lane == TOP_K + kk, e / den, meta)
    return meta, cnt


def _out_projection(o_attn, o_hy, o_sc, x, mod, w_out, ln_g, ln_b, router_w, router_b, layer, dec_seq):
    t_ctx = x[0].shape[0]
    T = t_ctx + x[1].shape[0]
    blocks_per_step = 2
    tm = blocks_per_step * TOKEN_BLOCK
    nct = t_ctx // tm
    row = lambda i: (layer, _mod_row(i, tm, t_ctx, dec_seq), 0, 0)
    tok = lambda w: pl.BlockSpec((tm, w), lambda i: (i, 0))
    const = lambda shape: pl.BlockSpec(shape, lambda i: (0, 0))
    rw = jnp.pad(router_w, ((0, 0), (0, 128 - N_EXP)))
    rw_hi = rw.astype(BF16)
    rw = jnp.stack([rw_hi, (rw - rw_hi.astype(F32)).astype(BF16)])
    rb = jnp.pad(router_b, (0, 128 - N_EXP)).reshape(1, 128)
    return pl.pallas_call(
        functools.partial(_outproj_kernel, nct=nct),
        grid=(T // tm,),
        in_specs=[*_ctx_lat_specs((tm, ATT_W), nct), *_ctx_lat_specs((tm, HY_W), nct),
                  *_ctx_lat_specs((tm, SC_W), nct), *_ctx_lat_specs((tm, D), nct),
                  pl.BlockSpec((1, 1, 6, D), row),
                  pl.BlockSpec((None, D, D), lambda i: (layer, 0, 0)), const((1, D)), const((1, D)),
                  pl.BlockSpec((2, D, 128), lambda i: (0, 0, 0)), const((1, 128))],
        out_specs=[tok(D), tok(D), tok(128), pl.BlockSpec((blocks_per_step, 8, 128), lambda i: (i, 0, 0))],
        out_shape=[jax.ShapeDtypeStruct((T, D), F32), jax.ShapeDtypeStruct((T, D), BF16),
                   jax.ShapeDtypeStruct((T, 128), F32), jax.ShapeDtypeStruct((T // TOKEN_BLOCK, 8, 128), F32)],
        scratch_shapes=[pltpu.VMEM((D, D), BF16)],
        compiler_params=_cp(("arbitrary",)),
        name="out_projection_router",
    )(*o_attn, *o_hy, *o_sc, *x, mod, w_out, ln_g.reshape(1, D), ln_b.reshape(1, D), rw, rb)


def _routing_tables(cnt):
    nb = cnt.shape[0]
    ex = jnp.arange(N_EXP, dtype=jnp.int32)
    before_e = (ex[:, None] < ex[None, :]).astype(jnp.int32)
    bl = jnp.arange(nb, dtype=jnp.int32)
    before_b = (bl[None, :] < bl[:, None]).astype(jnp.int32)
    pc = (cnt + ROW_PAD - 1) // ROW_PAD * ROW_PAD
    lo = jnp.sum(pc[:, :, None] * before_e[None], axis=1)
    tot = jnp.sum(pc, axis=0)
    tot_pad = (tot + EXPERT_TILE - 1) // EXPERT_TILE * EXPERT_TILE
    off = jnp.sum(tot_pad[:, None] * before_e, axis=0)
    gstart = off[None, :] + jnp.sum(before_b[:, :, None] * pc[None], axis=1)
    ntile = tot_pad // EXPERT_TILE
    cum = jnp.sum(ntile[:, None] * before_e, axis=0) + ntile
    n_tiles = _max_expert_tiles(nb)
    i = jnp.arange(n_tiles, dtype=jnp.int32)
    ic = jnp.maximum(jnp.minimum(i, cum[-1] - 1), 0)
    te = jnp.sum((cum[None, :] <= ic[:, None]).astype(jnp.int32), axis=1)
    owner = (te[:, None] == ex[None, :]).astype(jnp.int32)
    pick = lambda per_expert: jnp.sum(owner * per_expert[None, :], axis=1)
    first_tile = pick(cum - ntile)
    first = ic == first_tile
    used = jnp.clip(pick(tot) - (ic - first_tile) * EXPERT_TILE, 0, EXPERT_TILE)
    part = EXPERT_TILE // TILE_PARTS
    parts = jnp.clip((used + part - 1) // part, 1, TILE_PARTS)
    flags = jnp.where(i < cum[-1], 1 + 2 * first.astype(jnp.int32) + 4 * parts, 0)
    later = (ex[None, :] > ex[:, None]) & (ntile[None, :] > 0)
    next_e = jnp.min(jnp.where(later, ex[None, :], N_EXP), axis=1)
    next_e = jnp.where(next_e < N_EXP, next_e, -1)
    i32 = lambda a: a.astype(jnp.int32).reshape(-1)
    return dict(pc=i32(pc), lo=i32(lo), gstart=i32(gstart), tail_start=i32(off + tot), tail_len=i32(tot_pad - tot),
                rows=i32(jnp.sum(pc, axis=1)), tile_expert=i32(te), tile_row=ic, tile_flags=flags,
                tile_next=i32(pick(next_e)))


def _max_expert_tiles(nb):
    rows = TOP_K * nb * TOKEN_BLOCK + nb * N_EXP * (ROW_PAD - 1) + N_EXP * (EXPERT_TILE - 1)
    return rows // EXPERT_TILE + 1


def _run_copies(n, src_ref, src0, dst_ref, dst0, sem, wait):
    done = jnp.int32(0)
    for p in RUN_PIECES:
        take = (n & p) != 0

        @pl.when(take)
        def _():
            cp = pltpu.make_async_copy(src_ref.at[pl.ds(pl.multiple_of(src0 + done, ROW_PAD), p), :],
                                       dst_ref.at[pl.ds(pl.multiple_of(dst0 + done, ROW_PAD), p), :], sem)
            if wait:
                cp.wait()
            else:
                cp.start()

        done = done + jnp.where(take, p, 0)


def _wait_rows(n, vmem_ref, hbm_ref, sem, to_vmem):
    base = TOP_K * TOKEN_BLOCK

    def wait(p):
        src, dst = (hbm_ref, vmem_ref) if to_vmem else (vmem_ref, hbm_ref)
        pltpu.make_async_copy(src.at[pl.ds(0, p), :], dst.at[pl.ds(0, p), :], sem).wait()

    wait(base)
    for p in RUN_PIECES[1:]:
        pl.when(((n - base) & p) != 0)(functools.partial(wait, p))


SORTED_W = D + 128


def _dispatch_kernel(pc_ref, lo_ref, gs_ref, ts_ref, tl_ref, rows_ref, h_ref, meta_ref, xs_ref,
                     buf_ref, zero_ref, sem, zsem):
    b = pl.program_id(0)
    nb = pl.num_programs(0)
    s = buf_ref.shape[1]
    slot = b % 2

    @pl.when(b == 0)
    def _():
        zero_ref[...] = jnp.zeros_like(zero_ref)
        for wait in (False, True):
            def tail(e, carry):
                _run_copies(tl_ref[e], zero_ref, 0, xs_ref, ts_ref[e], zsem, wait)
                return carry
            lax.fori_loop(0, N_EXP, tail, 0)

    mt = meta_ref[...].T
    srow = lax.broadcasted_iota(jnp.int32, (s, 1), 0).astype(F32)
    perm = jnp.zeros((s, TOKEN_BLOCK), F32)
    gperm = jnp.zeros((s, TOKEN_BLOCK), F32)
    for k in range(TOP_K):
        hit = srow == mt[k:k + 1]
        perm = jnp.where(hit, 1.0, perm)
        gperm = jnp.where(hit, mt[TOP_K + k:TOP_K + k + 1], gperm)
    sorted_rows = _dot(perm.astype(BF16), h_ref[...])
    gate_col = jnp.sum(gperm, axis=-1, keepdims=True)

    @pl.when(b >= 2)
    def _():
        _wait_rows(rows_ref[b - 2], buf_ref.at[slot], xs_ref, sem.at[slot], to_vmem=False)

    buf = buf_ref.at[slot]
    buf[:, :D] = sorted_rows
    buf[:, D:] = jnp.broadcast_to(gate_col, (s, 128))

    def run(e, carry):
        j = b * N_EXP + e
        _run_copies(pc_ref[j], buf, lo_ref[j], xs_ref, gs_ref[j], sem.at[slot], wait=False)
        return carry
    lax.fori_loop(0, N_EXP, run, 0)

    @pl.when(b == nb - 1)
    def _():
        @pl.when(b >= 1)
        def _():
            _wait_rows(rows_ref[b - 1], buf_ref.at[1 - slot], xs_ref, sem.at[1 - slot], to_vmem=False)
        _wait_rows(rows_ref[b], buf, xs_ref, sem.at[slot], to_vmem=False)


def _dispatch(tabs, h2, meta):
    T = h2.shape[0]
    nb = T // TOKEN_BLOCK
    rows = _max_expert_tiles(nb) * EXPERT_TILE
    tok = lambda w: pl.BlockSpec((TOKEN_BLOCK, w), lambda b, *_: (b, 0))
    return pl.pallas_call(
        _dispatch_kernel,
        grid_spec=pltpu.PrefetchScalarGridSpec(
            num_scalar_prefetch=6,
            grid=(nb,),
            in_specs=[tok(D), tok(128)],
            out_specs=pl.BlockSpec(memory_space=pl.ANY),
            scratch_shapes=[pltpu.VMEM((2, LOCAL_SLOTS, SORTED_W), F32), pltpu.VMEM((EXPERT_TILE, SORTED_W), F32),
                            pltpu.SemaphoreType.DMA((2,)), pltpu.SemaphoreType.DMA(())],
        ),
        out_shape=jax.ShapeDtypeStruct((rows, SORTED_W), F32),
        compiler_params=_cp(("arbitrary",)),
        name="moe_dispatch",
    )(tabs["pc"], tabs["lo"], tabs["gstart"], tabs["tail_start"], tabs["tail_len"], tabs["rows"], h2, meta)


def _expert_ffn_kernel(te_ref, tr_ref, fl_ref, nx_ref, x_ref, wg_hbm, bg_ref, wu_hbm, bu_ref, wd_hbm, bd_ref, y_ref,
                       wf_ref, wbf_ref, sem, *, layer):
    i = pl.program_id(0)
    fl = fl_ref[i]
    weights = (wg_hbm, wu_hbm, wd_hbm)

    def weight_copy(j, expert):
        return pltpu.make_async_copy(weights[j].at[layer, expert], wf_ref.at[j], sem.at[j])

    @pl.when((fl & 1) != 0)
    def _():
        @pl.when((fl & 2) != 0)
        def _():
            @pl.when(i == 0)
            def _():
                for j in range(3):
                    weight_copy(j, te_ref[i]).start()

            for j in range(3):
                weight_copy(j, te_ref[i]).wait()
                wbf_ref[j] = wf_ref[j].astype(BF16)

            @pl.when(nx_ref[i] >= 0)
            def _():
                for j in range(3):
                    weight_copy(j, nx_ref[i]).start()

        def ffn(rows):
            x = x_ref[:rows, :D].astype(BF16)
            gate = x_ref[:rows, D:D + 1]
            g = jnp.minimum(_dot(x, wbf_ref[0]) + bg_ref[0], SWIGLU_LIMIT)
            u = jnp.clip(_dot(x, wbf_ref[1]) + bu_ref[0], -SWIGLU_LIMIT, SWIGLU_LIMIT)
            a = (u + 1.0) * (g * jax.nn.sigmoid(SWIGLU_ALPHA * g))
            y_ref[:rows, :] = gate * (_dot(a.astype(BF16), wbf_ref[2]) + bd_ref[0])

        quarters = fl >> 2
        for q in range(1, TILE_PARTS + 1):
            pl.when(quarters == q)(functools.partial(ffn, q * (EXPERT_TILE // TILE_PARTS)))


def _expert_ffn(tabs, xs, w_gate, b_gate, w_up, b_up, w_down, b_down, layer):
    rows = xs.shape[0]
    hbm = pl.BlockSpec(memory_space=pl.ANY)
    bspec = pl.BlockSpec((1, 1, D), lambda i, te, *_: (te[i], 0, 0))
    return pl.pallas_call(
        functools.partial(_expert_ffn_kernel, layer=layer),
        grid_spec=pltpu.PrefetchScalarGridSpec(
            num_scalar_prefetch=4,
            grid=(rows // EXPERT_TILE,),
            in_specs=[pl.BlockSpec((EXPERT_TILE, SORTED_W), lambda i, te, tr, *_: (tr[i], 0)),
                      hbm, bspec, hbm, bspec, hbm, bspec],
            out_specs=pl.BlockSpec((EXPERT_TILE, D), lambda i, te, tr, *_: (tr[i], 0)),
            scratch_shapes=[pltpu.VMEM((3, D, D), F32), pltpu.VMEM((3, D, D), BF16), pltpu.SemaphoreType.DMA((3,))],
        ),
        out_shape=jax.ShapeDtypeStruct((rows, D), F32),
        compiler_params=_cp(("arbitrary",)),
        name="expert_ffn",
    )(tabs["tile_expert"], tabs["tile_row"], tabs["tile_flags"], tabs["tile_next"], xs,
      w_gate, b_gate[layer].reshape(N_EXP, 1, D), w_up, b_up[layer].reshape(N_EXP, 1, D),
      w_down, b_down[layer].reshape(N_EXP, 1, D))


def _combine_kernel(pc_ref, lo_ref, gs_ref, rows_ref, ys_ref, meta_ref, x1_ref, mod_ref, lg_ref, lb_ref,
                    oc_ref, ol_ref, buf_ref, sem, *, nct):
    b = pl.program_id(0)
    nb = pl.num_programs(0)
    s = buf_ref.shape[1]
    slot = b % 2

    def fetch(blk, to_slot):
        def run(e, carry):
            j = blk * N_EXP + e
            _run_copies(pc_ref[j], ys_ref, gs_ref[j], buf_ref.at[to_slot], lo_ref[j], sem.at[to_slot], wait=False)
            return carry
        lax.fori_loop(0, N_EXP, run, 0)

    @pl.when(b == 0)
    def _():
        buf_ref[...] = jnp.zeros_like(buf_ref)
        fetch(b, slot)

    @pl.when(b + 1 < nb)
    def _():
        fetch(b + 1, 1 - slot)

    _wait_rows(rows_ref[b], buf_ref.at[slot], ys_ref, sem.at[slot], to_vmem=True)

    meta = meta_ref[...]
    scol = lax.broadcasted_iota(jnp.int32, (1, s), 1).astype(F32)
    w = jnp.zeros((TOKEN_BLOCK, s), F32)
    for k in range(TOP_K):
        w = jnp.where(meta[:, k:k + 1] == scol, 1.0, w)
    w = w.astype(BF16)
    moe = _dot(w, buf_ref[slot].astype(BF16))
    m = mod_ref[0, 0]
    out = _layer_norm(DN_ALPHA * x1_ref[...] + m[5:6] * moe, lg_ref[...], lb_ref[...])

    @pl.when(b < nct)
    def _():
        oc_ref[...] = out

    @pl.when(b >= nct)
    def _():
        ol_ref[...] = out


def _combine_ln2(tabs, ys, meta, x1, mod, ln_g, ln_b, layer, t_ctx, dec_seq):
    T = x1.shape[0]
    nct = t_ctx // TOKEN_BLOCK
    row = lambda b, *_: (layer, _mod_row(b, TOKEN_BLOCK, t_ctx, dec_seq), 0, 0)
    tok = lambda w: pl.BlockSpec((TOKEN_BLOCK, w), lambda b, *_: (b, 0))
    const = lambda shape: pl.BlockSpec(shape, lambda b, *_: (0, 0))
    return pl.pallas_call(
        functools.partial(_combine_kernel, nct=nct),
        grid_spec=pltpu.PrefetchScalarGridSpec(
            num_scalar_prefetch=4,
            grid=(T // TOKEN_BLOCK,),
            in_specs=[pl.BlockSpec(memory_space=pl.ANY), tok(128), tok(D), pl.BlockSpec((1, 1, 6, D), row),
                      const((1, D)), const((1, D))],
            out_specs=list(_ctx_lat_specs((TOKEN_BLOCK, D), nct)),
            scratch_shapes=[pltpu.VMEM((2, LOCAL_SLOTS, D), F32), pltpu.SemaphoreType.DMA((2,))],
        ),
        out_shape=[jax.ShapeDtypeStruct((t_ctx, D), F32), jax.ShapeDtypeStruct((T - t_ctx, D), F32)],
        compiler_params=_cp(("arbitrary",)),
        name="moe_combine_ln2",
    )(tabs["pc"], tabs["lo"], tabs["gstart"], tabs["rows"], ys, meta, x1, mod, ln_g.reshape(1, D),
      ln_b.reshape(1, D))


def _moe_ln2(h2, meta, cnt, x1, mod, w_gate, b_gate, w_up, b_up, w_down, b_down, ln_g, ln_b, layer, t_ctx, dec_seq):
    tabs = _routing_tables(cnt[:, 0, :N_EXP].astype(jnp.int32))
    xs = _dispatch(tabs, h2, meta)
    ys = _expert_ffn(tabs, xs, w_gate, b_gate, w_up, b_up, w_down, b_down, layer)
    return _combine_ln2(tabs, ys, meta, x1, mod, ln_g, ln_b, layer, t_ctx, dec_seq)


def kernel(x_prompt, x_sample, cache_k, cache_v, c, c_ctx, w_mod, b_mod, w_in, w_out, attn_sink, hy_conv_w, hy_conv_b, hy_w1, hy_b1, hy_f1, hy_w2, hy_b2, hy_f2, hy_w3, hy_deltas, hy_bias, sc_conv_w, ln1_g, ln1_b, router_w, router_b, w_gate, b_gate, w_up, b_up, w_down, b_down, ln2_g, ln2_b):
    nb, seq, _ = x_prompt.shape
    db, dseq, _ = x_sample.shape
    past = cache_k.shape[2]
    t_ctx = nb * seq
    x = (x_prompt.reshape(t_ctx, D), x_sample.reshape(db * dseq, D))

    cond = jnp.zeros((8, D), F32).at[0].set(c_ctx).at[1:1 + db].set(c)
    mod = _modulation(cond, w_mod, b_mod)

    tables = _rope_tables(dseq)
    fft_c = _fft_constants(dseq)

    new_k, new_v = [], []
    for l in range(DEPTH):
        q, k, v, u_hy, u_sc = _in_projection(x[0], x[1], mod, w_in, l, dseq)
        new_k.append(k[:t_ctx].reshape(nb, seq, N_KV, HD))
        new_v.append(v[:t_ctx].reshape(nb, seq, N_KV, HD))
        oa_ctx = _context_attention(q, k, v, attn_sink[l], nb, seq)
        oa_lat = _latent_attention(q, k, v, cache_k[:, l].reshape(db, past, KV_W),
                                   cache_v[:, l].reshape(db, past, KV_W), attn_sink[l], tables, t_ctx, db, dseq)
        mlp = (hy_w1[l], hy_b1[l], hy_f1[l], hy_w2[l], hy_b2[l], hy_f2[l], hy_w3[l], hy_deltas[l])
        ohy_ctx, osc_ctx = _ctx_mixers(u_hy, u_sc, hy_conv_w[l], hy_conv_b[l], sc_conv_w[l], hy_bias[l], mlp,
                                       nb, seq)
        x0, g, gt, osc_lat = _mixer_prep(u_hy, u_sc, hy_conv_w[l], hy_conv_b[l], sc_conv_w[l], t_ctx, db, dseq)
        ohy_lat = _hyena_long_conv(x0, g, gt, _hyena_filters(dseq, mlp), hy_bias[l], fft_c, db, dseq)
        x1, h2, meta, cnt = _out_projection((oa_ctx, oa_lat), (ohy_ctx, ohy_lat), (osc_ctx, osc_lat), x, mod,
                                            w_out, ln1_g[l], ln1_b[l], router_w[l], router_b[l], l, dseq)
        x = _moe_ln2(h2, meta, cnt, x1, mod, w_gate, b_gate, w_up, b_up, w_down, b_down, ln2_g[l], ln2_b[l],
                     l, t_ctx, dseq)

    y_prompt = x[0].reshape(nb, seq, D)
    y_sample = x[1].reshape(db, dseq, D)
    return (y_prompt, y_sample, jnp.stack(new_k, axis=1), jnp.stack(new_v, axis=1))
```
